```python
import math
import jax, jax.numpy as jnp
from jax import lax
import numpy as np

D_MODEL = 4096
BATCH = 8
SEQ = 4096
DEPTH = 1

D_MIX = D_MODEL
D_SSM = D_MIX // 2
D_SC = D_MIX - D_SSM
SSM_HEAD_DIM = 64
SSM_HEADS = D_SSM // SSM_HEAD_DIM
SSM_GROUPS = 8
SSM_STATE = 128
SSM_CONV = 5
SSM_CHUNK = 128
SC_CONV = 3
SC_GROUPS = 16
D_XBC = D_SSM + 2 * SSM_GROUPS * SSM_STATE
D_IN = D_SSM + D_XBC + 2 * SSM_HEADS + 3 * D_SC
D_FF = 4 * D_MODEL
N_MOD = 6
DEEPNORM_ALPHA = (2 * DEPTH) ** 0.25
DEEPNORM_BETA = (8 * DEPTH) ** -0.25
DT_PROJ_SCALE = 0.1
LN_EPS = 1e-5
RMS_EPS = 1e-5

kernel_name = "hymba_ssd_shortconv_deepnorm_adaln_encoder"


def layer_norm(x, g, b):
    xf = x.astype(jnp.float32)
    mu = jnp.mean(xf, axis=-1, keepdims=True)
    var = jnp.mean(jnp.square(xf - mu), axis=-1, keepdims=True)
    return ((xf - mu) * lax.rsqrt(var + LN_EPS) * g + b).astype(x.dtype)


def group_rms_norm(y, w, n_groups):
    bsz, s, d = y.shape
    yf = y.astype(jnp.float32).reshape(bsz, s, n_groups, d // n_groups)
    yf = yf * lax.rsqrt(jnp.mean(yf * yf, axis=-1, keepdims=True) + RMS_EPS)
    return (yf.reshape(bsz, s, d) * w).astype(y.dtype)


def dwconv_centred(u, w):
    k_w, ch = w.shape
    return lax.conv_general_dilated(
        u, w[:, None, :].astype(u.dtype), window_strides=(1,),
        padding=[(k_w // 2, k_w // 2)], dimension_numbers=('NWC', 'WIO', 'NWC'),
        feature_group_count=ch)


def ssd_chunked(x, dt, a, b_in, c_in):
    bsz, s, h, p = x.shape
    g, n = b_in.shape[2], b_in.shape[3]
    r = h // g
    q = SSM_CHUNK
    nc = s // q
    xc = x.reshape(bsz, nc, q, g, r, p)
    dtc = dt.reshape(bsz, nc, q, g, r)
    bc = b_in.reshape(bsz, nc, q, g, n)
    cc = c_in.reshape(bsz, nc, q, g, n)
    a_cum = jnp.cumsum(dtc * a.reshape(g, r), axis=2)
    xdt = xc * dtc[..., None]
    lower = jnp.tril(jnp.ones((q, q), dtype=bool))[:, :, None, None]
    seg = a_cum[:, :, :, None] - a_cum[:, :, None, :]
    decay = jnp.exp(jnp.where(lower, seg, -jnp.inf))
    scores = jnp.einsum('bcign,bcjgn->bcijg', cc, bc)
    y_diag = jnp.einsum('bcijgr,bcjgrp->bcigrp', scores[..., None] * decay, xdt)
    to_end = jnp.exp(a_cum[:, :, -1:] - a_cum)
    states = jnp.einsum('bclgn,bclgrp->bcgrpn', bc, xdt * to_end[..., None])
    chunk_decay = jnp.exp(a_cum[:, :, -1])

    def step(carry, inp):
        st, dec = inp
        return carry * dec[..., None, None] + st, carry

    init = jnp.zeros((bsz, g, r, p, n), dtype=states.dtype)
    _, prev = lax.scan(step, init, (jnp.moveaxis(states, 1, 0), jnp.moveaxis(chunk_decay, 1, 0)))
    prev = jnp.moveaxis(prev, 0, 1)
    y_off = jnp.einsum('bclgn,bcgrpn->bclgrp', cc, prev) * jnp.exp(a_cum)[..., None]
    return (y_diag + y_off).reshape(bsz, s, h, p).astype(x.dtype)


def ssd_mixer(u_z, u_xbc, u_dt, conv_w, conv_b, dt_bias_f, dt_bias_b, a_log_f, a_log_b, d_skip, norm_w):
    bsz, s, _ = u_z.shape
    xbc = jax.nn.silu(dwconv_centred(u_xbc, conv_w) + conv_b)
    xs, bs, cs = jnp.split(xbc, [D_SSM, D_SSM + SSM_GROUPS * SSM_STATE], axis=-1)
    xs = xs.reshape(bsz, s, SSM_HEADS, SSM_HEAD_DIM)
    bs = bs.reshape(bsz, s, SSM_GROUPS, SSM_STATE)
    cs = cs.reshape(bsz, s, SSM_GROUPS, SSM_STATE)
    dt = u_dt.astype(jnp.float32)
    dt_f = jax.nn.softplus(dt[..., :SSM_HEADS] + dt_bias_f)
    dt_b = jax.nn.softplus(dt[..., SSM_HEADS:] + dt_bias_b)
    a_f = -jnp.exp(a_log_f.astype(jnp.float32))
    a_b = -jnp.exp(a_log_b.astype(jnp.float32))
    flip = lambda t: jnp.flip(t, axis=1)
    y_f = ssd_chunked(xs, dt_f, a_f, bs, cs)
    y_b = flip(ssd_chunked(flip(xs), flip(dt_b), a_b, flip(bs), flip(cs)))
    y = y_f + y_b + d_skip[:, None] * xs
    y = y.reshape(bsz, s, D_SSM) * jax.nn.silu(u_z)
    return group_rms_norm(y, norm_w, SSM_GROUPS)


def short_conv_mixer(u_h, u_b, u_c, conv_w, norm_w):
    y = u_b * dwconv_centred(u_c * u_h, conv_w)
    return group_rms_norm(y, norm_w, SC_GROUPS)


def _fwd_setup_inputs(seed: int = 0) -> dict:
    key = jax.random.key(seed)
    ks = jax.random.split(key, 24)
    nrm = jax.random.normal
    x = nrm(ks[0], (BATCH, SEQ, D_MODEL), jnp.float32)
    c = nrm(ks[1], (BATCH, D_MODEL), jnp.float32)
    w_ada = nrm(ks[2], (DEPTH, D_MODEL, N_MOD * D_MODEL), jnp.float32) * (0.1 * D_MODEL ** -0.5)
    b_ada = 0.01 * nrm(ks[3], (DEPTH, N_MOD * D_MODEL), jnp.float32)
    dt_lo = D_SSM + D_XBC
    col_scale = jnp.ones((D_IN,), jnp.float32).at[dt_lo:dt_lo + 2 * SSM_HEADS].set(DT_PROJ_SCALE)
    w_in = nrm(ks[4], (DEPTH, D_MODEL, D_IN), jnp.float32) * (D_MODEL ** -0.5) * col_scale
    ssm_conv_w = nrm(ks[5], (DEPTH, SSM_CONV, D_XBC), jnp.float32) * SSM_CONV ** -0.5
    ssm_conv_b = 0.01 * nrm(ks[6], (DEPTH, D_XBC), jnp.float32)

    def dt_bias(k):
        dt0 = jnp.exp(jax.random.uniform(k, (DEPTH, SSM_HEADS), jnp.float32, math.log(1e-3), math.log(1e-1)))
        return dt0 + jnp.log(-jnp.expm1(-dt0))

    ssm_dt_bias_f = dt_bias(ks[7])
    ssm_dt_bias_b = dt_bias(ks[8])
    ssm_a_log_f = jnp.log(jax.random.uniform(ks[9], (DEPTH, SSM_HEADS), jnp.float32, 1.0, 16.0))
    ssm_a_log_b = jnp.log(jax.random.uniform(ks[10], (DEPTH, SSM_HEADS), jnp.float32, 1.0, 16.0))
    ssm_d = 1.0 + 0.1 * nrm(ks[11], (DEPTH, SSM_HEADS), jnp.float32)
    ssm_norm_w = 1.0 + 0.02 * nrm(ks[12], (DEPTH, D_SSM), jnp.float32)
    sc_conv_w = nrm(ks[13], (DEPTH, SC_CONV, D_SC), jnp.float32) * SC_CONV ** -0.5
    sc_norm_w = 1.0 + 0.02 * nrm(ks[14], (DEPTH, D_SC), jnp.float32)
    w_out = nrm(ks[15], (DEPTH, D_MIX, D_MODEL), jnp.float32) * (D_MIX ** -0.5 * DEEPNORM_BETA)
    ln1_g = 1.0 + 0.02 * nrm(ks[16], (DEPTH, D_MODEL), jnp.float32)
    ln1_b = 0.01 * nrm(ks[17], (DEPTH, D_MODEL), jnp.float32)
    w_up = nrm(ks[18], (DEPTH, D_MODEL, D_FF), jnp.float32) * D_MODEL ** -0.5
    w_down = nrm(ks[19], (DEPTH, D_FF, D_MODEL), jnp.float32) * (D_FF ** -0.5 * DEEPNORM_BETA)
    ln2_g = 1.0 + 0.02 * nrm(ks[20], (DEPTH, D_MODEL), jnp.float32)
    ln2_b = 0.01 * nrm(ks[21], (DEPTH, D_MODEL), jnp.float32)
    return {"x": x, "c": c, "w_ada": w_ada, "b_ada": b_ada, "w_in": w_in,
            "ssm_conv_w": ssm_conv_w, "ssm_conv_b": ssm_conv_b,
            "ssm_dt_bias_f": ssm_dt_bias_f, "ssm_dt_bias_b": ssm_dt_bias_b,
            "ssm_a_log_f": ssm_a_log_f, "ssm_a_log_b": ssm_a_log_b, "ssm_d": ssm_d,
            "ssm_norm_w": ssm_norm_w, "sc_conv_w": sc_conv_w, "sc_norm_w": sc_norm_w,
            "w_out": w_out, "ln1_g": ln1_g, "ln1_b": ln1_b, "w_up": w_up, "w_down": w_down,
            "ln2_g": ln2_g, "ln2_b": ln2_b}


def _fwd_reference(x, c, w_ada, b_ada, w_in, ssm_conv_w, ssm_conv_b, ssm_dt_bias_f, ssm_dt_bias_b,
              ssm_a_log_f, ssm_a_log_b, ssm_d, ssm_norm_w, sc_conv_w, sc_norm_w, w_out,
              ln1_g, ln1_b, w_up, w_down, ln2_g, ln2_b):
    bounds = [int(v) for v in np.cumsum([D_SSM, D_XBC, 2 * SSM_HEADS, D_SC, D_SC])]
    for l in range(DEPTH):
        mod = jnp.einsum('bd,dm->bm', jax.nn.silu(c), w_ada[l]) + b_ada[l]
        shift1, scale1, gate1, shift2, scale2, gate2 = jnp.split(mod[:, None, :], N_MOD, axis=-1)
        h = x * (1.0 + scale1) + shift1
        proj = jnp.einsum('bsd,de->bse', h, w_in[l])
        u_z, u_xbc, u_dt, u_h, u_b, u_c = jnp.split(proj, bounds, axis=-1)
        y_ssm = ssd_mixer(u_z, u_xbc, u_dt, ssm_conv_w[l], ssm_conv_b[l], ssm_dt_bias_f[l],
                          ssm_dt_bias_b[l], ssm_a_log_f[l], ssm_a_log_b[l], ssm_d[l], ssm_norm_w[l])
        y_sc = short_conv_mixer(u_h, u_b, u_c, sc_conv_w[l], sc_norm_w[l])
        mix = jnp.einsum('bse,ed->bsd', jnp.concatenate([y_ssm, y_sc], axis=-1), w_out[l])
        x = layer_norm(DEEPNORM_ALPHA * x + (1.0 + gate1) * mix, ln1_g[l], ln1_b[l])
        h = x * (1.0 + scale2) + shift2
        ff = jnp.square(jax.nn.relu(jnp.einsum('bsd,df->bsf', h, w_up[l])))
        ff = jnp.einsum('bsf,fd->bsd', ff, w_down[l])
        x = layer_norm(DEEPNORM_ALPHA * x + (1.0 + gate2) * ff, ln2_g[l], ln2_b[l])
    return x


import jax as _jax
import jax.numpy as _jnp

TWIN_FORMAT = 'train_step'
FWD_PARAMS = ['x', 'c', 'w_ada', 'b_ada', 'w_in', 'ssm_conv_w', 'ssm_conv_b', 'ssm_dt_bias_f', 'ssm_dt_bias_b', 'ssm_a_log_f', 'ssm_a_log_b', 'ssm_d', 'ssm_norm_w', 'sc_conv_w', 'sc_norm_w', 'w_out', 'ln1_g', 'ln1_b', 'w_up', 'w_down', 'ln2_g', 'ln2_b']
TWIN_WEIGHTS = ['w_ada', 'b_ada', 'w_in', 'ssm_conv_w', 'ssm_conv_b', 'ssm_dt_bias_f', 'ssm_dt_bias_b', 'ssm_a_log_f', 'ssm_a_log_b', 'ssm_d', 'ssm_norm_w', 'sc_conv_w', 'sc_norm_w', 'w_out', 'ln1_g', 'ln1_b', 'w_up', 'w_down', 'ln2_g', 'ln2_b']
TWIN_DIFF_INPUT = 'x'
TWIN_INPUTS = ['x', 'c', 'w_ada', 'b_ada', 'w_in', 'ssm_conv_w', 'ssm_conv_b', 'ssm_dt_bias_f', 'ssm_dt_bias_b', 'ssm_a_log_f', 'ssm_a_log_b', 'ssm_d', 'ssm_norm_w', 'sc_conv_w', 'sc_norm_w', 'w_out', 'ln1_g', 'ln1_b', 'w_up', 'w_down', 'ln2_g', 'ln2_b', 'loss_target', 'm_w_ada', 'm_b_ada', 'm_w_in', 'm_ssm_conv_w', 'm_ssm_conv_b', 'm_ssm_dt_bias_f', 'm_ssm_dt_bias_b', 'm_ssm_a_log_f', 'm_ssm_a_log_b', 'm_ssm_d', 'm_ssm_norm_w', 'm_sc_conv_w', 'm_sc_norm_w', 'm_w_out', 'm_ln1_g', 'm_ln1_b', 'm_w_up', 'm_w_down', 'm_ln2_g', 'm_ln2_b', 'v_w_ada', 'v_b_ada', 'v_w_in', 'v_ssm_conv_w', 'v_ssm_conv_b', 'v_ssm_dt_bias_f', 'v_ssm_dt_bias_b', 'v_ssm_a_log_f', 'v_ssm_a_log_b', 'v_ssm_d', 'v_ssm_norm_w', 'v_sc_conv_w', 'v_sc_norm_w', 'v_w_out', 'v_ln1_g', 'v_ln1_b', 'v_w_up', 'v_w_down', 'v_ln2_g', 'v_ln2_b']
TWIN_OUTPUTS = ['loss', 'grad_x', 'grad_w_ada', 'grad_b_ada', 'grad_w_in', 'grad_ssm_conv_w', 'grad_ssm_conv_b', 'grad_ssm_dt_bias_f', 'grad_ssm_dt_bias_b', 'grad_ssm_a_log_f', 'grad_ssm_a_log_b', 'grad_ssm_d', 'grad_ssm_norm_w', 'grad_sc_conv_w', 'grad_sc_norm_w', 'grad_w_out', 'grad_ln1_g', 'grad_ln1_b', 'grad_w_up', 'grad_w_down', 'grad_ln2_g', 'grad_ln2_b', 'delta_w_ada', 'delta_b_ada', 'delta_w_in', 'delta_ssm_conv_w', 'delta_ssm_conv_b', 'delta_ssm_dt_bias_f', 'delta_ssm_dt_bias_b', 'delta_ssm_a_log_f', 'delta_ssm_a_log_b', 'delta_ssm_d', 'delta_ssm_norm_w', 'delta_sc_conv_w', 'delta_sc_norm_w', 'delta_w_out', 'delta_ln1_g', 'delta_ln1_b', 'delta_w_up', 'delta_w_down', 'delta_ln2_g', 'delta_ln2_b', 'new_m_w_ada', 'new_m_b_ada', 'new_m_w_in', 'new_m_ssm_conv_w', 'new_m_ssm_conv_b', 'new_m_ssm_dt_bias_f', 'new_m_ssm_dt_bias_b', 'new_m_ssm_a_log_f', 'new_m_ssm_a_log_b', 'new_m_ssm_d', 'new_m_ssm_norm_w', 'new_m_sc_conv_w', 'new_m_sc_norm_w', 'new_m_w_out', 'new_m_ln1_g', 'new_m_ln1_b', 'new_m_w_up', 'new_m_w_down', 'new_m_ln2_g', 'new_m_ln2_b', 'new_v_w_ada', 'new_v_b_ada', 'new_v_w_in', 'new_v_ssm_conv_w', 'new_v_ssm_conv_b', 'new_v_ssm_dt_bias_f', 'new_v_ssm_dt_bias_b', 'new_v_ssm_a_log_f', 'new_v_ssm_a_log_b', 'new_v_ssm_d', 'new_v_ssm_norm_w', 'new_v_sc_conv_w', 'new_v_sc_norm_w', 'new_v_w_out', 'new_v_ln1_g', 'new_v_ln1_b', 'new_v_w_up', 'new_v_w_down', 'new_v_ln2_g', 'new_v_ln2_b']
TWIN_LEAF_KINDS = {'loss': 'loss', 'grad_x': 'grad_x', 'grad_w_ada': 'grad_w', 'grad_b_ada': 'grad_w', 'grad_w_in': 'grad_w', 'grad_ssm_conv_w': 'grad_w', 'grad_ssm_conv_b': 'grad_w', 'grad_ssm_dt_bias_f': 'grad_w', 'grad_ssm_dt_bias_b': 'grad_w', 'grad_ssm_a_log_f': 'grad_w', 'grad_ssm_a_log_b': 'grad_w', 'grad_ssm_d': 'grad_w', 'grad_ssm_norm_w': 'grad_w', 'grad_sc_conv_w': 'grad_w', 'grad_sc_norm_w': 'grad_w', 'grad_w_out': 'grad_w', 'grad_ln1_g': 'grad_w', 'grad_ln1_b': 'grad_w', 'grad_w_up': 'grad_w', 'grad_w_down': 'grad_w', 'grad_ln2_g': 'grad_w', 'grad_ln2_b': 'grad_w', 'delta_w_ada': 'delta_w', 'delta_b_ada': 'delta_w', 'delta_w_in': 'delta_w', 'delta_ssm_conv_w': 'delta_w', 'delta_ssm_conv_b': 'delta_w', 'delta_ssm_dt_bias_f': 'delta_w', 'delta_ssm_dt_bias_b': 'delta_w', 'delta_ssm_a_log_f': 'delta_w', 'delta_ssm_a_log_b': 'delta_w', 'delta_ssm_d': 'delta_w', 'delta_ssm_norm_w': 'delta_w', 'delta_sc_conv_w': 'delta_w', 'delta_sc_norm_w': 'delta_w', 'delta_w_out': 'delta_w', 'delta_ln1_g': 'delta_w', 'delta_ln1_b': 'delta_w', 'delta_w_up': 'delta_w', 'delta_w_down': 'delta_w', 'delta_ln2_g': 'delta_w', 'delta_ln2_b': 'delta_w', 'new_m_w_ada': 'new_m', 'new_m_b_ada': 'new_m', 'new_m_w_in': 'new_m', 'new_m_ssm_conv_w': 'new_m', 'new_m_ssm_conv_b': 'new_m', 'new_m_ssm_dt_bias_f': 'new_m', 'new_m_ssm_dt_bias_b': 'new_m', 'new_m_ssm_a_log_f': 'new_m', 'new_m_ssm_a_log_b': 'new_m', 'new_m_ssm_d': 'new_m', 'new_m_ssm_norm_w': 'new_m', 'new_m_sc_conv_w': 'new_m', 'new_m_sc_norm_w': 'new_m', 'new_m_w_out': 'new_m', 'new_m_ln1_g': 'new_m', 'new_m_ln1_b': 'new_m', 'new_m_w_up': 'new_m', 'new_m_w_down': 'new_m', 'new_m_ln2_g': 'new_m', 'new_m_ln2_b': 'new_m', 'new_v_w_ada': 'new_v', 'new_v_b_ada': 'new_v', 'new_v_w_in': 'new_v', 'new_v_ssm_conv_w': 'new_v', 'new_v_ssm_conv_b': 'new_v', 'new_v_ssm_dt_bias_f': 'new_v', 'new_v_ssm_dt_bias_b': 'new_v', 'new_v_ssm_a_log_f': 'new_v', 'new_v_ssm_a_log_b': 'new_v', 'new_v_ssm_d': 'new_v', 'new_v_ssm_norm_w': 'new_v', 'new_v_sc_conv_w': 'new_v', 'new_v_sc_norm_w': 'new_v', 'new_v_w_out': 'new_v', 'new_v_ln1_g': 'new_v', 'new_v_ln1_b': 'new_v', 'new_v_w_up': 'new_v', 'new_v_w_down': 'new_v', 'new_v_ln2_g': 'new_v', 'new_v_ln2_b': 'new_v'}


def _forward(args):
    return _fwd_reference(*[args[k] for k in FWD_PARAMS])


def _output_shape():
    out = _jax.eval_shape(lambda: _forward(_fwd_setup_inputs(0)))
    return out.shape, out.dtype

N_MICROBATCH = 1
ADAM_LR = 0.001
ADAM_B1 = 0.9
ADAM_B2 = 0.999
ADAM_EPS = 1e-08
ADAM_WD = 0.01
ADAM_STEP = 10
PER_EXAMPLE_BATCH_AXIS = {'x': 0, 'c': 0, 'loss_target': 0}
SHARED_INPUTS = []
_WEIGHT_DTYPES = {'w_ada': _jnp.float32, 'b_ada': _jnp.float32, 'w_in': _jnp.float32, 'ssm_conv_w': _jnp.float32, 'ssm_conv_b': _jnp.float32, 'ssm_dt_bias_f': _jnp.float32, 'ssm_dt_bias_b': _jnp.float32, 'ssm_a_log_f': _jnp.float32, 'ssm_a_log_b': _jnp.float32, 'ssm_d': _jnp.float32, 'ssm_norm_w': _jnp.float32, 'sc_conv_w': _jnp.float32, 'sc_norm_w': _jnp.float32, 'w_out': _jnp.float32, 'ln1_g': _jnp.float32, 'ln1_b': _jnp.float32, 'w_up': _jnp.float32, 'w_down': _jnp.float32, 'ln2_g': _jnp.float32, 'ln2_b': _jnp.float32}
MOMENT_SCALE = {'w_ada': 1.922766e-02, 'b_ada': 3.868949e-02, 'w_in': 1.913004e-02, 'ssm_conv_w': 1.408068e-02, 'ssm_conv_b': 2.801244e-02, 'ssm_dt_bias_f': 3.129581e-02, 'ssm_dt_bias_b': 2.078606e-02, 'ssm_a_log_f': 3.873345e-02, 'ssm_a_log_b': 4.677511e-02, 'ssm_d': 9.902644e-02, 'ssm_norm_w': 2.065113e-02, 'sc_conv_w': 2.113337e-02, 'sc_norm_w': 2.115400e-02, 'w_out': 3.551808e-02, 'ln1_g': 1.773224e-01, 'ln1_b': 7.216423e-02, 'w_up': 1.347814e-02, 'w_down': 4.443655e-02, 'ln2_g': 8.015387e+00, 'ln2_b': 1.810858e+00}


def _to_microbatches(a, axis):
    t = _jnp.moveaxis(a, axis, 0)
    t = t.reshape((N_MICROBATCH, t.shape[0] // N_MICROBATCH) + t.shape[1:])
    return _jnp.moveaxis(t, 1, axis + 1)


def setup_inputs(seed: int = 0) -> dict:
    inp = _fwd_setup_inputs(seed)
    key = _jax.random.fold_in(_jax.random.key(seed), 7919)
    shape, _ = _output_shape()
    out = dict(inp)
    out["loss_target"] = _jax.random.normal(_jax.random.fold_in(key, 0), shape, _jnp.float32)
    for i, name in enumerate(TWIN_WEIGHTS):
        w = inp[name].astype(_jnp.float32)
        if MOMENT_SCALE is None:
            s = _jnp.sqrt(_jnp.mean(_jnp.square(w)) + 1e-30)
        else:
            s = MOMENT_SCALE[name]
        km, kv = _jax.random.split(_jax.random.fold_in(key, i + 1))
        out[name] = w
        out["m_" + name] = s * _jax.random.normal(km, w.shape, _jnp.float32)
        out["v_" + name] = (s * s) * _jax.random.uniform(kv, w.shape, _jnp.float32, 0.5, 1.5)
    if N_MICROBATCH > 1:
        for name, axis in PER_EXAMPLE_BATCH_AXIS.items():
            out[name] = _to_microbatches(out[name], axis)
    return {'x': out['x'], 'c': out['c'], 'w_ada': out['w_ada'], 'b_ada': out['b_ada'], 'w_in': out['w_in'], 'ssm_conv_w': out['ssm_conv_w'], 'ssm_conv_b': out['ssm_conv_b'], 'ssm_dt_bias_f': out['ssm_dt_bias_f'], 'ssm_dt_bias_b': out['ssm_dt_bias_b'], 'ssm_a_log_f': out['ssm_a_log_f'], 'ssm_a_log_b': out['ssm_a_log_b'], 'ssm_d': out['ssm_d'], 'ssm_norm_w': out['ssm_norm_w'], 'sc_conv_w': out['sc_conv_w'], 'sc_norm_w': out['sc_norm_w'], 'w_out': out['w_out'], 'ln1_g': out['ln1_g'], 'ln1_b': out['ln1_b'], 'w_up': out['w_up'], 'w_down': out['w_down'], 'ln2_g': out['ln2_g'], 'ln2_b': out['ln2_b'], 'loss_target': out['loss_target'], 'm_w_ada': out['m_w_ada'], 'm_b_ada': out['m_b_ada'], 'm_w_in': out['m_w_in'], 'm_ssm_conv_w': out['m_ssm_conv_w'], 'm_ssm_conv_b': out['m_ssm_conv_b'], 'm_ssm_dt_bias_f': out['m_ssm_dt_bias_f'], 'm_ssm_dt_bias_b': out['m_ssm_dt_bias_b'], 'm_ssm_a_log_f': out['m_ssm_a_log_f'], 'm_ssm_a_log_b': out['m_ssm_a_log_b'], 'm_ssm_d': out['m_ssm_d'], 'm_ssm_norm_w': out['m_ssm_norm_w'], 'm_sc_conv_w': out['m_sc_conv_w'], 'm_sc_norm_w': out['m_sc_norm_w'], 'm_w_out': out['m_w_out'], 'm_ln1_g': out['m_ln1_g'], 'm_ln1_b': out['m_ln1_b'], 'm_w_up': out['m_w_up'], 'm_w_down': out['m_w_down'], 'm_ln2_g': out['m_ln2_g'], 'm_ln2_b': out['m_ln2_b'], 'v_w_ada': out['v_w_ada'], 'v_b_ada': out['v_b_ada'], 'v_w_in': out['v_w_in'], 'v_ssm_conv_w': out['v_ssm_conv_w'], 'v_ssm_conv_b': out['v_ssm_conv_b'], 'v_ssm_dt_bias_f': out['v_ssm_dt_bias_f'], 'v_ssm_dt_bias_b': out['v_ssm_dt_bias_b'], 'v_ssm_a_log_f': out['v_ssm_a_log_f'], 'v_ssm_a_log_b': out['v_ssm_a_log_b'], 'v_ssm_d': out['v_ssm_d'], 'v_ssm_norm_w': out['v_ssm_norm_w'], 'v_sc_conv_w': out['v_sc_conv_w'], 'v_sc_norm_w': out['v_sc_norm_w'], 'v_w_out': out['v_w_out'], 'v_ln1_g': out['v_ln1_g'], 'v_ln1_b': out['v_ln1_b'], 'v_w_up': out['v_w_up'], 'v_w_down': out['v_w_down'], 'v_ln2_g': out['v_ln2_g'], 'v_ln2_b': out['v_ln2_b']}


def _loss(weights, diff, rest, loss_target):
    with _jax.named_scope("forward"):
        args = {**rest, TWIN_DIFF_INPUT: diff, **{k: w.astype(_WEIGHT_DTYPES[k]) for k, w in weights.items()}}
        y = _forward(args)
    with _jax.named_scope("loss_head"):
        err = _jnp.square(y.astype(_jnp.float32) - loss_target)
        return 0.5 * _jnp.sum(_jnp.mean(err, axis=-1)) if err.ndim else 0.5 * err


def _adamw(w, g, m, v):
    m = ADAM_B1 * m + (1.0 - ADAM_B1) * g
    v = ADAM_B2 * v + (1.0 - ADAM_B2) * _jnp.square(g)
    m_hat = m / (1.0 - ADAM_B1 ** ADAM_STEP)
    v_hat = v / (1.0 - ADAM_B2 ** ADAM_STEP)
    delta = -ADAM_LR * (m_hat / (_jnp.sqrt(v_hat) + ADAM_EPS) + ADAM_WD * w)
    return delta, m, v


def reference(x, c, w_ada, b_ada, w_in, ssm_conv_w, ssm_conv_b, ssm_dt_bias_f, ssm_dt_bias_b, ssm_a_log_f, ssm_a_log_b, ssm_d, ssm_norm_w, sc_conv_w, sc_norm_w, w_out, ln1_g, ln1_b, w_up, w_down, ln2_g, ln2_b, loss_target, m_w_ada, m_b_ada, m_w_in, m_ssm_conv_w, m_ssm_conv_b, m_ssm_dt_bias_f, m_ssm_dt_bias_b, m_ssm_a_log_f, m_ssm_a_log_b, m_ssm_d, m_ssm_norm_w, m_sc_conv_w, m_sc_norm_w, m_w_out, m_ln1_g, m_ln1_b, m_w_up, m_w_down, m_ln2_g, m_ln2_b, v_w_ada, v_b_ada, v_w_in, v_ssm_conv_w, v_ssm_conv_b, v_ssm_dt_bias_f, v_ssm_dt_bias_b, v_ssm_a_log_f, v_ssm_a_log_b, v_ssm_d, v_ssm_norm_w, v_sc_conv_w, v_sc_norm_w, v_w_out, v_ln1_g, v_ln1_b, v_w_up, v_w_down, v_ln2_g, v_ln2_b):
    given = dict(x=x, c=c, w_ada=w_ada, b_ada=b_ada, w_in=w_in, ssm_conv_w=ssm_conv_w, ssm_conv_b=ssm_conv_b, ssm_dt_bias_f=ssm_dt_bias_f, ssm_dt_bias_b=ssm_dt_bias_b, ssm_a_log_f=ssm_a_log_f, ssm_a_log_b=ssm_a_log_b, ssm_d=ssm_d, ssm_norm_w=ssm_norm_w, sc_conv_w=sc_conv_w, sc_norm_w=sc_norm_w, w_out=w_out, ln1_g=ln1_g, ln1_b=ln1_b, w_up=w_up, w_down=w_down, ln2_g=ln2_g, ln2_b=ln2_b, loss_target=loss_target, m_w_ada=m_w_ada, m_b_ada=m_b_ada, m_w_in=m_w_in, m_ssm_conv_w=m_ssm_conv_w, m_ssm_conv_b=m_ssm_conv_b, m_ssm_dt_bias_f=m_ssm_dt_bias_f, m_ssm_dt_bias_b=m_ssm_dt_bias_b, m_ssm_a_log_f=m_ssm_a_log_f, m_ssm_a_log_b=m_ssm_a_log_b, m_ssm_d=m_ssm_d, m_ssm_norm_w=m_ssm_norm_w, m_sc_conv_w=m_sc_conv_w, m_sc_norm_w=m_sc_norm_w, m_w_out=m_w_out, m_ln1_g=m_ln1_g, m_ln1_b=m_ln1_b, m_w_up=m_w_up, m_w_down=m_w_down, m_ln2_g=m_ln2_g, m_ln2_b=m_ln2_b, v_w_ada=v_w_ada, v_b_ada=v_b_ada, v_w_in=v_w_in, v_ssm_conv_w=v_ssm_conv_w, v_ssm_conv_b=v_ssm_conv_b, v_ssm_dt_bias_f=v_ssm_dt_bias_f, v_ssm_dt_bias_b=v_ssm_dt_bias_b, v_ssm_a_log_f=v_ssm_a_log_f, v_ssm_a_log_b=v_ssm_a_log_b, v_ssm_d=v_ssm_d, v_ssm_norm_w=v_ssm_norm_w, v_sc_conv_w=v_sc_conv_w, v_sc_norm_w=v_sc_norm_w, v_w_out=v_w_out, v_ln1_g=v_ln1_g, v_ln1_b=v_ln1_b, v_w_up=v_w_up, v_w_down=v_w_down, v_ln2_g=v_ln2_g, v_ln2_b=v_ln2_b)
    weights = {n: given[n] for n in TWIN_WEIGHTS}
    shared = {n: given[n] for n in SHARED_INPUTS}
    per_example = {n: given[n] for n in ['x', 'c']}
    grad_fn = _jax.value_and_grad(_loss, argnums=(0, 1))

    def one_microbatch(ex, loss_target):
        ex = dict(ex)
        diff = ex.pop(TWIN_DIFF_INPUT)
        return grad_fn(weights, diff, {**shared, **ex}, loss_target)

    if N_MICROBATCH == 1:
        loss, (grad_w, grad_x) = one_microbatch(per_example, given["loss_target"])
    else:
        def body(carry, xs):
            loss_sum, grad_sum = carry
            l_k, (gw_k, gx_k) = one_microbatch(xs[0], xs[1])
            with _jax.named_scope("update"):
                return (loss_sum + l_k, _jax.tree.map(_jnp.add, grad_sum, gw_k)), gx_k

        init = (_jnp.zeros((), _jnp.float32), _jax.tree.map(_jnp.zeros_like, weights))
        (loss, grad_w), grad_x = _jax.lax.scan(body, init, (per_example, given["loss_target"]))
    with _jax.named_scope("update"):
        delta_w, new_m, new_v = {}, {}, {}
        for n in TWIN_WEIGHTS:
            delta_w[n], new_m[n], new_v[n] = _adamw(weights[n], grad_w[n], given["m_" + n], given["v_" + n])
    return (loss, grad_x, *[grad_w[n] for n in TWIN_WEIGHTS], *[delta_w[n] for n in TWIN_WEIGHTS],
            *[new_m[n] for n in TWIN_WEIGHTS], *[new_v[n] for n in TWIN_WEIGHTS])
```

```python
import functools

import jax
import jax.numpy as jnp
from jax import lax
from jax.experimental import pallas as pl
from jax.experimental.pallas import tpu as pltpu

F32 = jnp.float32
BF16 = jnp.bfloat16
MESH = pl.DeviceIdType.MESH

N_DEV = 8
D_MODEL = 4096
D_SSM = 2048
D_SC = 2048
HEADS = 32
HEAD_DIM = 64
GROUPS = 8
GROUP_W = D_SSM // GROUPS
HEADS_PER_GROUP = 4
N_STATE = 128
CHUNK = 128
SSM_CONV = 5
SC_CONV = 3
SC_GROUP_W = 128
D_XBC = 4096
D_FF = 16384
D_IN = 12352
D_IN_SHARD = D_IN // N_DEV
D_MAIN = 12288
N_MOD = 6
ALPHA = (2 * 1) ** 0.25
LN_EPS = 1e-5
RMS_EPS = 1e-5
ADAM_LR = 0.001
ADAM_B1 = 0.9
ADAM_B2 = 0.999
ADAM_EPS = 1e-08
ADAM_WD = 0.01
ADAM_STEP = 10

VMEM_LIMIT = 56 * 1024 * 1024
HALO = 8

_DN = {
    "nn": (((1,), (0,)), ((), ())),
    "nt": (((1,), (1,)), ((), ())),
    "tn": (((0,), (0,)), ((), ())),
}


def _cparams(sem=None):
    return pltpu.CompilerParams(dimension_semantics=sem, vmem_limit_bytes=VMEM_LIMIT)


def _matmul(name, a, b, *, mode, grid, a_spec, b_spec, out_shapes, out_specs, acc_shape,
            epilogue=None, extras=(), extra_specs=()):
    nk = grid[2]
    n_extra = len(extras)
    n_out = len(out_shapes)

    def body(*refs):
        a_ref, b_ref = refs[0], refs[1]
        extra_refs = refs[2:2 + n_extra]
        out_refs = refs[2 + n_extra:2 + n_extra + n_out]
        part = lax.dot_general(a_ref[...], b_ref[...], _DN[mode], preferred_element_type=F32)

        def finish(acc):
            outs = epilogue(acc, *[r[...] for r in extra_refs]) if epilogue else (acc,)
            for o_ref, o in zip(out_refs, outs):
                o_ref[...] = o.astype(o_ref.dtype)

        if nk == 1:
            finish(part)
        else:
            acc_ref = refs[-1]
            k = pl.program_id(2)

            @pl.when(k == 0)
            def _():
                acc_ref[...] = part

            @pl.when(k > 0)
            def _():
                acc_ref[...] += part

            @pl.when(k == nk - 1)
            def _():
                finish(acc_ref[...])

    scratch = [pltpu.VMEM(acc_shape, F32)] if nk > 1 else []
    res = pl.pallas_call(
        body, name=name, grid=grid,
        in_specs=[a_spec, b_spec, *extra_specs],
        out_specs=list(out_specs),
        out_shape=list(out_shapes),
        scratch_shapes=scratch,
        compiler_params=_cparams(("parallel", "parallel", "arbitrary")),
    )(a, b, *extras)
    return res


def _tile(n, pref):
    t = min(n, pref)
    assert n % t == 0, (n, t)
    return t


def _mm_nn(name, a, b, out_dtype, tn=1024, tk=None, epilogue=None, out_dtypes=None):
    m, k = a.shape
    n = b.shape[1]
    tm, tn = _tile(m, 1024), _tile(n, tn)
    tk = _tile(k, tk or 4096)
    out_dtypes = out_dtypes or (out_dtype,)
    return _matmul(
        name, a, b, mode="nn", grid=(m // tm, n // tn, k // tk),
        a_spec=pl.BlockSpec((tm, tk), lambda i, j, kk: (i, kk)),
        b_spec=pl.BlockSpec((tk, tn), lambda i, j, kk: (kk, j)),
        out_shapes=[jax.ShapeDtypeStruct((m, n), dt) for dt in out_dtypes],
        out_specs=[pl.BlockSpec((tm, tn), lambda i, j, kk: (i, j)) for _ in out_dtypes],
        acc_shape=(tm, tn), epilogue=epilogue)


def _mm_nt(name, a, b, out_dtype, epilogue=None, extras=(), tk=None):
    m, k = a.shape
    n = b.shape[0]
    tm, tn = _tile(m, 1024), _tile(n, 1024)
    tk = _tile(k, tk or 4096)
    o_spec = pl.BlockSpec((tm, tn), lambda i, j, kk: (i, j))
    return _matmul(
        name, a, b, mode="nt", grid=(m // tm, n // tn, k // tk),
        a_spec=pl.BlockSpec((tm, tk), lambda i, j, kk: (i, kk)),
        b_spec=pl.BlockSpec((tn, tk), lambda i, j, kk: (j, kk)),
        out_shapes=[jax.ShapeDtypeStruct((m, n), out_dtype)],
        out_specs=[o_spec], acc_shape=(tm, tn), epilogue=epilogue,
        extras=extras, extra_specs=[o_spec for _ in extras])


def _mm_tn(name, a, b, out_dtype, tk=2048):
    k, m = a.shape
    n = b.shape[1]
    tm, tn = _tile(m, 1024), _tile(n, 1024)
    tk = _tile(k, tk)
    return _matmul(
        name, a, b, mode="tn", grid=(m // tm, n // tn, k // tk),
        a_spec=pl.BlockSpec((tk, tm), lambda i, j, kk: (kk, i)),
        b_spec=pl.BlockSpec((tk, tn), lambda i, j, kk: (kk, j)),
        out_shapes=[jax.ShapeDtypeStruct((m, n), out_dtype)],
        out_specs=[pl.BlockSpec((tm, tn), lambda i, j, kk: (i, j))],
        acc_shape=(tm, tn))


def _cast_bf16(name, w):
    r, c = w.shape
    tr = _tile(r, 512)

    def body(w_ref, o_ref):
        o_ref[...] = w_ref[...].astype(BF16)

    return pl.pallas_call(
        body, name=name, grid=(r // tr,),
        in_specs=[pl.BlockSpec((tr, c), lambda i: (i, 0))],
        out_specs=pl.BlockSpec((tr, c), lambda i: (i, 0)),
        out_shape=jax.ShapeDtypeStruct((r, c), BF16),
        compiler_params=_cparams(("parallel",)),
    )(w)


def _my_pos():
    return lax.axis_index("x"), lax.axis_index("y"), lax.axis_index("c")


def _other_chips(x, y):
    return [(1 - x, y), (x, 1 - y), (1 - x, 1 - y)]


def _allgather(name, shard, in_vmem):
    r, cdim = shard.shape
    space = pltpu.VMEM if in_vmem else pl.ANY

    def body(x_ref, out_ref, send_sems, recv_sems, local_sem):
        x, y, c = _my_pos()
        me, sibling = (x, y, c), (x, y, 1 - c)
        chips = _other_chips(x, y)

        def slab(px, py, pc):
            return out_ref.at[4 * px + 2 * py + pc]

        def copy(k, block, to, src=None):
            return pltpu.make_async_remote_copy(
                src_ref=slab(*block) if src is None else src, dst_ref=slab(*block),
                send_sem=send_sems.at[k], recv_sem=recv_sems.at[k],
                device_id=to, device_id_type=MESH)

        mine = pltpu.make_async_copy(x_ref, slab(*me), local_sem)
        mine.start()
        first = [copy(0, me, sibling, src=x_ref)]
        first += [copy(1 + j, me, (*chip, c), src=x_ref) for j, chip in enumerate(chips)]
        for cp in first:
            cp.start()
        passed = [copy(4 + j, (*chip, c), sibling) for j, chip in enumerate(chips)]
        for j, chip in enumerate(chips):
            copy(1 + j, (*chip, c), me).wait_recv()
            passed[j].start()
        copy(0, sibling, me).wait_recv()
        for j, chip in enumerate(chips):
            copy(4 + j, (*chip, 1 - c), me).wait_recv()
        for cp in first + passed:
            cp.wait_send()
        mine.wait()

    return pl.pallas_call(
        body, name=name,
        out_shape=jax.ShapeDtypeStruct((N_DEV, r, cdim), shard.dtype),
        in_specs=[pl.BlockSpec(memory_space=space)],
        out_specs=pl.BlockSpec(memory_space=space),
        scratch_shapes=[pltpu.SemaphoreType.DMA((7,)), pltpu.SemaphoreType.DMA((7,)),
                        pltpu.SemaphoreType.DMA],
        compiler_params=pltpu.CompilerParams(vmem_limit_bytes=VMEM_LIMIT),
    )(shard)


def _exchange_sibling(name, g):
    _, r, cdim = g.shape

    def body(g_ref, out_ref, send_sems, recv_sems):
        x, y, c = _my_pos()
        sibling = (x, y, 1 - c)
        copies = []
        for j in range(4):
            copies.append(pltpu.make_async_remote_copy(
                src_ref=g_ref.at[2 * j + (1 - c)], dst_ref=out_ref.at[j],
                send_sem=send_sems.at[j], recv_sem=recv_sems.at[j],
                device_id=sibling, device_id_type=MESH))
        for cp in copies:
            cp.start()
        for cp in copies:
            cp.wait()

    return pl.pallas_call(
        body, name=name,
        out_shape=jax.ShapeDtypeStruct((4, r, cdim), g.dtype),
        in_specs=[pl.BlockSpec(memory_space=pl.ANY)],
        out_specs=pl.BlockSpec(memory_space=pl.ANY),
        scratch_shapes=[pltpu.SemaphoreType.DMA((4,)), pltpu.SemaphoreType.DMA((4,))],
    )(g)


def _exchange_chips(name, p):
    _, r, cdim = p.shape

    def body(p_ref, out_ref, send_sems, recv_sems):
        x, y, c = _my_pos()
        copies = []
        for k, (px, py) in enumerate(_other_chips(x, y)):
            copies.append(pltpu.make_async_remote_copy(
                src_ref=p_ref.at[2 * px + py], dst_ref=out_ref.at[k],
                send_sem=send_sems.at[k], recv_sem=recv_sems.at[k],
                device_id=(px, py, c), device_id_type=MESH))
        for cp in copies:
            cp.start()
        for cp in copies:
            cp.wait()

    return pl.pallas_call(
        body, name=name,
        out_shape=jax.ShapeDtypeStruct((3, r, cdim), p.dtype),
        in_specs=[pl.BlockSpec(memory_space=pl.ANY)],
        out_specs=pl.BlockSpec(memory_space=pl.ANY),
        scratch_shapes=[pltpu.SemaphoreType.DMA((3,)), pltpu.SemaphoreType.DMA((3,))],
    )(p)


def _pair_add(name, g, r1, pos):
    _, r, cdim = g.shape
    tr = _tile(r, 512)

    def body(pos_ref, g_ref, r_ref, o_ref):
        o_ref[...] = (g_ref[...].astype(F32) + r_ref[...].astype(F32)).astype(o_ref.dtype)

    return pl.pallas_call(
        body, name=name,
        grid_spec=pltpu.PrefetchScalarGridSpec(
            num_scalar_prefetch=1, grid=(4, r // tr),
            in_specs=[pl.BlockSpec((None, tr, cdim), lambda j, i, pos: (2 * j + pos[2], i, 0)),
                      pl.BlockSpec((None, tr, cdim), lambda j, i, pos: (j, i, 0))],
            out_specs=pl.BlockSpec((None, tr, cdim), lambda j, i, pos: (j, i, 0))),
        out_shape=jax.ShapeDtypeStruct((4, r, cdim), BF16),
        compiler_params=_cparams(("parallel", "parallel")),
    )(pos, g, r1)


def _adamw_math(w, g, m, v):
    m = ADAM_B1 * m + (1.0 - ADAM_B1) * g
    v = ADAM_B2 * v + (1.0 - ADAM_B2) * jnp.square(g)
    m_hat = m / (1.0 - ADAM_B1 ** ADAM_STEP)
    v_hat = v / (1.0 - ADAM_B2 ** ADAM_STEP)
    delta = -ADAM_LR * (m_hat / (jnp.sqrt(v_hat) + ADAM_EPS) + ADAM_WD * w)
    return delta, m, v


def _reduce_adamw(name, p, r2, pos, w, m, v):
    r, cdim = w.shape
    tr = _tile(r, 128 if cdim >= D_MODEL else 256)
    blk = pl.BlockSpec((tr, cdim), lambda i, pos: (i, 0))

    def body(pos_ref, p_ref, r2_ref, w_ref, m_ref, v_ref, g_out, d_out, m_out, v_out):
        g = p_ref[...].astype(F32)
        for k in range(3):
            g = g + r2_ref[k].astype(F32)
        d, mn, vn = _adamw_math(w_ref[...], g, m_ref[...], v_ref[...])
        g_out[...] = g
        d_out[...] = d
        m_out[...] = mn
        v_out[...] = vn

    shp = jax.ShapeDtypeStruct((r, cdim), F32)
    return pl.pallas_call(
        body, name=name,
        grid_spec=pltpu.PrefetchScalarGridSpec(
            num_scalar_prefetch=1, grid=(r // tr,),
            in_specs=[pl.BlockSpec((None, tr, cdim), lambda i, pos: (2 * pos[0] + pos[1], i, 0)),
                      pl.BlockSpec((3, tr, cdim), lambda i, pos: (0, i, 0)),
                      blk, blk, blk],
            out_specs=[blk, blk, blk, blk]),
        out_shape=[shp, shp, shp, shp],
        compiler_params=_cparams(("parallel",)),
    )(pos, p, r2, w, m, v)


def _reduce_scatter_adamw(tag, gfull, pos, w, m, v):
    r1 = _exchange_sibling(f"rs_sibling_{tag}", gfull)
    p = _pair_add(f"rs_pair_add_{tag}", gfull, r1, pos)
    r2 = _exchange_chips(f"rs_chips_{tag}", p)
    return _reduce_adamw(f"rs_adamw_{tag}", p, r2, pos, w, m, v)


def _row_spec(t, width=D_MODEL):
    return pl.BlockSpec((t, width), lambda i: (i, 0))


def _full_spec(shape):
    return pl.BlockSpec(shape, lambda i: tuple(0 for _ in shape))


def _ln_stats(p):
    mu = jnp.mean(p, axis=-1, keepdims=True)
    xc = p - mu
    var = jnp.mean(xc * xc, axis=-1, keepdims=True)
    rstd = lax.rsqrt(var + LN_EPS)
    return xc * rstd, rstd


def _ln_bwd(dy, xhat, rstd, g):
    dxh = dy * g
    m1 = jnp.mean(dxh, axis=-1, keepdims=True)
    m2 = jnp.mean(dxh * xhat, axis=-1, keepdims=True)
    return rstd * (dxh - m1 - xhat * m2)


def _acc_rows(ref, val, first):
    s = jnp.sum(val, axis=0, keepdims=True)

    @pl.when(first)
    def _():
        ref[...] = s

    @pl.when(jnp.logical_not(first))
    def _():
        ref[...] += s


def _modulate(name, x, mod6):
    s = x.shape[0]
    t = _tile(s, 256)

    def body(x_ref, mod_ref, o_ref):
        o_ref[...] = (x_ref[...] * (1.0 + mod_ref[1:2, :]) + mod_ref[0:1, :]).astype(BF16)

    return pl.pallas_call(
        body, name=name, grid=(s // t,),
        in_specs=[_row_spec(t), _full_spec((N_MOD, D_MODEL))],
        out_specs=_row_spec(t),
        out_shape=jax.ShapeDtypeStruct((s, D_MODEL), BF16),
        compiler_params=_cparams(("parallel",)),
    )(x, mod6)


def _ln1_fwd(x, mix, mod6, g, b):
    s = x.shape[0]
    t = _tile(s, 256)

    def body(x_ref, mix_ref, mod_ref, g_ref, b_ref, x1_ref, h2_ref):
        pre = ALPHA * x_ref[...] + (1.0 + mod_ref[2:3, :]) * mix_ref[...]
        xhat, _ = _ln_stats(pre)
        x1 = xhat * g_ref[...] + b_ref[...]
        x1_ref[...] = x1
        h2_ref[...] = (x1 * (1.0 + mod_ref[4:5, :]) + mod_ref[3:4, :]).astype(BF16)

    vec = _full_spec((1, D_MODEL))
    return pl.pallas_call(
        body, name="ln1_fwd", grid=(s // t,),
        in_specs=[_row_spec(t), _row_spec(t), _full_spec((N_MOD, D_MODEL)), vec, vec],
        out_specs=[_row_spec(t), _row_spec(t)],
        out_shape=[jax.ShapeDtypeStruct((s, D_MODEL), F32), jax.ShapeDtypeStruct((s, D_MODEL), BF16)],
        compiler_params=_cparams(("parallel",)),
    )(x, mix, mod6, g, b)


def _ln2_loss_bwd(x1, f2, tgt, mod6, g, b):
    s = x1.shape[0]
    t = _tile(s, 128)

    def body(x1_ref, f2_ref, tgt_ref, mod_ref, g_ref, b_ref,
             df2_ref, dx1_ref, loss_ref, dg_ref, db_ref, dgate_ref):
        first = pl.program_id(0) == 0
        gate = 1.0 + mod_ref[5:6, :]
        f2v = f2_ref[...]
        pre = ALPHA * x1_ref[...] + gate * f2v
        xhat, rstd = _ln_stats(pre)
        err = xhat * g_ref[...] + b_ref[...] - tgt_ref[...]
        part = 0.5 * jnp.sum(jnp.mean(err * err, axis=-1, keepdims=True), axis=0, keepdims=True)
        dy = err / D_MODEL
        dpre = _ln_bwd(dy, xhat, rstd, g_ref[...])
        df2_ref[...] = (gate * dpre).astype(BF16)
        dx1_ref[...] = ALPHA * dpre
        _acc_rows(loss_ref, jnp.broadcast_to(part, (1, 128)), first)
        _acc_rows(dg_ref, dy * xhat, first)
        _acc_rows(db_ref, dy, first)
        _acc_rows(dgate_ref, dpre * f2v, first)

    vec = _full_spec((1, D_MODEL))
    vshape = jax.ShapeDtypeStruct((1, D_MODEL), F32)
    return pl.pallas_call(
        body, name="ln2_loss_bwd", grid=(s // t,),
        in_specs=[_row_spec(t), _row_spec(t), _row_spec(t), _full_spec((N_MOD, D_MODEL)), vec, vec],
        out_specs=[_row_spec(t), _row_spec(t), _full_spec((1, 128)), vec, vec, vec],
        out_shape=[jax.ShapeDtypeStruct((s, D_MODEL), BF16), jax.ShapeDtypeStruct((s, D_MODEL), F32),
                   jax.ShapeDtypeStruct((1, 128), F32), vshape, vshape, vshape],
        compiler_params=_cparams(("arbitrary",)),
    )(x1, f2, tgt, mod6, g, b)


def _ln1_bwd(dh2, dx1a, x1, x, mix, mod6, g):
    s = x.shape[0]
    t = _tile(s, 128)

    def body(dh2_ref, dx1a_ref, x1_ref, x_ref, mix_ref, mod_ref, g_ref,
             dmix_ref, dxa_ref, dscale_ref, dshift_ref, dg_ref, db_ref, dgate_ref):
        first = pl.program_id(0) == 0
        dh2v = dh2_ref[...]
        dx1 = dx1a_ref[...] + dh2v * (1.0 + mod_ref[4:5, :])
        gate = 1.0 + mod_ref[2:3, :]
        mixv = mix_ref[...]
        pre = ALPHA * x_ref[...] + gate * mixv
        xhat, rstd = _ln_stats(pre)
        dpre = _ln_bwd(dx1, xhat, rstd, g_ref[...])
        dmix_ref[...] = (gate * dpre).astype(BF16)
        dxa_ref[...] = ALPHA * dpre
        _acc_rows(dscale_ref, dh2v * x1_ref[...], first)
        _acc_rows(dshift_ref, dh2v, first)
        _acc_rows(dg_ref, dx1 * xhat, first)
        _acc_rows(db_ref, dx1, first)
        _acc_rows(dgate_ref, dpre * mixv, first)

    vec = _full_spec((1, D_MODEL))
    vshape = jax.ShapeDtypeStruct((1, D_MODEL), F32)
    return pl.pallas_call(
        body, name="ln1_bwd", grid=(s // t,),
        in_specs=[_row_spec(t)] * 5 + [_full_spec((N_MOD, D_MODEL)), vec],
        out_specs=[_row_spec(t), _row_spec(t), vec, vec, vec, vec, vec],
        out_shape=[jax.ShapeDtypeStruct((s, D_MODEL), BF16), jax.ShapeDtypeStruct((s, D_MODEL), F32),
                   vshape, vshape, vshape, vshape, vshape],
        compiler_params=_cparams(("arbitrary",)),
    )(dh2, dx1a, x1, x, mix, mod6, g)


def _grad_x(dxa, dh1, x, mod6):
    s = x.shape[0]
    t = _tile(s, 256)

    def body(dxa_ref, dh1_ref, x_ref, mod_ref, gx_ref, dscale_ref, dshift_ref):
        first = pl.program_id(0) == 0
        dh1v = dh1_ref[...]
        gx_ref[...] = dxa_ref[...] + dh1v * (1.0 + mod_ref[1:2, :])
        _acc_rows(dscale_ref, dh1v * x_ref[...], first)
        _acc_rows(dshift_ref, dh1v, first)

    vec = _full_spec((1, D_MODEL))
    vshape = jax.ShapeDtypeStruct((1, D_MODEL), F32)
    return pl.pallas_call(
        body, name="grad_x", grid=(s // t,),
        in_specs=[_row_spec(t)] * 3 + [_full_spec((N_MOD, D_MODEL))],
        out_specs=[_row_spec(t), vec, vec],
        out_shape=[jax.ShapeDtypeStruct((s, D_MODEL), F32), vshape, vshape],
        compiler_params=_cparams(("arbitrary",)),
    )(dxa, dh1, x, mod6)


def _window(ref, i, t, s):
    r0 = pl.multiple_of(i * t, t)
    cur = ref[pl.ds(r0, t), :]
    lo = pl.multiple_of(jnp.maximum(r0 - HALO, 0), HALO)
    hi = pl.multiple_of(jnp.minimum(r0 + t, s - HALO), HALO)
    before = ref[pl.ds(lo, HALO), :] * (i > 0).astype(F32)
    after = ref[pl.ds(hi, HALO), :] * (i < s // t - 1).astype(F32)
    return jnp.concatenate([before, cur, after], axis=0)


def _tap(ext, shift):
    n = ext.shape[0]
    if shift == 0:
        return ext
    return pltpu.roll(ext, (-shift) % n, 0)


def _centre(ext, t):
    return ext[HALO:HALO + t]


def _conv_taps(ext, w, width):
    acc = None
    for k in range(width):
        term = _tap(ext, k - width // 2) * w[k:k + 1, :]
        acc = term if acc is None else acc + term
    return acc


def _silu(a):
    return a * jax.nn.sigmoid(a)


def _conv_silu_fwd(proj, w, b):
    s = proj.shape[0]
    cb = 256
    t = _tile(s, 256)
    off = D_SSM // cb

    def body(u_ref, w_ref, b_ref, o_ref):
        wv = w_ref[...]
        bv = b_ref[...]

        def step(i, carry):
            ext = _window(u_ref, i, t, s)
            a = _centre(_conv_taps(ext, wv, SSM_CONV), t) + bv
            o_ref[pl.ds(pl.multiple_of(i * t, t), t), :] = _silu(a)
            return carry

        lax.fori_loop(0, s // t, step, 0)

    return pl.pallas_call(
        body, name="conv_silu_fwd", grid=(D_XBC // cb,),
        in_specs=[pl.BlockSpec((s, cb), lambda j: (0, off + j)),
                  pl.BlockSpec((SSM_CONV, cb), lambda j: (0, j)),
                  pl.BlockSpec((1, cb), lambda j: (0, j))],
        out_specs=pl.BlockSpec((s, cb), lambda j: (0, j)),
        out_shape=jax.ShapeDtypeStruct((s, D_XBC), F32),
        compiler_params=_cparams(("parallel",)),
    )(proj, w, b)


def _conv_silu_bwd(name, proj, w, b, col0, ncols, cots, scaled=None):
    s = proj.shape[0]
    cb = 128
    t = _tile(s, 256)
    off = (D_SSM + col0) // cb
    woff = col0 // cb
    n_cot = len(cots)

    def body(*refs):
        u_ref, w_ref, b_ref = refs[:3]
        cot_refs = refs[3:3 + n_cot]
        sc_refs = refs[3 + n_cot:3 + n_cot + (2 if scaled else 0)]
        du_ref, dw_ref, db_ref = refs[-3:]
        wv = w_ref[...]
        bv = b_ref[...]

        def step(i, carry):
            ext = _window(u_ref, i, t, s)
            a = _conv_taps(ext, wv, SSM_CONV) + bv
            cot = None
            for cr in cot_refs:
                term = _window(cr.at[0], i, t, s) + _window(cr.at[1], i, t, s)
                cot = term if cot is None else cot + term
            if scaled:
                cot = cot + _window(sc_refs[0], i, t, s) * sc_refs[1][...]
            sig = jax.nn.sigmoid(a)
            da = cot * (sig * (1.0 + a * (1.0 - sig)))
            du = None
            new = []
            for k in range(SSM_CONV):
                sh = k - SSM_CONV // 2
                term = _tap(da, -sh) * wv[k:k + 1, :]
                du = term if du is None else du + term
                prod = _centre(_tap(ext, sh) * da, t)
                new.append(carry[k] + jnp.sum(prod, axis=0, keepdims=True))
            new.append(carry[SSM_CONV] + jnp.sum(_centre(da, t), axis=0, keepdims=True))
            du_ref[pl.ds(pl.multiple_of(i * t, t), t), :] = _centre(du, t).astype(BF16)
            return tuple(new)

        zero = jnp.zeros((1, cb), F32)
        acc = lax.fori_loop(0, s // t, step, tuple(zero for _ in range(SSM_CONV + 1)))
        for k in range(SSM_CONV):
            dw_ref[k:k + 1, :] = acc[k]
        db_ref[...] = acc[SSM_CONV]

    in_specs = [pl.BlockSpec((s, cb), lambda j: (0, off + j)),
                pl.BlockSpec((SSM_CONV, cb), lambda j: (0, woff + j)),
                pl.BlockSpec((1, cb), lambda j: (0, woff + j))]
    in_specs += [pl.BlockSpec((2, s, cb), lambda j: (0, 0, j)) for _ in cots]
    args = [proj, w, b, *cots]
    if scaled:
        in_specs += [pl.BlockSpec((s, cb), lambda j: (0, j)), pl.BlockSpec((1, cb), lambda j: (0, j))]
        args += list(scaled)
    return pl.pallas_call(
        body, name=name, grid=(ncols // cb,),
        in_specs=in_specs,
        out_specs=[pl.BlockSpec((s, cb), lambda j: (0, j)),
                   pl.BlockSpec((SSM_CONV, cb), lambda j: (0, j)),
                   pl.BlockSpec((1, cb), lambda j: (0, j))],
        out_shape=[jax.ShapeDtypeStruct((s, ncols), BF16),
                   jax.ShapeDtypeStruct((SSM_CONV, ncols), F32),
                   jax.ShapeDtypeStruct((1, ncols), F32)],
        compiler_params=_cparams(("parallel",)),
    )(*args)


_SC_H = (D_SSM + D_XBC) // SC_GROUP_W
_SC_B = _SC_H + D_SC // SC_GROUP_W
_SC_C = _SC_B + D_SC // SC_GROUP_W


def _sc_fwd(proj, w, nw):
    s = proj.shape[0]
    cb = SC_GROUP_W
    t = _tile(s, 256)

    def body(uh_ref, ub_ref, uc_ref, w_ref, nw_ref, o_ref):
        wv = w_ref[...]
        nwv = nw_ref[...]

        def step(i, carry):
            p = _window(uc_ref, i, t, s) * _window(uh_ref, i, t, s)
            cv = _centre(_conv_taps(p, wv, SC_CONV), t)
            rows = pl.ds(pl.multiple_of(i * t, t), t)
            y = ub_ref[rows, :] * cv
            r = lax.rsqrt(jnp.mean(y * y, axis=-1, keepdims=True) + RMS_EPS)
            o_ref[rows, :] = (y * r * nwv).astype(BF16)
            return carry

        lax.fori_loop(0, s // t, step, 0)

    def col(base):
        return pl.BlockSpec((s, cb), lambda j: (0, base + j))

    return pl.pallas_call(
        body, name="sc_fwd", grid=(D_SC // cb,),
        in_specs=[col(_SC_H), col(_SC_B), col(_SC_C),
                  pl.BlockSpec((SC_CONV, cb), lambda j: (0, j)),
                  pl.BlockSpec((1, cb), lambda j: (0, j))],
        out_specs=pl.BlockSpec((s, cb), lambda j: (0, j)),
        out_shape=jax.ShapeDtypeStruct((s, D_SC), BF16),
        compiler_params=_cparams(("parallel",)),
    )(proj, proj, proj, w, nw)


def _sc_bwd(proj, dycat, w, nw):
    s = proj.shape[0]
    cb = SC_GROUP_W
    t = _tile(s, 256)
    dy_off = D_SSM // cb

    def body(uh_ref, ub_ref, uc_ref, dy_ref, w_ref, nw_ref, duh_ref, dub_ref, duc_ref, dw_ref, dnw_ref):
        wv = w_ref[...]
        nwv = nw_ref[...]

        def step(i, carry):
            uh = _window(uh_ref, i, t, s)
            ub = _window(ub_ref, i, t, s)
            uc = _window(uc_ref, i, t, s)
            do = _window(dy_ref, i, t, s)
            p = uc * uh
            cv = _conv_taps(p, wv, SC_CONV)
            y = ub * cv
            r = lax.rsqrt(jnp.mean(y * y, axis=-1, keepdims=True) + RMS_EPS)
            dyr = do * nwv
            dy = r * dyr - y * (r * r * r) * jnp.mean(dyr * y, axis=-1, keepdims=True)
            dcv = dy * ub
            dp = None
            new = []
            for k in range(SC_CONV):
                sh = k - SC_CONV // 2
                term = _tap(dcv, -sh) * wv[k:k + 1, :]
                dp = term if dp is None else dp + term
                new.append(carry[k] + jnp.sum(_centre(_tap(p, sh) * dcv, t), axis=0, keepdims=True))
            new.append(carry[SC_CONV] + jnp.sum(_centre(do * y * r, t), axis=0, keepdims=True))
            rows = pl.ds(pl.multiple_of(i * t, t), t)
            duh_ref[rows, :] = _centre(dp * uc, t).astype(BF16)
            duc_ref[rows, :] = _centre(dp * uh, t).astype(BF16)
            dub_ref[rows, :] = _centre(dy * cv, t).astype(BF16)
            return tuple(new)

        zero = jnp.zeros((1, cb), F32)
        acc = lax.fori_loop(0, s // t, step, tuple(zero for _ in range(SC_CONV + 1)))
        for k in range(SC_CONV):
            dw_ref[k:k + 1, :] = acc[k]
        dnw_ref[...] = acc[SC_CONV]

    def col(base):
        return pl.BlockSpec((s, cb), lambda j: (0, base + j))

    out_col = pl.BlockSpec((s, cb), lambda j: (0, j))
    act = jax.ShapeDtypeStruct((s, D_SC), BF16)
    return pl.pallas_call(
        body, name="sc_bwd", grid=(D_SC // cb,),
        in_specs=[col(_SC_H), col(_SC_B), col(_SC_C), col(dy_off),
                  pl.BlockSpec((SC_CONV, cb), lambda j: (0, j)),
                  pl.BlockSpec((1, cb), lambda j: (0, j))],
        out_specs=[out_col, out_col, out_col,
                   pl.BlockSpec((SC_CONV, cb), lambda j: (0, j)),
                   pl.BlockSpec((1, cb), lambda j: (0, j))],
        out_shape=[act, act, act, jax.ShapeDtypeStruct((SC_CONV, D_SC), F32),
                   jax.ShapeDtypeStruct((1, D_SC), F32)],
        compiler_params=_cparams(("parallel",)),
    )(proj, proj, proj, dycat, w, nw)


def _make_dot(hi, differentiable):
    def raw(a, b, mode):
        if hi:
            return lax.dot_general(a, b, _DN[mode], precision=lax.Precision.HIGHEST,
                                   preferred_element_type=F32)
        return lax.dot_general(a.astype(BF16), b.astype(BF16), _DN[mode], preferred_element_type=F32)

    if not differentiable:
        return raw

    @functools.partial(jax.custom_vjp, nondiff_argnums=(2,))
    def dot(a, b, mode):
        return raw(a, b, mode)

    def fwd(a, b, mode):
        return raw(a, b, mode), (a, b)

    def bwd(mode, res, g):
        a, b = res
        if mode == "nn":
            return raw(g, b, "nt"), raw(a, g, "tn")
        if mode == "nt":
            return raw(g, b, "nn"), raw(g, a, "tn")
        return raw(b, g, "nt"), raw(a, g, "nn")

    dot.defvjp(fwd, bwd)
    return dot


def _ssd_chunk(xs, bm, cm, dt_all, a_all, prev, tri, ex, eh, differentiable):
    _bdot = _make_dot(False, differentiable)
    _hdot = _make_dot(True, differentiable)
    dta = dt_all * a_all
    acum = _hdot(tri, dta, "nn")
    dtx = _hdot(dt_all, ex, "nn")
    acx = _hdot(acum, ex, "nn")
    atx = jnp.sum(_hdot(dta, ex, "nn"), axis=0, keepdims=True)
    xdt = xs * dtx
    mask = tri > 0.0
    scores = _bdot(cm, bm, "nt")
    lane = lax.broadcasted_iota(jnp.int32, (1, GROUP_W), 1) // HEAD_DIM
    y = _bdot(cm, prev, "nn") * jnp.exp(acx)
    for h in range(HEADS_PER_GROUP):
        m1 = _hdot(acum, eh[h], "nn")
        seg = m1 - m1.T
        decay = jnp.where(mask, jnp.exp(jnp.where(mask, seg, 0.0)), 0.0)
        xh = xdt * (lane == h).astype(F32)
        y = y + _bdot(scores * decay, xh, "nn")
    new = prev * jnp.exp(atx) + _bdot(bm, xdt * jnp.exp(atx - acx), "tn")
    return y, new


def _ssd_consts():
    q = CHUNK
    r = lax.broadcasted_iota(jnp.int32, (q, q), 0)
    c = lax.broadcasted_iota(jnp.int32, (q, q), 1)
    tri = jnp.stack([(c <= r), (c >= r)]).astype(F32)
    src = lax.broadcasted_iota(jnp.int32, (2, GROUPS, 128, GROUP_W), 2)
    d = lax.broadcasted_iota(jnp.int32, (2, GROUPS, 128, GROUP_W), 0)
    g = lax.broadcasted_iota(jnp.int32, (2, GROUPS, 128, GROUP_W), 1)
    col = lax.broadcasted_iota(jnp.int32, (2, GROUPS, 128, GROUP_W), 3)
    ex = (src == d * HEADS + g * HEADS_PER_GROUP + col // HEAD_DIM).astype(F32)
    shp = (2, GROUPS, HEADS_PER_GROUP, 128, 128)
    src = lax.broadcasted_iota(jnp.int32, shp, 3)
    d = lax.broadcasted_iota(jnp.int32, shp, 0)
    g = lax.broadcasted_iota(jnp.int32, shp, 1)
    h = lax.broadcasted_iota(jnp.int32, shp, 2)
    eh = (src == d * HEADS + g * HEADS_PER_GROUP + h).astype(F32)
    return tri, ex, eh


def _ssd_specs(nc, chunk_of):
    q = CHUNK
    xs = pl.BlockSpec((q, GROUP_W), lambda d, ci, g: (chunk_of(d, ci), g))
    bm = pl.BlockSpec((q, N_STATE), lambda d, ci, g: (chunk_of(d, ci), D_SSM // N_STATE + g))
    cm = pl.BlockSpec((q, N_STATE), lambda d, ci, g: (chunk_of(d, ci), D_SSM // N_STATE + GROUPS + g))
    dt = pl.BlockSpec((q, 128), lambda d, ci, g: (chunk_of(d, ci), 0))
    a = pl.BlockSpec((1, 128), lambda d, ci, g: (0, 0))
    tri = pl.BlockSpec((None, q, q), lambda d, ci, g: (d, 0, 0))
    ex = pl.BlockSpec((None, None, 128, GROUP_W), lambda d, ci, g: (d, g, 0, 0))
    eh = pl.BlockSpec((None, None, HEADS_PER_GROUP, 128, 128), lambda d, ci, g: (d, g, 0, 0, 0))
    st = pl.BlockSpec((None, None, None, N_STATE, GROUP_W), lambda d, ci, g: (d, chunk_of(d, ci), g, 0, 0))
    return xs, bm, cm, dt, a, tri, ex, eh, st


def _ssd_fwd(xbc, dt_all, a_all):
    s = xbc.shape[0]
    nc = s // CHUNK
    tri, ex, eh = _ssd_consts()

    def chunk_of(d, ci):
        return ci + d * (nc - 1 - 2 * ci)

    def body(xs_ref, b_ref, c_ref, dt_ref, a_ref, tri_ref, ex_ref, eh_ref, y_ref, st_ref, state):
        ci = pl.program_id(1)
        g = pl.program_id(2)

        @pl.when(ci == 0)
        def _():
            state[g] = jnp.zeros((N_STATE, GROUP_W), F32)

        prev = state[g]
        st_ref[...] = prev
        y, new = _ssd_chunk(xs_ref[...], b_ref[...], c_ref[...], dt_ref[...], a_ref[...], prev,
                            tri_ref[...], ex_ref[...], eh_ref[...], False)
        y_ref[...] = y
        state[g] = new

    xs, bm, cm, dt, a, tri_s, ex_s, eh_s, st = _ssd_specs(nc, chunk_of)
    return pl.pallas_call(
        body, name="ssd_fwd", grid=(2, nc, GROUPS),
        in_specs=[xs, bm, cm, dt, a, tri_s, ex_s, eh_s],
        out_specs=[pl.BlockSpec((None, CHUNK, GROUP_W), lambda d, ci, g: (d, chunk_of(d, ci), g)), st],
        out_shape=[jax.ShapeDtypeStruct((2, s, D_SSM), F32),
                   jax.ShapeDtypeStruct((2, nc, GROUPS, N_STATE, GROUP_W), F32)],
        scratch_shapes=[pltpu.VMEM((GROUPS, N_STATE, GROUP_W), F32)],
        compiler_params=_cparams(("arbitrary", "arbitrary", "arbitrary")),
    )(xbc, xbc, xbc, dt_all, a_all, tri, ex, eh)


def _ssd_bwd(xbc, dt_all, a_all, states, dy):
    s = xbc.shape[0]
    nc = s // CHUNK
    tri, ex, eh = _ssd_consts()

    def chunk_of(d, ci):
        return (nc - 1 - ci) + d * (2 * ci - (nc - 1))

    def body(xs_ref, b_ref, c_ref, dt_ref, a_ref, tri_ref, ex_ref, eh_ref, st_ref, dy_ref,
             dxs_ref, db_ref, dc_ref, ddt_ref, da_ref, dstate):
        ci = pl.program_id(1)
        g = pl.program_id(2)

        @pl.when(ci == 0)
        def _():
            dstate[g] = jnp.zeros((N_STATE, GROUP_W), F32)

        tri_v, ex_v, eh_v = tri_ref[...], ex_ref[...], eh_ref[...]

        def f(xs, bm, cm, dt, a, prev):
            return _ssd_chunk(xs, bm, cm, dt, a, prev, tri_v, ex_v, eh_v, True)

        _, vjp = jax.vjp(f, xs_ref[...], b_ref[...], c_ref[...], dt_ref[...], a_ref[...], st_ref[...])
        dxs, dbm, dcm, ddt, da, dprev = vjp((dy_ref[...], dstate[g]))
        dxs_ref[...] = dxs
        db_ref[...] = dbm
        dc_ref[...] = dcm
        dstate[g] = dprev

        @pl.when(g == 0)
        def _():
            ddt_ref[...] = ddt

        @pl.when(g > 0)
        def _():
            ddt_ref[...] += ddt

        @pl.when(jnp.logical_and(ci == 0, g == 0))
        def _():
            da_ref[...] = da

        @pl.when(jnp.logical_or(ci > 0, g > 0))
        def _():
            da_ref[...] += da

    xs, bm, cm, dt, a, tri_s, ex_s, eh_s, st = _ssd_specs(nc, chunk_of)
    dy_s = pl.BlockSpec((CHUNK, GROUP_W), lambda d, ci, g: (chunk_of(d, ci), g))
    return pl.pallas_call(
        body, name="ssd_bwd", grid=(2, nc, GROUPS),
        in_specs=[xs, bm, cm, dt, a, tri_s, ex_s, eh_s, st, dy_s],
        out_specs=[pl.BlockSpec((None, CHUNK, GROUP_W), lambda d, ci, g: (d, chunk_of(d, ci), g)),
                   pl.BlockSpec((None, CHUNK, N_STATE), lambda d, ci, g: (d, chunk_of(d, ci), g)),
                   pl.BlockSpec((None, CHUNK, N_STATE), lambda d, ci, g: (d, chunk_of(d, ci), g)),
                   pl.BlockSpec((None, CHUNK, 128), lambda d, ci, g: (d, chunk_of(d, ci), 0)),
                   pl.BlockSpec((None, 1, 128), lambda d, ci, g: (d, 0, 0))],
        out_shape=[jax.ShapeDtypeStruct((2, s, D_SSM), F32),
                   jax.ShapeDtypeStruct((2, s, GROUPS * N_STATE), F32),
                   jax.ShapeDtypeStruct((2, s, GROUPS * N_STATE), F32),
                   jax.ShapeDtypeStruct((2, s, 128), F32),
                   jax.ShapeDtypeStruct((2, 1, 128), F32)],
        scratch_shapes=[pltpu.VMEM((GROUPS, N_STATE, GROUP_W), F32)],
        compiler_params=_cparams(("arbitrary", "arbitrary", "arbitrary")),
    )(xbc, xbc, xbc, dt_all, a_all, tri, ex, eh, states, dy)


def _softplus(v):
    return jnp.maximum(v, 0.0) + jnp.log(1.0 + jnp.exp(-jnp.abs(v)))


def _dt_fwd(proj_dt, bias_all):
    s = proj_dt.shape[0]
    t = _tile(s, 1024)

    def body(u_ref, b_ref, o_ref):
        o_ref[...] = _softplus(u_ref[...] + b_ref[...])

    return pl.pallas_call(
        body, name="dt_fwd", grid=(s // t,),
        in_specs=[_row_spec(t, 128), _full_spec((1, 128))],
        out_specs=_row_spec(t, 128),
        out_shape=jax.ShapeDtypeStruct((s, 128), F32),
        compiler_params=_cparams(("parallel",)),
    )(proj_dt, bias_all)


def _dt_bwd(proj_dt, bias_all, ddt):
    s = proj_dt.shape[0]
    t = _tile(s, 1024)

    def body(u_ref, b_ref, ddt_ref, du_ref, db_ref):
        d = (ddt_ref[0] + ddt_ref[1]) * jax.nn.sigmoid(u_ref[...] + b_ref[...])
        du_ref[...] = d.astype(BF16)
        _acc_rows(db_ref, d, pl.program_id(0) == 0)

    return pl.pallas_call(
        body, name="dt_bwd", grid=(s // t,),
        in_specs=[_row_spec(t, 128), _full_spec((1, 128)), pl.BlockSpec((2, t, 128), lambda i: (0, i, 0))],
        out_specs=[_row_spec(t, 128), _full_spec((1, 128))],
        out_shape=[jax.ShapeDtypeStruct((s, 128), BF16), jax.ShapeDtypeStruct((1, 128), F32)],
        compiler_params=_cparams(("arbitrary",)),
    )(proj_dt, bias_all, ddt)


def _ssd_gate_fwd(y2, xbc, proj, dx, nw):
    s = xbc.shape[0]
    t = _tile(s, 512)

    def body(y_ref, xs_ref, z_ref, dx_ref, nw_ref, o_ref):
        y = (y_ref[0] + y_ref[1] + dx_ref[...] * xs_ref[...]) * _silu(z_ref[...])
        r = lax.rsqrt(jnp.mean(y * y, axis=-1, keepdims=True) + RMS_EPS)
        o_ref[...] = (y * r * nw_ref[...]).astype(BF16)

    blk = pl.BlockSpec((t, GROUP_W), lambda j, i: (i, j))
    vec = pl.BlockSpec((1, GROUP_W), lambda j, i: (0, j))
    return pl.pallas_call(
        body, name="ssd_gate_fwd", grid=(GROUPS, s // t),
        in_specs=[pl.BlockSpec((2, t, GROUP_W), lambda j, i: (0, i, j)), blk, blk, vec, vec],
        out_specs=blk,
        out_shape=jax.ShapeDtypeStruct((s, D_SSM), BF16),
        compiler_params=_cparams(("parallel", "parallel")),
    )(y2, xbc, proj, dx, nw)


def _ssd_gate_bwd(y2, xbc, proj, dycat, dx, nw):
    s = xbc.shape[0]
    t = _tile(s, 512)

    def body(y_ref, xs_ref, z_ref, do_ref, dx_ref, nw_ref, dyc_ref, dz_ref, dd_ref, dnw_ref):
        first = pl.program_id(1) == 0
        z = z_ref[...]
        xs = xs_ref[...]
        sig = jax.nn.sigmoid(z)
        gate = z * sig
        yc = y_ref[0] + y_ref[1] + dx_ref[...] * xs
        y = yc * gate
        r = lax.rsqrt(jnp.mean(y * y, axis=-1, keepdims=True) + RMS_EPS)
        do = do_ref[...]
        dyr = do * nw_ref[...]
        dy = r * dyr - y * (r * r * r) * jnp.mean(dyr * y, axis=-1, keepdims=True)
        dyc = dy * gate
        dyc_ref[...] = dyc
        dz_ref[...] = (dy * yc * (sig * (1.0 + z * (1.0 - sig)))).astype(BF16)
        _acc_rows(dd_ref, dyc * xs, first)
        _acc_rows(dnw_ref, do * y * r, first)

    blk = pl.BlockSpec((t, GROUP_W), lambda j, i: (i, j))
    vec = pl.BlockSpec((1, GROUP_W), lambda j, i: (0, j))
    return pl.pallas_call(
        body, name="ssd_gate_bwd", grid=(GROUPS, s // t),
        in_specs=[pl.BlockSpec((2, t, GROUP_W), lambda j, i: (0, i, j)), blk, blk, blk, vec, vec],
        out_specs=[blk, blk, vec, vec],
        out_shape=[jax.ShapeDtypeStruct((s, D_SSM), F32), jax.ShapeDtypeStruct((s, D_SSM), BF16),
                   jax.ShapeDtypeStruct((1, D_SSM), F32), jax.ShapeDtypeStruct((1, D_SSM), F32)],
        compiler_params=_cparams(("parallel", "arbitrary")),
    )(y2, xbc, proj, dycat, dx, nw)


def _ada_fwd(c16, w_ada):
    k, n = w_ada.shape
    tn = 512

    def body(c_ref, w_ref, o_ref):
        a = _silu(c_ref[...]).astype(BF16)
        o_ref[...] = jnp.dot(a, w_ref[...].astype(BF16), preferred_element_type=F32)

    return pl.pallas_call(
        body, name="ada_fwd", grid=(n // tn,),
        in_specs=[_full_spec((16, k)), pl.BlockSpec((k, tn), lambda j: (0, j))],
        out_specs=pl.BlockSpec((16, tn), lambda j: (0, j)),
        out_shape=jax.ShapeDtypeStruct((16, n), F32),
        compiler_params=_cparams(("parallel",)),
    )(c16, w_ada)


def _ada_bwd_adamw(c16, dmod16, w, m, v):
    k, n = w.shape
    tm, tn = 512, 1024
    blk = pl.BlockSpec((tm, tn), lambda i, j: (i, j))

    def body(c_ref, d_ref, w_ref, m_ref, v_ref, g_out, d_out, m_out, v_out):
        a = _silu(c_ref[...]).astype(BF16)
        g = lax.dot_general(a, d_ref[...].astype(BF16), _DN["tn"], preferred_element_type=F32)
        d, mn, vn = _adamw_math(w_ref[...], g, m_ref[...], v_ref[...])
        g_out[...] = g
        d_out[...] = d
        m_out[...] = mn
        v_out[...] = vn

    shp = jax.ShapeDtypeStruct((k, n), F32)
    return pl.pallas_call(
        body, name="ada_bwd_adamw", grid=(k // tm, n // tn),
        in_specs=[pl.BlockSpec((16, tm), lambda i, j: (0, i)), pl.BlockSpec((16, tn), lambda i, j: (0, j)),
                  blk, blk, blk],
        out_specs=[blk, blk, blk, blk],
        out_shape=[shp, shp, shp, shp],
        compiler_params=_cparams(("parallel", "parallel")),
    )(c16, dmod16, w, m, v)


def _sum8_adamw(gathered, w, m, v):
    n = w.shape[1]
    tn = _tile(n, 8192)
    vec = pl.BlockSpec((1, tn), lambda j: (0, j))

    def body(g8_ref, w_ref, m_ref, v_ref, g_out, d_out, m_out, v_out):
        g = g8_ref[0:1, :]
        for k in range(1, N_DEV):
            g = g + g8_ref[k:k + 1, :]
        d, mn, vn = _adamw_math(w_ref[...], g, m_ref[...], v_ref[...])
        g_out[...] = g
        d_out[...] = d
        m_out[...] = mn
        v_out[...] = vn

    shp = jax.ShapeDtypeStruct((1, n), F32)
    return pl.pallas_call(
        body, name="sum8_adamw", grid=(n // tn,),
        in_specs=[pl.BlockSpec((N_DEV, tn), lambda j: (0, j)), vec, vec, vec],
        out_specs=[vec, vec, vec, vec],
        out_shape=[shp, shp, shp, shp],
        compiler_params=_cparams(("parallel",)),
    )(gathered, w, m, v)


def _gather_vec(name, v):
    n = v.shape[1]
    out = _allgather(name, v.reshape(8, n // 8), in_vmem=True)
    return out.reshape(N_DEV, n)


def _pad_lanes(v, n):
    return jnp.pad(v, ((0, 0), (0, n - v.shape[1])))


def _local_step(x, tgt, mod, w_in_main, w_in_dt, w_in_g, w_out_g, w_up_g, w_down_g,
                conv_w, conv_b, dt_bias_f, dt_bias_b, a_log_f, a_log_b, ssm_d, ssm_nw,
                sc_w, sc_nw, ln1_g, ln1_b, ln2_g, ln2_b):
    s = x.shape[0]
    mod6 = mod.reshape(N_MOD, D_MODEL)
    bias_all = _pad_lanes(jnp.concatenate([dt_bias_f, dt_bias_b], axis=1), 128)
    a_all = _pad_lanes(-jnp.exp(jnp.concatenate([a_log_f, a_log_b], axis=1)), 128)
    d_lanes = jnp.repeat(ssm_d, HEAD_DIM, axis=1)

    h1 = _modulate("mod1", x, mod6)
    proj = _mm_nn("in_proj", h1, w_in_main, F32)[0]
    proj_dt = _mm_nn("in_proj_dt", h1, w_in_dt, F32)[0]
    xbc = _conv_silu_fwd(proj, conv_w, conv_b)
    dt_all = _dt_fwd(proj_dt, bias_all)
    y2, states = _ssd_fwd(xbc, dt_all, a_all)
    y_ssm = _ssd_gate_fwd(y2, xbc, proj, d_lanes, ssm_nw)
    y_sc = _sc_fwd(proj, sc_w, sc_nw)
    ycat = jnp.concatenate([y_ssm, y_sc], axis=1)
    mix = _mm_nn("out_proj", ycat, w_out_g, F32)[0]
    x1, h2 = _ln1_fwd(x, mix, mod6, ln1_g, ln1_b)

    def relu2(acc):
        u = acc.astype(BF16)
        r = jnp.maximum(acc, 0.0)
        return u, r * r

    w_up3 = w_up_g
    nper = w_up3.shape[2]
    tm = _tile(s, 1024)
    tn = 1024
    nb = nper // tn
    u_spec = pl.BlockSpec((tm, tn), lambda i, j, kk: (i, j))
    u, ff = _matmul(
        "up_proj", h2, w_up3, mode="nn", grid=(s // tm, D_FF // tn, 1),
        a_spec=pl.BlockSpec((tm, D_MODEL), lambda i, j, kk: (i, 0)),
        b_spec=pl.BlockSpec((None, D_MODEL, tn), lambda i, j, kk: (j // nb, 0, j % nb)),
        out_shapes=[jax.ShapeDtypeStruct((s, D_FF), BF16)] * 2, out_specs=[u_spec, u_spec],
        acc_shape=(tm, tn), epilogue=relu2)
    f2 = _mm_nn("down_proj", ff, w_down_g, F32)[0]
    df2, dx1a, loss, g_ln2_g, g_ln2_b, dgate2 = _ln2_loss_bwd(x1, f2, tgt, mod6, ln2_g, ln2_b)

    def relu_grad(acc, uu):
        return (acc * (2.0 * jnp.maximum(uu.astype(F32), 0.0)),)

    du = _mm_nt("d_ff", df2, w_down_g, BF16, epilogue=relu_grad, extras=(u,))[0]
    g_down = _mm_tn("g_w_down", ff, df2, BF16)[0].reshape(N_DEV, D_FF // N_DEV, D_MODEL)
    g_up = _matmul(
        "g_w_up", h2, du, mode="tn", grid=(D_MODEL // 1024, D_FF // tn, s // _tile(s, 2048)),
        a_spec=pl.BlockSpec((_tile(s, 2048), 1024), lambda i, j, kk: (kk, i)),
        b_spec=pl.BlockSpec((_tile(s, 2048), tn), lambda i, j, kk: (kk, j)),
        out_shapes=[jax.ShapeDtypeStruct((N_DEV, D_MODEL, nper), BF16)],
        out_specs=[pl.BlockSpec((None, 1024, tn), lambda i, j, kk: (j // nb, i, j % nb))],
        acc_shape=(1024, tn))[0]
    dh2 = _matmul(
        "d_h2", du, w_up3, mode="nt", grid=(s // tm, D_MODEL // 1024, D_FF // nper),
        a_spec=pl.BlockSpec((tm, nper), lambda i, j, kk: (i, kk)),
        b_spec=pl.BlockSpec((None, 1024, nper), lambda i, j, kk: (kk, j, 0)),
        out_shapes=[jax.ShapeDtypeStruct((s, D_MODEL), F32)],
        out_specs=[pl.BlockSpec((tm, 1024), lambda i, j, kk: (i, j))],
        acc_shape=(tm, 1024))[0]
    dmix, dxa, dscale2, dshift2, g_ln1_g, g_ln1_b, dgate1 = _ln1_bwd(dh2, dx1a, x1, x, mix, mod6, ln1_g)

    dycat = _mm_nt("d_ycat", dmix, w_out_g, F32)[0]
    g_out = _mm_tn("g_w_out", ycat, dmix, BF16)[0].reshape(N_DEV, D_MODEL // N_DEV, D_MODEL)
    duh, dub, duc, g_sc_w, g_sc_nw = _sc_bwd(proj, dycat, sc_w, sc_nw)
    dyc, dz, dd_lanes, g_ssm_nw = _ssd_gate_bwd(y2, xbc, proj, dycat, d_lanes, ssm_nw)
    dxs2, db2, dc2, ddt2, da2 = _ssd_bwd(xbc, dt_all, a_all, states, dyc)
    n_bc = GROUPS * N_STATE
    du_xs, gw_xs, gb_xs = _conv_silu_bwd("conv_bwd_x", proj, conv_w, conv_b, 0, D_SSM, [dxs2],
                                         scaled=(dyc, d_lanes))
    du_b, gw_b, gb_b = _conv_silu_bwd("conv_bwd_b", proj, conv_w, conv_b, D_SSM, n_bc, [db2])
    du_c, gw_c, gb_c = _conv_silu_bwd("conv_bwd_c", proj, conv_w, conv_b, D_SSM + n_bc, n_bc, [dc2])
    du_dt, g_bias_all = _dt_bwd(proj_dt, bias_all, ddt2)

    dproj = jnp.concatenate([dz, du_xs, du_b, du_c, du_dt[:, :2 * HEADS], duh, dub, duc], axis=1)
    dproj3 = dproj.reshape(s, N_DEV, D_IN_SHARD).transpose(1, 0, 2)
    tk = _tile(s, 2048)
    g_in = _matmul(
        "g_w_in", h1, dproj3, mode="tn", grid=(N_DEV, D_MODEL // 1024, s // tk),
        a_spec=pl.BlockSpec((tk, 1024), lambda i, j, kk: (kk, j)),
        b_spec=pl.BlockSpec((None, tk, D_IN_SHARD), lambda i, j, kk: (i, kk, 0)),
        out_shapes=[jax.ShapeDtypeStruct((N_DEV, D_MODEL, D_IN_SHARD), BF16)],
        out_specs=[pl.BlockSpec((None, 1024, D_IN_SHARD), lambda i, j, kk: (i, j, 0))],
        acc_shape=(1024, D_IN_SHARD))[0]
    dh1 = _matmul(
        "d_h1", dproj3, w_in_g, mode="nt", grid=(s // tm, D_MODEL // 1024, N_DEV),
        a_spec=pl.BlockSpec((None, tm, D_IN_SHARD), lambda i, j, kk: (kk, i, 0)),
        b_spec=pl.BlockSpec((None, 1024, D_IN_SHARD), lambda i, j, kk: (kk, j, 0)),
        out_shapes=[jax.ShapeDtypeStruct((s, D_MODEL), F32)],
        out_specs=[pl.BlockSpec((tm, 1024), lambda i, j, kk: (i, j))],
        acc_shape=(tm, 1024))[0]
    grad_x, dscale1, dshift1 = _grad_x(dxa, dh1, x, mod6)

    dmod = jnp.concatenate([dshift1, dscale1, dgate1, dshift2, dscale2, dgate2], axis=1)
    g_a_all = da2[0] + da2[1]
    small = {
        "dmod": dmod,
        "ssm_conv_w": jnp.concatenate([gw_xs, gw_b, gw_c], axis=1),
        "ssm_conv_b": jnp.concatenate([gb_xs, gb_b, gb_c], axis=1),
        "ssm_dt_bias_f": g_bias_all[:, :HEADS],
        "ssm_dt_bias_b": g_bias_all[:, HEADS:2 * HEADS],
        "ssm_a_log_f": (g_a_all * a_all)[:, :HEADS],
        "ssm_a_log_b": (g_a_all * a_all)[:, HEADS:2 * HEADS],
        "ssm_d": dd_lanes.reshape(HEADS, HEAD_DIM).sum(axis=1).reshape(1, HEADS),
        "ssm_norm_w": g_ssm_nw,
        "sc_conv_w": g_sc_w,
        "sc_norm_w": g_sc_nw,
        "ln1_g": g_ln1_g, "ln1_b": g_ln1_b, "ln2_g": g_ln2_g, "ln2_b": g_ln2_b,
    }
    return loss, grad_x, (g_in, g_out, g_up, g_down), small


_SUMMED = [("ssm_conv_b", D_XBC), ("ssm_dt_bias_f", HEADS), ("ssm_dt_bias_b", HEADS),
           ("ssm_a_log_f", HEADS), ("ssm_a_log_b", HEADS), ("ssm_d", HEADS),
           ("ssm_norm_w", D_SSM), ("sc_norm_w", D_SC),
           ("ln1_g", D_MODEL), ("ln1_b", D_MODEL), ("ln2_g", D_MODEL), ("ln2_b", D_MODEL)]


def _round_up(n, k):
    return (n + k - 1) // k * k


def _w_in_sections(w_in_g):
    w = w_in_g.transpose(1, 0, 2).reshape(D_MODEL, D_IN)
    dt_lo = D_SSM + D_XBC
    main = jnp.concatenate([w[:, :dt_lo], w[:, dt_lo + 2 * HEADS:]], axis=1)
    dt = _pad_lanes(w[:, dt_lo:dt_lo + 2 * HEADS], 128)
    return main, dt


def kernel(x, c, w_ada, b_ada, w_in, ssm_conv_w, ssm_conv_b, ssm_dt_bias_f, ssm_dt_bias_b, ssm_a_log_f, ssm_a_log_b, ssm_d, ssm_norm_w, sc_conv_w, sc_norm_w, w_out, ln1_g, ln1_b, w_up, w_down, ln2_g, ln2_b, loss_target, m_w_ada, m_b_ada, m_w_in, m_ssm_conv_w, m_ssm_conv_b, m_ssm_dt_bias_f, m_ssm_dt_bias_b, m_ssm_a_log_f, m_ssm_a_log_b, m_ssm_d, m_ssm_norm_w, m_sc_conv_w, m_sc_norm_w, m_w_out, m_ln1_g, m_ln1_b, m_w_up, m_w_down, m_ln2_g, m_ln2_b, v_w_ada, v_b_ada, v_w_in, v_ssm_conv_w, v_ssm_conv_b, v_ssm_dt_bias_f, v_ssm_dt_bias_b, v_ssm_a_log_f, v_ssm_a_log_b, v_ssm_d, v_ssm_norm_w, v_sc_conv_w, v_sc_norm_w, v_w_out, v_ln1_g, v_ln1_b, v_w_up, v_w_down, v_ln2_g, v_ln2_b):
    args = dict(locals())
    xi, yi, ci = _my_pos()
    me = 4 * xi + 2 * yi + ci
    pos = jnp.stack([xi, yi, ci]).astype(jnp.int32)
    s = x.shape[1]

    n_cw, n_sw = SSM_CONV * D_XBC // N_DEV, SC_CONV * D_SC // N_DEV
    vec = jnp.concatenate([c, ssm_conv_w[0].reshape(1, n_cw), sc_conv_w[0].reshape(1, n_sw)], axis=1)
    vec = _pad_lanes(vec, 8192)
    gath = _gather_vec("gather_c_conv", vec)
    c_all = gath[:, :D_MODEL]
    conv_w = gath[:, D_MODEL:D_MODEL + n_cw].reshape(N_DEV, SSM_CONV, D_XBC // N_DEV)
    conv_w = conv_w.transpose(1, 0, 2).reshape(SSM_CONV, D_XBC)
    sc_w = gath[:, D_MODEL + n_cw:D_MODEL + n_cw + n_sw].reshape(N_DEV, SC_CONV, D_SC // N_DEV)
    sc_w = sc_w.transpose(1, 0, 2).reshape(SC_CONV, D_SC)
    c16 = jnp.pad(c_all, ((0, 8), (0, 0)))

    n_ada = w_ada.shape[2]
    mod_cols = _ada_fwd(c16, w_ada[0])[:N_DEV]
    mod_all = _allgather("gather_mod", mod_cols, in_vmem=True)
    mod = lax.dynamic_index_in_dim(mod_all, me, axis=1, keepdims=False)
    mod = mod.reshape(1, N_MOD * D_MODEL) + b_ada

    w_in_g = _allgather("gather_w_in", _cast_bf16("cast_w_in", w_in[0]), in_vmem=False)
    w_out_g = _allgather("gather_w_out", _cast_bf16("cast_w_out", w_out[0]), in_vmem=False)
    w_up_g = _allgather("gather_w_up", _cast_bf16("cast_w_up", w_up[0]), in_vmem=False)
    w_down_g = _allgather("gather_w_down", _cast_bf16("cast_w_down", w_down[0]), in_vmem=False)
    w_in_main, w_in_dt = _w_in_sections(w_in_g)

    loss, grad_x, (g_in, g_out, g_up, g_down), small = _local_step(
        x[0], loss_target[0], mod, w_in_main, w_in_dt, w_in_g,
        w_out_g.reshape(D_MODEL, D_MODEL), w_up_g, w_down_g.reshape(D_FF, D_MODEL),
        conv_w, ssm_conv_b, ssm_dt_bias_f, ssm_dt_bias_b, ssm_a_log_f, ssm_a_log_b, ssm_d,
        ssm_norm_w, sc_w, sc_norm_w, ln1_g, ln1_b, ln2_g, ln2_b)

    out = {}

    for tag, gfull, name in (("down", g_down, "w_down"), ("up", g_up, "w_up"),
                             ("out", g_out, "w_out"), ("in", g_in, "w_in")):
        g, d, mn, vn = _reduce_scatter_adamw(tag, gfull, pos, args[name][0], args["m_" + name][0],
                                             args["v_" + name][0])
        out[name] = (g[None], d[None], mn[None], vn[None])

    parts = [small["dmod"]]
    parts += [_pad_lanes(small[n], _round_up(w, 128)) for n, w in _SUMMED]
    parts += [small["ssm_conv_w"].reshape(1, SSM_CONV * D_XBC), small["sc_conv_w"].reshape(1, SC_CONV * D_SC)]
    parts += [loss]
    gvec = jnp.concatenate(parts, axis=1)
    n_vec = _round_up(gvec.shape[1], 8192)
    gall = _gather_vec("gather_small_grads", _pad_lanes(gvec, n_vec))

    def shard_cols(full, k, per):
        return lax.dynamic_slice_in_dim(full.reshape(k, N_DEV, per), me, 1, axis=1).reshape(1, k * per)

    def placed(vals, n_rows=1):
        return jnp.concatenate(vals, axis=1)

    n_mod = N_MOD * D_MODEL
    ws, ms, vs = [b_ada], [m_b_ada], [v_b_ada]
    for n, w in _SUMMED:
        pw = _round_up(w, 128)
        ws.append(_pad_lanes(args[n], pw))
        ms.append(_pad_lanes(args["m_" + n], pw))
        vs.append(_pad_lanes(args["v_" + n], pw))

    def full_rows(shard, k, per):
        z = jnp.zeros((k, N_DEV, per), F32)
        z = lax.dynamic_update_slice_in_dim(z, shard.reshape(k, 1, per), me, axis=1)
        return z.reshape(1, k * N_DEV * per)

    for nm, k, per in (("ssm_conv_w", SSM_CONV, D_XBC // N_DEV), ("sc_conv_w", SC_CONV, D_SC // N_DEV)):
        ws.append(full_rows(args[nm][0], k, per))
        ms.append(full_rows(args["m_" + nm][0], k, per))
        vs.append(full_rows(args["v_" + nm][0], k, per))
    tail = n_vec - sum(a.shape[1] for a in ws)
    ws.append(jnp.zeros((1, tail), F32))
    ms.append(jnp.zeros((1, tail), F32))
    vs.append(jnp.ones((1, tail), F32))
    g_s, d_s, m_s, v_s = _sum8_adamw(gall, placed(ws), placed(ms), placed(vs))

    off = 0

    def take(w):
        nonlocal off
        sl = tuple(a[:, off:off + w] for a in (g_s, d_s, m_s, v_s))
        off += _round_up(w, 128)
        return sl

    out["b_ada"] = take(n_mod)
    for n, w in _SUMMED:
        out[n] = take(w)
    for nm, k, per in (("ssm_conv_w", SSM_CONV, D_XBC // N_DEV), ("sc_conv_w", SC_CONV, D_SC // N_DEV)):
        full = take(k * N_DEV * per)
        out[nm] = tuple(shard_cols(a, k, per).reshape(1, k, per) for a in full)
    loss_total = g_s[0, off]

    dmod_all = gall[:, :n_mod]
    dmod_cols = lax.dynamic_slice_in_dim(dmod_all.reshape(N_DEV, N_DEV, n_ada), me, 1, axis=1)
    dmod16 = jnp.pad(dmod_cols.reshape(N_DEV, n_ada), ((0, 8), (0, 0)))
    out["w_ada"] = tuple(a[None] for a in _ada_bwd_adamw(c16, dmod16, w_ada[0], m_w_ada[0], v_w_ada[0]))

    names = ['w_ada', 'b_ada', 'w_in', 'ssm_conv_w', 'ssm_conv_b', 'ssm_dt_bias_f', 'ssm_dt_bias_b',
             'ssm_a_log_f', 'ssm_a_log_b', 'ssm_d', 'ssm_norm_w', 'sc_conv_w', 'sc_norm_w', 'w_out',
             'ln1_g', 'ln1_b', 'w_up', 'w_down', 'ln2_g', 'ln2_b']
    res = [loss_total, grad_x[None]]
    for k in range(4):
        res += [out[n][k] for n in names]
    return tuple(res)
```

```python
import functools

import jax
import jax.numpy as jnp
from jax import lax
from jax.experimental import pallas as pl
from jax.experimental.pallas import tpu as pltpu

F32 = jnp.float32
BF16 = jnp.bfloat16
MESH = pl.DeviceIdType.MESH

N_DEV = 8
D_MODEL = 4096
D_SSM = 2048
D_SC = 2048
HEADS = 32
HEAD_DIM = 64
GROUPS = 8
GROUP_W = D_SSM // GROUPS
HEADS_PER_GROUP = 4
N_STATE = 128
CHUNK = 128
SSM_CONV = 5
SC_CONV = 3
SC_GROUP_W = 128
D_XBC = 4096
D_FF = 16384
D_IN = 12352
D_IN_SHARD = D_IN // N_DEV
D_MAIN = 12288
N_MOD = 6
ALPHA = (2 * 1) ** 0.25
LN_EPS = 1e-5
RMS_EPS = 1e-5
ADAM_LR = 0.001
ADAM_B1 = 0.9
ADAM_B2 = 0.999
ADAM_EPS = 1e-08
ADAM_WD = 0.01
ADAM_STEP = 10

VMEM_LIMIT = 56 * 1024 * 1024
HALO = 8

_DN = {
    "nn": (((1,), (0,)), ((), ())),
    "nt": (((1,), (1,)), ((), ())),
    "tn": (((0,), (0,)), ((), ())),
}


def _cparams(sem=None):
    return pltpu.CompilerParams(dimension_semantics=sem, vmem_limit_bytes=VMEM_LIMIT)


def _my_pos():
    return lax.axis_index("x"), lax.axis_index("y"), lax.axis_index("c")


def _other_chips(x, y):
    return [(1 - x, y), (x, 1 - y), (1 - x, 1 - y)]


class _GatherJob:
    n_remote = 7

    def __init__(self, shard, space=pl.ANY):
        self.ins = (shard,)
        self.out_shapes = (jax.ShapeDtypeStruct((N_DEV,) + shard.shape, shard.dtype),)
        self.space = space

    def _parts(self, ins, outs, send, recv, local):
        x_ref, out_ref = ins[0], outs[0]
        x, y, c = _my_pos()
        me, sibling = (x, y, c), (x, y, 1 - c)
        chips = _other_chips(x, y)

        def slab(px, py, pc):
            return out_ref.at[4 * px + 2 * py + pc]

        def copy(k, block, to, src=None):
            return pltpu.make_async_remote_copy(
                src_ref=slab(*block) if src is None else src, dst_ref=slab(*block),
                send_sem=send.at[k], recv_sem=recv.at[k], device_id=to, device_id_type=MESH)

        mine = pltpu.make_async_copy(x_ref, slab(*me), local.at[0])
        first = [copy(0, me, sibling, src=x_ref)]
        first += [copy(1 + j, me, (*chip, c), src=x_ref) for j, chip in enumerate(chips)]
        passed = [copy(4 + j, (*chip, c), sibling) for j, chip in enumerate(chips)]
        landed = [copy(1 + j, (*chip, c), me) for j, chip in enumerate(chips)]
        handed = [copy(0, sibling, me)] + [copy(4 + j, (*chip, 1 - c), me) for j, chip in enumerate(chips)]
        return mine, first, passed, landed, handed

    def start(self, *refs):
        mine, first, _, _, _ = self._parts(*refs)
        mine.start()
        for cp in first:
            cp.start()

    def mid(self, *refs):
        _, _, passed, landed, _ = self._parts(*refs)
        for arrived, onward in zip(landed, passed):
            arrived.wait_recv()
            onward.start()

    def finish(self, *refs):
        mine, first, passed, _, handed = self._parts(*refs)
        for cp in handed:
            cp.wait_recv()
        for cp in first + passed:
            cp.wait_send()
        mine.wait()


class _SiblingJob:
    n_remote = 4
    space = pl.ANY

    def __init__(self, g):
        self.ins = (g,)
        self.out_shapes = (jax.ShapeDtypeStruct((4,) + g.shape[1:], g.dtype),)

    def _copies(self, ins, outs, send, recv, local):
        x, y, c = _my_pos()
        return [pltpu.make_async_remote_copy(
            src_ref=ins[0].at[2 * j + (1 - c)], dst_ref=outs[0].at[j],
            send_sem=send.at[j], recv_sem=recv.at[j],
            device_id=(x, y, 1 - c), device_id_type=MESH) for j in range(4)]

    def start(self, *refs):
        for cp in self._copies(*refs):
            cp.start()

    def mid(self, *refs):
        pass

    def finish(self, *refs):
        for cp in self._copies(*refs):
            cp.wait()


class _ChipsJob:
    n_remote = 3
    space = pl.ANY

    def __init__(self, p):
        self.ins = (p,)
        self.out_shapes = (jax.ShapeDtypeStruct((3,) + p.shape[1:], p.dtype),)

    def _copies(self, ins, outs, send, recv, local):
        x, y, c = _my_pos()
        return [pltpu.make_async_remote_copy(
            src_ref=ins[0].at[2 * px + py], dst_ref=outs[0].at[k],
            send_sem=send.at[k], recv_sem=recv.at[k],
            device_id=(px, py, c), device_id_type=MESH) for k, (px, py) in enumerate(_other_chips(x, y))]

    def start(self, *refs):
        for cp in self._copies(*refs):
            cp.start()

    def mid(self, *refs):
        pass

    def finish(self, *refs):
        for cp in self._copies(*refs):
            cp.wait()


MID_STEP_FRACTION = 0.7


def _call(name, body, *, grid, in_specs, out_specs, out_shape, args, scratch_shapes=(), sem=None,
          jobs=(), n_prefetch=0):
    out_shape, out_specs, in_specs = list(out_shape), list(out_specs), list(in_specs)
    scratch_shapes = list(scratch_shapes)
    jobs = list(jobs)
    n_in, n_out, n_scr = len(in_specs), len(out_shape), len(scratch_shapes)
    job_ins = [a for j in jobs for a in j.ins]
    job_outs = [o for j in jobs for o in j.out_shapes]
    steps = 1
    for n in grid:
        steps *= n
    mid_step = min(steps - 1, int(steps * MID_STEP_FRACTION))

    def wrapped(*refs):
        pre, refs = refs[:n_prefetch], refs[n_prefetch:]
        core_in, refs = refs[:n_in], refs[n_in:]
        jin, refs = refs[:len(job_ins)], refs[len(job_ins):]
        core_out, refs = refs[:n_out], refs[n_out:]
        jout, refs = refs[:len(job_outs)], refs[len(job_outs):]
        core_scr, sems = refs[:n_scr], refs[n_scr:]
        lin = 0
        for ax, n in enumerate(grid):
            lin = lin * n + pl.program_id(ax)
        bound = []
        for j in jobs:
            ji, jin = jin[:len(j.ins)], jin[len(j.ins):]
            jo, jout = jout[:len(j.out_shapes)], jout[len(j.out_shapes):]
            (send, recv, local), sems = sems[:3], sems[3:]
            bound.append((j, (ji, jo, send, recv, local)))

        if jobs:
            @pl.when(lin == 0)
            def _():
                for j, r in bound:
                    j.start(*r)

        body(*pre, *core_in, *core_out, *core_scr)

        if jobs:
            @pl.when(lin == mid_step)
            def _():
                for j, r in bound:
                    j.mid(*r)

            @pl.when(lin == steps - 1)
            def _():
                for j, r in bound:
                    j.finish(*r)

    sem_shapes = []
    for j in jobs:
        sem_shapes += [pltpu.SemaphoreType.DMA((j.n_remote,)), pltpu.SemaphoreType.DMA((j.n_remote,)),
                       pltpu.SemaphoreType.DMA((1,))]
    if jobs:
        sem = tuple("arbitrary" for _ in grid)
    res = pl.pallas_call(
        wrapped, name=name,
        grid_spec=pltpu.PrefetchScalarGridSpec(
            num_scalar_prefetch=n_prefetch, grid=tuple(grid),
            in_specs=in_specs + [pl.BlockSpec(memory_space=j.space) for j in jobs for _ in j.ins],
            out_specs=out_specs + [pl.BlockSpec(memory_space=j.space) for j in jobs for _ in j.out_shapes],
            scratch_shapes=scratch_shapes + sem_shapes),
        out_shape=out_shape + job_outs,
        compiler_params=_cparams(sem),
    )(*args, *job_ins)
    res = list(res) if isinstance(res, (list, tuple)) else [res]
    return res[:n_out], res[n_out:]


def _run_jobs(name, jobs):
    return _call(name, lambda: None, grid=(1,), in_specs=[], out_specs=[], out_shape=[], args=(),
                 jobs=jobs)[1]


def _matmul(name, a, b, *, mode, grid, a_spec, b_spec, out_shapes, out_specs, acc_shape,
            epilogue=None, extras=(), extra_specs=(), jobs=()):
    nk = grid[2]
    n_extra = len(extras)
    n_out = len(out_shapes)

    def body(*refs):
        a_ref, b_ref = refs[0], refs[1]
        extra_refs = refs[2:2 + n_extra]
        out_refs = refs[2 + n_extra:2 + n_extra + n_out]
        part = lax.dot_general(a_ref[...], b_ref[...], _DN[mode], preferred_element_type=F32)

        def finish(acc):
            outs = epilogue(acc, *[r[...] for r in extra_refs]) if epilogue else (acc,)
            for o_ref, o in zip(out_refs, outs):
                o_ref[...] = o.astype(o_ref.dtype)

        if nk == 1:
            finish(part)
        else:
            acc_ref = refs[-1]
            k = pl.program_id(2)

            @pl.when(k == 0)
            def _():
                acc_ref[...] = part

            @pl.when(k > 0)
            def _():
                acc_ref[...] += part

            @pl.when(k == nk - 1)
            def _():
                finish(acc_ref[...])

    scratch = [pltpu.VMEM(acc_shape, F32)] if nk > 1 else []
    return _call(name, body, grid=grid, in_specs=[a_spec, b_spec, *extra_specs],
                 out_specs=out_specs, out_shape=out_shapes, scratch_shapes=scratch,
                 sem=("parallel", "parallel", "arbitrary"), args=(a, b, *extras), jobs=jobs)


def _tile(n, pref):
    t = min(n, pref)
    assert n % t == 0, (n, t)
    return t


def _mm_nn(name, a, b, out_dtype, tn=1024, tk=None, epilogue=None, out_dtypes=None, jobs=()):
    m, k = a.shape
    n = b.shape[1]
    tm, tn = _tile(m, 1024), _tile(n, tn)
    tk = _tile(k, tk or 4096)
    out_dtypes = out_dtypes or (out_dtype,)
    return _matmul(
        name, a, b, mode="nn", grid=(m // tm, n // tn, k // tk),
        a_spec=pl.BlockSpec((tm, tk), lambda i, j, kk: (i, kk)),
        b_spec=pl.BlockSpec((tk, tn), lambda i, j, kk: (kk, j)),
        out_shapes=[jax.ShapeDtypeStruct((m, n), dt) for dt in out_dtypes],
        out_specs=[pl.BlockSpec((tm, tn), lambda i, j, kk: (i, j)) for _ in out_dtypes],
        acc_shape=(tm, tn), epilogue=epilogue, jobs=jobs)


def _mm_nt(name, a, b, out_dtype, epilogue=None, extras=(), tk=None, jobs=()):
    m, k = a.shape
    n = b.shape[0]
    tm, tn = _tile(m, 1024), _tile(n, 1024)
    tk = _tile(k, tk or 4096)
    o_spec = pl.BlockSpec((tm, tn), lambda i, j, kk: (i, j))
    return _matmul(
        name, a, b, mode="nt", grid=(m // tm, n // tn, k // tk),
        a_spec=pl.BlockSpec((tm, tk), lambda i, j, kk: (i, kk)),
        b_spec=pl.BlockSpec((tn, tk), lambda i, j, kk: (j, kk)),
        out_shapes=[jax.ShapeDtypeStruct((m, n), out_dtype)],
        out_specs=[o_spec], acc_shape=(tm, tn), epilogue=epilogue,
        extras=extras, extra_specs=[o_spec for _ in extras], jobs=jobs)


def _mm_tn(name, a, b, out_dtype, tk=2048, jobs=()):
    k, m = a.shape
    n = b.shape[1]
    tm, tn = _tile(m, 1024), _tile(n, 1024)
    tk = _tile(k, tk)
    return _matmul(
        name, a, b, mode="tn", grid=(m // tm, n // tn, k // tk),
        a_spec=pl.BlockSpec((tk, tm), lambda i, j, kk: (kk, i)),
        b_spec=pl.BlockSpec((tk, tn), lambda i, j, kk: (kk, j)),
        out_shapes=[jax.ShapeDtypeStruct((m, n), out_dtype)],
        out_specs=[pl.BlockSpec((tm, tn), lambda i, j, kk: (i, j))],
        acc_shape=(tm, tn), jobs=jobs)


def _cast_bf16(name, w):
    r, c = w.shape
    tr = _tile(r, 512)

    def body(w_ref, o_ref):
        o_ref[...] = w_ref[...].astype(BF16)

    return pl.pallas_call(
        body, name=name, grid=(r // tr,),
        in_specs=[pl.BlockSpec((tr, c), lambda i: (i, 0))],
        out_specs=pl.BlockSpec((tr, c), lambda i: (i, 0)),
        out_shape=jax.ShapeDtypeStruct((r, c), BF16),
        compiler_params=_cparams(("parallel",)),
    )(w)


def _pair_add(name, g, r1, pos):
    _, r, cdim = g.shape
    tr = _tile(r, 512)

    def body(pos_ref, g_ref, r_ref, o_ref):
        o_ref[...] = (g_ref[...].astype(F32) + r_ref[...].astype(F32)).astype(o_ref.dtype)

    return pl.pallas_call(
        body, name=name,
        grid_spec=pltpu.PrefetchScalarGridSpec(
            num_scalar_prefetch=1, grid=(4, r // tr),
            in_specs=[pl.BlockSpec((None, tr, cdim), lambda j, i, pos: (2 * j + pos[2], i, 0)),
                      pl.BlockSpec((None, tr, cdim), lambda j, i, pos: (j, i, 0))],
            out_specs=pl.BlockSpec((None, tr, cdim), lambda j, i, pos: (j, i, 0))),
        out_shape=jax.ShapeDtypeStruct((4, r, cdim), BF16),
        compiler_params=_cparams(("parallel", "parallel")),
    )(pos, g, r1)


def _adamw_math(w, g, m, v):
    m = ADAM_B1 * m + (1.0 - ADAM_B1) * g
    v = ADAM_B2 * v + (1.0 - ADAM_B2) * jnp.square(g)
    m_hat = m / (1.0 - ADAM_B1 ** ADAM_STEP)
    v_hat = v / (1.0 - ADAM_B2 ** ADAM_STEP)
    delta = -ADAM_LR * (m_hat / (jnp.sqrt(v_hat) + ADAM_EPS) + ADAM_WD * w)
    return delta, m, v


def _reduce_adamw(name, p, r2, pos, w, m, v, jobs=()):
    r, cdim = w.shape
    tr = _tile(r, 128 if cdim >= D_MODEL else 256)
    blk = pl.BlockSpec((tr, cdim), lambda i, pos: (i, 0))

    def body(pos_ref, p_ref, r2_ref, w_ref, m_ref, v_ref, g_out, d_out, m_out, v_out):
        g = p_ref[...].astype(F32)
        for k in range(3):
            g = g + r2_ref[k].astype(F32)
        d, mn, vn = _adamw_math(w_ref[...], g, m_ref[...], v_ref[...])
        g_out[...] = g
        d_out[...] = d
        m_out[...] = mn
        v_out[...] = vn

    shp = jax.ShapeDtypeStruct((r, cdim), F32)
    return _call(
        name, body, grid=(r // tr,), n_prefetch=1,
        in_specs=[pl.BlockSpec((None, tr, cdim), lambda i, pos: (2 * pos[0] + pos[1], i, 0)),
                  pl.BlockSpec((3, tr, cdim), lambda i, pos: (0, i, 0)),
                  blk, blk, blk],
        out_specs=[blk, blk, blk, blk], out_shape=[shp, shp, shp, shp],
        sem=("parallel",), args=(pos, p, r2, w, m, v), jobs=jobs)


def _row_spec(t, width=D_MODEL):
    return pl.BlockSpec((t, width), lambda i: (i, 0))


def _full_spec(shape):
    return pl.BlockSpec(shape, lambda i: tuple(0 for _ in shape))


def _ln_stats(p):
    mu = jnp.mean(p, axis=-1, keepdims=True)
    xc = p - mu
    var = jnp.mean(xc * xc, axis=-1, keepdims=True)
    rstd = lax.rsqrt(var + LN_EPS)
    return xc * rstd, rstd


def _ln_bwd(dy, xhat, rstd, g):
    dxh = dy * g
    m1 = jnp.mean(dxh, axis=-1, keepdims=True)
    m2 = jnp.mean(dxh * xhat, axis=-1, keepdims=True)
    return rstd * (dxh - m1 - xhat * m2)


def _acc_rows(ref, val, first):
    s = jnp.sum(val, axis=0, keepdims=True)

    @pl.when(first)
    def _():
        ref[...] = s

    @pl.when(jnp.logical_not(first))
    def _():
        ref[...] += s


def _modulate(name, x, mod6):
    s = x.shape[0]
    t = _tile(s, 256)

    def body(x_ref, mod_ref, o_ref):
        o_ref[...] = (x_ref[...] * (1.0 + mod_ref[1:2, :]) + mod_ref[0:1, :]).astype(BF16)

    return pl.pallas_call(
        body, name=name, grid=(s // t,),
        in_specs=[_row_spec(t), _full_spec((N_MOD, D_MODEL))],
        out_specs=_row_spec(t),
        out_shape=jax.ShapeDtypeStruct((s, D_MODEL), BF16),
        compiler_params=_cparams(("parallel",)),
    )(x, mod6)


def _ln1_fwd(x, mix, mod6, g, b):
    s = x.shape[0]
    t = _tile(s, 256)

    def body(x_ref, mix_ref, mod_ref, g_ref, b_ref, x1_ref, h2_ref):
        pre = ALPHA * x_ref[...] + (1.0 + mod_ref[2:3, :]) * mix_ref[...]
        xhat, _ = _ln_stats(pre)
        x1 = xhat * g_ref[...] + b_ref[...]
        x1_ref[...] = x1
        h2_ref[...] = (x1 * (1.0 + mod_ref[4:5, :]) + mod_ref[3:4, :]).astype(BF16)

    vec = _full_spec((1, D_MODEL))
    return pl.pallas_call(
        body, name="ln1_fwd", grid=(s // t,),
        in_specs=[_row_spec(t), _row_spec(t), _full_spec((N_MOD, D_MODEL)), vec, vec],
        out_specs=[_row_spec(t), _row_spec(t)],
        out_shape=[jax.ShapeDtypeStruct((s, D_MODEL), F32), jax.ShapeDtypeStruct((s, D_MODEL), BF16)],
        compiler_params=_cparams(("parallel",)),
    )(x, mix, mod6, g, b)


def _ln2_loss_bwd(x1, f2, tgt, mod6, g, b):
    s = x1.shape[0]
    t = _tile(s, 128)

    def body(x1_ref, f2_ref, tgt_ref, mod_ref, g_ref, b_ref,
             df2_ref, dx1_ref, loss_ref, dg_ref, db_ref, dgate_ref):
        first = pl.program_id(0) == 0
        gate = 1.0 + mod_ref[5:6, :]
        f2v = f2_ref[...]
        pre = ALPHA * x1_ref[...] + gate * f2v
        xhat, rstd = _ln_stats(pre)
        err = xhat * g_ref[...] + b_ref[...] - tgt_ref[...]
        part = 0.5 * jnp.sum(jnp.mean(err * err, axis=-1, keepdims=True), axis=0, keepdims=True)
        dy = err / D_MODEL
        dpre = _ln_bwd(dy, xhat, rstd, g_ref[...])
        df2_ref[...] = (gate * dpre).astype(BF16)
        dx1_ref[...] = ALPHA * dpre
        _acc_rows(loss_ref, jnp.broadcast_to(part, (1, 128)), first)
        _acc_rows(dg_ref, dy * xhat, first)
        _acc_rows(db_ref, dy, first)
        _acc_rows(dgate_ref, dpre * f2v, first)

    vec = _full_spec((1, D_MODEL))
    vshape = jax.ShapeDtypeStruct((1, D_MODEL), F32)
    return pl.pallas_call(
        body, name="ln2_loss_bwd", grid=(s // t,),
        in_specs=[_row_spec(t), _row_spec(t), _row_spec(t), _full_spec((N_MOD, D_MODEL)), vec, vec],
        out_specs=[_row_spec(t), _row_spec(t), _full_spec((1, 128)), vec, vec, vec],
        out_shape=[jax.ShapeDtypeStruct((s, D_MODEL), BF16), jax.ShapeDtypeStruct((s, D_MODEL), F32),
                   jax.ShapeDtypeStruct((1, 128), F32), vshape, vshape, vshape],
        compiler_params=_cparams(("arbitrary",)),
    )(x1, f2, tgt, mod6, g, b)


def _ln1_bwd(dh2, dx1a, x1, x, mix, mod6, g):
    s = x.shape[0]
    t = _tile(s, 128)

    def body(dh2_ref, dx1a_ref, x1_ref, x_ref, mix_ref, mod_ref, g_ref,
             dmix_ref, dxa_ref, dscale_ref, dshift_ref, dg_ref, db_ref, dgate_ref):
        first = pl.program_id(0) == 0
        dh2v = dh2_ref[...]
        dx1 = dx1a_ref[...] + dh2v * (1.0 + mod_ref[4:5, :])
        gate = 1.0 + mod_ref[2:3, :]
        mixv = mix_ref[...]
        pre = ALPHA * x_ref[...] + gate * mixv
        xhat, rstd = _ln_stats(pre)
        dpre = _ln_bwd(dx1, xhat, rstd, g_ref[...])
        dmix_ref[...] = (gate * dpre).astype(BF16)
        dxa_ref[...] = ALPHA * dpre
        _acc_rows(dscale_ref, dh2v * x1_ref[...], first)
        _acc_rows(dshift_ref, dh2v, first)
        _acc_rows(dg_ref, dx1 * xhat, first)
        _acc_rows(db_ref, dx1, first)
        _acc_rows(dgate_ref, dpre * mixv, first)

    vec = _full_spec((1, D_MODEL))
    vshape = jax.ShapeDtypeStruct((1, D_MODEL), F32)
    return pl.pallas_call(
        body, name="ln1_bwd", grid=(s // t,),
        in_specs=[_row_spec(t)] * 5 + [_full_spec((N_MOD, D_MODEL)), vec],
        out_specs=[_row_spec(t), _row_spec(t), vec, vec, vec, vec, vec],
        out_shape=[jax.ShapeDtypeStruct((s, D_MODEL), BF16), jax.ShapeDtypeStruct((s, D_MODEL), F32),
                   vshape, vshape, vshape, vshape, vshape],
        compiler_params=_cparams(("arbitrary",)),
    )(dh2, dx1a, x1, x, mix, mod6, g)


def _grad_x(dxa, dh1, x, mod6):
    s = x.shape[0]
    t = _tile(s, 256)

    def body(dxa_ref, dh1_ref, x_ref, mod_ref, gx_ref, dscale_ref, dshift_ref):
        first = pl.program_id(0) == 0
        dh1v = dh1_ref[...]
        gx_ref[...] = dxa_ref[...] + dh1v * (1.0 + mod_ref[1:2, :])
        _acc_rows(dscale_ref, dh1v * x_ref[...], first)
        _acc_rows(dshift_ref, dh1v, first)

    vec = _full_spec((1, D_MODEL))
    vshape = jax.ShapeDtypeStruct((1, D_MODEL), F32)
    return pl.pallas_call(
        body, name="grad_x", grid=(s // t,),
        in_specs=[_row_spec(t)] * 3 + [_full_spec((N_MOD, D_MODEL))],
        out_specs=[_row_spec(t), vec, vec],
        out_shape=[jax.ShapeDtypeStruct((s, D_MODEL), F32), vshape, vshape],
        compiler_params=_cparams(("arbitrary",)),
    )(dxa, dh1, x, mod6)


def _window(ref, i, t, s):
    r0 = pl.multiple_of(i * t, t)
    cur = ref[pl.ds(r0, t), :]
    lo = pl.multiple_of(jnp.maximum(r0 - HALO, 0), HALO)
    hi = pl.multiple_of(jnp.minimum(r0 + t, s - HALO), HALO)
    before = ref[pl.ds(lo, HALO), :] * (i > 0).astype(F32)
    after = ref[pl.ds(hi, HALO), :] * (i < s // t - 1).astype(F32)
    return jnp.concatenate([before, cur, after], axis=0)


def _tap(ext, shift):
    n = ext.shape[0]
    if shift == 0:
        return ext
    return pltpu.roll(ext, (-shift) % n, 0)


def _centre(ext, t):
    return ext[HALO:HALO + t]


def _conv_taps(ext, w, width):
    acc = None
    for k in range(width):
        term = _tap(ext, k - width // 2) * w[k:k + 1, :]
        acc = term if acc is None else acc + term
    return acc


def _silu(a):
    return a * jax.nn.sigmoid(a)


def _conv_silu_fwd(proj, w, b):
    s = proj.shape[0]
    cb = 256
    t = _tile(s, 256)
    off = D_SSM // cb

    def body(u_ref, w_ref, b_ref, o_ref):
        wv = w_ref[...]
        bv = b_ref[...]

        def step(i, carry):
            ext = _window(u_ref, i, t, s)
            a = _centre(_conv_taps(ext, wv, SSM_CONV), t) + bv
            o_ref[pl.ds(pl.multiple_of(i * t, t), t), :] = _silu(a)
            return carry

        lax.fori_loop(0, s // t, step, 0)

    return pl.pallas_call(
        body, name="conv_silu_fwd", grid=(D_XBC // cb,),
        in_specs=[pl.BlockSpec((s, cb), lambda j: (0, off + j)),
                  pl.BlockSpec((SSM_CONV, cb), lambda j: (0, j)),
                  pl.BlockSpec((1, cb), lambda j: (0, j))],
        out_specs=pl.BlockSpec((s, cb), lambda j: (0, j)),
        out_shape=jax.ShapeDtypeStruct((s, D_XBC), F32),
        compiler_params=_cparams(("parallel",)),
    )(proj, w, b)


def _conv_silu_bwd(name, proj, w, b, col0, ncols, cots, scaled=None):
    s = proj.shape[0]
    cb = 128
    t = _tile(s, 256)
    off = (D_SSM + col0) // cb
    woff = col0 // cb
    n_cot = len(cots)

    def body(*refs):
        u_ref, w_ref, b_ref = refs[:3]
        cot_refs = refs[3:3 + n_cot]
        sc_refs = refs[3 + n_cot:3 + n_cot + (2 if scaled else 0)]
        du_ref, dw_ref, db_ref = refs[-3:]
        wv = w_ref[...]
        bv = b_ref[...]

        def step(i, carry):
            ext = _window(u_ref, i, t, s)
            a = _conv_taps(ext, wv, SSM_CONV) + bv
            cot = None
            for cr in cot_refs:
                term = _window(cr.at[0], i, t, s) + _window(cr.at[1], i, t, s)
                cot = term if cot is None else cot + term
            if scaled:
                cot = cot + _window(sc_refs[0], i, t, s) * sc_refs[1][...]
            sig = jax.nn.sigmoid(a)
            da = cot * (sig * (1.0 + a * (1.0 - sig)))
            du = None
            new = []
            for k in range(SSM_CONV):
                sh = k - SSM_CONV // 2
                term = _tap(da, -sh) * wv[k:k + 1, :]
                du = term if du is None else du + term
                prod = _centre(_tap(ext, sh) * da, t)
                new.append(carry[k] + jnp.sum(prod, axis=0, keepdims=True))
            new.append(carry[SSM_CONV] + jnp.sum(_centre(da, t), axis=0, keepdims=True))
            du_ref[pl.ds(pl.multiple_of(i * t, t), t), :] = _centre(du, t).astype(BF16)
            return tuple(new)

        zero = jnp.zeros((1, cb), F32)
        acc = lax.fori_loop(0, s // t, step, tuple(zero for _ in range(SSM_CONV + 1)))
        for k in range(SSM_CONV):
            dw_ref[k:k + 1, :] = acc[k]
        db_ref[...] = acc[SSM_CONV]

    in_specs = [pl.BlockSpec((s, cb), lambda j: (0, off + j)),
                pl.BlockSpec((SSM_CONV, cb), lambda j: (0, woff + j)),
                pl.BlockSpec((1, cb), lambda j: (0, woff + j))]
    in_specs += [pl.BlockSpec((2, s, cb), lambda j: (0, 0, j)) for _ in cots]
    args = [proj, w, b, *cots]
    if scaled:
        in_specs += [pl.BlockSpec((s, cb), lambda j: (0, j)), pl.BlockSpec((1, cb), lambda j: (0, j))]
        args += list(scaled)
    return pl.pallas_call(
        body, name=name, grid=(ncols // cb,),
        in_specs=in_specs,
        out_specs=[pl.BlockSpec((s, cb), lambda j: (0, j)),
                   pl.BlockSpec((SSM_CONV, cb), lambda j: (0, j)),
                   pl.BlockSpec((1, cb), lambda j: (0, j))],
        out_shape=[jax.ShapeDtypeStruct((s, ncols), BF16),
                   jax.ShapeDtypeStruct((SSM_CONV, ncols), F32),
                   jax.ShapeDtypeStruct((1, ncols), F32)],
        compiler_params=_cparams(("parallel",)),
    )(*args)


_SC_H = (D_SSM + D_XBC) // SC_GROUP_W
_SC_B = _SC_H + D_SC // SC_GROUP_W
_SC_C = _SC_B + D_SC // SC_GROUP_W


def _sc_fwd(proj, w, nw):
    s = proj.shape[0]
    cb = SC_GROUP_W
    t = _tile(s, 256)

    def body(uh_ref, ub_ref, uc_ref, w_ref, nw_ref, o_ref):
        wv = w_ref[...]
        nwv = nw_ref[...]

        def step(i, carry):
            p = _window(uc_ref, i, t, s) * _window(uh_ref, i, t, s)
            cv = _centre(_conv_taps(p, wv, SC_CONV), t)
            rows = pl.ds(pl.multiple_of(i * t, t), t)
            y = ub_ref[rows, :] * cv
            r = lax.rsqrt(jnp.mean(y * y, axis=-1, keepdims=True) + RMS_EPS)
            o_ref[rows, :] = (y * r * nwv).astype(BF16)
            return carry

        lax.fori_loop(0, s // t, step, 0)

    def col(base):
        return pl.BlockSpec((s, cb), lambda j: (0, base + j))

    return pl.pallas_call(
        body, name="sc_fwd", grid=(D_SC // cb,),
        in_specs=[col(_SC_H), col(_SC_B), col(_SC_C),
                  pl.BlockSpec((SC_CONV, cb), lambda j: (0, j)),
                  pl.BlockSpec((1, cb), lambda j: (0, j))],
        out_specs=pl.BlockSpec((s, cb), lambda j: (0, j)),
        out_shape=jax.ShapeDtypeStruct((s, D_SC), BF16),
        compiler_params=_cparams(("parallel",)),
    )(proj, proj, proj, w, nw)


def _sc_bwd(proj, dycat, w, nw):
    s = proj.shape[0]
    cb = SC_GROUP_W
    t = _tile(s, 256)
    dy_off = D_SSM // cb

    def body(uh_ref, ub_ref, uc_ref, dy_ref, w_ref, nw_ref, duh_ref, dub_ref, duc_ref, dw_ref, dnw_ref):
        wv = w_ref[...]
        nwv = nw_ref[...]

        def step(i, carry):
            uh = _window(uh_ref, i, t, s)
            ub = _window(ub_ref, i, t, s)
            uc = _window(uc_ref, i, t, s)
            do = _window(dy_ref, i, t, s)
            p = uc * uh
            cv = _conv_taps(p, wv, SC_CONV)
            y = ub * cv
            r = lax.rsqrt(jnp.mean(y * y, axis=-1, keepdims=True) + RMS_EPS)
            dyr = do * nwv
            dy = r * dyr - y * (r * r * r) * jnp.mean(dyr * y, axis=-1, keepdims=True)
            dcv = dy * ub
            dp = None
            new = []
            for k in range(SC_CONV):
                sh = k - SC_CONV // 2
                term = _tap(dcv, -sh) * wv[k:k + 1, :]
                dp = term if dp is None else dp + term
                new.append(carry[k] + jnp.sum(_centre(_tap(p, sh) * dcv, t), axis=0, keepdims=True))
            new.append(carry[SC_CONV] + jnp.sum(_centre(do * y * r, t), axis=0, keepdims=True))
            rows = pl.ds(pl.multiple_of(i * t, t), t)
            duh_ref[rows, :] = _centre(dp * uc, t).astype(BF16)
            duc_ref[rows, :] = _centre(dp * uh, t).astype(BF16)
            dub_ref[rows, :] = _centre(dy * cv, t).astype(BF16)
            return tuple(new)

        zero = jnp.zeros((1, cb), F32)
        acc = lax.fori_loop(0, s // t, step, tuple(zero for _ in range(SC_CONV + 1)))
        for k in range(SC_CONV):
            dw_ref[k:k + 1, :] = acc[k]
        dnw_ref[...] = acc[SC_CONV]

    def col(base):
        return pl.BlockSpec((s, cb), lambda j: (0, base + j))

    out_col = pl.BlockSpec((s, cb), lambda j: (0, j))
    act = jax.ShapeDtypeStruct((s, D_SC), BF16)
    return pl.pallas_call(
        body, name="sc_bwd", grid=(D_SC // cb,),
        in_specs=[col(_SC_H), col(_SC_B), col(_SC_C), col(dy_off),
                  pl.BlockSpec((SC_CONV, cb), lambda j: (0, j)),
                  pl.BlockSpec((1, cb), lambda j: (0, j))],
        out_specs=[out_col, out_col, out_col,
                   pl.BlockSpec((SC_CONV, cb), lambda j: (0, j)),
                   pl.BlockSpec((1, cb), lambda j: (0, j))],
        out_shape=[act, act, act, jax.ShapeDtypeStruct((SC_CONV, D_SC), F32),
                   jax.ShapeDtypeStruct((1, D_SC), F32)],
        compiler_params=_cparams(("parallel",)),
    )(proj, proj, proj, dycat, w, nw)


def _make_dot(hi, differentiable):
    def raw(a, b, mode):
        if hi:
            return lax.dot_general(a, b, _DN[mode], precision=lax.Precision.HIGHEST,
                                   preferred_element_type=F32)
        return lax.dot_general(a.astype(BF16), b.astype(BF16), _DN[mode], preferred_element_type=F32)

    if not differentiable:
        return raw

    @functools.partial(jax.custom_vjp, nondiff_argnums=(2,))
    def dot(a, b, mode):
        return raw(a, b, mode)

    def fwd(a, b, mode):
        return raw(a, b, mode), (a, b)

    def bwd(mode, res, g):
        a, b = res
        if mode == "nn":
            return raw(g, b, "nt"), raw(a, g, "tn")
        if mode == "nt":
            return raw(g, b, "nn"), raw(g, a, "tn")
        return raw(b, g, "nt"), raw(a, g, "nn")

    dot.defvjp(fwd, bwd)
    return dot


def _make_swap(differentiable):
    def raw(v):
        return pltpu.roll(v, HEAD_DIM, 1)

    if not differentiable:
        return raw
    swap = jax.custom_vjp(raw)
    swap.defvjp(lambda v: (raw(v), None), lambda _, g: (raw(g),))
    return swap


def _ssd_chunk(xs, bm, cm, dtx, acx, ax, prev, tri, differentiable):
    _bdot = _make_dot(False, differentiable)
    swap = _make_swap(differentiable)
    atx = jnp.sum(dtx * ax, axis=0, keepdims=True)
    xdt = xs * dtx
    mask = tri > 0.0
    scores = _bdot(cm, bm, "nt")
    head = lax.broadcasted_iota(jnp.int32, (1, GROUP_W), 1) // HEAD_DIM
    low = lax.broadcasted_iota(jnp.int32, (1, 128), 1) < HEAD_DIM
    y = _bdot(cm, prev, "nn") * jnp.exp(acx)
    for h in range(HEADS_PER_GROUP):
        pair = acx[:, 128 * (h // 2):128 * (h // 2) + 128]
        other = swap(pair)
        m1 = jnp.where(low, pair, other) if h % 2 == 0 else jnp.where(low, other, pair)
        seg = m1 - m1.T
        decay = jnp.where(mask, jnp.exp(jnp.where(mask, seg, 0.0)), 0.0)
        xh = xdt * (head == h).astype(F32)
        y = y + _bdot(scores * decay, xh, "nn")
    new = prev * jnp.exp(atx) + _bdot(bm, xdt * jnp.exp(atx - acx), "tn")
    return y, new


def _softplus(v):
    return jnp.maximum(v, 0.0) + jnp.log(1.0 + jnp.exp(-jnp.abs(v)))


def _dt_spread(u, bias, a, tri2, exf, differentiable):
    _hdot = _make_dot(True, differentiable)
    dt = _softplus(u + bias)
    dta = dt * a
    out = []
    for d in range(2):
        acum = _hdot(tri2[d], dta, "nn")
        out += [_hdot(dt, exf[d], "nn"), _hdot(acum, exf[d], "nn")]
    return tuple(out)


def _ssd_consts():
    q = CHUNK
    r = lax.broadcasted_iota(jnp.int32, (q, q), 0)
    c = lax.broadcasted_iota(jnp.int32, (q, q), 1)
    tri = jnp.stack([(c <= r), (c >= r)]).astype(F32)
    shp = (2, 128, D_SSM)
    src = lax.broadcasted_iota(jnp.int32, shp, 1)
    d = lax.broadcasted_iota(jnp.int32, shp, 0)
    col = lax.broadcasted_iota(jnp.int32, shp, 2)
    exf = (src == d * HEADS + col // HEAD_DIM).astype(F32)
    return tri, exf


def _dt_prep(proj_dt, bias_all, a_all):
    s = proj_dt.shape[0]
    tri, exf = _ssd_consts()

    def body(u_ref, b_ref, a_ref, tri_ref, exf_ref, dtx_ref, acx_ref):
        dtx0, acx0, dtx1, acx1 = _dt_spread(u_ref[...], b_ref[...], a_ref[...], tri_ref[...],
                                            exf_ref[...], False)
        dtx_ref[0] = dtx0
        dtx_ref[1] = dtx1
        acx_ref[0] = acx0
        acx_ref[1] = acx1

    out = pl.BlockSpec((2, CHUNK, D_SSM), lambda i: (0, i, 0))
    shp = jax.ShapeDtypeStruct((2, s, D_SSM), F32)
    return pl.pallas_call(
        body, name="dt_prep", grid=(s // CHUNK,),
        in_specs=[_row_spec(CHUNK, 128), _full_spec((1, 128)), _full_spec((1, 128)),
                  _full_spec((2, CHUNK, CHUNK)), _full_spec((2, 128, D_SSM))],
        out_specs=[out, out], out_shape=[shp, shp],
        compiler_params=_cparams(("parallel",)),
    )(proj_dt, bias_all, a_all, tri, exf)


def _dt_prep_bwd(proj_dt, bias_all, a_all, d_dtx, d_acx):
    s = proj_dt.shape[0]
    tri, exf = _ssd_consts()

    def body(u_ref, b_ref, a_ref, tri_ref, exf_ref, ddtx_ref, dacx_ref, du_ref, db_ref, da_ref):
        tri_v, exf_v = tri_ref[...], exf_ref[...]

        def f(u, bias, a):
            return _dt_spread(u, bias, a, tri_v, exf_v, True)

        _, vjp = jax.vjp(f, u_ref[...], b_ref[...], a_ref[...])
        du, db, da = vjp((ddtx_ref[0], dacx_ref[0], ddtx_ref[1], dacx_ref[1]))
        du_ref[...] = du.astype(BF16)
        first = pl.program_id(0) == 0
        _acc_rows(db_ref, db, first)
        _acc_rows(da_ref, da, first)

    cot = pl.BlockSpec((2, CHUNK, D_SSM), lambda i: (0, i, 0))
    vec = _full_spec((1, 128))
    return pl.pallas_call(
        body, name="dt_prep_bwd", grid=(s // CHUNK,),
        in_specs=[_row_spec(CHUNK, 128), vec, vec, _full_spec((2, CHUNK, CHUNK)),
                  _full_spec((2, 128, D_SSM)), cot, cot],
        out_specs=[_row_spec(CHUNK, 128), vec, vec],
        out_shape=[jax.ShapeDtypeStruct((s, 128), BF16), jax.ShapeDtypeStruct((1, 128), F32),
                   jax.ShapeDtypeStruct((1, 128), F32)],
        compiler_params=_cparams(("arbitrary",)),
    )(proj_dt, bias_all, a_all, tri, exf, d_dtx, d_acx)


def _ssd_specs(chunk_of):
    q = CHUNK
    xs = pl.BlockSpec((q, GROUP_W), lambda d, g, ci: (chunk_of(d, ci), g))
    bm = pl.BlockSpec((q, N_STATE), lambda d, g, ci: (chunk_of(d, ci), D_SSM // N_STATE + g))
    cm = pl.BlockSpec((q, N_STATE), lambda d, g, ci: (chunk_of(d, ci), D_SSM // N_STATE + GROUPS + g))
    spread = pl.BlockSpec((None, q, GROUP_W), lambda d, g, ci: (d, chunk_of(d, ci), g))
    ax = pl.BlockSpec((None, 1, GROUP_W), lambda d, g, ci: (d, 0, g))
    tri = pl.BlockSpec((None, q, q), lambda d, g, ci: (d, 0, 0))
    st = pl.BlockSpec((None, None, None, N_STATE, GROUP_W), lambda d, g, ci: (d, chunk_of(d, ci), g, 0, 0))
    return xs, bm, cm, spread, ax, tri, st


def _ssd_fwd(xbc, dtx, acx, ax, jobs=()):
    s = xbc.shape[0]
    nc = s // CHUNK
    tri, _ = _ssd_consts()

    def chunk_of(d, ci):
        return ci + d * (nc - 1 - 2 * ci)

    def body(xs_ref, b_ref, c_ref, dtx_ref, acx_ref, ax_ref, tri_ref, y_ref, st_ref, state):
        @pl.when(pl.program_id(2) == 0)
        def _():
            state[...] = jnp.zeros((N_STATE, GROUP_W), F32)

        prev = state[...]
        st_ref[...] = prev
        y, new = _ssd_chunk(xs_ref[...], b_ref[...], c_ref[...], dtx_ref[...], acx_ref[...],
                            ax_ref[...], prev, tri_ref[...], False)
        y_ref[...] = y
        state[...] = new

    xs, bm, cm, spread, ax_s, tri_s, st = _ssd_specs(chunk_of)
    return _call(
        "ssd_fwd", body, grid=(2, GROUPS, nc),
        in_specs=[xs, bm, cm, spread, spread, ax_s, tri_s],
        out_specs=[spread, st],
        out_shape=[jax.ShapeDtypeStruct((2, s, D_SSM), F32),
                   jax.ShapeDtypeStruct((2, nc, GROUPS, N_STATE, GROUP_W), F32)],
        scratch_shapes=[pltpu.VMEM((N_STATE, GROUP_W), F32)],
        sem=("arbitrary", "arbitrary", "arbitrary"),
        args=(xbc, xbc, xbc, dtx, acx, ax, tri), jobs=jobs)


def _ssd_bwd(xbc, dtx, acx, ax, states, dy, jobs=()):
    s = xbc.shape[0]
    nc = s // CHUNK
    tri, _ = _ssd_consts()

    def chunk_of(d, ci):
        return (nc - 1 - ci) + d * (2 * ci - (nc - 1))

    def body(xs_ref, b_ref, c_ref, dtx_ref, acx_ref, ax_ref, tri_ref, st_ref, dy_ref,
             dxs_ref, db_ref, dc_ref, ddtx_ref, dacx_ref, dax_ref, dstate):
        first = pl.program_id(2) == 0

        @pl.when(first)
        def _():
            dstate[...] = jnp.zeros((N_STATE, GROUP_W), F32)

        tri_v = tri_ref[...]

        def f(xs, bm, cm, dtx_v, acx_v, ax_v, prev):
            return _ssd_chunk(xs, bm, cm, dtx_v, acx_v, ax_v, prev, tri_v, True)

        _, vjp = jax.vjp(f, xs_ref[...], b_ref[...], c_ref[...], dtx_ref[...], acx_ref[...],
                         ax_ref[...], st_ref[...])
        dxs, dbm, dcm, ddtx, dacx, dax, dprev = vjp((dy_ref[...], dstate[...]))
        dxs_ref[...] = dxs
        db_ref[...] = dbm
        dc_ref[...] = dcm
        ddtx_ref[...] = ddtx
        dacx_ref[...] = dacx
        dstate[...] = dprev
        _acc_rows(dax_ref, dax, first)

    xs, bm, cm, spread, ax_s, tri_s, st = _ssd_specs(chunk_of)
    dy_s = pl.BlockSpec((CHUNK, GROUP_W), lambda d, g, ci: (chunk_of(d, ci), g))
    bc_s = pl.BlockSpec((None, CHUNK, N_STATE), lambda d, g, ci: (d, chunk_of(d, ci), g))
    wide = jax.ShapeDtypeStruct((2, s, D_SSM), F32)
    narrow = jax.ShapeDtypeStruct((2, s, GROUPS * N_STATE), F32)
    return _call(
        "ssd_bwd", body, grid=(2, GROUPS, nc),
        in_specs=[xs, bm, cm, spread, spread, ax_s, tri_s, st, dy_s],
        out_specs=[spread, bc_s, bc_s, spread, spread, ax_s],
        out_shape=[wide, narrow, narrow, wide, wide, jax.ShapeDtypeStruct((2, 1, D_SSM), F32)],
        scratch_shapes=[pltpu.VMEM((N_STATE, GROUP_W), F32)],
        sem=("arbitrary", "arbitrary", "arbitrary"),
        args=(xbc, xbc, xbc, dtx, acx, ax, tri, states, dy), jobs=jobs)


def _ssd_gate_fwd(y2, xbc, proj, dx, nw):
    s = xbc.shape[0]
    t = _tile(s, 512)

    def body(y_ref, xs_ref, z_ref, dx_ref, nw_ref, o_ref):
        y = (y_ref[0] + y_ref[1] + dx_ref[...] * xs_ref[...]) * _silu(z_ref[...])
        r = lax.rsqrt(jnp.mean(y * y, axis=-1, keepdims=True) + RMS_EPS)
        o_ref[...] = (y * r * nw_ref[...]).astype(BF16)

    blk = pl.BlockSpec((t, GROUP_W), lambda j, i: (i, j))
    vec = pl.BlockSpec((1, GROUP_W), lambda j, i: (0, j))
    return pl.pallas_call(
        body, name="ssd_gate_fwd", grid=(GROUPS, s // t),
        in_specs=[pl.BlockSpec((2, t, GROUP_W), lambda j, i: (0, i, j)), blk, blk, vec, vec],
        out_specs=blk,
        out_shape=jax.ShapeDtypeStruct((s, D_SSM), BF16),
        compiler_params=_cparams(("parallel", "parallel")),
    )(y2, xbc, proj, dx, nw)


def _ssd_gate_bwd(y2, xbc, proj, dycat, dx, nw, jobs=()):
    s = xbc.shape[0]
    t = _tile(s, 512)

    def body(y_ref, xs_ref, z_ref, do_ref, dx_ref, nw_ref, dyc_ref, dz_ref, dd_ref, dnw_ref):
        first = pl.program_id(1) == 0
        z = z_ref[...]
        xs = xs_ref[...]
        sig = jax.nn.sigmoid(z)
        gate = z * sig
        yc = y_ref[0] + y_ref[1] + dx_ref[...] * xs
        y = yc * gate
        r = lax.rsqrt(jnp.mean(y * y, axis=-1, keepdims=True) + RMS_EPS)
        do = do_ref[...]
        dyr = do * nw_ref[...]
        dy = r * dyr - y * (r * r * r) * jnp.mean(dyr * y, axis=-1, keepdims=True)
        dyc = dy * gate
        dyc_ref[...] = dyc
        dz_ref[...] = (dy * yc * (sig * (1.0 + z * (1.0 - sig)))).astype(BF16)
        _acc_rows(dd_ref, dyc * xs, first)
        _acc_rows(dnw_ref, do * y * r, first)

    blk = pl.BlockSpec((t, GROUP_W), lambda j, i: (i, j))
    vec = pl.BlockSpec((1, GROUP_W), lambda j, i: (0, j))
    return _call(
        "ssd_gate_bwd", body, grid=(GROUPS, s // t),
        in_specs=[pl.BlockSpec((2, t, GROUP_W), lambda j, i: (0, i, j)), blk, blk, blk, vec, vec],
        out_specs=[blk, blk, vec, vec],
        out_shape=[jax.ShapeDtypeStruct((s, D_SSM), F32), jax.ShapeDtypeStruct((s, D_SSM), BF16),
                   jax.ShapeDtypeStruct((1, D_SSM), F32), jax.ShapeDtypeStruct((1, D_SSM), F32)],
        sem=("parallel", "arbitrary"), args=(y2, xbc, proj, dycat, dx, nw), jobs=jobs)


def _ada_fwd(c16, w_ada):
    k, n = w_ada.shape
    tn = 512

    def body(c_ref, w_ref, o_ref):
        a = _silu(c_ref[...]).astype(BF16)
        o_ref[...] = jnp.dot(a, w_ref[...].astype(BF16), preferred_element_type=F32)

    return pl.pallas_call(
        body, name="ada_fwd", grid=(n // tn,),
        in_specs=[_full_spec((16, k)), pl.BlockSpec((k, tn), lambda j: (0, j))],
        out_specs=pl.BlockSpec((16, tn), lambda j: (0, j)),
        out_shape=jax.ShapeDtypeStruct((16, n), F32),
        compiler_params=_cparams(("parallel",)),
    )(c16, w_ada)


def _ada_bwd_adamw(c16, dmod16, w, m, v):
    k, n = w.shape
    tm, tn = 512, 1024
    blk = pl.BlockSpec((tm, tn), lambda i, j: (i, j))

    def body(c_ref, d_ref, w_ref, m_ref, v_ref, g_out, d_out, m_out, v_out):
        a = _silu(c_ref[...]).astype(BF16)
        g = lax.dot_general(a, d_ref[...].astype(BF16), _DN["tn"], preferred_element_type=F32)
        d, mn, vn = _adamw_math(w_ref[...], g, m_ref[...], v_ref[...])
        g_out[...] = g
        d_out[...] = d
        m_out[...] = mn
        v_out[...] = vn

    shp = jax.ShapeDtypeStruct((k, n), F32)
    return pl.pallas_call(
        body, name="ada_bwd_adamw", grid=(k // tm, n // tn),
        in_specs=[pl.BlockSpec((16, tm), lambda i, j: (0, i)), pl.BlockSpec((16, tn), lambda i, j: (0, j)),
                  blk, blk, blk],
        out_specs=[blk, blk, blk, blk],
        out_shape=[shp, shp, shp, shp],
        compiler_params=_cparams(("parallel", "parallel")),
    )(c16, dmod16, w, m, v)


def _sum8_adamw(gathered, w, m, v):
    n = w.shape[1]
    tn = _tile(n, 8192)
    vec = pl.BlockSpec((1, tn), lambda j: (0, j))

    def body(g8_ref, w_ref, m_ref, v_ref, g_out, d_out, m_out, v_out):
        g = g8_ref[0:1, :]
        for k in range(1, N_DEV):
            g = g + g8_ref[k:k + 1, :]
        d, mn, vn = _adamw_math(w_ref[...], g, m_ref[...], v_ref[...])
        g_out[...] = g
        d_out[...] = d
        m_out[...] = mn
        v_out[...] = vn

    shp = jax.ShapeDtypeStruct((1, n), F32)
    return pl.pallas_call(
        body, name="sum8_adamw", grid=(n // tn,),
        in_specs=[pl.BlockSpec((N_DEV, tn), lambda j: (0, j)), vec, vec, vec],
        out_specs=[vec, vec, vec, vec],
        out_shape=[shp, shp, shp, shp],
        compiler_params=_cparams(("parallel",)),
    )(gathered, w, m, v)


def _gather_vec(name, v):
    n = v.shape[1]
    out = _run_jobs(name, [_GatherJob(v.reshape(8, n // 8), pltpu.VMEM)])[0]
    return out.reshape(N_DEV, n)


class _Plan:
    _RESULT = {"gather": "", "rs1": "r1_", "rs2": "r2_"}

    def __init__(self, hosted, store, hooks=None):
        self.hosted, self.store, self.hooks = hosted, dict(store), hooks or {}

    def get(self, key):
        if key not in self.store and key.startswith("p_"):
            tag = key[2:]
            self.store[key] = _pair_add("rs_pair_add_" + tag, self.get("g_" + tag), self.get("r1_" + tag),
                                        self.get("pos"))
        return self.store[key]

    def put(self, key, val):
        self.store[key] = val

    def jobs(self, host):
        made = []
        for kind, tag in self.hosted.get(host, ()):
            if kind == "gather":
                made.append(_GatherJob(self.get("shard_" + tag)))
            elif kind == "rs1":
                made.append(_SiblingJob(self.get("g_" + tag)))
            else:
                made.append(_ChipsJob(self.get("p_" + tag)))
        return made

    def run(self, host, fn, *args, **kw):
        outs, results = fn(*args, jobs=self.jobs(host), **kw)
        for (kind, tag), res in zip(self.hosted.get(host, ()), results):
            self.store[self._RESULT[kind] + tag] = res
        return outs

    def hook(self, name):
        if name in self.hooks:
            self.hooks[name](self)


def _pad_lanes(v, n):
    return jnp.pad(v, ((0, 0), (0, n - v.shape[1])))


def _local_step(plan, x, tgt, mod, conv_w, conv_b, dt_bias_f, dt_bias_b, a_log_f, a_log_b,
                ssm_d, ssm_nw, sc_w, sc_nw, ln1_g, ln1_b, ln2_g, ln2_b):
    s = x.shape[0]
    run = plan.run
    mod6 = mod.reshape(N_MOD, D_MODEL)
    bias_all = _pad_lanes(jnp.concatenate([dt_bias_f, dt_bias_b], axis=1), 128)
    a_all = _pad_lanes(-jnp.exp(jnp.concatenate([a_log_f, a_log_b], axis=1)), 128)
    a_x = jnp.stack([jnp.repeat(a_all[:, d * HEADS:(d + 1) * HEADS], HEAD_DIM, axis=1) for d in range(2)])
    d_lanes = jnp.repeat(ssm_d, HEAD_DIM, axis=1)

    w_in_g = plan.get("w_in")
    w_in_main, w_in_dt = _w_in_sections(w_in_g)
    h1 = _modulate("mod1", x, mod6)
    proj, = run("in_proj", _mm_nn, "in_proj", h1, w_in_main, F32)
    proj_dt = _mm_nn("in_proj_dt", h1, w_in_dt, F32)[0][0]
    xbc = _conv_silu_fwd(proj, conv_w, conv_b)
    dtx, acx = _dt_prep(proj_dt, bias_all, a_all)
    y2, states = run("ssd_fwd", _ssd_fwd, xbc, dtx, acx, a_x)
    y_ssm = _ssd_gate_fwd(y2, xbc, proj, d_lanes, ssm_nw)
    y_sc = _sc_fwd(proj, sc_w, sc_nw)
    ycat = jnp.concatenate([y_ssm, y_sc], axis=1)
    w_out_g = plan.get("w_out").reshape(D_MODEL, D_MODEL)
    mix = _mm_nn("out_proj", ycat, w_out_g, F32)[0][0]
    x1, h2 = _ln1_fwd(x, mix, mod6, ln1_g, ln1_b)

    def relu2(acc):
        u = acc.astype(BF16)
        r = jnp.maximum(acc, 0.0)
        return u, r * r

    w_up3 = plan.get("w_up")
    nper = w_up3.shape[2]
    tm = _tile(s, 1024)
    tn = 1024
    nb = nper // tn
    u_spec = pl.BlockSpec((tm, tn), lambda i, j, kk: (i, j))
    u, ff = run(
        "up_proj", _matmul, "up_proj", h2, w_up3, mode="nn", grid=(s // tm, D_FF // tn, 1),
        a_spec=pl.BlockSpec((tm, D_MODEL), lambda i, j, kk: (i, 0)),
        b_spec=pl.BlockSpec((None, D_MODEL, tn), lambda i, j, kk: (j // nb, 0, j % nb)),
        out_shapes=[jax.ShapeDtypeStruct((s, D_FF), BF16)] * 2, out_specs=[u_spec, u_spec],
        acc_shape=(tm, tn), epilogue=relu2)
    w_down_g = plan.get("w_down").reshape(D_FF, D_MODEL)
    f2 = _mm_nn("down_proj", ff, w_down_g, F32)[0][0]
    df2, dx1a, loss, g_ln2_g, g_ln2_b, dgate2 = _ln2_loss_bwd(x1, f2, tgt, mod6, ln2_g, ln2_b)

    def relu_grad(acc, uu):
        return (acc * (2.0 * jnp.maximum(uu.astype(F32), 0.0)),)

    du = _mm_nt("d_ff", df2, w_down_g, BF16, epilogue=relu_grad, extras=(u,))[0][0]
    plan.put("g_down", _mm_tn("g_w_down", ff, df2, BF16)[0][0].reshape(N_DEV, D_FF // N_DEV, D_MODEL))
    g_up, = run(
        "g_w_up", _matmul, "g_w_up", h2, du, mode="tn",
        grid=(D_MODEL // 1024, D_FF // tn, s // _tile(s, 2048)),
        a_spec=pl.BlockSpec((_tile(s, 2048), 1024), lambda i, j, kk: (kk, i)),
        b_spec=pl.BlockSpec((_tile(s, 2048), tn), lambda i, j, kk: (kk, j)),
        out_shapes=[jax.ShapeDtypeStruct((N_DEV, D_MODEL, nper), BF16)],
        out_specs=[pl.BlockSpec((None, 1024, tn), lambda i, j, kk: (j // nb, i, j % nb))],
        acc_shape=(1024, tn))
    plan.put("g_up", g_up)
    dh2, = run(
        "d_h2", _matmul, "d_h2", du, w_up3, mode="nt", grid=(s // tm, D_MODEL // 1024, D_FF // nper),
        a_spec=pl.BlockSpec((tm, nper), lambda i, j, kk: (i, kk)),
        b_spec=pl.BlockSpec((None, 1024, nper), lambda i, j, kk: (kk, j, 0)),
        out_shapes=[jax.ShapeDtypeStruct((s, D_MODEL), F32)],
        out_specs=[pl.BlockSpec((tm, 1024), lambda i, j, kk: (i, j))],
        acc_shape=(tm, 1024))
    dmix, dxa, dscale2, dshift2, g_ln1_g, g_ln1_b, dgate1 = _ln1_bwd(dh2, dx1a, x1, x, mix, mod6, ln1_g)

    dycat = _mm_nt("d_ycat", dmix, w_out_g, F32)[0][0]
    plan.put("g_out", _mm_tn("g_w_out", ycat, dmix, BF16)[0][0].reshape(N_DEV, D_MODEL // N_DEV, D_MODEL))
    duh, dub, duc, g_sc_w, g_sc_nw = _sc_bwd(proj, dycat, sc_w, sc_nw)
    dyc, dz, dd_lanes, g_ssm_nw = run("ssd_gate_bwd", _ssd_gate_bwd, y2, xbc, proj, dycat, d_lanes, ssm_nw)
    dxs2, db2, dc2, ddtx, dacx, dax = run("ssd_bwd", _ssd_bwd, xbc, dtx, acx, a_x, states, dyc)
    n_bc = GROUPS * N_STATE
    du_xs, gw_xs, gb_xs = _conv_silu_bwd("conv_bwd_x", proj, conv_w, conv_b, 0, D_SSM, [dxs2],
                                         scaled=(dyc, d_lanes))
    du_b, gw_b, gb_b = _conv_silu_bwd("conv_bwd_b", proj, conv_w, conv_b, D_SSM, n_bc, [db2])
    du_c, gw_c, gb_c = _conv_silu_bwd("conv_bwd_c", proj, conv_w, conv_b, D_SSM + n_bc, n_bc, [dc2])
    du_dt, g_bias_all, g_a_sums = _dt_prep_bwd(proj_dt, bias_all, a_all, ddtx, dacx)

    dproj = jnp.concatenate([dz, du_xs, du_b, du_c, du_dt[:, :2 * HEADS], duh, dub, duc], axis=1)
    dproj3 = dproj.reshape(s, N_DEV, D_IN_SHARD).transpose(1, 0, 2)
    tk = _tile(s, 2048)
    g_in, = run(
        "g_w_in", _matmul, "g_w_in", h1, dproj3, mode="tn", grid=(N_DEV, D_MODEL // 1024, s // tk),
        a_spec=pl.BlockSpec((tk, 1024), lambda i, j, kk: (kk, j)),
        b_spec=pl.BlockSpec((None, tk, D_IN_SHARD), lambda i, j, kk: (i, kk, 0)),
        out_shapes=[jax.ShapeDtypeStruct((N_DEV, D_MODEL, D_IN_SHARD), BF16)],
        out_specs=[pl.BlockSpec((None, 1024, D_IN_SHARD), lambda i, j, kk: (i, j, 0))],
        acc_shape=(1024, D_IN_SHARD))
    plan.put("g_in", g_in)
    plan.hook("after_g_w_in")
    dh1, = run(
        "d_h1", _matmul, "d_h1", dproj3, w_in_g, mode="nt", grid=(s // tm, D_MODEL // 1024, N_DEV),
        a_spec=pl.BlockSpec((None, tm, D_IN_SHARD), lambda i, j, kk: (kk, i, 0)),
        b_spec=pl.BlockSpec((None, 1024, D_IN_SHARD), lambda i, j, kk: (kk, j, 0)),
        out_shapes=[jax.ShapeDtypeStruct((s, D_MODEL), F32)],
        out_specs=[pl.BlockSpec((tm, 1024), lambda i, j, kk: (i, j))],
        acc_shape=(tm, 1024))
    grad_x, dscale1, dshift1 = _grad_x(dxa, dh1, x, mod6)

    dmod = jnp.concatenate([dshift1, dscale1, dgate1, dshift2, dscale2, dgate2], axis=1)
    g_a_direct = dax.reshape(2, HEADS, HEAD_DIM).sum(axis=-1).reshape(1, 2 * HEADS)
    g_a_all = g_a_sums + _pad_lanes(g_a_direct, 128)
    small = {
        "dmod": dmod,
        "ssm_conv_w": jnp.concatenate([gw_xs, gw_b, gw_c], axis=1),
        "ssm_conv_b": jnp.concatenate([gb_xs, gb_b, gb_c], axis=1),
        "ssm_dt_bias_f": g_bias_all[:, :HEADS],
        "ssm_dt_bias_b": g_bias_all[:, HEADS:2 * HEADS],
        "ssm_a_log_f": (g_a_all * a_all)[:, :HEADS],
        "ssm_a_log_b": (g_a_all * a_all)[:, HEADS:2 * HEADS],
        "ssm_d": dd_lanes.reshape(HEADS, HEAD_DIM).sum(axis=1).reshape(1, HEADS),
        "ssm_norm_w": g_ssm_nw,
        "sc_conv_w": g_sc_w,
        "sc_norm_w": g_sc_nw,
        "ln1_g": g_ln1_g, "ln1_b": g_ln1_b, "ln2_g": g_ln2_g, "ln2_b": g_ln2_b,
    }
    return loss, grad_x, small


_SUMMED = [("ssm_conv_b", D_XBC), ("ssm_dt_bias_f", HEADS), ("ssm_dt_bias_b", HEADS),
           ("ssm_a_log_f", HEADS), ("ssm_a_log_b", HEADS), ("ssm_d", HEADS),
           ("ssm_norm_w", D_SSM), ("sc_norm_w", D_SC),
           ("ln1_g", D_MODEL), ("ln1_b", D_MODEL), ("ln2_g", D_MODEL), ("ln2_b", D_MODEL)]


def _round_up(n, k):
    return (n + k - 1) // k * k


def _w_in_sections(w_in_g):
    w = w_in_g.transpose(1, 0, 2).reshape(D_MODEL, D_IN)
    dt_lo = D_SSM + D_XBC
    main = jnp.concatenate([w[:, :dt_lo], w[:, dt_lo + 2 * HEADS:]], axis=1)
    dt = _pad_lanes(w[:, dt_lo:dt_lo + 2 * HEADS], 128)
    return main, dt


def kernel(x, c, w_ada, b_ada, w_in, ssm_conv_w, ssm_conv_b, ssm_dt_bias_f, ssm_dt_bias_b, ssm_a_log_f, ssm_a_log_b, ssm_d, ssm_norm_w, sc_conv_w, sc_norm_w, w_out, ln1_g, ln1_b, w_up, w_down, ln2_g, ln2_b, loss_target, m_w_ada, m_b_ada, m_w_in, m_ssm_conv_w, m_ssm_conv_b, m_ssm_dt_bias_f, m_ssm_dt_bias_b, m_ssm_a_log_f, m_ssm_a_log_b, m_ssm_d, m_ssm_norm_w, m_sc_conv_w, m_sc_norm_w, m_w_out, m_ln1_g, m_ln1_b, m_w_up, m_w_down, m_ln2_g, m_ln2_b, v_w_ada, v_b_ada, v_w_in, v_ssm_conv_w, v_ssm_conv_b, v_ssm_dt_bias_f, v_ssm_dt_bias_b, v_ssm_a_log_f, v_ssm_a_log_b, v_ssm_d, v_ssm_norm_w, v_sc_conv_w, v_sc_norm_w, v_w_out, v_ln1_g, v_ln1_b, v_w_up, v_w_down, v_ln2_g, v_ln2_b):
    args = dict(locals())
    xi, yi, ci = _my_pos()
    me = 4 * xi + 2 * yi + ci
    pos = jnp.stack([xi, yi, ci]).astype(jnp.int32)
    s = x.shape[1]

    n_cw, n_sw = SSM_CONV * D_XBC // N_DEV, SC_CONV * D_SC // N_DEV
    vec = jnp.concatenate([c, ssm_conv_w[0].reshape(1, n_cw), sc_conv_w[0].reshape(1, n_sw)], axis=1)
    vec = _pad_lanes(vec, 8192)
    gath = _gather_vec("gather_c_conv", vec)
    c_all = gath[:, :D_MODEL]
    conv_w = gath[:, D_MODEL:D_MODEL + n_cw].reshape(N_DEV, SSM_CONV, D_XBC // N_DEV)
    conv_w = conv_w.transpose(1, 0, 2).reshape(SSM_CONV, D_XBC)
    sc_w = gath[:, D_MODEL + n_cw:D_MODEL + n_cw + n_sw].reshape(N_DEV, SC_CONV, D_SC // N_DEV)
    sc_w = sc_w.transpose(1, 0, 2).reshape(SC_CONV, D_SC)
    c16 = jnp.pad(c_all, ((0, 8), (0, 0)))

    n_ada = w_ada.shape[2]
    mod_cols = _ada_fwd(c16, w_ada[0])[:N_DEV]
    mod_all = _run_jobs("gather_mod", [_GatherJob(mod_cols, pltpu.VMEM)])[0]
    mod = lax.dynamic_index_in_dim(mod_all, me, axis=1, keepdims=False)
    mod = mod.reshape(1, N_MOD * D_MODEL) + b_ada

    out = {}

    def adamw(plan, tag):
        name = "w_" + tag
        res = plan.run("rs_adamw_" + tag, _reduce_adamw, "rs_adamw_" + tag, plan.get("p_" + tag),
                       plan.get("r2_" + tag), pos, args[name][0], args["m_" + name][0], args["v_" + name][0])
        out[name] = tuple(a[None] for a in res)

    hosted = {
        "in_proj": [("gather", "w_out")],
        "ssd_fwd": [("gather", "w_up")],
        "up_proj": [("gather", "w_down")],
        "g_w_up": [("rs1", "down")],
        "d_h2": [("rs2", "down"), ("rs1", "up")],
        "ssd_gate_bwd": [("rs1", "out")],
        "ssd_bwd": [("rs2", "up"), ("rs2", "out")],
        "rs_adamw_down": [("rs1", "in")],
        "d_h1": [("rs2", "in")],
    }
    store = {"pos": pos}
    for tag, w in (("w_in", w_in), ("w_out", w_out), ("w_up", w_up), ("w_down", w_down)):
        store["shard_" + tag] = _cast_bf16("cast_" + tag, w[0])
    store["w_in"] = _run_jobs("gather_w_in", [_GatherJob(store["shard_w_in"])])[0]
    plan = _Plan(hosted, store, hooks={"after_g_w_in": lambda p: adamw(p, "down")})
    loss, grad_x, small = _local_step(
        plan, x[0], loss_target[0], mod, conv_w, ssm_conv_b, ssm_dt_bias_f, ssm_dt_bias_b,
        ssm_a_log_f, ssm_a_log_b, ssm_d, ssm_norm_w, sc_w, sc_norm_w, ln1_g, ln1_b, ln2_g, ln2_b)
    for tag in ("up", "out", "in"):
        adamw(plan, tag)

    parts = [small["dmod"]]
    parts += [_pad_lanes(small[n], _round_up(w, 128)) for n, w in _SUMMED]
    parts += [small["ssm_conv_w"].reshape(1, SSM_CONV * D_XBC), small["sc_conv_w"].reshape(1, SC_CONV * D_SC)]
    parts += [loss]
    gvec = jnp.concatenate(parts, axis=1)
    n_vec = _round_up(gvec.shape[1], 8192)
    gall = _gather_vec("gather_small_grads", _pad_lanes(gvec, n_vec))

    def shard_cols(full, k, per):
        return lax.dynamic_slice_in_dim(full.reshape(k, N_DEV, per), me, 1, axis=1).reshape(1, k * per)

    def placed(vals, n_rows=1):
        return jnp.concatenate(vals, axis=1)

    n_mod = N_MOD * D_MODEL
    ws, ms, vs = [b_ada], [m_b_ada], [v_b_ada]
    for n, w in _SUMMED:
        pw = _round_up(w, 128)
        ws.append(_pad_lanes(args[n], pw))
        ms.append(_pad_lanes(args["m_" + n], pw))
        vs.append(_pad_lanes(args["v_" + n], pw))

    def full_rows(shard, k, per):
        z = jnp.zeros((k, N_DEV, per), F32)
        z = lax.dynamic_update_slice_in_dim(z, shard.reshape(k, 1, per), me, axis=1)
        return z.reshape(1, k * N_DEV * per)

    for nm, k, per in (("ssm_conv_w", SSM_CONV, D_XBC // N_DEV), ("sc_conv_w", SC_CONV, D_SC // N_DEV)):
        ws.append(full_rows(args[nm][0], k, per))
        ms.append(full_rows(args["m_" + nm][0], k, per))
        vs.append(full_rows(args["v_" + nm][0], k, per))
    tail = n_vec - sum(a.shape[1] for a in ws)
    ws.append(jnp.zeros((1, tail), F32))
    ms.append(jnp.zeros((1, tail), F32))
    vs.append(jnp.ones((1, tail), F32))
    g_s, d_s, m_s, v_s = _sum8_adamw(gall, placed(ws), placed(ms), placed(vs))

    off = 0

    def take(w):
        nonlocal off
        sl = tuple(a[:, off:off + w] for a in (g_s, d_s, m_s, v_s))
        off += _round_up(w, 128)
        return sl

    out["b_ada"] = take(n_mod)
    for n, w in _SUMMED:
        out[n] = take(w)
    for nm, k, per in (("ssm_conv_w", SSM_CONV, D_XBC // N_DEV), ("sc_conv_w", SC_CONV, D_SC // N_DEV)):
        full = take(k * N_DEV * per)
        out[nm] = tuple(shard_cols(a, k, per).reshape(1, k, per) for a in full)
    loss_total = g_s[0, off]

    dmod_all = gall[:, :n_mod]
    dmod_cols = lax.dynamic_slice_in_dim(dmod_all.reshape(N_DEV, N_DEV, n_ada), me, 1, axis=1)
    dmod16 = jnp.pad(dmod_cols.reshape(N_DEV, n_ada), ((0, 8), (0, 0)))
    out["w_ada"] = tuple(a[None] for a in _ada_bwd_adamw(c16, dmod16, w_ada[0], m_w_ada[0], v_w_ada[0]))

    names = ['w_ada', 'b_ada', 'w_in', 'ssm_conv_w', 'ssm_conv_b', 'ssm_dt_bias_f', 'ssm_dt_bias_b',
             'ssm_a_log_f', 'ssm_a_log_b', 'ssm_d', 'ssm_norm_w', 'sc_conv_w', 'sc_norm_w', 'w_out',
             'ln1_g', 'ln1_b', 'w_up', 'w_down', 'ln2_g', 'ln2_b']
    res = [loss_total, grad_x[None]]
    for k in range(4):
        res += [out[n][k] for n in names]
    return tuple(res)
```

```python
import functools

import jax
import jax.numpy as jnp
from jax import lax
from jax.experimental import pallas as pl
from jax.experimental.pallas import tpu as pltpu

F32 = jnp.float32
BF16 = jnp.bfloat16
MESH = pl.DeviceIdType.MESH

N_DEV = 8
D_MODEL = 4096
D_SSM = 2048
D_SC = 2048
HEADS = 32
HEAD_DIM = 64
GROUPS = 8
GROUP_W = D_SSM // GROUPS
HEADS_PER_GROUP = 4
N_STATE = 128
CHUNK = 128
SSM_CONV = 5
SC_CONV = 3
SC_GROUP_W = 128
D_XBC = 4096
D_FF = 16384
D_IN = 12352
D_IN_SHARD = D_IN // N_DEV
D_MAIN = 12288
N_MOD = 6
ALPHA = (2 * 1) ** 0.25
LN_EPS = 1e-5
RMS_EPS = 1e-5
ADAM_LR = 0.001
ADAM_B1 = 0.9
ADAM_B2 = 0.999
ADAM_EPS = 1e-08
ADAM_WD = 0.01
ADAM_STEP = 10

VMEM_LIMIT = 56 * 1024 * 1024
HALO = 8

_DN = {
    "nn": (((1,), (0,)), ((), ())),
    "nt": (((1,), (1,)), ((), ())),
    "tn": (((0,), (0,)), ((), ())),
}


def _cparams(sem=None):
    return pltpu.CompilerParams(dimension_semantics=sem, vmem_limit_bytes=VMEM_LIMIT)


def _my_pos():
    return lax.axis_index("x"), lax.axis_index("y"), lax.axis_index("c")


def _other_chips(x, y):
    return [(1 - x, y), (x, 1 - y), (1 - x, 1 - y)]


class _GatherJob:
    n_remote = 7

    def __init__(self, shard, space=pl.ANY):
        self.ins = (shard,)
        self.out_shapes = (jax.ShapeDtypeStruct((N_DEV,) + shard.shape, shard.dtype),)
        self.space = space

    def _parts(self, ins, outs, send, recv, local):
        x_ref, out_ref = ins[0], outs[0]
        x, y, c = _my_pos()
        me, sibling = (x, y, c), (x, y, 1 - c)
        chips = _other_chips(x, y)

        def slab(px, py, pc):
            return out_ref.at[4 * px + 2 * py + pc]

        def copy(k, block, to, src=None):
            return pltpu.make_async_remote_copy(
                src_ref=slab(*block) if src is None else src, dst_ref=slab(*block),
                send_sem=send.at[k], recv_sem=recv.at[k], device_id=to, device_id_type=MESH)

        mine = pltpu.make_async_copy(x_ref, slab(*me), local.at[0])
        own = [copy(0, me, sibling, src=x_ref), copy(1, me, (*chips[0], c), src=x_ref),
               copy(2, me, (*chips[1], c), src=x_ref)]
        relayed = (x + (1 - c) * (1 - 2 * x), y + c * (1 - 2 * y), c)
        relay = copy(3, relayed, (x + c * (1 - 2 * x), y + (1 - c) * (1 - 2 * y), c))
        hand = [copy(4 + j, (*chip, c), sibling) for j, chip in enumerate(chips)]
        landed = [copy(1 + j, (*chip, c), me) for j, chip in enumerate(chips)]
        handed = [copy(0, sibling, me)] + [copy(4 + j, (*chip, 1 - c), me) for j, chip in enumerate(chips)]
        return mine, own, relay, hand, landed, handed

    def start(self, *refs):
        mine, own, _, _, _, _ = self._parts(*refs)
        mine.start()
        for cp in own:
            cp.start()

    def mid(self, *refs):
        _, _, relay, hand, landed, _ = self._parts(*refs)
        landed[0].wait_recv()
        landed[1].wait_recv()
        relay.start()
        hand[0].start()
        hand[1].start()

    def finish(self, *refs):
        mine, own, relay, hand, landed, handed = self._parts(*refs)
        landed[2].wait_recv()
        hand[2].start()
        for cp in handed:
            cp.wait_recv()
        for cp in own + [relay] + hand:
            cp.wait_send()
        mine.wait()


class _SiblingJob:
    n_remote = 4
    space = pl.ANY

    def __init__(self, g):
        self.ins = (g,)
        self.out_shapes = (jax.ShapeDtypeStruct((4,) + g.shape[1:], g.dtype),)

    def _copies(self, ins, outs, send, recv, local):
        x, y, c = _my_pos()
        return [pltpu.make_async_remote_copy(
            src_ref=ins[0].at[2 * j + (1 - c)], dst_ref=outs[0].at[j],
            send_sem=send.at[j], recv_sem=recv.at[j],
            device_id=(x, y, 1 - c), device_id_type=MESH) for j in range(4)]

    def start(self, *refs):
        for cp in self._copies(*refs):
            cp.start()

    def mid(self, *refs):
        pass

    def finish(self, *refs):
        for cp in self._copies(*refs):
            cp.wait()


class _ChipsJob:
    n_remote = 3
    space = pl.ANY

    def __init__(self, p):
        self.ins = (p,)
        self.out_shapes = (jax.ShapeDtypeStruct((3,) + p.shape[1:], p.dtype),)

    def _copies(self, ins, outs, send, recv, local):
        x, y, c = _my_pos()
        return [pltpu.make_async_remote_copy(
            src_ref=ins[0].at[2 * px + py], dst_ref=outs[0].at[k],
            send_sem=send.at[k], recv_sem=recv.at[k],
            device_id=(px, py, c), device_id_type=MESH) for k, (px, py) in enumerate(_other_chips(x, y))]

    def start(self, *refs):
        for cp in self._copies(*refs):
            cp.start()

    def mid(self, *refs):
        pass

    def finish(self, *refs):
        for cp in self._copies(*refs):
            cp.wait()


def _relay_route(x, y, c):
    first = (x + c * (1 - 2 * x), y + (1 - c) * (1 - 2 * y))
    second = (x + (1 - c) * (1 - 2 * x), y + c * (1 - 2 * y))
    return first, second


class _RelayFirstJob:
    n_remote = 2
    space = pl.ANY

    def __init__(self, p):
        self.ins = (p,)
        self.out_shapes = (jax.ShapeDtypeStruct((2,) + p.shape[1:], p.dtype),)

    def _copies(self, ins, outs, send, recv, local):
        x, y, c = _my_pos()
        (fx, fy), _ = _relay_route(x, y, c)
        slabs = [2 * (1 - x) + (1 - y), 2 * fx + fy]
        return [pltpu.make_async_remote_copy(
            src_ref=ins[0].at[slabs[k]], dst_ref=outs[0].at[k], send_sem=send.at[k], recv_sem=recv.at[k],
            device_id=(fx, fy, c), device_id_type=MESH) for k in range(2)]

    def start(self, *refs):
        for cp in self._copies(*refs):
            cp.start()

    def mid(self, *refs):
        pass

    def finish(self, *refs):
        for cp in self._copies(*refs):
            cp.wait()


class _RelaySecondJob:
    n_remote = 1
    space = pl.ANY

    def __init__(self, q):
        self.ins = (q,)
        self.out_shapes = (jax.ShapeDtypeStruct(q.shape, q.dtype),)

    def _copy(self, ins, outs, send, recv, local):
        x, y, c = _my_pos()
        _, (sx, sy) = _relay_route(x, y, c)
        return pltpu.make_async_remote_copy(
            src_ref=ins[0], dst_ref=outs[0], send_sem=send.at[0], recv_sem=recv.at[0],
            device_id=(sx, sy, c), device_id_type=MESH)

    def start(self, *refs):
        self._copy(*refs).start()

    def mid(self, *refs):
        pass

    def finish(self, *refs):
        self._copy(*refs).wait()


MID_STEP_FRACTION = 0.64


def _call(name, body, *, grid, in_specs, out_specs, out_shape, args, scratch_shapes=(), sem=None,
          jobs=(), n_prefetch=0):
    out_shape, out_specs, in_specs = list(out_shape), list(out_specs), list(in_specs)
    scratch_shapes = list(scratch_shapes)
    jobs = list(jobs)
    n_in, n_out, n_scr = len(in_specs), len(out_shape), len(scratch_shapes)
    job_ins = [a for j in jobs for a in j.ins]
    job_outs = [o for j in jobs for o in j.out_shapes]
    steps = 1
    for n in grid:
        steps *= n
    mid_step = min(steps - 1, int(steps * MID_STEP_FRACTION))

    def wrapped(*refs):
        pre, refs = refs[:n_prefetch], refs[n_prefetch:]
        core_in, refs = refs[:n_in], refs[n_in:]
        jin, refs = refs[:len(job_ins)], refs[len(job_ins):]
        core_out, refs = refs[:n_out], refs[n_out:]
        jout, refs = refs[:len(job_outs)], refs[len(job_outs):]
        core_scr, sems = refs[:n_scr], refs[n_scr:]
        lin = 0
        for ax, n in enumerate(grid):
            lin = lin * n + pl.program_id(ax)
        bound = []
        for j in jobs:
            ji, jin = jin[:len(j.ins)], jin[len(j.ins):]
            jo, jout = jout[:len(j.out_shapes)], jout[len(j.out_shapes):]
            (send, recv, local), sems = sems[:3], sems[3:]
            bound.append((j, (ji, jo, send, recv, local)))

        if jobs:
            @pl.when(lin == 0)
            def _():
                for j, r in bound:
                    j.start(*r)

        body(*pre, *core_in, *core_out, *core_scr)

        if jobs:
            @pl.when(lin == mid_step)
            def _():
                for j, r in bound:
                    j.mid(*r)

            @pl.when(lin == steps - 1)
            def _():
                for j, r in bound:
                    j.finish(*r)

    sem_shapes = []
    for j in jobs:
        sem_shapes += [pltpu.SemaphoreType.DMA((j.n_remote,)), pltpu.SemaphoreType.DMA((j.n_remote,)),
                       pltpu.SemaphoreType.DMA((1,))]
    if jobs:
        sem = tuple("arbitrary" for _ in grid)
    res = pl.pallas_call(
        wrapped, name=name,
        grid_spec=pltpu.PrefetchScalarGridSpec(
            num_scalar_prefetch=n_prefetch, grid=tuple(grid),
            in_specs=in_specs + [pl.BlockSpec(memory_space=j.space) for j in jobs for _ in j.ins],
            out_specs=out_specs + [pl.BlockSpec(memory_space=j.space) for j in jobs for _ in j.out_shapes],
            scratch_shapes=scratch_shapes + sem_shapes),
        out_shape=out_shape + job_outs,
        compiler_params=_cparams(sem),
    )(*args, *job_ins)
    res = list(res) if isinstance(res, (list, tuple)) else [res]
    return res[:n_out], res[n_out:]


def _run_jobs(name, jobs):
    return _call(name, lambda: None, grid=(1,), in_specs=[], out_specs=[], out_shape=[], args=(),
                 jobs=jobs)[1]


def _matmul(name, a, b, *, mode, grid, a_spec, b_spec, out_shapes, out_specs, acc_shape,
            epilogue=None, extras=(), extra_specs=(), jobs=()):
    nk = grid[2]
    n_extra = len(extras)
    n_out = len(out_shapes)

    def body(*refs):
        a_ref, b_ref = refs[0], refs[1]
        extra_refs = refs[2:2 + n_extra]
        out_refs = refs[2 + n_extra:2 + n_extra + n_out]
        part = lax.dot_general(a_ref[...], b_ref[...], _DN[mode], preferred_element_type=F32)

        def finish(acc):
            outs = epilogue(acc, *[r[...] for r in extra_refs]) if epilogue else (acc,)
            for o_ref, o in zip(out_refs, outs):
                o_ref[...] = o.astype(o_ref.dtype)

        if nk == 1:
            finish(part)
        else:
            acc_ref = refs[-1]
            k = pl.program_id(2)

            @pl.when(k == 0)
            def _():
                acc_ref[...] = part

            @pl.when(k > 0)
            def _():
                acc_ref[...] += part

            @pl.when(k == nk - 1)
            def _():
                finish(acc_ref[...])

    scratch = [pltpu.VMEM(acc_shape, F32)] if nk > 1 else []
    return _call(name, body, grid=grid, in_specs=[a_spec, b_spec, *extra_specs],
                 out_specs=out_specs, out_shape=out_shapes, scratch_shapes=scratch,
                 sem=("parallel", "parallel", "arbitrary"), args=(a, b, *extras), jobs=jobs)


def _tile(n, pref):
    t = min(n, pref)
    assert n % t == 0, (n, t)
    return t


def _mm_nn(name, a, b, out_dtype, tn=1024, tk=None, epilogue=None, out_dtypes=None, jobs=()):
    m, k = a.shape
    n = b.shape[1]
    tm, tn = _tile(m, 1024), _tile(n, tn)
    tk = _tile(k, tk or 4096)
    out_dtypes = out_dtypes or (out_dtype,)
    return _matmul(
        name, a, b, mode="nn", grid=(m // tm, n // tn, k // tk),
        a_spec=pl.BlockSpec((tm, tk), lambda i, j, kk: (i, kk)),
        b_spec=pl.BlockSpec((tk, tn), lambda i, j, kk: (kk, j)),
        out_shapes=[jax.ShapeDtypeStruct((m, n), dt) for dt in out_dtypes],
        out_specs=[pl.BlockSpec((tm, tn), lambda i, j, kk: (i, j)) for _ in out_dtypes],
        acc_shape=(tm, tn), epilogue=epilogue, jobs=jobs)


def _mm_nt(name, a, b, out_dtype, epilogue=None, extras=(), tk=None, jobs=()):
    m, k = a.shape
    n = b.shape[0]
    tm, tn = _tile(m, 1024), _tile(n, 1024)
    tk = _tile(k, tk or 4096)
    o_spec = pl.BlockSpec((tm, tn), lambda i, j, kk: (i, j))
    return _matmul(
        name, a, b, mode="nt", grid=(m // tm, n // tn, k // tk),
        a_spec=pl.BlockSpec((tm, tk), lambda i, j, kk: (i, kk)),
        b_spec=pl.BlockSpec((tn, tk), lambda i, j, kk: (j, kk)),
        out_shapes=[jax.ShapeDtypeStruct((m, n), out_dtype)],
        out_specs=[o_spec], acc_shape=(tm, tn), epilogue=epilogue,
        extras=extras, extra_specs=[o_spec for _ in extras], jobs=jobs)


def _mm_tn(name, a, b, out_dtype, tk=2048, jobs=()):
    k, m = a.shape
    n = b.shape[1]
    tm, tn = _tile(m, 1024), _tile(n, 1024)
    tk = _tile(k, tk)
    return _matmul(
        name, a, b, mode="tn", grid=(m // tm, n // tn, k // tk),
        a_spec=pl.BlockSpec((tk, tm), lambda i, j, kk: (kk, i)),
        b_spec=pl.BlockSpec((tk, tn), lambda i, j, kk: (kk, j)),
        out_shapes=[jax.ShapeDtypeStruct((m, n), out_dtype)],
        out_specs=[pl.BlockSpec((tm, tn), lambda i, j, kk: (i, j))],
        acc_shape=(tm, tn), jobs=jobs)


def _cast_bf16(name, w):
    r, c = w.shape
    tr = _tile(r, 512)

    def body(w_ref, o_ref):
        o_ref[...] = w_ref[...].astype(BF16)

    return pl.pallas_call(
        body, name=name, grid=(r // tr,),
        in_specs=[pl.BlockSpec((tr, c), lambda i: (i, 0))],
        out_specs=pl.BlockSpec((tr, c), lambda i: (i, 0)),
        out_shape=jax.ShapeDtypeStruct((r, c), BF16),
        compiler_params=_cparams(("parallel",)),
    )(w)


def _pair_add(name, g, r1, pos):
    _, r, cdim = g.shape
    tr = _tile(r, 512)

    def body(pos_ref, g_ref, r_ref, o_ref):
        o_ref[...] = (g_ref[...].astype(F32) + r_ref[...].astype(F32)).astype(o_ref.dtype)

    return pl.pallas_call(
        body, name=name,
        grid_spec=pltpu.PrefetchScalarGridSpec(
            num_scalar_prefetch=1, grid=(4, r // tr),
            in_specs=[pl.BlockSpec((None, tr, cdim), lambda j, i, pos: (2 * j + pos[2], i, 0)),
                      pl.BlockSpec((None, tr, cdim), lambda j, i, pos: (j, i, 0))],
            out_specs=pl.BlockSpec((None, tr, cdim), lambda j, i, pos: (j, i, 0))),
        out_shape=jax.ShapeDtypeStruct((4, r, cdim), BF16),
        compiler_params=_cparams(("parallel", "parallel")),
    )(pos, g, r1)


def _adamw_math(w, g, m, v):
    m = ADAM_B1 * m + (1.0 - ADAM_B1) * g
    v = ADAM_B2 * v + (1.0 - ADAM_B2) * jnp.square(g)
    m_hat = m / (1.0 - ADAM_B1 ** ADAM_STEP)
    v_hat = v / (1.0 - ADAM_B2 ** ADAM_STEP)
    delta = -ADAM_LR * (m_hat / (jnp.sqrt(v_hat) + ADAM_EPS) + ADAM_WD * w)
    return delta, m, v


def _relay_add(name, p, ra, pos):
    _, r, cdim = p.shape
    tr = _tile(r, 512)

    def second(pos):
        (_, _), (sx, sy) = _relay_route(pos[0], pos[1], pos[2])
        return 2 * sx + sy

    def body(pos_ref, p_ref, r_ref, o_ref):
        o_ref[...] = (p_ref[...].astype(F32) + r_ref[...].astype(F32)).astype(o_ref.dtype)

    return pl.pallas_call(
        body, name=name,
        grid_spec=pltpu.PrefetchScalarGridSpec(
            num_scalar_prefetch=1, grid=(r // tr,),
            in_specs=[pl.BlockSpec((None, tr, cdim), lambda i, pos: (second(pos), i, 0)),
                      pl.BlockSpec((None, tr, cdim), lambda i, pos: (0, i, 0))],
            out_specs=pl.BlockSpec((tr, cdim), lambda i, pos: (i, 0))),
        out_shape=jax.ShapeDtypeStruct((r, cdim), BF16),
        compiler_params=_cparams(("parallel",)),
    )(pos, p, ra)


def _reduce_adamw(name, p, others, pos, w, m, v, jobs=()):
    r, cdim = w.shape
    tr = _tile(r, 128 if cdim >= D_MODEL else 256)
    blk = pl.BlockSpec((tr, cdim), lambda i, pos: (i, 0))
    n_other = len(others)

    def body(pos_ref, p_ref, *refs):
        other_refs, (w_ref, m_ref, v_ref, g_out, d_out, m_out, v_out) = refs[:n_other], refs[n_other:]
        g = p_ref[...].astype(F32)
        for o_ref in other_refs:
            g = g + o_ref[...].astype(F32)
        d, mn, vn = _adamw_math(w_ref[...], g, m_ref[...], v_ref[...])
        g_out[...] = g
        d_out[...] = d
        m_out[...] = mn
        v_out[...] = vn

    def other_spec(lead):
        if lead is None:
            return blk
        return pl.BlockSpec((None, tr, cdim), lambda i, pos: (lead, i, 0))

    shp = jax.ShapeDtypeStruct((r, cdim), F32)
    return _call(
        name, body, grid=(r // tr,), n_prefetch=1,
        in_specs=[pl.BlockSpec((None, tr, cdim), lambda i, pos: (2 * pos[0] + pos[1], i, 0))]
        + [other_spec(lead) for _, lead in others] + [blk, blk, blk],
        out_specs=[blk, blk, blk, blk], out_shape=[shp, shp, shp, shp],
        sem=("parallel",), args=(pos, p, *[a for a, _ in others], w, m, v), jobs=jobs)


def _row_spec(t, width=D_MODEL):
    return pl.BlockSpec((t, width), lambda i: (i, 0))


def _full_spec(shape):
    return pl.BlockSpec(shape, lambda i: tuple(0 for _ in shape))


def _ln_stats(p):
    mu = jnp.mean(p, axis=-1, keepdims=True)
    xc = p - mu
    var = jnp.mean(xc * xc, axis=-1, keepdims=True)
    rstd = lax.rsqrt(var + LN_EPS)
    return xc * rstd, rstd


def _ln_bwd(dy, xhat, rstd, g):
    dxh = dy * g
    m1 = jnp.mean(dxh, axis=-1, keepdims=True)
    m2 = jnp.mean(dxh * xhat, axis=-1, keepdims=True)
    return rstd * (dxh - m1 - xhat * m2)


def _acc_rows(ref, val, first):
    s = jnp.sum(val, axis=0, keepdims=True)

    @pl.when(first)
    def _():
        ref[...] = s

    @pl.when(jnp.logical_not(first))
    def _():
        ref[...] += s


def _modulate(name, x, mod6):
    s = x.shape[0]
    t = _tile(s, 256)

    def body(x_ref, mod_ref, o_ref):
        o_ref[...] = (x_ref[...] * (1.0 + mod_ref[1:2, :]) + mod_ref[0:1, :]).astype(BF16)

    return pl.pallas_call(
        body, name=name, grid=(s // t,),
        in_specs=[_row_spec(t), _full_spec((N_MOD, D_MODEL))],
        out_specs=_row_spec(t),
        out_shape=jax.ShapeDtypeStruct((s, D_MODEL), BF16),
        compiler_params=_cparams(("parallel",)),
    )(x, mod6)


def _ln1_fwd(x, mix, mod6, g, b):
    s = x.shape[0]
    t = _tile(s, 256)

    def body(x_ref, mix_ref, mod_ref, g_ref, b_ref, x1_ref, h2_ref):
        pre = ALPHA * x_ref[...] + (1.0 + mod_ref[2:3, :]) * mix_ref[...]
        xhat, _ = _ln_stats(pre)
        x1 = xhat * g_ref[...] + b_ref[...]
        x1_ref[...] = x1
        h2_ref[...] = (x1 * (1.0 + mod_ref[4:5, :]) + mod_ref[3:4, :]).astype(BF16)

    vec = _full_spec((1, D_MODEL))
    return pl.pallas_call(
        body, name="ln1_fwd", grid=(s // t,),
        in_specs=[_row_spec(t), _row_spec(t), _full_spec((N_MOD, D_MODEL)), vec, vec],
        out_specs=[_row_spec(t), _row_spec(t)],
        out_shape=[jax.ShapeDtypeStruct((s, D_MODEL), F32), jax.ShapeDtypeStruct((s, D_MODEL), BF16)],
        compiler_params=_cparams(("parallel",)),
    )(x, mix, mod6, g, b)


def _ln2_loss_bwd(x1, f2, tgt, mod6, g, b):
    s = x1.shape[0]
    t = _tile(s, 128)

    def body(x1_ref, f2_ref, tgt_ref, mod_ref, g_ref, b_ref,
             df2_ref, dx1_ref, loss_ref, dg_ref, db_ref, dgate_ref):
        first = pl.program_id(0) == 0
        gate = 1.0 + mod_ref[5:6, :]
        f2v = f2_ref[...]
        pre = ALPHA * x1_ref[...] + gate * f2v
        xhat, rstd = _ln_stats(pre)
        err = xhat * g_ref[...] + b_ref[...] - tgt_ref[...]
        part = 0.5 * jnp.sum(jnp.mean(err * err, axis=-1, keepdims=True), axis=0, keepdims=True)
        dy = err / D_MODEL
        dpre = _ln_bwd(dy, xhat, rstd, g_ref[...])
        df2_ref[...] = (gate * dpre).astype(BF16)
        dx1_ref[...] = ALPHA * dpre
        _acc_rows(loss_ref, jnp.broadcast_to(part, (1, 128)), first)
        _acc_rows(dg_ref, dy * xhat, first)
        _acc_rows(db_ref, dy, first)
        _acc_rows(dgate_ref, dpre * f2v, first)

    vec = _full_spec((1, D_MODEL))
    vshape = jax.ShapeDtypeStruct((1, D_MODEL), F32)
    return pl.pallas_call(
        body, name="ln2_loss_bwd", grid=(s // t,),
        in_specs=[_row_spec(t), _row_spec(t), _row_spec(t), _full_spec((N_MOD, D_MODEL)), vec, vec],
        out_specs=[_row_spec(t), _row_spec(t), _full_spec((1, 128)), vec, vec, vec],
        out_shape=[jax.ShapeDtypeStruct((s, D_MODEL), BF16), jax.ShapeDtypeStruct((s, D_MODEL), F32),
                   jax.ShapeDtypeStruct((1, 128), F32), vshape, vshape, vshape],
        compiler_params=_cparams(("arbitrary",)),
    )(x1, f2, tgt, mod6, g, b)


def _ln1_bwd(dh2, dx1a, x1, x, mix, mod6, g):
    s = x.shape[0]
    t = _tile(s, 128)

    def body(dh2_ref, dx1a_ref, x1_ref, x_ref, mix_ref, mod_ref, g_ref,
             dmix_ref, dxa_ref, dscale_ref, dshift_ref, dg_ref, db_ref, dgate_ref):
        first = pl.program_id(0) == 0
        dh2v = dh2_ref[...]
        dx1 = dx1a_ref[...] + dh2v * (1.0 + mod_ref[4:5, :])
        gate = 1.0 + mod_ref[2:3, :]
        mixv = mix_ref[...]
        pre = ALPHA * x_ref[...] + gate * mixv
        xhat, rstd = _ln_stats(pre)
        dpre = _ln_bwd(dx1, xhat, rstd, g_ref[...])
        dmix_ref[...] = (gate * dpre).astype(BF16)
        dxa_ref[...] = ALPHA * dpre
        _acc_rows(dscale_ref, dh2v * x1_ref[...], first)
        _acc_rows(dshift_ref, dh2v, first)
        _acc_rows(dg_ref, dx1 * xhat, first)
        _acc_rows(db_ref, dx1, first)
        _acc_rows(dgate_ref, dpre * mixv, first)

    vec = _full_spec((1, D_MODEL))
    vshape = jax.ShapeDtypeStruct((1, D_MODEL), F32)
    return pl.pallas_call(
        body, name="ln1_bwd", grid=(s // t,),
        in_specs=[_row_spec(t)] * 5 + [_full_spec((N_MOD, D_MODEL)), vec],
        out_specs=[_row_spec(t), _row_spec(t), vec, vec, vec, vec, vec],
        out_shape=[jax.ShapeDtypeStruct((s, D_MODEL), BF16), jax.ShapeDtypeStruct((s, D_MODEL), F32),
                   vshape, vshape, vshape, vshape, vshape],
        compiler_params=_cparams(("arbitrary",)),
    )(dh2, dx1a, x1, x, mix, mod6, g)


def _grad_x(dxa, dh1, x, mod6):
    s = x.shape[0]
    t = _tile(s, 256)

    def body(dxa_ref, dh1_ref, x_ref, mod_ref, gx_ref, dscale_ref, dshift_ref):
        first = pl.program_id(0) == 0
        dh1v = dh1_ref[...]
        gx_ref[...] = dxa_ref[...] + dh1v * (1.0 + mod_ref[1:2, :])
        _acc_rows(dscale_ref, dh1v * x_ref[...], first)
        _acc_rows(dshift_ref, dh1v, first)

    vec = _full_spec((1, D_MODEL))
    vshape = jax.ShapeDtypeStruct((1, D_MODEL), F32)
    return pl.pallas_call(
        body, name="grad_x", grid=(s // t,),
        in_specs=[_row_spec(t)] * 3 + [_full_spec((N_MOD, D_MODEL))],
        out_specs=[_row_spec(t), vec, vec],
        out_shape=[jax.ShapeDtypeStruct((s, D_MODEL), F32), vshape, vshape],
        compiler_params=_cparams(("arbitrary",)),
    )(dxa, dh1, x, mod6)


def _window(ref, i, t, s):
    r0 = pl.multiple_of(i * t, t)
    cur = ref[pl.ds(r0, t), :]
    lo = pl.multiple_of(jnp.maximum(r0 - HALO, 0), HALO)
    hi = pl.multiple_of(jnp.minimum(r0 + t, s - HALO), HALO)
    before = ref[pl.ds(lo, HALO), :] * (i > 0).astype(F32)
    after = ref[pl.ds(hi, HALO), :] * (i < s // t - 1).astype(F32)
    return jnp.concatenate([before, cur, after], axis=0)


def _tap(ext, shift):
    n = ext.shape[0]
    if shift == 0:
        return ext
    return pltpu.roll(ext, (-shift) % n, 0)


def _centre(ext, t):
    return ext[HALO:HALO + t]


def _conv_taps(ext, w, width):
    acc = None
    for k in range(width):
        term = _tap(ext, k - width // 2) * w[k:k + 1, :]
        acc = term if acc is None else acc + term
    return acc


def _silu(a):
    return a * jax.nn.sigmoid(a)


def _conv_silu_fwd(proj, w, b):
    s = proj.shape[0]
    cb = 256
    t = _tile(s, 256)
    off = D_SSM // cb

    def body(u_ref, w_ref, b_ref, o_ref):
        wv = w_ref[...]
        bv = b_ref[...]

        def step(i, carry):
            ext = _window(u_ref, i, t, s)
            a = _centre(_conv_taps(ext, wv, SSM_CONV), t) + bv
            o_ref[pl.ds(pl.multiple_of(i * t, t), t), :] = _silu(a)
            return carry

        lax.fori_loop(0, s // t, step, 0)

    return pl.pallas_call(
        body, name="conv_silu_fwd", grid=(D_XBC // cb,),
        in_specs=[pl.BlockSpec((s, cb), lambda j: (0, off + j)),
                  pl.BlockSpec((SSM_CONV, cb), lambda j: (0, j)),
                  pl.BlockSpec((1, cb), lambda j: (0, j))],
        out_specs=pl.BlockSpec((s, cb), lambda j: (0, j)),
        out_shape=jax.ShapeDtypeStruct((s, D_XBC), F32),
        compiler_params=_cparams(("parallel",)),
    )(proj, w, b)


def _conv_silu_bwd(name, proj, w, b, col0, ncols, cots, scaled=None):
    s = proj.shape[0]
    cb = 128
    t = _tile(s, 256)
    off = (D_SSM + col0) // cb
    woff = col0 // cb
    n_cot = len(cots)

    def body(*refs):
        u_ref, w_ref, b_ref = refs[:3]
        cot_refs = refs[3:3 + n_cot]
        sc_refs = refs[3 + n_cot:3 + n_cot + (2 if scaled else 0)]
        du_ref, dw_ref, db_ref = refs[-3:]
        wv = w_ref[...]
        bv = b_ref[...]

        def step(i, carry):
            ext = _window(u_ref, i, t, s)
            a = _conv_taps(ext, wv, SSM_CONV) + bv
            cot = None
            for cr in cot_refs:
                term = _window(cr.at[0], i, t, s) + _window(cr.at[1], i, t, s)
                cot = term if cot is None else cot + term
            if scaled:
                cot = cot + _window(sc_refs[0], i, t, s) * sc_refs[1][...]
            sig = jax.nn.sigmoid(a)
            da = cot * (sig * (1.0 + a * (1.0 - sig)))
            du = None
            new = []
            for k in range(SSM_CONV):
                sh = k - SSM_CONV // 2
                term = _tap(da, -sh) * wv[k:k + 1, :]
                du = term if du is None else du + term
                prod = _centre(_tap(ext, sh) * da, t)
                new.append(carry[k] + jnp.sum(prod, axis=0, keepdims=True))
            new.append(carry[SSM_CONV] + jnp.sum(_centre(da, t), axis=0, keepdims=True))
            du_ref[pl.ds(pl.multiple_of(i * t, t), t), :] = _centre(du, t).astype(BF16)
            return tuple(new)

        zero = jnp.zeros((1, cb), F32)
        acc = lax.fori_loop(0, s // t, step, tuple(zero for _ in range(SSM_CONV + 1)))
        for k in range(SSM_CONV):
            dw_ref[k:k + 1, :] = acc[k]
        db_ref[...] = acc[SSM_CONV]

    in_specs = [pl.BlockSpec((s, cb), lambda j: (0, off + j)),
                pl.BlockSpec((SSM_CONV, cb), lambda j: (0, woff + j)),
                pl.BlockSpec((1, cb), lambda j: (0, woff + j))]
    in_specs += [pl.BlockSpec((2, s, cb), lambda j: (0, 0, j)) for _ in cots]
    args = [proj, w, b, *cots]
    if scaled:
        in_specs += [pl.BlockSpec((s, cb), lambda j: (0, j)), pl.BlockSpec((1, cb), lambda j: (0, j))]
        args += list(scaled)
    return pl.pallas_call(
        body, name=name, grid=(ncols // cb,),
        in_specs=in_specs,
        out_specs=[pl.BlockSpec((s, cb), lambda j: (0, j)),
                   pl.BlockSpec((SSM_CONV, cb), lambda j: (0, j)),
                   pl.BlockSpec((1, cb), lambda j: (0, j))],
        out_shape=[jax.ShapeDtypeStruct((s, ncols), BF16),
                   jax.ShapeDtypeStruct((SSM_CONV, ncols), F32),
                   jax.ShapeDtypeStruct((1, ncols), F32)],
        compiler_params=_cparams(("parallel",)),
    )(*args)


_SC_H = (D_SSM + D_XBC) // SC_GROUP_W
_SC_B = _SC_H + D_SC // SC_GROUP_W
_SC_C = _SC_B + D_SC // SC_GROUP_W


def _sc_fwd(proj, w, nw):
    s = proj.shape[0]
    cb = SC_GROUP_W
    t = _tile(s, 256)

    def body(uh_ref, ub_ref, uc_ref, w_ref, nw_ref, o_ref):
        wv = w_ref[...]
        nwv = nw_ref[...]

        def step(i, carry):
            p = _window(uc_ref, i, t, s) * _window(uh_ref, i, t, s)
            cv = _centre(_conv_taps(p, wv, SC_CONV), t)
            rows = pl.ds(pl.multiple_of(i * t, t), t)
            y = ub_ref[rows, :] * cv
            r = lax.rsqrt(jnp.mean(y * y, axis=-1, keepdims=True) + RMS_EPS)
            o_ref[rows, :] = (y * r * nwv).astype(BF16)
            return carry

        lax.fori_loop(0, s // t, step, 0)

    def col(base):
        return pl.BlockSpec((s, cb), lambda j: (0, base + j))

    return pl.pallas_call(
        body, name="sc_fwd", grid=(D_SC // cb,),
        in_specs=[col(_SC_H), col(_SC_B), col(_SC_C),
                  pl.BlockSpec((SC_CONV, cb), lambda j: (0, j)),
                  pl.BlockSpec((1, cb), lambda j: (0, j))],
        out_specs=pl.BlockSpec((s, cb), lambda j: (0, j)),
        out_shape=jax.ShapeDtypeStruct((s, D_SC), BF16),
        compiler_params=_cparams(("parallel",)),
    )(proj, proj, proj, w, nw)


def _sc_bwd(proj, dycat, w, nw):
    s = proj.shape[0]
    cb = SC_GROUP_W
    t = _tile(s, 256)
    dy_off = D_SSM // cb

    def body(uh_ref, ub_ref, uc_ref, dy_ref, w_ref, nw_ref, duh_ref, dub_ref, duc_ref, dw_ref, dnw_ref):
        wv = w_ref[...]
        nwv = nw_ref[...]

        def step(i, carry):
            uh = _window(uh_ref, i, t, s)
            ub = _window(ub_ref, i, t, s)
            uc = _window(uc_ref, i, t, s)
            do = _window(dy_ref, i, t, s)
            p = uc * uh
            cv = _conv_taps(p, wv, SC_CONV)
            y = ub * cv
            r = lax.rsqrt(jnp.mean(y * y, axis=-1, keepdims=True) + RMS_EPS)
            dyr = do * nwv
            dy = r * dyr - y * (r * r * r) * jnp.mean(dyr * y, axis=-1, keepdims=True)
            dcv = dy * ub
            dp = None
            new = []
            for k in range(SC_CONV):
                sh = k - SC_CONV // 2
                term = _tap(dcv, -sh) * wv[k:k + 1, :]
                dp = term if dp is None else dp + term
                new.append(carry[k] + jnp.sum(_centre(_tap(p, sh) * dcv, t), axis=0, keepdims=True))
            new.append(carry[SC_CONV] + jnp.sum(_centre(do * y * r, t), axis=0, keepdims=True))
            rows = pl.ds(pl.multiple_of(i * t, t), t)
            duh_ref[rows, :] = _centre(dp * uc, t).astype(BF16)
            duc_ref[rows, :] = _centre(dp * uh, t).astype(BF16)
            dub_ref[rows, :] = _centre(dy * cv, t).astype(BF16)
            return tuple(new)

        zero = jnp.zeros((1, cb), F32)
        acc = lax.fori_loop(0, s // t, step, tuple(zero for _ in range(SC_CONV + 1)))
        for k in range(SC_CONV):
            dw_ref[k:k + 1, :] = acc[k]
        dnw_ref[...] = acc[SC_CONV]

    def col(base):
        return pl.BlockSpec((s, cb), lambda j: (0, base + j))

    out_col = pl.BlockSpec((s, cb), lambda j: (0, j))
    act = jax.ShapeDtypeStruct((s, D_SC), BF16)
    return pl.pallas_call(
        body, name="sc_bwd", grid=(D_SC // cb,),
        in_specs=[col(_SC_H), col(_SC_B), col(_SC_C), col(dy_off),
                  pl.BlockSpec((SC_CONV, cb), lambda j: (0, j)),
                  pl.BlockSpec((1, cb), lambda j: (0, j))],
        out_specs=[out_col, out_col, out_col,
                   pl.BlockSpec((SC_CONV, cb), lambda j: (0, j)),
                   pl.BlockSpec((1, cb), lambda j: (0, j))],
        out_shape=[act, act, act, jax.ShapeDtypeStruct((SC_CONV, D_SC), F32),
                   jax.ShapeDtypeStruct((1, D_SC), F32)],
        compiler_params=_cparams(("parallel",)),
    )(proj, proj, proj, dycat, w, nw)


def _make_select_dot(differentiable):
    def raw(a, b, mode, const):
        ops = [a, b]
        v = ops[1 - const]
        acc = None
        for _ in range(3):
            piece = v.astype(BF16)
            v = v - piece.astype(F32)
            ops[1 - const] = piece
            part = lax.dot_general(ops[0].astype(BF16), ops[1].astype(BF16), _DN[mode],
                                   preferred_element_type=F32)
            acc = part if acc is None else acc + part
        return acc

    if not differentiable:
        return raw

    @functools.partial(jax.custom_vjp, nondiff_argnums=(2, 3))
    def dot(a, b, mode, const):
        return raw(a, b, mode, const)

    def fwd(a, b, mode, const):
        return raw(a, b, mode, const), (a, b)

    def bwd(mode, const, res, g):
        a, b = res
        assert mode == "nn"
        if const == 1:
            return raw(g, b, "nt", 1), jnp.zeros_like(b)
        return jnp.zeros_like(a), raw(a, g, "tn", 0)

    dot.defvjp(fwd, bwd)
    return dot


def _make_dot(differentiable):
    def raw(a, b, mode):
        return lax.dot_general(a.astype(BF16), b.astype(BF16), _DN[mode], preferred_element_type=F32)

    if not differentiable:
        return raw

    @functools.partial(jax.custom_vjp, nondiff_argnums=(2,))
    def dot(a, b, mode):
        return raw(a, b, mode)

    def fwd(a, b, mode):
        return raw(a, b, mode), (a, b)

    def bwd(mode, res, g):
        a, b = res
        if mode == "nn":
            return raw(g, b, "nt"), raw(a, g, "tn")
        if mode == "nt":
            return raw(g, b, "nn"), raw(g, a, "tn")
        return raw(b, g, "nt"), raw(a, g, "nn")

    dot.defvjp(fwd, bwd)
    return dot


def _make_swap(differentiable):
    def raw(v):
        return pltpu.roll(v, HEAD_DIM, 1)

    if not differentiable:
        return raw
    swap = jax.custom_vjp(raw)
    swap.defvjp(lambda v: (raw(v), None), lambda _, g: (raw(g),))
    return swap


def _ssd_chunk(xs, bm, cm, dtx, acx, ax, prev, tri, differentiable):
    _bdot = _make_dot(differentiable)
    swap = _make_swap(differentiable)
    atx = jnp.sum(dtx * ax, axis=0, keepdims=True)
    xdt = xs * dtx
    mask = tri > 0.0
    scores = _bdot(cm, bm, "nt")
    head = lax.broadcasted_iota(jnp.int32, (1, GROUP_W), 1) // HEAD_DIM
    low = lax.broadcasted_iota(jnp.int32, (1, 128), 1) < HEAD_DIM
    y = _bdot(cm, prev, "nn") * jnp.exp(acx)
    for h in range(HEADS_PER_GROUP):
        pair = acx[:, 128 * (h // 2):128 * (h // 2) + 128]
        other = swap(pair)
        m1 = jnp.where(low, pair, other) if h % 2 == 0 else jnp.where(low, other, pair)
        seg = m1 - m1.T
        decay = jnp.where(mask, jnp.exp(jnp.where(mask, seg, 0.0)), 0.0)
        xh = xdt * (head == h).astype(F32)
        y = y + _bdot(scores * decay, xh, "nn")
    new = prev * jnp.exp(atx) + _bdot(bm, xdt * jnp.exp(atx - acx), "tn")
    return y, new


def _softplus(v):
    return jnp.maximum(v, 0.0) + jnp.log(1.0 + jnp.exp(-jnp.abs(v)))


def _dt_spread(u, bias, a, tri2, exf, differentiable):
    sel = _make_select_dot(differentiable)
    dt = _softplus(u + bias)
    dta = dt * a
    out = []
    for d in range(2):
        acum = sel(tri2[d], dta, "nn", 0)
        out += [sel(dt, exf[d], "nn", 1), sel(acum, exf[d], "nn", 1)]
    return tuple(out)


def _ssd_consts():
    q = CHUNK
    r = lax.broadcasted_iota(jnp.int32, (q, q), 0)
    c = lax.broadcasted_iota(jnp.int32, (q, q), 1)
    tri = jnp.stack([(c <= r), (c >= r)]).astype(F32)
    shp = (2, 128, D_SSM)
    src = lax.broadcasted_iota(jnp.int32, shp, 1)
    d = lax.broadcasted_iota(jnp.int32, shp, 0)
    col = lax.broadcasted_iota(jnp.int32, shp, 2)
    exf = (src == d * HEADS + col // HEAD_DIM).astype(F32)
    return tri, exf


def _dt_prep(proj_dt, bias_all, a_all):
    s = proj_dt.shape[0]
    tri, exf = _ssd_consts()

    def body(u_ref, b_ref, a_ref, tri_ref, exf_ref, dtx_ref, acx_ref):
        dtx0, acx0, dtx1, acx1 = _dt_spread(u_ref[...], b_ref[...], a_ref[...], tri_ref[...],
                                            exf_ref[...], False)
        dtx_ref[0] = dtx0
        dtx_ref[1] = dtx1
        acx_ref[0] = acx0
        acx_ref[1] = acx1

    out = pl.BlockSpec((2, CHUNK, D_SSM), lambda i: (0, i, 0))
    shp = jax.ShapeDtypeStruct((2, s, D_SSM), F32)
    return pl.pallas_call(
        body, name="dt_prep", grid=(s // CHUNK,),
        in_specs=[_row_spec(CHUNK, 128), _full_spec((1, 128)), _full_spec((1, 128)),
                  _full_spec((2, CHUNK, CHUNK)), _full_spec((2, 128, D_SSM))],
        out_specs=[out, out], out_shape=[shp, shp],
        compiler_params=_cparams(("parallel",)),
    )(proj_dt, bias_all, a_all, tri, exf)


def _dt_prep_bwd(proj_dt, bias_all, a_all, d_dtx, d_acx):
    s = proj_dt.shape[0]
    tri, exf = _ssd_consts()

    def body(u_ref, b_ref, a_ref, tri_ref, exf_ref, ddtx_ref, dacx_ref, du_ref, db_ref, da_ref):
        tri_v, exf_v = tri_ref[...], exf_ref[...]

        def f(u, bias, a):
            return _dt_spread(u, bias, a, tri_v, exf_v, True)

        _, vjp = jax.vjp(f, u_ref[...], b_ref[...], a_ref[...])
        du, db, da = vjp((ddtx_ref[0], dacx_ref[0], ddtx_ref[1], dacx_ref[1]))
        du_ref[...] = du.astype(BF16)
        first = pl.program_id(0) == 0
        _acc_rows(db_ref, db, first)
        _acc_rows(da_ref, da, first)

    cot = pl.BlockSpec((2, CHUNK, D_SSM), lambda i: (0, i, 0))
    vec = _full_spec((1, 128))
    return pl.pallas_call(
        body, name="dt_prep_bwd", grid=(s // CHUNK,),
        in_specs=[_row_spec(CHUNK, 128), vec, vec, _full_spec((2, CHUNK, CHUNK)),
                  _full_spec((2, 128, D_SSM)), cot, cot],
        out_specs=[_row_spec(CHUNK, 128), vec, vec],
        out_shape=[jax.ShapeDtypeStruct((s, 128), BF16), jax.ShapeDtypeStruct((1, 128), F32),
                   jax.ShapeDtypeStruct((1, 128), F32)],
        compiler_params=_cparams(("arbitrary",)),
    )(proj_dt, bias_all, a_all, tri, exf, d_dtx, d_acx)


def _ssd_specs(chunk_of):
    q = CHUNK
    xs = pl.BlockSpec((q, GROUP_W), lambda d, g, ci: (chunk_of(d, ci), g))
    bm = pl.BlockSpec((q, N_STATE), lambda d, g, ci: (chunk_of(d, ci), D_SSM // N_STATE + g))
    cm = pl.BlockSpec((q, N_STATE), lambda d, g, ci: (chunk_of(d, ci), D_SSM // N_STATE + GROUPS + g))
    spread = pl.BlockSpec((None, q, GROUP_W), lambda d, g, ci: (d, chunk_of(d, ci), g))
    ax = pl.BlockSpec((None, 1, GROUP_W), lambda d, g, ci: (d, 0, g))
    tri = pl.BlockSpec((None, q, q), lambda d, g, ci: (d, 0, 0))
    st = pl.BlockSpec((None, None, None, N_STATE, GROUP_W), lambda d, g, ci: (d, chunk_of(d, ci), g, 0, 0))
    return xs, bm, cm, spread, ax, tri, st


def _ssd_fwd(xbc, dtx, acx, ax, jobs=()):
    s = xbc.shape[0]
    nc = s // CHUNK
    tri, _ = _ssd_consts()

    def chunk_of(d, ci):
        return ci + d * (nc - 1 - 2 * ci)

    def body(xs_ref, b_ref, c_ref, dtx_ref, acx_ref, ax_ref, tri_ref, y_ref, st_ref, state):
        @pl.when(pl.program_id(2) == 0)
        def _():
            state[...] = jnp.zeros((N_STATE, GROUP_W), F32)

        prev = state[...]
        st_ref[...] = prev
        y, new = _ssd_chunk(xs_ref[...], b_ref[...], c_ref[...], dtx_ref[...], acx_ref[...],
                            ax_ref[...], prev, tri_ref[...], False)
        y_ref[...] = y
        state[...] = new

    xs, bm, cm, spread, ax_s, tri_s, st = _ssd_specs(chunk_of)
    return _call(
        "ssd_fwd", body, grid=(2, GROUPS, nc),
        in_specs=[xs, bm, cm, spread, spread, ax_s, tri_s],
        out_specs=[spread, st],
        out_shape=[jax.ShapeDtypeStruct((2, s, D_SSM), F32),
                   jax.ShapeDtypeStruct((2, nc, GROUPS, N_STATE, GROUP_W), F32)],
        scratch_shapes=[pltpu.VMEM((N_STATE, GROUP_W), F32)],
        sem=("arbitrary", "arbitrary", "arbitrary"),
        args=(xbc, xbc, xbc, dtx, acx, ax, tri), jobs=jobs)


def _ssd_bwd(xbc, dtx, acx, ax, states, dy, jobs=()):
    s = xbc.shape[0]
    nc = s // CHUNK
    tri, _ = _ssd_consts()

    def chunk_of(d, ci):
        return (nc - 1 - ci) + d * (2 * ci - (nc - 1))

    def body(xs_ref, b_ref, c_ref, dtx_ref, acx_ref, ax_ref, tri_ref, st_ref, dy_ref,
             dxs_ref, db_ref, dc_ref, ddtx_ref, dacx_ref, dax_ref, dstate):
        first = pl.program_id(2) == 0

        @pl.when(first)
        def _():
            dstate[...] = jnp.zeros((N_STATE, GROUP_W), F32)

        tri_v = tri_ref[...]

        def f(xs, bm, cm, dtx_v, acx_v, ax_v, prev):
            return _ssd_chunk(xs, bm, cm, dtx_v, acx_v, ax_v, prev, tri_v, True)

        _, vjp = jax.vjp(f, xs_ref[...], b_ref[...], c_ref[...], dtx_ref[...], acx_ref[...],
                         ax_ref[...], st_ref[...])
        dxs, dbm, dcm, ddtx, dacx, dax, dprev = vjp((dy_ref[...], dstate[...]))
        dxs_ref[...] = dxs
        db_ref[...] = dbm
        dc_ref[...] = dcm
        ddtx_ref[...] = ddtx
        dacx_ref[...] = dacx
        dstate[...] = dprev
        _acc_rows(dax_ref, dax, first)

    xs, bm, cm, spread, ax_s, tri_s, st = _ssd_specs(chunk_of)
    dy_s = pl.BlockSpec((CHUNK, GROUP_W), lambda d, g, ci: (chunk_of(d, ci), g))
    bc_s = pl.BlockSpec((None, CHUNK, N_STATE), lambda d, g, ci: (d, chunk_of(d, ci), g))
    wide = jax.ShapeDtypeStruct((2, s, D_SSM), F32)
    narrow = jax.ShapeDtypeStruct((2, s, GROUPS * N_STATE), F32)
    return _call(
        "ssd_bwd", body, grid=(2, GROUPS, nc),
        in_specs=[xs, bm, cm, spread, spread, ax_s, tri_s, st, dy_s],
        out_specs=[spread, bc_s, bc_s, spread, spread, ax_s],
        out_shape=[wide, narrow, narrow, wide, wide, jax.ShapeDtypeStruct((2, 1, D_SSM), F32)],
        scratch_shapes=[pltpu.VMEM((N_STATE, GROUP_W), F32)],
        sem=("arbitrary", "arbitrary", "arbitrary"),
        args=(xbc, xbc, xbc, dtx, acx, ax, tri, states, dy), jobs=jobs)


def _ssd_gate_fwd(y2, xbc, proj, dx, nw):
    s = xbc.shape[0]
    t = _tile(s, 512)

    def body(y_ref, xs_ref, z_ref, dx_ref, nw_ref, o_ref):
        y = (y_ref[0] + y_ref[1] + dx_ref[...] * xs_ref[...]) * _silu(z_ref[...])
        r = lax.rsqrt(jnp.mean(y * y, axis=-1, keepdims=True) + RMS_EPS)
        o_ref[...] = (y * r * nw_ref[...]).astype(BF16)

    blk = pl.BlockSpec((t, GROUP_W), lambda j, i: (i, j))
    vec = pl.BlockSpec((1, GROUP_W), lambda j, i: (0, j))
    return pl.pallas_call(
        body, name="ssd_gate_fwd", grid=(GROUPS, s // t),
        in_specs=[pl.BlockSpec((2, t, GROUP_W), lambda j, i: (0, i, j)), blk, blk, vec, vec],
        out_specs=blk,
        out_shape=jax.ShapeDtypeStruct((s, D_SSM), BF16),
        compiler_params=_cparams(("parallel", "parallel")),
    )(y2, xbc, proj, dx, nw)


def _ssd_gate_bwd(y2, xbc, proj, dycat, dx, nw, jobs=()):
    s = xbc.shape[0]
    t = _tile(s, 512)

    def body(y_ref, xs_ref, z_ref, do_ref, dx_ref, nw_ref, dyc_ref, dz_ref, dd_ref, dnw_ref):
        first = pl.program_id(1) == 0
        z = z_ref[...]
        xs = xs_ref[...]
        sig = jax.nn.sigmoid(z)
        gate = z * sig
        yc = y_ref[0] + y_ref[1] + dx_ref[...] * xs
        y = yc * gate
        r = lax.rsqrt(jnp.mean(y * y, axis=-1, keepdims=True) + RMS_EPS)
        do = do_ref[...]
        dyr = do * nw_ref[...]
        dy = r * dyr - y * (r * r * r) * jnp.mean(dyr * y, axis=-1, keepdims=True)
        dyc = dy * gate
        dyc_ref[...] = dyc
        dz_ref[...] = (dy * yc * (sig * (1.0 + z * (1.0 - sig)))).astype(BF16)
        _acc_rows(dd_ref, dyc * xs, first)
        _acc_rows(dnw_ref, do * y * r, first)

    blk = pl.BlockSpec((t, GROUP_W), lambda j, i: (i, j))
    vec = pl.BlockSpec((1, GROUP_W), lambda j, i: (0, j))
    return _call(
        "ssd_gate_bwd", body, grid=(GROUPS, s // t),
        in_specs=[pl.BlockSpec((2, t, GROUP_W), lambda j, i: (0, i, j)), blk, blk, blk, vec, vec],
        out_specs=[blk, blk, vec, vec],
        out_shape=[jax.ShapeDtypeStruct((s, D_SSM), F32), jax.ShapeDtypeStruct((s, D_SSM), BF16),
                   jax.ShapeDtypeStruct((1, D_SSM), F32), jax.ShapeDtypeStruct((1, D_SSM), F32)],
        sem=("parallel", "arbitrary"), args=(y2, xbc, proj, dycat, dx, nw), jobs=jobs)


def _ada_fwd(c16, w_ada):
    k, n = w_ada.shape
    tn = 512

    def body(c_ref, w_ref, o_ref):
        a = _silu(c_ref[...]).astype(BF16)
        o_ref[...] = jnp.dot(a, w_ref[...].astype(BF16), preferred_element_type=F32)

    return pl.pallas_call(
        body, name="ada_fwd", grid=(n // tn,),
        in_specs=[_full_spec((16, k)), pl.BlockSpec((k, tn), lambda j: (0, j))],
        out_specs=pl.BlockSpec((16, tn), lambda j: (0, j)),
        out_shape=jax.ShapeDtypeStruct((16, n), F32),
        compiler_params=_cparams(("parallel",)),
    )(c16, w_ada)


def _ada_bwd_adamw(c16, dmod16, w, m, v):
    k, n = w.shape
    tm, tn = 512, 1024
    blk = pl.BlockSpec((tm, tn), lambda i, j: (i, j))

    def body(c_ref, d_ref, w_ref, m_ref, v_ref, g_out, d_out, m_out, v_out):
        a = _silu(c_ref[...]).astype(BF16)
        g = lax.dot_general(a, d_ref[...].astype(BF16), _DN["tn"], preferred_element_type=F32)
        d, mn, vn = _adamw_math(w_ref[...], g, m_ref[...], v_ref[...])
        g_out[...] = g
        d_out[...] = d
        m_out[...] = mn
        v_out[...] = vn

    shp = jax.ShapeDtypeStruct((k, n), F32)
    return pl.pallas_call(
        body, name="ada_bwd_adamw", grid=(k // tm, n // tn),
        in_specs=[pl.BlockSpec((16, tm), lambda i, j: (0, i)), pl.BlockSpec((16, tn), lambda i, j: (0, j)),
                  blk, blk, blk],
        out_specs=[blk, blk, blk, blk],
        out_shape=[shp, shp, shp, shp],
        compiler_params=_cparams(("parallel", "parallel")),
    )(c16, dmod16, w, m, v)


def _sum8_adamw(gathered, w, m, v):
    n = w.shape[1]
    tn = _tile(n, 8192)
    vec = pl.BlockSpec((1, tn), lambda j: (0, j))

    def body(g8_ref, w_ref, m_ref, v_ref, g_out, d_out, m_out, v_out):
        g = g8_ref[0:1, :]
        for k in range(1, N_DEV):
            g = g + g8_ref[k:k + 1, :]
        d, mn, vn = _adamw_math(w_ref[...], g, m_ref[...], v_ref[...])
        g_out[...] = g
        d_out[...] = d
        m_out[...] = mn
        v_out[...] = vn

    shp = jax.ShapeDtypeStruct((1, n), F32)
    return pl.pallas_call(
        body, name="sum8_adamw", grid=(n // tn,),
        in_specs=[pl.BlockSpec((N_DEV, tn), lambda j: (0, j)), vec, vec, vec],
        out_specs=[vec, vec, vec, vec],
        out_shape=[shp, shp, shp, shp],
        compiler_params=_cparams(("parallel",)),
    )(gathered, w, m, v)


def _gather_vec(name, v):
    n = v.shape[1]
    out = _run_jobs(name, [_GatherJob(v.reshape(8, n // 8), pltpu.VMEM)])[0]
    return out.reshape(N_DEV, n)


class _Plan:
    _RESULT = {"gather": "", "rs1": "r1_", "rs2": "r2_", "rs2a": "ra_", "rs2b": "rb_"}

    def __init__(self, hosted, store, hooks=None):
        self.hosted, self.store, self.hooks = hosted, dict(store), hooks or {}

    def get(self, key):
        if key not in self.store and key.startswith("p_"):
            tag = key[2:]
            self.store[key] = _pair_add("rs_pair_add_" + tag, self.get("g_" + tag), self.get("r1_" + tag),
                                        self.get("pos"))
        if key not in self.store and key.startswith("q_"):
            tag = key[2:]
            self.store[key] = _relay_add("rs_relay_add_" + tag, self.get("p_" + tag), self.get("ra_" + tag),
                                         self.get("pos"))
        return self.store[key]

    def put(self, key, val):
        self.store[key] = val

    def jobs(self, host):
        make = {"gather": lambda t: _GatherJob(self.get("shard_" + t)),
                "rs1": lambda t: _SiblingJob(self.get("g_" + t)),
                "rs2": lambda t: _ChipsJob(self.get("p_" + t)),
                "rs2a": lambda t: _RelayFirstJob(self.get("p_" + t)),
                "rs2b": lambda t: _RelaySecondJob(self.get("q_" + t))}
        return [make[kind](tag) for kind, tag in self.hosted.get(host, ())]

    def run(self, host, fn, *args, **kw):
        outs, results = fn(*args, jobs=self.jobs(host), **kw)
        for (kind, tag), res in zip(self.hosted.get(host, ()), results):
            self.store[self._RESULT[kind] + tag] = res
        return outs

    def hook(self, name):
        if name in self.hooks:
            self.hooks[name](self)


def _pad_lanes(v, n):
    return jnp.pad(v, ((0, 0), (0, n - v.shape[1])))


def _local_step(plan, x, tgt, mod, conv_w, conv_b, dt_bias_f, dt_bias_b, a_log_f, a_log_b,
                ssm_d, ssm_nw, sc_w, sc_nw, ln1_g, ln1_b, ln2_g, ln2_b):
    s = x.shape[0]
    run = plan.run
    mod6 = mod.reshape(N_MOD, D_MODEL)
    bias_all = _pad_lanes(jnp.concatenate([dt_bias_f, dt_bias_b], axis=1), 128)
    a_all = _pad_lanes(-jnp.exp(jnp.concatenate([a_log_f, a_log_b], axis=1)), 128)
    a_x = jnp.stack([jnp.repeat(a_all[:, d * HEADS:(d + 1) * HEADS], HEAD_DIM, axis=1) for d in range(2)])
    d_lanes = jnp.repeat(ssm_d, HEAD_DIM, axis=1)

    w_in_g = plan.get("w_in")
    w_in_main, w_in_dt = _w_in_sections(w_in_g)
    h1 = _modulate("mod1", x, mod6)
    proj, = run("in_proj", _mm_nn, "in_proj", h1, w_in_main, F32)
    proj_dt = _mm_nn("in_proj_dt", h1, w_in_dt, F32)[0][0]
    xbc = _conv_silu_fwd(proj, conv_w, conv_b)
    dtx, acx = _dt_prep(proj_dt, bias_all, a_all)
    y2, states = run("ssd_fwd", _ssd_fwd, xbc, dtx, acx, a_x)
    y_ssm = _ssd_gate_fwd(y2, xbc, proj, d_lanes, ssm_nw)
    y_sc = _sc_fwd(proj, sc_w, sc_nw)
    ycat = jnp.concatenate([y_ssm, y_sc], axis=1)
    w_out_g = plan.get("w_out").reshape(D_MODEL, D_MODEL)
    mix = _mm_nn("out_proj", ycat, w_out_g, F32)[0][0]
    x1, h2 = _ln1_fwd(x, mix, mod6, ln1_g, ln1_b)

    def relu2(acc):
        u = acc.astype(BF16)
        r = jnp.maximum(acc, 0.0)
        return u, r * r

    w_up3 = plan.get("w_up")
    nper = w_up3.shape[2]
    tm = _tile(s, 1024)
    tn = 1024
    nb = nper // tn
    u_spec = pl.BlockSpec((tm, tn), lambda i, j, kk: (i, j))
    u, ff = run(
        "up_proj", _matmul, "up_proj", h2, w_up3, mode="nn", grid=(s // tm, D_FF // tn, 1),
        a_spec=pl.BlockSpec((tm, D_MODEL), lambda i, j, kk: (i, 0)),
        b_spec=pl.BlockSpec((None, D_MODEL, tn), lambda i, j, kk: (j // nb, 0, j % nb)),
        out_shapes=[jax.ShapeDtypeStruct((s, D_FF), BF16)] * 2, out_specs=[u_spec, u_spec],
        acc_shape=(tm, tn), epilogue=relu2)
    w_down_g = plan.get("w_down").reshape(D_FF, D_MODEL)
    f2 = _mm_nn("down_proj", ff, w_down_g, F32)[0][0]
    df2, dx1a, loss, g_ln2_g, g_ln2_b, dgate2 = _ln2_loss_bwd(x1, f2, tgt, mod6, ln2_g, ln2_b)

    def relu_grad(acc, uu):
        return (acc * (2.0 * jnp.maximum(uu.astype(F32), 0.0)),)

    du = _mm_nt("d_ff", df2, w_down_g, BF16, epilogue=relu_grad, extras=(u,))[0][0]
    plan.put("g_down", _mm_tn("g_w_down", ff, df2, BF16)[0][0].reshape(N_DEV, D_FF // N_DEV, D_MODEL))
    g_up, = run(
        "g_w_up", _matmul, "g_w_up", h2, du, mode="tn",
        grid=(D_MODEL // 1024, D_FF // tn, s // _tile(s, 2048)),
        a_spec=pl.BlockSpec((_tile(s, 2048), 1024), lambda i, j, kk: (kk, i)),
        b_spec=pl.BlockSpec((_tile(s, 2048), tn), lambda i, j, kk: (kk, j)),
        out_shapes=[jax.ShapeDtypeStruct((N_DEV, D_MODEL, nper), BF16)],
        out_specs=[pl.BlockSpec((None, 1024, tn), lambda i, j, kk: (j // nb, i, j % nb))],
        acc_shape=(1024, tn))
    plan.put("g_up", g_up)
    dh2, = run(
        "d_h2", _matmul, "d_h2", du, w_up3, mode="nt", grid=(s // tm, D_MODEL // 1024, D_FF // nper),
        a_spec=pl.BlockSpec((tm, nper), lambda i, j, kk: (i, kk)),
        b_spec=pl.BlockSpec((None, 1024, nper), lambda i, j, kk: (kk, j, 0)),
        out_shapes=[jax.ShapeDtypeStruct((s, D_MODEL), F32)],
        out_specs=[pl.BlockSpec((tm, 1024), lambda i, j, kk: (i, j))],
        acc_shape=(tm, 1024))
    dmix, dxa, dscale2, dshift2, g_ln1_g, g_ln1_b, dgate1 = _ln1_bwd(dh2, dx1a, x1, x, mix, mod6, ln1_g)

    dycat = _mm_nt("d_ycat", dmix, w_out_g, F32)[0][0]
    plan.put("g_out", run("g_w_out", _mm_tn, "g_w_out", ycat, dmix, BF16)[0].reshape(
        N_DEV, D_MODEL // N_DEV, D_MODEL))
    duh, dub, duc, g_sc_w, g_sc_nw = _sc_bwd(proj, dycat, sc_w, sc_nw)
    dyc, dz, dd_lanes, g_ssm_nw = run("ssd_gate_bwd", _ssd_gate_bwd, y2, xbc, proj, dycat, d_lanes, ssm_nw)
    dxs2, db2, dc2, ddtx, dacx, dax = run("ssd_bwd", _ssd_bwd, xbc, dtx, acx, a_x, states, dyc)
    n_bc = GROUPS * N_STATE
    du_xs, gw_xs, gb_xs = _conv_silu_bwd("conv_bwd_x", proj, conv_w, conv_b, 0, D_SSM, [dxs2],
                                         scaled=(dyc, d_lanes))
    du_b, gw_b, gb_b = _conv_silu_bwd("conv_bwd_b", proj, conv_w, conv_b, D_SSM, n_bc, [db2])
    du_c, gw_c, gb_c = _conv_silu_bwd("conv_bwd_c", proj, conv_w, conv_b, D_SSM + n_bc, n_bc, [dc2])
    du_dt, g_bias_all, g_a_sums = _dt_prep_bwd(proj_dt, bias_all, a_all, ddtx, dacx)

    dproj = jnp.concatenate([dz, du_xs, du_b, du_c, du_dt[:, :2 * HEADS], duh, dub, duc], axis=1)
    dproj3 = dproj.reshape(s, N_DEV, D_IN_SHARD).transpose(1, 0, 2)
    tk = _tile(s, 2048)
    g_in, = run(
        "g_w_in", _matmul, "g_w_in", h1, dproj3, mode="tn", grid=(N_DEV, D_MODEL // 1024, s // tk),
        a_spec=pl.BlockSpec((tk, 1024), lambda i, j, kk: (kk, j)),
        b_spec=pl.BlockSpec((None, tk, D_IN_SHARD), lambda i, j, kk: (i, kk, 0)),
        out_shapes=[jax.ShapeDtypeStruct((N_DEV, D_MODEL, D_IN_SHARD), BF16)],
        out_specs=[pl.BlockSpec((None, 1024, D_IN_SHARD), lambda i, j, kk: (i, j, 0))],
        acc_shape=(1024, D_IN_SHARD))
    plan.put("g_in", g_in)
    plan.hook("after_g_w_in")
    dh1, = run(
        "d_h1", _matmul, "d_h1", dproj3, w_in_g, mode="nt", grid=(s // tm, D_MODEL // 1024, N_DEV),
        a_spec=pl.BlockSpec((None, tm, D_IN_SHARD), lambda i, j, kk: (kk, i, 0)),
        b_spec=pl.BlockSpec((None, 1024, D_IN_SHARD), lambda i, j, kk: (kk, j, 0)),
        out_shapes=[jax.ShapeDtypeStruct((s, D_MODEL), F32)],
        out_specs=[pl.BlockSpec((tm, 1024), lambda i, j, kk: (i, j))],
        acc_shape=(tm, 1024))
    grad_x, dscale1, dshift1 = _grad_x(dxa, dh1, x, mod6)

    dmod = jnp.concatenate([dshift1, dscale1, dgate1, dshift2, dscale2, dgate2], axis=1)
    g_a_direct = dax.reshape(2, HEADS, HEAD_DIM).sum(axis=-1).reshape(1, 2 * HEADS)
    g_a_all = g_a_sums + _pad_lanes(g_a_direct, 128)
    small = {
        "dmod": dmod,
        "ssm_conv_w": jnp.concatenate([gw_xs, gw_b, gw_c], axis=1),
        "ssm_conv_b": jnp.concatenate([gb_xs, gb_b, gb_c], axis=1),
        "ssm_dt_bias_f": g_bias_all[:, :HEADS],
        "ssm_dt_bias_b": g_bias_all[:, HEADS:2 * HEADS],
        "ssm_a_log_f": (g_a_all * a_all)[:, :HEADS],
        "ssm_a_log_b": (g_a_all * a_all)[:, HEADS:2 * HEADS],
        "ssm_d": dd_lanes.reshape(HEADS, HEAD_DIM).sum(axis=1).reshape(1, HEADS),
        "ssm_norm_w": g_ssm_nw,
        "sc_conv_w": g_sc_w,
        "sc_norm_w": g_sc_nw,
        "ln1_g": g_ln1_g, "ln1_b": g_ln1_b, "ln2_g": g_ln2_g, "ln2_b": g_ln2_b,
    }
    return loss, grad_x, small


_SUMMED = [("ssm_conv_b", D_XBC), ("ssm_dt_bias_f", HEADS), ("ssm_dt_bias_b", HEADS),
           ("ssm_a_log_f", HEADS), ("ssm_a_log_b", HEADS), ("ssm_d", HEADS),
           ("ssm_norm_w", D_SSM), ("sc_norm_w", D_SC),
           ("ln1_g", D_MODEL), ("ln1_b", D_MODEL), ("ln2_g", D_MODEL), ("ln2_b", D_MODEL)]


def _round_up(n, k):
    return (n + k - 1) // k * k


def _w_in_sections(w_in_g):
    w = w_in_g.transpose(1, 0, 2).reshape(D_MODEL, D_IN)
    dt_lo = D_SSM + D_XBC
    main = jnp.concatenate([w[:, :dt_lo], w[:, dt_lo + 2 * HEADS:]], axis=1)
    dt = _pad_lanes(w[:, dt_lo:dt_lo + 2 * HEADS], 128)
    return main, dt


def kernel(x, c, w_ada, b_ada, w_in, ssm_conv_w, ssm_conv_b, ssm_dt_bias_f, ssm_dt_bias_b, ssm_a_log_f, ssm_a_log_b, ssm_d, ssm_norm_w, sc_conv_w, sc_norm_w, w_out, ln1_g, ln1_b, w_up, w_down, ln2_g, ln2_b, loss_target, m_w_ada, m_b_ada, m_w_in, m_ssm_conv_w, m_ssm_conv_b, m_ssm_dt_bias_f, m_ssm_dt_bias_b, m_ssm_a_log_f, m_ssm_a_log_b, m_ssm_d, m_ssm_norm_w, m_sc_conv_w, m_sc_norm_w, m_w_out, m_ln1_g, m_ln1_b, m_w_up, m_w_down, m_ln2_g, m_ln2_b, v_w_ada, v_b_ada, v_w_in, v_ssm_conv_w, v_ssm_conv_b, v_ssm_dt_bias_f, v_ssm_dt_bias_b, v_ssm_a_log_f, v_ssm_a_log_b, v_ssm_d, v_ssm_norm_w, v_sc_conv_w, v_sc_norm_w, v_w_out, v_ln1_g, v_ln1_b, v_w_up, v_w_down, v_ln2_g, v_ln2_b):
    args = dict(locals())
    xi, yi, ci = _my_pos()
    me = 4 * xi + 2 * yi + ci
    pos = jnp.stack([xi, yi, ci]).astype(jnp.int32)
    s = x.shape[1]

    n_cw, n_sw = SSM_CONV * D_XBC // N_DEV, SC_CONV * D_SC // N_DEV
    vec = jnp.concatenate([c, ssm_conv_w[0].reshape(1, n_cw), sc_conv_w[0].reshape(1, n_sw)], axis=1)
    vec = _pad_lanes(vec, 8192)
    gath = _gather_vec("gather_c_conv", vec)
    c_all = gath[:, :D_MODEL]
    conv_w = gath[:, D_MODEL:D_MODEL + n_cw].reshape(N_DEV, SSM_CONV, D_XBC // N_DEV)
    conv_w = conv_w.transpose(1, 0, 2).reshape(SSM_CONV, D_XBC)
    sc_w = gath[:, D_MODEL + n_cw:D_MODEL + n_cw + n_sw].reshape(N_DEV, SC_CONV, D_SC // N_DEV)
    sc_w = sc_w.transpose(1, 0, 2).reshape(SC_CONV, D_SC)
    c16 = jnp.pad(c_all, ((0, 8), (0, 0)))

    n_ada = w_ada.shape[2]
    mod_cols = _ada_fwd(c16, w_ada[0])[:N_DEV]
    mod_all = _run_jobs("gather_mod", [_GatherJob(mod_cols, pltpu.VMEM)])[0]
    mod = lax.dynamic_index_in_dim(mod_all, me, axis=1, keepdims=False)
    mod = mod.reshape(1, N_MOD * D_MODEL) + b_ada

    out = {}

    def adamw(plan, tag):
        name = "w_" + tag
        if tag in ("down", "up"):
            others = [(plan.get("ra_" + tag), 1), (plan.get("rb_" + tag), None)]
        else:
            others = [(plan.get("r2_" + tag), k) for k in range(3)]
        res = plan.run("rs_adamw_" + tag, _reduce_adamw, "rs_adamw_" + tag, plan.get("p_" + tag),
                       others, pos, args[name][0], args["m_" + name][0], args["v_" + name][0])
        out[name] = tuple(a[None] for a in res)

    hosted = {
        "in_proj": [("gather", "w_out")],
        "ssd_fwd": [("gather", "w_up")],
        "up_proj": [("gather", "w_down")],
        "g_w_up": [("rs1", "down")],
        "d_h2": [("rs2a", "down"), ("rs1", "up")],
        "g_w_out": [("rs2b", "down")],
        "ssd_gate_bwd": [("rs1", "out")],
        "ssd_bwd": [("rs2a", "up")],
        "g_w_in": [("rs2b", "up"), ("rs2", "out")],
        "d_h1": [("rs2", "in")],
    }

    def sibling_exchange_in(plan):
        plan.put("r1_in", _run_jobs("rs_sibling_in", [_SiblingJob(plan.get("g_in"))])[0])

    store = {"pos": pos}
    for tag, w in (("w_in", w_in), ("w_out", w_out), ("w_up", w_up), ("w_down", w_down)):
        store["shard_" + tag] = _cast_bf16("cast_" + tag, w[0])
    store["w_in"] = _run_jobs("gather_w_in", [_GatherJob(store["shard_w_in"])])[0]
    plan = _Plan(hosted, store, hooks={"after_g_w_in": sibling_exchange_in})
    loss, grad_x, small = _local_step(
        plan, x[0], loss_target[0], mod, conv_w, ssm_conv_b, ssm_dt_bias_f, ssm_dt_bias_b,
        ssm_a_log_f, ssm_a_log_b, ssm_d, ssm_norm_w, sc_w, sc_norm_w, ln1_g, ln1_b, ln2_g, ln2_b)
    for tag in ("down", "up", "out", "in"):
        adamw(plan, tag)

    parts = [small["dmod"]]
    parts += [_pad_lanes(small[n], _round_up(w, 128)) for n, w in _SUMMED]
    parts += [small["ssm_conv_w"].reshape(1, SSM_CONV * D_XBC), small["sc_conv_w"].reshape(1, SC_CONV * D_SC)]
    parts += [loss]
    gvec = jnp.concatenate(parts, axis=1)
    n_vec = _round_up(gvec.shape[1], 8192)
    gall = _gather_vec("gather_small_grads", _pad_lanes(gvec, n_vec))

    def shard_cols(full, k, per):
        return lax.dynamic_slice_in_dim(full.reshape(k, N_DEV, per), me, 1, axis=1).reshape(1, k * per)

    def placed(vals, n_rows=1):
        return jnp.concatenate(vals, axis=1)

    n_mod = N_MOD * D_MODEL
    ws, ms, vs = [b_ada], [m_b_ada], [v_b_ada]
    for n, w in _SUMMED:
        pw = _round_up(w, 128)
        ws.append(_pad_lanes(args[n], pw))
        ms.append(_pad_lanes(args["m_" + n], pw))
        vs.append(_pad_lanes(args["v_" + n], pw))

    def full_rows(shard, k, per):
        z = jnp.zeros((k, N_DEV, per), F32)
        z = lax.dynamic_update_slice_in_dim(z, shard.reshape(k, 1, per), me, axis=1)
        return z.reshape(1, k * N_DEV * per)

    for nm, k, per in (("ssm_conv_w", SSM_CONV, D_XBC // N_DEV), ("sc_conv_w", SC_CONV, D_SC // N_DEV)):
        ws.append(full_rows(args[nm][0], k, per))
        ms.append(full_rows(args["m_" + nm][0], k, per))
        vs.append(full_rows(args["v_" + nm][0], k, per))
    tail = n_vec - sum(a.shape[1] for a in ws)
    ws.append(jnp.zeros((1, tail), F32))
    ms.append(jnp.zeros((1, tail), F32))
    vs.append(jnp.ones((1, tail), F32))
    g_s, d_s, m_s, v_s = _sum8_adamw(gall, placed(ws), placed(ms), placed(vs))

    off = 0

    def take(w):
        nonlocal off
        sl = tuple(a[:, off:off + w] for a in (g_s, d_s, m_s, v_s))
        off += _round_up(w, 128)
        return sl

    out["b_ada"] = take(n_mod)
    for n, w in _SUMMED:
        out[n] = take(w)
    for nm, k, per in (("ssm_conv_w", SSM_CONV, D_XBC // N_DEV), ("sc_conv_w", SC_CONV, D_SC // N_DEV)):
        full = take(k * N_DEV * per)
        out[nm] = tuple(shard_cols(a, k, per).reshape(1, k, per) for a in full)
    loss_total = g_s[0, off]

    dmod_all = gall[:, :n_mod]
    dmod_cols = lax.dynamic_slice_in_dim(dmod_all.reshape(N_DEV, N_DEV, n_ada), me, 1, axis=1)
    dmod16 = jnp.pad(dmod_cols.reshape(N_DEV, n_ada), ((0, 8), (0, 0)))
    out["w_ada"] = tuple(a[None] for a in _ada_bwd_adamw(c16, dmod16, w_ada[0], m_w_ada[0], v_w_ada[0]))

    names = ['w_ada', 'b_ada', 'w_in', 'ssm_conv_w', 'ssm_conv_b', 'ssm_dt_bias_f', 'ssm_dt_bias_b',
             'ssm_a_log_f', 'ssm_a_log_b', 'ssm_d', 'ssm_norm_w', 'sc_conv_w', 'sc_norm_w', 'w_out',
             'ln1_g', 'ln1_b', 'w_up', 'w_down', 'ln2_g', 'ln2_b']
    res = [loss_total, grad_x[None]]
    for k in range(4):
        res += [out[n][k] for n in names]
    return tuple(res)
```

```python
import functools

import jax
import jax.numpy as jnp
from jax import lax
from jax.experimental import pallas as pl
from jax.experimental.pallas import tpu as pltpu

F32 = jnp.float32
BF16 = jnp.bfloat16
MESH = pl.DeviceIdType.MESH

N_DEV = 8
D_MODEL = 4096
D_SSM = 2048
D_SC = 2048
HEADS = 32
HEAD_DIM = 64
GROUPS = 8
GROUP_W = D_SSM // GROUPS
HEADS_PER_GROUP = 4
N_STATE = 128
CHUNK = 128
SSM_CONV = 5
SC_CONV = 3
SC_GROUP_W = 128
D_XBC = 4096
D_FF = 16384
D_IN = 12352
D_IN_SHARD = D_IN // N_DEV
D_MAIN = 12288
N_MOD = 6
ALPHA = (2 * 1) ** 0.25
LN_EPS = 1e-5
RMS_EPS = 1e-5
ADAM_LR = 0.001
ADAM_B1 = 0.9
ADAM_B2 = 0.999
ADAM_EPS = 1e-08
ADAM_WD = 0.01
ADAM_STEP = 10

VMEM_LIMIT = 56 * 1024 * 1024
HALO = 8

_DN = {
    "nn": (((1,), (0,)), ((), ())),
    "nt": (((1,), (1,)), ((), ())),
    "tn": (((0,), (0,)), ((), ())),
}


def _cparams(sem=None):
    return pltpu.CompilerParams(dimension_semantics=sem, vmem_limit_bytes=VMEM_LIMIT)


def _my_pos():
    return lax.axis_index("x"), lax.axis_index("y"), lax.axis_index("c")


def _other_chips(x, y):
    return [(1 - x, y), (x, 1 - y), (1 - x, 1 - y)]


class _GatherJob:
    n_remote = 7

    def __init__(self, shard, space=pl.ANY, rows=None, into=None):
        self.ins = (shard,) if into is None else (shard, into)
        self.alias = None if into is None else 1
        self.out_shapes = (jax.ShapeDtypeStruct((N_DEV,) + shard.shape, shard.dtype),)
        self.space = space
        self.rows = rows

    def _parts(self, ins, outs, send, recv, local):
        x_ref, out_ref = ins[0], outs[0]
        if self.rows is not None:
            x_ref = x_ref.at[pl.ds(*self.rows)]
        x, y, c = _my_pos()
        me, sibling = (x, y, c), (x, y, 1 - c)
        chips = _other_chips(x, y)

        def slab(px, py, pc):
            whole = out_ref.at[4 * px + 2 * py + pc]
            return whole if self.rows is None else whole.at[pl.ds(*self.rows)]

        def copy(k, block, to, src=None):
            return pltpu.make_async_remote_copy(
                src_ref=slab(*block) if src is None else src, dst_ref=slab(*block),
                send_sem=send.at[k], recv_sem=recv.at[k], device_id=to, device_id_type=MESH)

        mine = pltpu.make_async_copy(x_ref, slab(*me), local.at[0])
        own = [copy(0, me, sibling, src=x_ref), copy(1, me, (*chips[0], c), src=x_ref),
               copy(2, me, (*chips[1], c), src=x_ref)]
        relayed = (x + (1 - c) * (1 - 2 * x), y + c * (1 - 2 * y), c)
        relay = copy(3, relayed, (x + c * (1 - 2 * x), y + (1 - c) * (1 - 2 * y), c))
        hand = [copy(4 + j, (*chip, c), sibling) for j, chip in enumerate(chips)]
        landed = [copy(1 + j, (*chip, c), me) for j, chip in enumerate(chips)]
        handed = [copy(0, sibling, me)] + [copy(4 + j, (*chip, 1 - c), me) for j, chip in enumerate(chips)]
        return mine, own, relay, hand, landed, handed

    def start(self, *refs):
        mine, own, _, _, _, _ = self._parts(*refs)
        mine.start()
        for cp in own:
            cp.start()

    def mid(self, *refs):
        _, _, relay, hand, landed, _ = self._parts(*refs)
        landed[0].wait_recv()
        landed[1].wait_recv()
        relay.start()
        hand[0].start()
        hand[1].start()

    def finish(self, *refs):
        mine, own, relay, hand, landed, handed = self._parts(*refs)
        landed[2].wait_recv()
        hand[2].start()
        for cp in handed:
            cp.wait_recv()
        for cp in own + [relay] + hand:
            cp.wait_send()
        mine.wait()


class _SiblingJob:
    n_remote = 4
    space = pl.ANY

    def __init__(self, g):
        self.ins = (g,)
        self.out_shapes = (jax.ShapeDtypeStruct((4,) + g.shape[1:], g.dtype),)

    def _copies(self, ins, outs, send, recv, local):
        x, y, c = _my_pos()
        return [pltpu.make_async_remote_copy(
            src_ref=ins[0].at[2 * j + (1 - c)], dst_ref=outs[0].at[j],
            send_sem=send.at[j], recv_sem=recv.at[j],
            device_id=(x, y, 1 - c), device_id_type=MESH) for j in range(4)]

    def start(self, *refs):
        for cp in self._copies(*refs):
            cp.start()

    def mid(self, *refs):
        pass

    def finish(self, *refs):
        for cp in self._copies(*refs):
            cp.wait()


class _ChipsJob:
    n_remote = 3
    space = pl.ANY

    def __init__(self, p):
        self.ins = (p,)
        self.out_shapes = (jax.ShapeDtypeStruct(p.shape, p.dtype),)

    def _copies(self, ins, outs, send, recv, local):
        x, y, c = _my_pos()
        return [pltpu.make_async_remote_copy(
            src_ref=ins[0].at[k], dst_ref=outs[0].at[k],
            send_sem=send.at[k], recv_sem=recv.at[k],
            device_id=(px, py, c), device_id_type=MESH) for k, (px, py) in enumerate(_other_chips(x, y))]

    def start(self, *refs):
        for cp in self._copies(*refs):
            cp.start()

    def mid(self, *refs):
        pass

    def finish(self, *refs):
        for cp in self._copies(*refs):
            cp.wait()


def _relay_route(x, y, c):
    first = (x + c * (1 - 2 * x), y + (1 - c) * (1 - 2 * y))
    second = (x + (1 - c) * (1 - 2 * x), y + c * (1 - 2 * y))
    return first, second


class _RelayFirstJob:
    n_remote = 2
    space = pl.ANY

    def __init__(self, p):
        self.ins = (p,)
        self.out_shapes = (jax.ShapeDtypeStruct(p.shape, p.dtype),)

    def _copies(self, ins, outs, send, recv, local):
        x, y, c = _my_pos()
        (fx, fy), _ = _relay_route(x, y, c)
        return [pltpu.make_async_remote_copy(
            src_ref=ins[0].at[k], dst_ref=outs[0].at[k], send_sem=send.at[k], recv_sem=recv.at[k],
            device_id=(fx, fy, c), device_id_type=MESH) for k in range(2)]

    def start(self, *refs):
        for cp in self._copies(*refs):
            cp.start()

    def mid(self, *refs):
        pass

    def finish(self, *refs):
        for cp in self._copies(*refs):
            cp.wait()


class _RelaySecondJob:
    n_remote = 1
    space = pl.ANY

    def __init__(self, q):
        self.ins = (q,)
        self.out_shapes = (jax.ShapeDtypeStruct(q.shape, q.dtype),)

    def _copy(self, ins, outs, send, recv, local):
        x, y, c = _my_pos()
        _, (sx, sy) = _relay_route(x, y, c)
        return pltpu.make_async_remote_copy(
            src_ref=ins[0], dst_ref=outs[0], send_sem=send.at[0], recv_sem=recv.at[0],
            device_id=(sx, sy, c), device_id_type=MESH)

    def start(self, *refs):
        self._copy(*refs).start()

    def mid(self, *refs):
        pass

    def finish(self, *refs):
        self._copy(*refs).wait()


MID_STEP_FRACTION = 0.64


def _call(name, body, *, grid, in_specs, out_specs, out_shape, args, scratch_shapes=(), sem=None,
          jobs=(), n_prefetch=0):
    out_shape, out_specs, in_specs = list(out_shape), list(out_specs), list(in_specs)
    scratch_shapes = list(scratch_shapes)
    jobs = list(jobs)
    n_in, n_out, n_scr = len(in_specs), len(out_shape), len(scratch_shapes)
    job_ins = [a for j in jobs for a in j.ins]
    job_outs = [o for j in jobs for o in j.out_shapes]
    steps = 1
    for n in grid:
        steps *= n
    mid_step = min(steps - 1, int(steps * MID_STEP_FRACTION))

    def wrapped(*refs):
        pre, refs = refs[:n_prefetch], refs[n_prefetch:]
        core_in, refs = refs[:n_in], refs[n_in:]
        jin, refs = refs[:len(job_ins)], refs[len(job_ins):]
        core_out, refs = refs[:n_out], refs[n_out:]
        jout, refs = refs[:len(job_outs)], refs[len(job_outs):]
        core_scr, sems = refs[:n_scr], refs[n_scr:]
        lin = 0
        for ax, n in enumerate(grid):
            lin = lin * n + pl.program_id(ax)
        bound = []
        for j in jobs:
            ji, jin = jin[:len(j.ins)], jin[len(j.ins):]
            jo, jout = jout[:len(j.out_shapes)], jout[len(j.out_shapes):]
            (send, recv, local), sems = sems[:3], sems[3:]
            bound.append((j, (ji, jo, send, recv, local)))

        if jobs:
            @pl.when(lin == 0)
            def _():
                for j, r in bound:
                    j.start(*r)

        body(*pre, *core_in, *core_out, *core_scr)

        if jobs:
            @pl.when(lin == mid_step)
            def _():
                for j, r in bound:
                    j.mid(*r)

            @pl.when(lin == steps - 1)
            def _():
                for j, r in bound:
                    j.finish(*r)

    sem_shapes = []
    for j in jobs:
        sem_shapes += [pltpu.SemaphoreType.DMA((j.n_remote,)), pltpu.SemaphoreType.DMA((j.n_remote,)),
                       pltpu.SemaphoreType.DMA((1,))]
    if jobs:
        sem = tuple("arbitrary" for _ in grid)
    aliases = {}
    in_at, out_at = len(args), n_out
    for j in jobs:
        if getattr(j, "alias", None) is not None:
            aliases[in_at + j.alias] = out_at
        in_at, out_at = in_at + len(j.ins), out_at + len(j.out_shapes)
    res = pl.pallas_call(
        wrapped, name=name, input_output_aliases=aliases,
        grid_spec=pltpu.PrefetchScalarGridSpec(
            num_scalar_prefetch=n_prefetch, grid=tuple(grid),
            in_specs=in_specs + [pl.BlockSpec(memory_space=j.space) for j in jobs for _ in j.ins],
            out_specs=out_specs + [pl.BlockSpec(memory_space=j.space) for j in jobs for _ in j.out_shapes],
            scratch_shapes=scratch_shapes + sem_shapes),
        out_shape=out_shape + job_outs,
        compiler_params=_cparams(sem),
    )(*args, *job_ins)
    res = list(res) if isinstance(res, (list, tuple)) else [res]
    return res[:n_out], res[n_out:]


def _run_jobs(name, jobs):
    return _call(name, lambda: None, grid=(1,), in_specs=[], out_specs=[], out_shape=[], args=(),
                 jobs=jobs)[1]


def _matmul(name, a, b, *, mode, grid, a_spec, b_spec, out_shapes, out_specs, acc_shape,
            epilogue=None, extras=(), extra_specs=(), jobs=()):
    nk = grid[2]
    n_extra = len(extras)
    n_out = len(out_shapes)

    def body(*refs):
        a_ref, b_ref = refs[0], refs[1]
        extra_refs = refs[2:2 + n_extra]
        out_refs = refs[2 + n_extra:2 + n_extra + n_out]
        part = lax.dot_general(a_ref[...], b_ref[...], _DN[mode], preferred_element_type=F32)

        def finish(acc):
            outs = epilogue(acc, *[r[...] for r in extra_refs]) if epilogue else (acc,)
            for o_ref, o in zip(out_refs, outs):
                o_ref[...] = o.astype(o_ref.dtype)

        if nk == 1:
            finish(part)
        else:
            acc_ref = refs[-1]
            k = pl.program_id(2)

            @pl.when(k == 0)
            def _():
                acc_ref[...] = part

            @pl.when(k > 0)
            def _():
                acc_ref[...] += part

            @pl.when(k == nk - 1)
            def _():
                finish(acc_ref[...])

    scratch = [pltpu.VMEM(acc_shape, F32)] if nk > 1 else []
    return _call(name, body, grid=grid, in_specs=[a_spec, b_spec, *extra_specs],
                 out_specs=out_specs, out_shape=out_shapes, scratch_shapes=scratch,
                 sem=("parallel", "parallel", "arbitrary"), args=(a, b, *extras), jobs=jobs)


def _tile(n, pref):
    t = min(n, pref)
    assert n % t == 0, (n, t)
    return t


def _mm_nn(name, a, b, out_dtype, tn=1024, tk=None, epilogue=None, out_dtypes=None, jobs=()):
    m, k = a.shape
    n = b.shape[1]
    tm, tn = _tile(m, 1024), _tile(n, tn)
    tk = _tile(k, tk or 4096)
    out_dtypes = out_dtypes or (out_dtype,)
    return _matmul(
        name, a, b, mode="nn", grid=(m // tm, n // tn, k // tk),
        a_spec=pl.BlockSpec((tm, tk), lambda i, j, kk: (i, kk)),
        b_spec=pl.BlockSpec((tk, tn), lambda i, j, kk: (kk, j)),
        out_shapes=[jax.ShapeDtypeStruct((m, n), dt) for dt in out_dtypes],
        out_specs=[pl.BlockSpec((tm, tn), lambda i, j, kk: (i, j)) for _ in out_dtypes],
        acc_shape=(tm, tn), epilogue=epilogue, jobs=jobs)


def _mm_nt(name, a, b, out_dtype, epilogue=None, extras=(), tk=None, jobs=()):
    m, k = a.shape
    n = b.shape[0]
    tm, tn = _tile(m, 1024), _tile(n, 1024)
    tk = _tile(k, tk or 4096)
    o_spec = pl.BlockSpec((tm, tn), lambda i, j, kk: (i, j))
    return _matmul(
        name, a, b, mode="nt", grid=(m // tm, n // tn, k // tk),
        a_spec=pl.BlockSpec((tm, tk), lambda i, j, kk: (i, kk)),
        b_spec=pl.BlockSpec((tn, tk), lambda i, j, kk: (j, kk)),
        out_shapes=[jax.ShapeDtypeStruct((m, n), out_dtype)],
        out_specs=[o_spec], acc_shape=(tm, tn), epilogue=epilogue,
        extras=extras, extra_specs=[o_spec for _ in extras], jobs=jobs)


def _mm_tn(name, a, b, out_dtype, tk=2048, jobs=()):
    k, m = a.shape
    n = b.shape[1]
    tm, tn = _tile(m, 1024), _tile(n, 1024)
    tk = _tile(k, tk)
    return _matmul(
        name, a, b, mode="tn", grid=(m // tm, n // tn, k // tk),
        a_spec=pl.BlockSpec((tk, tm), lambda i, j, kk: (kk, i)),
        b_spec=pl.BlockSpec((tk, tn), lambda i, j, kk: (kk, j)),
        out_shapes=[jax.ShapeDtypeStruct((m, n), out_dtype)],
        out_specs=[pl.BlockSpec((tm, tn), lambda i, j, kk: (i, j))],
        acc_shape=(tm, tn), jobs=jobs)


def _cast_bf16(name, w):
    r, c = w.shape
    tr = _tile(r, 512)

    def body(w_ref, o_ref):
        o_ref[...] = w_ref[...].astype(BF16)

    return pl.pallas_call(
        body, name=name, grid=(r // tr,),
        in_specs=[pl.BlockSpec((tr, c), lambda i: (i, 0))],
        out_specs=pl.BlockSpec((tr, c), lambda i: (i, 0)),
        out_shape=jax.ShapeDtypeStruct((r, c), BF16),
        compiler_params=_cparams(("parallel",)),
    )(w)


def _chip_of(pos, which):
    x, y, c = pos[0], pos[1], pos[2]
    first, second = _relay_route(x, y, c)
    chips = _other_chips(x, y) + [first, second, (x, y)]
    px, py = chips[which]
    return 2 * px + py


def _pair_add(name, g, r1, pos, dests):
    _, r, cdim = g.shape
    tr = _tile(r, 512)

    def chip(k, pos):
        idx = _chip_of(pos, dests[-1])
        for n in range(len(dests) - 2, -1, -1):
            idx = jnp.where(k == n, _chip_of(pos, dests[n]), idx)
        return idx

    def body(pos_ref, g_ref, r_ref, o_ref):
        o_ref[...] = (g_ref[...].astype(F32) + r_ref[...].astype(F32)).astype(o_ref.dtype)

    return pl.pallas_call(
        body, name=name,
        grid_spec=pltpu.PrefetchScalarGridSpec(
            num_scalar_prefetch=1, grid=(len(dests), r // tr),
            in_specs=[pl.BlockSpec((None, tr, cdim), lambda k, i, pos: (2 * chip(k, pos) + pos[2], i, 0)),
                      pl.BlockSpec((None, tr, cdim), lambda k, i, pos: (chip(k, pos), i, 0))],
            out_specs=pl.BlockSpec((None, tr, cdim), lambda k, i, pos: (k, i, 0))),
        out_shape=jax.ShapeDtypeStruct((len(dests), r, cdim), BF16),
        compiler_params=_cparams(("parallel", "parallel")),
    )(pos, g, r1)


def _adamw_math(w, g, m, v):
    m = ADAM_B1 * m + (1.0 - ADAM_B1) * g
    v = ADAM_B2 * v + (1.0 - ADAM_B2) * jnp.square(g)
    m_hat = m / (1.0 - ADAM_B1 ** ADAM_STEP)
    v_hat = v / (1.0 - ADAM_B2 ** ADAM_STEP)
    delta = -ADAM_LR * (m_hat / (jnp.sqrt(v_hat) + ADAM_EPS) + ADAM_WD * w)
    return delta, m, v


def _relay_add(name, g, r1, ra, pos):
    _, r, cdim = g.shape
    tr = _tile(r, 512)

    def body(pos_ref, g_ref, r1_ref, ra_ref, o_ref):
        q = g_ref[...].astype(F32) + r1_ref[...].astype(F32) + ra_ref[...].astype(F32)
        o_ref[...] = q.astype(o_ref.dtype)

    return pl.pallas_call(
        body, name=name,
        grid_spec=pltpu.PrefetchScalarGridSpec(
            num_scalar_prefetch=1, grid=(r // tr,),
            in_specs=[pl.BlockSpec((None, tr, cdim), lambda i, pos: (2 * _chip_of(pos, 4) + pos[2], i, 0)),
                      pl.BlockSpec((None, tr, cdim), lambda i, pos: (_chip_of(pos, 4), i, 0)),
                      pl.BlockSpec((None, tr, cdim), lambda i, pos: (0, i, 0))],
            out_specs=pl.BlockSpec((tr, cdim), lambda i, pos: (i, 0))),
        out_shape=jax.ShapeDtypeStruct((r, cdim), BF16),
        compiler_params=_cparams(("parallel",)),
    )(pos, g, r1, ra)


def _reduce_adamw(name, g8, r1, others, pos, w, m, v, jobs=()):
    r, cdim = w.shape
    tr = _tile(r, 128 if cdim >= D_MODEL else 256)
    blk = pl.BlockSpec((tr, cdim), lambda i, pos: (i, 0))
    n_other = len(others)

    def body(pos_ref, g_ref, r1_ref, *refs):
        other_refs, (w_ref, m_ref, v_ref, g_out, d_out, m_out, v_out) = refs[:n_other], refs[n_other:]
        g = g_ref[...].astype(F32) + r1_ref[...].astype(F32)
        for o_ref in other_refs:
            g = g + o_ref[...].astype(F32)
        d, mn, vn = _adamw_math(w_ref[...], g, m_ref[...], v_ref[...])
        g_out[...] = g
        d_out[...] = d
        m_out[...] = mn
        v_out[...] = vn

    def other_spec(lead):
        if lead is None:
            return blk
        return pl.BlockSpec((None, tr, cdim), lambda i, pos: (lead, i, 0))

    shp = jax.ShapeDtypeStruct((r, cdim), F32)
    return _call(
        name, body, grid=(r // tr,), n_prefetch=1,
        in_specs=[pl.BlockSpec((None, tr, cdim), lambda i, pos: (2 * _chip_of(pos, 5) + pos[2], i, 0)),
                  pl.BlockSpec((None, tr, cdim), lambda i, pos: (_chip_of(pos, 5), i, 0))]
        + [other_spec(lead) for _, lead in others] + [blk, blk, blk],
        out_specs=[blk, blk, blk, blk], out_shape=[shp, shp, shp, shp],
        sem=("parallel",), args=(pos, g8, r1, *[a for a, _ in others], w, m, v), jobs=jobs)


def _row_spec(t, width=D_MODEL):
    return pl.BlockSpec((t, width), lambda i: (i, 0))


def _full_spec(shape):
    return pl.BlockSpec(shape, lambda i: tuple(0 for _ in shape))


def _ln_stats(p):
    mu = jnp.mean(p, axis=-1, keepdims=True)
    xc = p - mu
    var = jnp.mean(xc * xc, axis=-1, keepdims=True)
    rstd = lax.rsqrt(var + LN_EPS)
    return xc * rstd, rstd


def _ln_bwd(dy, xhat, rstd, g):
    dxh = dy * g
    m1 = jnp.mean(dxh, axis=-1, keepdims=True)
    m2 = jnp.mean(dxh * xhat, axis=-1, keepdims=True)
    return rstd * (dxh - m1 - xhat * m2)


def _acc_rows(ref, val, first):
    s = jnp.sum(val, axis=0, keepdims=True)

    @pl.when(first)
    def _():
        ref[...] = s

    @pl.when(jnp.logical_not(first))
    def _():
        ref[...] += s


def _modulate(name, x, mod6):
    s = x.shape[0]
    t = _tile(s, 256)

    def body(x_ref, mod_ref, o_ref):
        o_ref[...] = (x_ref[...] * (1.0 + mod_ref[1:2, :]) + mod_ref[0:1, :]).astype(BF16)

    return pl.pallas_call(
        body, name=name, grid=(s // t,),
        in_specs=[_row_spec(t), _full_spec((N_MOD, D_MODEL))],
        out_specs=_row_spec(t),
        out_shape=jax.ShapeDtypeStruct((s, D_MODEL), BF16),
        compiler_params=_cparams(("parallel",)),
    )(x, mod6)


def _ln1_fwd(x, mix, mod6, g, b):
    s = x.shape[0]
    t = _tile(s, 256)

    def body(x_ref, mix_ref, mod_ref, g_ref, b_ref, x1_ref, h2_ref):
        pre = ALPHA * x_ref[...] + (1.0 + mod_ref[2:3, :]) * mix_ref[...]
        xhat, _ = _ln_stats(pre)
        x1 = xhat * g_ref[...] + b_ref[...]
        x1_ref[...] = x1
        h2_ref[...] = (x1 * (1.0 + mod_ref[4:5, :]) + mod_ref[3:4, :]).astype(BF16)

    vec = _full_spec((1, D_MODEL))
    return pl.pallas_call(
        body, name="ln1_fwd", grid=(s // t,),
        in_specs=[_row_spec(t), _row_spec(t), _full_spec((N_MOD, D_MODEL)), vec, vec],
        out_specs=[_row_spec(t), _row_spec(t)],
        out_shape=[jax.ShapeDtypeStruct((s, D_MODEL), F32), jax.ShapeDtypeStruct((s, D_MODEL), BF16)],
        compiler_params=_cparams(("parallel",)),
    )(x, mix, mod6, g, b)


def _ln2_loss_bwd(x1, f2, tgt, mod6, g, b):
    s = x1.shape[0]
    t = _tile(s, 128)

    def body(x1_ref, f2_ref, tgt_ref, mod_ref, g_ref, b_ref,
             df2_ref, dx1_ref, loss_ref, dg_ref, db_ref, dgate_ref):
        first = pl.program_id(0) == 0
        gate = 1.0 + mod_ref[5:6, :]
        f2v = f2_ref[...]
        pre = ALPHA * x1_ref[...] + gate * f2v
        xhat, rstd = _ln_stats(pre)
        err = xhat * g_ref[...] + b_ref[...] - tgt_ref[...]
        part = 0.5 * jnp.sum(jnp.mean(err * err, axis=-1, keepdims=True), axis=0, keepdims=True)
        dy = err / D_MODEL
        dpre = _ln_bwd(dy, xhat, rstd, g_ref[...])
        df2_ref[...] = (gate * dpre).astype(BF16)
        dx1_ref[...] = ALPHA * dpre
        _acc_rows(loss_ref, jnp.broadcast_to(part, (1, 128)), first)
        _acc_rows(dg_ref, dy * xhat, first)
        _acc_rows(db_ref, dy, first)
        _acc_rows(dgate_ref, dpre * f2v, first)

    vec = _full_spec((1, D_MODEL))
    vshape = jax.ShapeDtypeStruct((1, D_MODEL), F32)
    return pl.pallas_call(
        body, name="ln2_loss_bwd", grid=(s // t,),
        in_specs=[_row_spec(t), _row_spec(t), _row_spec(t), _full_spec((N_MOD, D_MODEL)), vec, vec],
        out_specs=[_row_spec(t), _row_spec(t), _full_spec((1, 128)), vec, vec, vec],
        out_shape=[jax.ShapeDtypeStruct((s, D_MODEL), BF16), jax.ShapeDtypeStruct((s, D_MODEL), F32),
                   jax.ShapeDtypeStruct((1, 128), F32), vshape, vshape, vshape],
        compiler_params=_cparams(("arbitrary",)),
    )(x1, f2, tgt, mod6, g, b)


def _ln1_bwd(dh2, dx1a, x1, x, mix, mod6, g):
    s = x.shape[0]
    t = _tile(s, 128)

    def body(dh2_ref, dx1a_ref, x1_ref, x_ref, mix_ref, mod_ref, g_ref,
             dmix_ref, dxa_ref, dscale_ref, dshift_ref, dg_ref, db_ref, dgate_ref):
        first = pl.program_id(0) == 0
        dh2v = dh2_ref[...]
        dx1 = dx1a_ref[...] + dh2v * (1.0 + mod_ref[4:5, :])
        gate = 1.0 + mod_ref[2:3, :]
        mixv = mix_ref[...]
        pre = ALPHA * x_ref[...] + gate * mixv
        xhat, rstd = _ln_stats(pre)
        dpre = _ln_bwd(dx1, xhat, rstd, g_ref[...])
        dmix_ref[...] = (gate * dpre).astype(BF16)
        dxa_ref[...] = ALPHA * dpre
        _acc_rows(dscale_ref, dh2v * x1_ref[...], first)
        _acc_rows(dshift_ref, dh2v, first)
        _acc_rows(dg_ref, dx1 * xhat, first)
        _acc_rows(db_ref, dx1, first)
        _acc_rows(dgate_ref, dpre * mixv, first)

    vec = _full_spec((1, D_MODEL))
    vshape = jax.ShapeDtypeStruct((1, D_MODEL), F32)
    return pl.pallas_call(
        body, name="ln1_bwd", grid=(s // t,),
        in_specs=[_row_spec(t)] * 5 + [_full_spec((N_MOD, D_MODEL)), vec],
        out_specs=[_row_spec(t), _row_spec(t), vec, vec, vec, vec, vec],
        out_shape=[jax.ShapeDtypeStruct((s, D_MODEL), BF16), jax.ShapeDtypeStruct((s, D_MODEL), F32),
                   vshape, vshape, vshape, vshape, vshape],
        compiler_params=_cparams(("arbitrary",)),
    )(dh2, dx1a, x1, x, mix, mod6, g)


def _grad_x(dxa, dh1, x, mod6):
    s = x.shape[0]
    t = _tile(s, 256)

    def body(dxa_ref, dh1_ref, x_ref, mod_ref, gx_ref, dscale_ref, dshift_ref):
        first = pl.program_id(0) == 0
        dh1v = dh1_ref[...]
        gx_ref[...] = dxa_ref[...] + dh1v * (1.0 + mod_ref[1:2, :])
        _acc_rows(dscale_ref, dh1v * x_ref[...], first)
        _acc_rows(dshift_ref, dh1v, first)

    vec = _full_spec((1, D_MODEL))
    vshape = jax.ShapeDtypeStruct((1, D_MODEL), F32)
    return pl.pallas_call(
        body, name="grad_x", grid=(s // t,),
        in_specs=[_row_spec(t)] * 3 + [_full_spec((N_MOD, D_MODEL))],
        out_specs=[_row_spec(t), vec, vec],
        out_shape=[jax.ShapeDtypeStruct((s, D_MODEL), F32), vshape, vshape],
        compiler_params=_cparams(("arbitrary",)),
    )(dxa, dh1, x, mod6)


def _window(ref, i, t, s):
    r0 = pl.multiple_of(i * t, t)
    cur = ref[pl.ds(r0, t), :]
    lo = pl.multiple_of(jnp.maximum(r0 - HALO, 0), HALO)
    hi = pl.multiple_of(jnp.minimum(r0 + t, s - HALO), HALO)
    before = ref[pl.ds(lo, HALO), :] * (i > 0).astype(F32)
    after = ref[pl.ds(hi, HALO), :] * (i < s // t - 1).astype(F32)
    return jnp.concatenate([before, cur, after], axis=0)


def _tap(ext, shift):
    n = ext.shape[0]
    if shift == 0:
        return ext
    return pltpu.roll(ext, (-shift) % n, 0)


def _centre(ext, t):
    return ext[HALO:HALO + t]


def _conv_taps(ext, w, width):
    acc = None
    for k in range(width):
        term = _tap(ext, k - width // 2) * w[k:k + 1, :]
        acc = term if acc is None else acc + term
    return acc


def _silu(a):
    return a * jax.nn.sigmoid(a)


def _conv_silu_fwd(proj, w, b):
    s = proj.shape[0]
    cb = 256
    t = _tile(s, 256)
    off = D_SSM // cb

    def body(u_ref, w_ref, b_ref, o_ref):
        wv = w_ref[...]
        bv = b_ref[...]

        def step(i, carry):
            ext = _window(u_ref, i, t, s)
            a = _centre(_conv_taps(ext, wv, SSM_CONV), t) + bv
            o_ref[pl.ds(pl.multiple_of(i * t, t), t), :] = _silu(a)
            return carry

        lax.fori_loop(0, s // t, step, 0)

    return pl.pallas_call(
        body, name="conv_silu_fwd", grid=(D_XBC // cb,),
        in_specs=[pl.BlockSpec((s, cb), lambda j: (0, off + j)),
                  pl.BlockSpec((SSM_CONV, cb), lambda j: (0, j)),
                  pl.BlockSpec((1, cb), lambda j: (0, j))],
        out_specs=pl.BlockSpec((s, cb), lambda j: (0, j)),
        out_shape=jax.ShapeDtypeStruct((s, D_XBC), F32),
        compiler_params=_cparams(("parallel",)),
    )(proj, w, b)


def _conv_silu_bwd(name, proj, w, b, col0, ncols, cots, scaled=None):
    s = proj.shape[0]
    cb = 128
    t = _tile(s, 256)
    off = (D_SSM + col0) // cb
    woff = col0 // cb
    n_cot = len(cots)

    def body(*refs):
        u_ref, w_ref, b_ref = refs[:3]
        cot_refs = refs[3:3 + n_cot]
        sc_refs = refs[3 + n_cot:3 + n_cot + (2 if scaled else 0)]
        du_ref, dw_ref, db_ref = refs[-3:]
        wv = w_ref[...]
        bv = b_ref[...]

        def step(i, carry):
            ext = _window(u_ref, i, t, s)
            a = _conv_taps(ext, wv, SSM_CONV) + bv
            cot = None
            for cr in cot_refs:
                term = _window(cr.at[0], i, t, s) + _window(cr.at[1], i, t, s)
                cot = term if cot is None else cot + term
            if scaled:
                cot = cot + _window(sc_refs[0], i, t, s) * sc_refs[1][...]
            sig = jax.nn.sigmoid(a)
            da = cot * (sig * (1.0 + a * (1.0 - sig)))
            du = None
            new = []
            for k in range(SSM_CONV):
                sh = k - SSM_CONV // 2
                term = _tap(da, -sh) * wv[k:k + 1, :]
                du = term if du is None else du + term
                prod = _centre(_tap(ext, sh) * da, t)
                new.append(carry[k] + jnp.sum(prod, axis=0, keepdims=True))
            new.append(carry[SSM_CONV] + jnp.sum(_centre(da, t), axis=0, keepdims=True))
            du_ref[pl.ds(pl.multiple_of(i * t, t), t), :] = _centre(du, t).astype(BF16)
            return tuple(new)

        zero = jnp.zeros((1, cb), F32)
        acc = lax.fori_loop(0, s // t, step, tuple(zero for _ in range(SSM_CONV + 1)))
        for k in range(SSM_CONV):
            dw_ref[k:k + 1, :] = acc[k]
        db_ref[...] = acc[SSM_CONV]

    in_specs = [pl.BlockSpec((s, cb), lambda j: (0, off + j)),
                pl.BlockSpec((SSM_CONV, cb), lambda j: (0, woff + j)),
                pl.BlockSpec((1, cb), lambda j: (0, woff + j))]
    in_specs += [pl.BlockSpec((2, s, cb), lambda j: (0, 0, j)) for _ in cots]
    args = [proj, w, b, *cots]
    if scaled:
        in_specs += [pl.BlockSpec((s, cb), lambda j: (0, j)), pl.BlockSpec((1, cb), lambda j: (0, j))]
        args += list(scaled)
    return pl.pallas_call(
        body, name=name, grid=(ncols // cb,),
        in_specs=in_specs,
        out_specs=[pl.BlockSpec((s, cb), lambda j: (0, j)),
                   pl.BlockSpec((SSM_CONV, cb), lambda j: (0, j)),
                   pl.BlockSpec((1, cb), lambda j: (0, j))],
        out_shape=[jax.ShapeDtypeStruct((s, ncols), BF16),
                   jax.ShapeDtypeStruct((SSM_CONV, ncols), F32),
                   jax.ShapeDtypeStruct((1, ncols), F32)],
        compiler_params=_cparams(("parallel",)),
    )(*args)


_SC_H = (D_SSM + D_XBC) // SC_GROUP_W
_SC_B = _SC_H + D_SC // SC_GROUP_W
_SC_C = _SC_B + D_SC // SC_GROUP_W


def _sc_fwd(proj, w, nw):
    s = proj.shape[0]
    cb = SC_GROUP_W
    t = _tile(s, 256)

    def body(uh_ref, ub_ref, uc_ref, w_ref, nw_ref, o_ref):
        wv = w_ref[...]
        nwv = nw_ref[...]

        def step(i, carry):
            p = _window(uc_ref, i, t, s) * _window(uh_ref, i, t, s)
            cv = _centre(_conv_taps(p, wv, SC_CONV), t)
            rows = pl.ds(pl.multiple_of(i * t, t), t)
            y = ub_ref[rows, :] * cv
            r = lax.rsqrt(jnp.mean(y * y, axis=-1, keepdims=True) + RMS_EPS)
            o_ref[rows, :] = (y * r * nwv).astype(BF16)
            return carry

        lax.fori_loop(0, s // t, step, 0)

    def col(base):
        return pl.BlockSpec((s, cb), lambda j: (0, base + j))

    return pl.pallas_call(
        body, name="sc_fwd", grid=(D_SC // cb,),
        in_specs=[col(_SC_H), col(_SC_B), col(_SC_C),
                  pl.BlockSpec((SC_CONV, cb), lambda j: (0, j)),
                  pl.BlockSpec((1, cb), lambda j: (0, j))],
        out_specs=pl.BlockSpec((s, cb), lambda j: (0, j)),
        out_shape=jax.ShapeDtypeStruct((s, D_SC), BF16),
        compiler_params=_cparams(("parallel",)),
    )(proj, proj, proj, w, nw)


def _sc_bwd(proj, dycat, w, nw):
    s = proj.shape[0]
    cb = SC_GROUP_W
    t = _tile(s, 256)
    dy_off = D_SSM // cb

    def body(uh_ref, ub_ref, uc_ref, dy_ref, w_ref, nw_ref, duh_ref, dub_ref, duc_ref, dw_ref, dnw_ref):
        wv = w_ref[...]
        nwv = nw_ref[...]

        def step(i, carry):
            uh = _window(uh_ref, i, t, s)
            ub = _window(ub_ref, i, t, s)
            uc = _window(uc_ref, i, t, s)
            do = _window(dy_ref, i, t, s)
            p = uc * uh
            cv = _conv_taps(p, wv, SC_CONV)
            y = ub * cv
            r = lax.rsqrt(jnp.mean(y * y, axis=-1, keepdims=True) + RMS_EPS)
            dyr = do * nwv
            dy = r * dyr - y * (r * r * r) * jnp.mean(dyr * y, axis=-1, keepdims=True)
            dcv = dy * ub
            dp = None
            new = []
            for k in range(SC_CONV):
                sh = k - SC_CONV // 2
                term = _tap(dcv, -sh) * wv[k:k + 1, :]
                dp = term if dp is None else dp + term
                new.append(carry[k] + jnp.sum(_centre(_tap(p, sh) * dcv, t), axis=0, keepdims=True))
            new.append(carry[SC_CONV] + jnp.sum(_centre(do * y * r, t), axis=0, keepdims=True))
            rows = pl.ds(pl.multiple_of(i * t, t), t)
            duh_ref[rows, :] = _centre(dp * uc, t).astype(BF16)
            duc_ref[rows, :] = _centre(dp * uh, t).astype(BF16)
            dub_ref[rows, :] = _centre(dy * cv, t).astype(BF16)
            return tuple(new)

        zero = jnp.zeros((1, cb), F32)
        acc = lax.fori_loop(0, s // t, step, tuple(zero for _ in range(SC_CONV + 1)))
        for k in range(SC_CONV):
            dw_ref[k:k + 1, :] = acc[k]
        dnw_ref[...] = acc[SC_CONV]

    def col(base):
        return pl.BlockSpec((s, cb), lambda j: (0, base + j))

    out_col = pl.BlockSpec((s, cb), lambda j: (0, j))
    act = jax.ShapeDtypeStruct((s, D_SC), BF16)
    return pl.pallas_call(
        body, name="sc_bwd", grid=(D_SC // cb,),
        in_specs=[col(_SC_H), col(_SC_B), col(_SC_C), col(dy_off),
                  pl.BlockSpec((SC_CONV, cb), lambda j: (0, j)),
                  pl.BlockSpec((1, cb), lambda j: (0, j))],
        out_specs=[out_col, out_col, out_col,
                   pl.BlockSpec((SC_CONV, cb), lambda j: (0, j)),
                   pl.BlockSpec((1, cb), lambda j: (0, j))],
        out_shape=[act, act, act, jax.ShapeDtypeStruct((SC_CONV, D_SC), F32),
                   jax.ShapeDtypeStruct((1, D_SC), F32)],
        compiler_params=_cparams(("parallel",)),
    )(proj, proj, proj, dycat, w, nw)


def _make_select_dot(differentiable):
    def raw(a, b, mode, const):
        ops = [a, b]
        v = ops[1 - const]
        acc = None
        for _ in range(3):
            piece = v.astype(BF16)
            v = v - piece.astype(F32)
            ops[1 - const] = piece
            part = lax.dot_general(ops[0].astype(BF16), ops[1].astype(BF16), _DN[mode],
                                   preferred_element_type=F32)
            acc = part if acc is None else acc + part
        return acc

    if not differentiable:
        return raw

    @functools.partial(jax.custom_vjp, nondiff_argnums=(2, 3))
    def dot(a, b, mode, const):
        return raw(a, b, mode, const)

    def fwd(a, b, mode, const):
        return raw(a, b, mode, const), (a, b)

    def bwd(mode, const, res, g):
        a, b = res
        assert mode == "nn"
        if const == 1:
            return raw(g, b, "nt", 1), jnp.zeros_like(b)
        return jnp.zeros_like(a), raw(a, g, "tn", 0)

    dot.defvjp(fwd, bwd)
    return dot


def _make_dot(differentiable):
    def raw(a, b, mode):
        return lax.dot_general(a.astype(BF16), b.astype(BF16), _DN[mode], preferred_element_type=F32)

    if not differentiable:
        return raw

    @functools.partial(jax.custom_vjp, nondiff_argnums=(2,))
    def dot(a, b, mode):
        return raw(a, b, mode)

    def fwd(a, b, mode):
        return raw(a, b, mode), (a, b)

    def bwd(mode, res, g):
        a, b = res
        if mode == "nn":
            return raw(g, b, "nt"), raw(a, g, "tn")
        if mode == "nt":
            return raw(g, b, "nn"), raw(g, a, "tn")
        return raw(b, g, "nt"), raw(a, g, "nn")

    dot.defvjp(fwd, bwd)
    return dot


def _make_swap(differentiable):
    def raw(v):
        return pltpu.roll(v, HEAD_DIM, 1)

    if not differentiable:
        return raw
    swap = jax.custom_vjp(raw)
    swap.defvjp(lambda v: (raw(v), None), lambda _, g: (raw(g),))
    return swap


def _ssd_chunk(xs, bm, cm, dtx, acx, ax, prev, tri, differentiable):
    _bdot = _make_dot(differentiable)
    swap = _make_swap(differentiable)
    atx = jnp.sum(dtx * ax, axis=0, keepdims=True)
    xdt = xs * dtx
    mask = tri > 0.0
    scores = _bdot(cm, bm, "nt")
    head = lax.broadcasted_iota(jnp.int32, (1, GROUP_W), 1) // HEAD_DIM
    low = lax.broadcasted_iota(jnp.int32, (1, 128), 1) < HEAD_DIM
    y = _bdot(cm, prev, "nn") * jnp.exp(acx)
    for h in range(HEADS_PER_GROUP):
        pair = acx[:, 128 * (h // 2):128 * (h // 2) + 128]
        other = swap(pair)
        m1 = jnp.where(low, pair, other) if h % 2 == 0 else jnp.where(low, other, pair)
        seg = m1 - m1.T
        decay = jnp.where(mask, jnp.exp(jnp.where(mask, seg, 0.0)), 0.0)
        xh = xdt * (head == h).astype(F32)
        y = y + _bdot(scores * decay, xh, "nn")
    new = prev * jnp.exp(atx) + _bdot(bm, xdt * jnp.exp(atx - acx), "tn")
    return y, new


def _softplus(v):
    return jnp.maximum(v, 0.0) + jnp.log(1.0 + jnp.exp(-jnp.abs(v)))


def _dt_spread(u, bias, a, tri2, exf, differentiable):
    sel = _make_select_dot(differentiable)
    dt = _softplus(u + bias)
    dta = dt * a
    out = []
    for d in range(2):
        acum = sel(tri2[d], dta, "nn", 0)
        out += [sel(dt, exf[d], "nn", 1), sel(acum, exf[d], "nn", 1)]
    return tuple(out)


def _ssd_consts():
    q = CHUNK
    r = lax.broadcasted_iota(jnp.int32, (q, q), 0)
    c = lax.broadcasted_iota(jnp.int32, (q, q), 1)
    tri = jnp.stack([(c <= r), (c >= r)]).astype(F32)
    shp = (2, 128, D_SSM)
    src = lax.broadcasted_iota(jnp.int32, shp, 1)
    d = lax.broadcasted_iota(jnp.int32, shp, 0)
    col = lax.broadcasted_iota(jnp.int32, shp, 2)
    exf = (src == d * HEADS + col // HEAD_DIM).astype(F32)
    return tri, exf


def _dt_prep(proj_dt, bias_all, a_all):
    s = proj_dt.shape[0]
    tri, exf = _ssd_consts()

    def body(u_ref, b_ref, a_ref, tri_ref, exf_ref, dtx_ref, acx_ref):
        dtx0, acx0, dtx1, acx1 = _dt_spread(u_ref[...], b_ref[...], a_ref[...], tri_ref[...],
                                            exf_ref[...], False)
        dtx_ref[0] = dtx0
        dtx_ref[1] = dtx1
        acx_ref[0] = acx0
        acx_ref[1] = acx1

    out = pl.BlockSpec((2, CHUNK, D_SSM), lambda i: (0, i, 0))
    shp = jax.ShapeDtypeStruct((2, s, D_SSM), F32)
    return pl.pallas_call(
        body, name="dt_prep", grid=(s // CHUNK,),
        in_specs=[_row_spec(CHUNK, 128), _full_spec((1, 128)), _full_spec((1, 128)),
                  _full_spec((2, CHUNK, CHUNK)), _full_spec((2, 128, D_SSM))],
        out_specs=[out, out], out_shape=[shp, shp],
        compiler_params=_cparams(("parallel",)),
    )(proj_dt, bias_all, a_all, tri, exf)


def _dt_prep_bwd(proj_dt, bias_all, a_all, d_dtx, d_acx):
    s = proj_dt.shape[0]
    tri, exf = _ssd_consts()

    def body(u_ref, b_ref, a_ref, tri_ref, exf_ref, ddtx_ref, dacx_ref, du_ref, db_ref, da_ref):
        tri_v, exf_v = tri_ref[...], exf_ref[...]

        def f(u, bias, a):
            return _dt_spread(u, bias, a, tri_v, exf_v, True)

        _, vjp = jax.vjp(f, u_ref[...], b_ref[...], a_ref[...])
        du, db, da = vjp((ddtx_ref[0], dacx_ref[0], ddtx_ref[1], dacx_ref[1]))
        du_ref[...] = du.astype(BF16)
        first = pl.program_id(0) == 0
        _acc_rows(db_ref, db, first)
        _acc_rows(da_ref, da, first)

    cot = pl.BlockSpec((2, CHUNK, D_SSM), lambda i: (0, i, 0))
    vec = _full_spec((1, 128))
    return pl.pallas_call(
        body, name="dt_prep_bwd", grid=(s // CHUNK,),
        in_specs=[_row_spec(CHUNK, 128), vec, vec, _full_spec((2, CHUNK, CHUNK)),
                  _full_spec((2, 128, D_SSM)), cot, cot],
        out_specs=[_row_spec(CHUNK, 128), vec, vec],
        out_shape=[jax.ShapeDtypeStruct((s, 128), BF16), jax.ShapeDtypeStruct((1, 128), F32),
                   jax.ShapeDtypeStruct((1, 128), F32)],
        compiler_params=_cparams(("arbitrary",)),
    )(proj_dt, bias_all, a_all, tri, exf, d_dtx, d_acx)


GROUPS_PER_STEP = 2
_PAIR_W = GROUPS_PER_STEP * GROUP_W
_PAIR_N = GROUPS_PER_STEP * N_STATE


def _ssd_specs(chunk_of):
    q = CHUNK
    b0 = D_SSM // _PAIR_N
    xs = pl.BlockSpec((q, _PAIR_W), lambda d, g, ci: (chunk_of(d, ci), g))
    bm = pl.BlockSpec((q, _PAIR_N), lambda d, g, ci: (chunk_of(d, ci), b0 + g))
    cm = pl.BlockSpec((q, _PAIR_N), lambda d, g, ci: (chunk_of(d, ci), b0 + GROUPS // GROUPS_PER_STEP + g))
    spread = pl.BlockSpec((None, q, _PAIR_W), lambda d, g, ci: (d, chunk_of(d, ci), g))
    ax = pl.BlockSpec((None, 1, _PAIR_W), lambda d, g, ci: (d, 0, g))
    tri = pl.BlockSpec((None, q, q), lambda d, g, ci: (d, 0, 0))
    st = pl.BlockSpec((None, None, GROUPS_PER_STEP, N_STATE, GROUP_W),
                      lambda d, g, ci: (d, chunk_of(d, ci), g, 0, 0))
    return xs, bm, cm, spread, ax, tri, st


def _wide(k):
    return slice(k * GROUP_W, (k + 1) * GROUP_W)


def _narrow(k):
    return slice(k * N_STATE, (k + 1) * N_STATE)


def _ssd_fwd(xbc, dtx, acx, ax, jobs=()):
    s = xbc.shape[0]
    nc = s // CHUNK
    tri, _ = _ssd_consts()

    def chunk_of(d, ci):
        return ci + d * (nc - 1 - 2 * ci)

    def body(xs_ref, b_ref, c_ref, dtx_ref, acx_ref, ax_ref, tri_ref, y_ref, st_ref, state):
        @pl.when(pl.program_id(2) == 0)
        def _():
            state[...] = jnp.zeros(state.shape, F32)

        tri_v = tri_ref[...]
        for k in range(GROUPS_PER_STEP):
            prev = state[k]
            st_ref[k] = prev
            y, new = _ssd_chunk(xs_ref[:, _wide(k)], b_ref[:, _narrow(k)], c_ref[:, _narrow(k)],
                                dtx_ref[:, _wide(k)], acx_ref[:, _wide(k)], ax_ref[:, _wide(k)],
                                prev, tri_v, False)
            y_ref[:, _wide(k)] = y
            state[k] = new

    xs, bm, cm, spread, ax_s, tri_s, st = _ssd_specs(chunk_of)
    return _call(
        "ssd_fwd", body, grid=(2, GROUPS // GROUPS_PER_STEP, nc),
        in_specs=[xs, bm, cm, spread, spread, ax_s, tri_s],
        out_specs=[spread, st],
        out_shape=[jax.ShapeDtypeStruct((2, s, D_SSM), F32),
                   jax.ShapeDtypeStruct((2, nc, GROUPS, N_STATE, GROUP_W), F32)],
        scratch_shapes=[pltpu.VMEM((GROUPS_PER_STEP, N_STATE, GROUP_W), F32)],
        sem=("arbitrary", "arbitrary", "arbitrary"),
        args=(xbc, xbc, xbc, dtx, acx, ax, tri), jobs=jobs)


def _ssd_bwd(xbc, dtx, acx, ax, states, dy, jobs=()):
    s = xbc.shape[0]
    nc = s // CHUNK
    tri, _ = _ssd_consts()

    def chunk_of(d, ci):
        return (nc - 1 - ci) + d * (2 * ci - (nc - 1))

    def body(xs_ref, b_ref, c_ref, dtx_ref, acx_ref, ax_ref, tri_ref, st_ref, dy_ref,
             dxs_ref, db_ref, dc_ref, ddtx_ref, dacx_ref, dax_ref, dstate):
        first = pl.program_id(2) == 0

        @pl.when(first)
        def _():
            dstate[...] = jnp.zeros(dstate.shape, F32)

        tri_v = tri_ref[...]

        def f(xs, bm, cm, dtx_v, acx_v, ax_v, prev):
            return _ssd_chunk(xs, bm, cm, dtx_v, acx_v, ax_v, prev, tri_v, True)

        dax_parts = []
        for k in range(GROUPS_PER_STEP):
            _, vjp = jax.vjp(f, xs_ref[:, _wide(k)], b_ref[:, _narrow(k)], c_ref[:, _narrow(k)],
                             dtx_ref[:, _wide(k)], acx_ref[:, _wide(k)], ax_ref[:, _wide(k)], st_ref[k])
            dxs, dbm, dcm, ddtx, dacx, dax, dprev = vjp((dy_ref[:, _wide(k)], dstate[k]))
            dxs_ref[:, _wide(k)] = dxs
            db_ref[:, _narrow(k)] = dbm
            dc_ref[:, _narrow(k)] = dcm
            ddtx_ref[:, _wide(k)] = ddtx
            dacx_ref[:, _wide(k)] = dacx
            dstate[k] = dprev
            dax_parts.append(dax)
        _acc_rows(dax_ref, jnp.concatenate(dax_parts, axis=1), first)

    xs, bm, cm, spread, ax_s, tri_s, st = _ssd_specs(chunk_of)
    dy_s = pl.BlockSpec((CHUNK, _PAIR_W), lambda d, g, ci: (chunk_of(d, ci), g))
    bc_s = pl.BlockSpec((None, CHUNK, _PAIR_N), lambda d, g, ci: (d, chunk_of(d, ci), g))
    wide = jax.ShapeDtypeStruct((2, s, D_SSM), F32)
    narrow = jax.ShapeDtypeStruct((2, s, GROUPS * N_STATE), F32)
    return _call(
        "ssd_bwd", body, grid=(2, GROUPS // GROUPS_PER_STEP, nc),
        in_specs=[xs, bm, cm, spread, spread, ax_s, tri_s, st, dy_s],
        out_specs=[spread, bc_s, bc_s, spread, spread, ax_s],
        out_shape=[wide, narrow, narrow, wide, wide, jax.ShapeDtypeStruct((2, 1, D_SSM), F32)],
        scratch_shapes=[pltpu.VMEM((GROUPS_PER_STEP, N_STATE, GROUP_W), F32)],
        sem=("arbitrary", "arbitrary", "arbitrary"),
        args=(xbc, xbc, xbc, dtx, acx, ax, tri, states, dy), jobs=jobs)


def _ssd_gate_fwd(y2, xbc, proj, dx, nw):
    s = xbc.shape[0]
    t = _tile(s, 512)

    def body(y_ref, xs_ref, z_ref, dx_ref, nw_ref, o_ref):
        y = (y_ref[0] + y_ref[1] + dx_ref[...] * xs_ref[...]) * _silu(z_ref[...])
        r = lax.rsqrt(jnp.mean(y * y, axis=-1, keepdims=True) + RMS_EPS)
        o_ref[...] = (y * r * nw_ref[...]).astype(BF16)

    blk = pl.BlockSpec((t, GROUP_W), lambda j, i: (i, j))
    vec = pl.BlockSpec((1, GROUP_W), lambda j, i: (0, j))
    return pl.pallas_call(
        body, name="ssd_gate_fwd", grid=(GROUPS, s // t),
        in_specs=[pl.BlockSpec((2, t, GROUP_W), lambda j, i: (0, i, j)), blk, blk, vec, vec],
        out_specs=blk,
        out_shape=jax.ShapeDtypeStruct((s, D_SSM), BF16),
        compiler_params=_cparams(("parallel", "parallel")),
    )(y2, xbc, proj, dx, nw)


def _ssd_gate_bwd(y2, xbc, proj, dycat, dx, nw, jobs=()):
    s = xbc.shape[0]
    t = _tile(s, 512)

    def body(y_ref, xs_ref, z_ref, do_ref, dx_ref, nw_ref, dyc_ref, dz_ref, dd_ref, dnw_ref):
        first = pl.program_id(1) == 0
        z = z_ref[...]
        xs = xs_ref[...]
        sig = jax.nn.sigmoid(z)
        gate = z * sig
        yc = y_ref[0] + y_ref[1] + dx_ref[...] * xs
        y = yc * gate
        r = lax.rsqrt(jnp.mean(y * y, axis=-1, keepdims=True) + RMS_EPS)
        do = do_ref[...]
        dyr = do * nw_ref[...]
        dy = r * dyr - y * (r * r * r) * jnp.mean(dyr * y, axis=-1, keepdims=True)
        dyc = dy * gate
        dyc_ref[...] = dyc
        dz_ref[...] = (dy * yc * (sig * (1.0 + z * (1.0 - sig)))).astype(BF16)
        _acc_rows(dd_ref, dyc * xs, first)
        _acc_rows(dnw_ref, do * y * r, first)

    blk = pl.BlockSpec((t, GROUP_W), lambda j, i: (i, j))
    vec = pl.BlockSpec((1, GROUP_W), lambda j, i: (0, j))
    return _call(
        "ssd_gate_bwd", body, grid=(GROUPS, s // t),
        in_specs=[pl.BlockSpec((2, t, GROUP_W), lambda j, i: (0, i, j)), blk, blk, blk, vec, vec],
        out_specs=[blk, blk, vec, vec],
        out_shape=[jax.ShapeDtypeStruct((s, D_SSM), F32), jax.ShapeDtypeStruct((s, D_SSM), BF16),
                   jax.ShapeDtypeStruct((1, D_SSM), F32), jax.ShapeDtypeStruct((1, D_SSM), F32)],
        sem=("parallel", "arbitrary"), args=(y2, xbc, proj, dycat, dx, nw), jobs=jobs)


def _ada_fwd(c16, w_ada):
    k, n = w_ada.shape
    tn = 512

    def body(c_ref, w_ref, o_ref):
        a = _silu(c_ref[...]).astype(BF16)
        o_ref[...] = jnp.dot(a, w_ref[...].astype(BF16), preferred_element_type=F32)

    return pl.pallas_call(
        body, name="ada_fwd", grid=(n // tn,),
        in_specs=[_full_spec((16, k)), pl.BlockSpec((k, tn), lambda j: (0, j))],
        out_specs=pl.BlockSpec((16, tn), lambda j: (0, j)),
        out_shape=jax.ShapeDtypeStruct((16, n), F32),
        compiler_params=_cparams(("parallel",)),
    )(c16, w_ada)


def _ada_bwd_adamw(c16, dmod16, w, m, v):
    k, n = w.shape
    tm, tn = 256, n
    blk = pl.BlockSpec((tm, tn), lambda i, j: (i, j))

    def body(c_ref, d_ref, w_ref, m_ref, v_ref, g_out, d_out, m_out, v_out):
        a = _silu(c_ref[...]).astype(BF16)
        g = lax.dot_general(a, d_ref[...].astype(BF16), _DN["tn"], preferred_element_type=F32)
        d, mn, vn = _adamw_math(w_ref[...], g, m_ref[...], v_ref[...])
        g_out[...] = g
        d_out[...] = d
        m_out[...] = mn
        v_out[...] = vn

    shp = jax.ShapeDtypeStruct((k, n), F32)
    return pl.pallas_call(
        body, name="ada_bwd_adamw", grid=(k // tm, n // tn),
        in_specs=[pl.BlockSpec((16, tm), lambda i, j: (0, i)), pl.BlockSpec((16, tn), lambda i, j: (0, j)),
                  blk, blk, blk],
        out_specs=[blk, blk, blk, blk],
        out_shape=[shp, shp, shp, shp],
        compiler_params=_cparams(("parallel", "parallel")),
    )(c16, dmod16, w, m, v)


def _sum8_adamw(gathered, w, m, v):
    n = w.shape[1]
    tn = _tile(n, 8192)
    vec = pl.BlockSpec((1, tn), lambda j: (0, j))

    def body(g8_ref, w_ref, m_ref, v_ref, g_out, d_out, m_out, v_out):
        g = g8_ref[0:1, :]
        for k in range(1, N_DEV):
            g = g + g8_ref[k:k + 1, :]
        d, mn, vn = _adamw_math(w_ref[...], g, m_ref[...], v_ref[...])
        g_out[...] = g
        d_out[...] = d
        m_out[...] = mn
        v_out[...] = vn

    shp = jax.ShapeDtypeStruct((1, n), F32)
    return pl.pallas_call(
        body, name="sum8_adamw", grid=(n // tn,),
        in_specs=[pl.BlockSpec((N_DEV, tn), lambda j: (0, j)), vec, vec, vec],
        out_specs=[vec, vec, vec, vec],
        out_shape=[shp, shp, shp, shp],
        compiler_params=_cparams(("parallel",)),
    )(gathered, w, m, v)


def _gather_vec(name, v):
    n = v.shape[1]
    out = _run_jobs(name, [_GatherJob(v.reshape(8, n // 8), pltpu.VMEM)])[0]
    return out.reshape(N_DEV, n)


class _Plan:
    _RESULT = {"gather": "", "gather_lo": "lo_", "gather_hi": "", "rs1": "r1_", "rs2": "r2_",
               "rs2a": "ra_", "rs2b": "rb_"}

    def __init__(self, hosted, store, hooks=None, two_leg=()):
        self.hosted, self.store, self.hooks, self.two_leg = hosted, dict(store), hooks or {}, two_leg

    def get(self, key):
        if key not in self.store and key.startswith("p_"):
            tag = key[2:]
            dests = (2, 3) if tag in self.two_leg else (0, 1, 2)
            self.store[key] = _pair_add("rs_pair_add_" + tag, self.get("g_" + tag), self.get("r1_" + tag),
                                        self.get("pos"), dests)
        if key not in self.store and key.startswith("q_"):
            tag = key[2:]
            self.store[key] = _relay_add("rs_relay_add_" + tag, self.get("g_" + tag), self.get("r1_" + tag),
                                         self.get("ra_" + tag), self.get("pos"))
        return self.store[key]

    def put(self, key, val):
        self.store[key] = val

    def jobs(self, host):
        def half(t, hi):
            shard = self.get("shard_" + t)
            n = shard.shape[0] // 2
            return _GatherJob(shard, rows=(n * hi, n), into=self.get("lo_" + t) if hi else None)

        make = {"gather": lambda t: _GatherJob(self.get("shard_" + t)),
                "gather_lo": lambda t: half(t, 0), "gather_hi": lambda t: half(t, 1),
                "rs1": lambda t: _SiblingJob(self.get("g_" + t)),
                "rs2": lambda t: _ChipsJob(self.get("p_" + t)),
                "rs2a": lambda t: _RelayFirstJob(self.get("p_" + t)),
                "rs2b": lambda t: _RelaySecondJob(self.get("q_" + t))}
        return [make[kind](tag) for kind, tag in self.hosted.get(host, ())]

    def run(self, host, fn, *args, **kw):
        outs, results = fn(*args, jobs=self.jobs(host), **kw)
        for (kind, tag), res in zip(self.hosted.get(host, ()), results):
            self.store[self._RESULT[kind] + tag] = res
        return outs

    def hook(self, name):
        if name in self.hooks:
            self.hooks[name](self)


def _pad_lanes(v, n):
    return jnp.pad(v, ((0, 0), (0, n - v.shape[1])))


def _local_step(plan, x, tgt, mod, conv_w, conv_b, dt_bias_f, dt_bias_b, a_log_f, a_log_b,
                ssm_d, ssm_nw, sc_w, sc_nw, ln1_g, ln1_b, ln2_g, ln2_b):
    s = x.shape[0]
    run = plan.run
    mod6 = mod.reshape(N_MOD, D_MODEL)
    bias_all = _pad_lanes(jnp.concatenate([dt_bias_f, dt_bias_b], axis=1), 128)
    a_all = _pad_lanes(-jnp.exp(jnp.concatenate([a_log_f, a_log_b], axis=1)), 128)
    a_x = jnp.stack([jnp.repeat(a_all[:, d * HEADS:(d + 1) * HEADS], HEAD_DIM, axis=1) for d in range(2)])
    d_lanes = jnp.repeat(ssm_d, HEAD_DIM, axis=1)

    w_in_g = plan.get("w_in")
    w_in_main, w_in_dt = _w_in_sections(w_in_g)
    h1 = _modulate("mod1", x, mod6)
    proj, = run("in_proj", _mm_nn, "in_proj", h1, w_in_main, F32)
    proj_dt = _mm_nn("in_proj_dt", h1, w_in_dt, F32)[0][0]
    xbc = _conv_silu_fwd(proj, conv_w, conv_b)
    dtx, acx = _dt_prep(proj_dt, bias_all, a_all)
    y2, states = run("ssd_fwd", _ssd_fwd, xbc, dtx, acx, a_x)
    y_ssm = _ssd_gate_fwd(y2, xbc, proj, d_lanes, ssm_nw)
    y_sc = _sc_fwd(proj, sc_w, sc_nw)
    ycat = jnp.concatenate([y_ssm, y_sc], axis=1)
    w_out_g = plan.get("w_out").reshape(D_MODEL, D_MODEL)
    mix = _mm_nn("out_proj", ycat, w_out_g, F32)[0][0]
    x1, h2 = _ln1_fwd(x, mix, mod6, ln1_g, ln1_b)

    def relu2(acc):
        u = acc.astype(BF16)
        r = jnp.maximum(acc, 0.0)
        return u, r * r

    w_up3 = plan.get("w_up")
    nper = w_up3.shape[2]
    tm = _tile(s, 1024)
    tn = 1024
    nb = nper // tn
    u_spec = pl.BlockSpec((tm, tn), lambda i, j, kk: (i, j))
    u, ff = run(
        "up_proj", _matmul, "up_proj", h2, w_up3, mode="nn", grid=(s // tm, D_FF // tn, 1),
        a_spec=pl.BlockSpec((tm, D_MODEL), lambda i, j, kk: (i, 0)),
        b_spec=pl.BlockSpec((None, D_MODEL, tn), lambda i, j, kk: (j // nb, 0, j % nb)),
        out_shapes=[jax.ShapeDtypeStruct((s, D_FF), BF16)] * 2, out_specs=[u_spec, u_spec],
        acc_shape=(tm, tn), epilogue=relu2)
    w_down_g = plan.get("w_down").reshape(D_FF, D_MODEL)
    f2 = _mm_nn("down_proj", ff, w_down_g, F32)[0][0]
    df2, dx1a, loss, g_ln2_g, g_ln2_b, dgate2 = _ln2_loss_bwd(x1, f2, tgt, mod6, ln2_g, ln2_b)

    def relu_grad(acc, uu):
        return (acc * (2.0 * jnp.maximum(uu.astype(F32), 0.0)),)

    du = _mm_nt("d_ff", df2, w_down_g, BF16, epilogue=relu_grad, extras=(u,))[0][0]
    plan.put("g_down", _mm_tn("g_w_down", ff, df2, BF16)[0][0].reshape(N_DEV, D_FF // N_DEV, D_MODEL))
    g_up, = run(
        "g_w_up", _matmul, "g_w_up", h2, du, mode="tn",
        grid=(D_MODEL // 1024, D_FF // tn, s // _tile(s, 2048)),
        a_spec=pl.BlockSpec((_tile(s, 2048), 1024), lambda i, j, kk: (kk, i)),
        b_spec=pl.BlockSpec((_tile(s, 2048), tn), lambda i, j, kk: (kk, j)),
        out_shapes=[jax.ShapeDtypeStruct((N_DEV, D_MODEL, nper), BF16)],
        out_specs=[pl.BlockSpec((None, 1024, tn), lambda i, j, kk: (j // nb, i, j % nb))],
        acc_shape=(1024, tn))
    plan.put("g_up", g_up)
    dh2, = run(
        "d_h2", _matmul, "d_h2", du, w_up3, mode="nt", grid=(s // tm, D_MODEL // 1024, D_FF // nper),
        a_spec=pl.BlockSpec((tm, nper), lambda i, j, kk: (i, kk)),
        b_spec=pl.BlockSpec((None, 1024, nper), lambda i, j, kk: (kk, j, 0)),
        out_shapes=[jax.ShapeDtypeStruct((s, D_MODEL), F32)],
        out_specs=[pl.BlockSpec((tm, 1024), lambda i, j, kk: (i, j))],
        acc_shape=(tm, 1024))
    dmix, dxa, dscale2, dshift2, g_ln1_g, g_ln1_b, dgate1 = _ln1_bwd(dh2, dx1a, x1, x, mix, mod6, ln1_g)

    dycat = _mm_nt("d_ycat", dmix, w_out_g, F32)[0][0]
    plan.put("g_out", run("g_w_out", _mm_tn, "g_w_out", ycat, dmix, BF16)[0].reshape(
        N_DEV, D_MODEL // N_DEV, D_MODEL))
    duh, dub, duc, g_sc_w, g_sc_nw = _sc_bwd(proj, dycat, sc_w, sc_nw)
    dyc, dz, dd_lanes, g_ssm_nw = run("ssd_gate_bwd", _ssd_gate_bwd, y2, xbc, proj, dycat, d_lanes, ssm_nw)
    dxs2, db2, dc2, ddtx, dacx, dax = run("ssd_bwd", _ssd_bwd, xbc, dtx, acx, a_x, states, dyc)
    n_bc = GROUPS * N_STATE
    du_xs, gw_xs, gb_xs = _conv_silu_bwd("conv_bwd_x", proj, conv_w, conv_b, 0, D_SSM, [dxs2],
                                         scaled=(dyc, d_lanes))
    du_b, gw_b, gb_b = _conv_silu_bwd("conv_bwd_b", proj, conv_w, conv_b, D_SSM, n_bc, [db2])
    du_c, gw_c, gb_c = _conv_silu_bwd("conv_bwd_c", proj, conv_w, conv_b, D_SSM + n_bc, n_bc, [dc2])
    du_dt, g_bias_all, g_a_sums = _dt_prep_bwd(proj_dt, bias_all, a_all, ddtx, dacx)

    dproj = jnp.concatenate([dz, du_xs, du_b, du_c, du_dt[:, :2 * HEADS], duh, dub, duc], axis=1)
    dproj3 = dproj.reshape(s, N_DEV, D_IN_SHARD).transpose(1, 0, 2)
    tk = _tile(s, 2048)
    g_in, = run(
        "g_w_in", _matmul, "g_w_in", h1, dproj3, mode="tn", grid=(N_DEV, D_MODEL // 1024, s // tk),
        a_spec=pl.BlockSpec((tk, 1024), lambda i, j, kk: (kk, j)),
        b_spec=pl.BlockSpec((None, tk, D_IN_SHARD), lambda i, j, kk: (i, kk, 0)),
        out_shapes=[jax.ShapeDtypeStruct((N_DEV, D_MODEL, D_IN_SHARD), BF16)],
        out_specs=[pl.BlockSpec((None, 1024, D_IN_SHARD), lambda i, j, kk: (i, j, 0))],
        acc_shape=(1024, D_IN_SHARD))
    plan.put("g_in", g_in)
    plan.hook("after_g_w_in")
    dh1, = run(
        "d_h1", _matmul, "d_h1", dproj3, w_in_g, mode="nt", grid=(s // tm, D_MODEL // 1024, N_DEV),
        a_spec=pl.BlockSpec((None, tm, D_IN_SHARD), lambda i, j, kk: (kk, i, 0)),
        b_spec=pl.BlockSpec((None, 1024, D_IN_SHARD), lambda i, j, kk: (kk, j, 0)),
        out_shapes=[jax.ShapeDtypeStruct((s, D_MODEL), F32)],
        out_specs=[pl.BlockSpec((tm, 1024), lambda i, j, kk: (i, j))],
        acc_shape=(tm, 1024))
    grad_x, dscale1, dshift1 = _grad_x(dxa, dh1, x, mod6)

    dmod = jnp.concatenate([dshift1, dscale1, dgate1, dshift2, dscale2, dgate2], axis=1)
    g_a_direct = dax.reshape(2, HEADS, HEAD_DIM).sum(axis=-1).reshape(1, 2 * HEADS)
    g_a_all = g_a_sums + _pad_lanes(g_a_direct, 128)
    small = {
        "dmod": dmod,
        "ssm_conv_w": jnp.concatenate([gw_xs, gw_b, gw_c], axis=1),
        "ssm_conv_b": jnp.concatenate([gb_xs, gb_b, gb_c], axis=1),
        "ssm_dt_bias_f": g_bias_all[:, :HEADS],
        "ssm_dt_bias_b": g_bias_all[:, HEADS:2 * HEADS],
        "ssm_a_log_f": (g_a_all * a_all)[:, :HEADS],
        "ssm_a_log_b": (g_a_all * a_all)[:, HEADS:2 * HEADS],
        "ssm_d": dd_lanes.reshape(HEADS, HEAD_DIM).sum(axis=1).reshape(1, HEADS),
        "ssm_norm_w": g_ssm_nw,
        "sc_conv_w": g_sc_w,
        "sc_norm_w": g_sc_nw,
        "ln1_g": g_ln1_g, "ln1_b": g_ln1_b, "ln2_g": g_ln2_g, "ln2_b": g_ln2_b,
    }
    return loss, grad_x, small


_SUMMED = [("ssm_conv_b", D_XBC), ("ssm_dt_bias_f", HEADS), ("ssm_dt_bias_b", HEADS),
           ("ssm_a_log_f", HEADS), ("ssm_a_log_b", HEADS), ("ssm_d", HEADS),
           ("ssm_norm_w", D_SSM), ("sc_norm_w", D_SC),
           ("ln1_g", D_MODEL), ("ln1_b", D_MODEL), ("ln2_g", D_MODEL), ("ln2_b", D_MODEL)]


def _round_up(n, k):
    return (n + k - 1) // k * k


def _w_in_sections(w_in_g):
    w = w_in_g.transpose(1, 0, 2).reshape(D_MODEL, D_IN)
    dt_lo = D_SSM + D_XBC
    main = jnp.concatenate([w[:, :dt_lo], w[:, dt_lo + 2 * HEADS:]], axis=1)
    dt = _pad_lanes(w[:, dt_lo:dt_lo + 2 * HEADS], 128)
    return main, dt


def kernel(x, c, w_ada, b_ada, w_in, ssm_conv_w, ssm_conv_b, ssm_dt_bias_f, ssm_dt_bias_b, ssm_a_log_f, ssm_a_log_b, ssm_d, ssm_norm_w, sc_conv_w, sc_norm_w, w_out, ln1_g, ln1_b, w_up, w_down, ln2_g, ln2_b, loss_target, m_w_ada, m_b_ada, m_w_in, m_ssm_conv_w, m_ssm_conv_b, m_ssm_dt_bias_f, m_ssm_dt_bias_b, m_ssm_a_log_f, m_ssm_a_log_b, m_ssm_d, m_ssm_norm_w, m_sc_conv_w, m_sc_norm_w, m_w_out, m_ln1_g, m_ln1_b, m_w_up, m_w_down, m_ln2_g, m_ln2_b, v_w_ada, v_b_ada, v_w_in, v_ssm_conv_w, v_ssm_conv_b, v_ssm_dt_bias_f, v_ssm_dt_bias_b, v_ssm_a_log_f, v_ssm_a_log_b, v_ssm_d, v_ssm_norm_w, v_sc_conv_w, v_sc_norm_w, v_w_out, v_ln1_g, v_ln1_b, v_w_up, v_w_down, v_ln2_g, v_ln2_b):
    args = dict(locals())
    xi, yi, ci = _my_pos()
    me = 4 * xi + 2 * yi + ci
    pos = jnp.stack([xi, yi, ci]).astype(jnp.int32)
    s = x.shape[1]

    n_cw, n_sw = SSM_CONV * D_XBC // N_DEV, SC_CONV * D_SC // N_DEV
    vec = jnp.concatenate([c, ssm_conv_w[0].reshape(1, n_cw), sc_conv_w[0].reshape(1, n_sw)], axis=1)
    vec = _pad_lanes(vec, 8192)
    gath = _gather_vec("gather_c_conv", vec)
    c_all = gath[:, :D_MODEL]
    conv_w = gath[:, D_MODEL:D_MODEL + n_cw].reshape(N_DEV, SSM_CONV, D_XBC // N_DEV)
    conv_w = conv_w.transpose(1, 0, 2).reshape(SSM_CONV, D_XBC)
    sc_w = gath[:, D_MODEL + n_cw:D_MODEL + n_cw + n_sw].reshape(N_DEV, SC_CONV, D_SC // N_DEV)
    sc_w = sc_w.transpose(1, 0, 2).reshape(SC_CONV, D_SC)
    c16 = jnp.pad(c_all, ((0, 8), (0, 0)))

    n_ada = w_ada.shape[2]
    mod_cols = _ada_fwd(c16, w_ada[0])[:N_DEV]
    mod_all = _run_jobs("gather_mod", [_GatherJob(mod_cols, pltpu.VMEM)])[0]
    mod = lax.dynamic_index_in_dim(mod_all, me, axis=1, keepdims=False)
    mod = mod.reshape(1, N_MOD * D_MODEL) + b_ada

    out = {}

    def adamw(plan, tag):
        name = "w_" + tag
        if tag in two_leg:
            others = [(plan.get("ra_" + tag), 1), (plan.get("rb_" + tag), None)]
        else:
            others = [(plan.get("r2_" + tag), k) for k in range(3)]
        res = plan.run("rs_adamw_" + tag, _reduce_adamw, "rs_adamw_" + tag, plan.get("g_" + tag),
                       plan.get("r1_" + tag), others, pos, args[name][0], args["m_" + name][0],
                       args["v_" + name][0])
        out[name] = tuple(a[None] for a in res)

    two_leg = ("down", "up")
    hosted = {
        "in_proj": [("gather", "w_out"), ("gather_lo", "w_up")],
        "ssd_fwd": [("gather_hi", "w_up")],
        "up_proj": [("gather", "w_down")],
        "g_w_up": [("rs1", "down")],
        "d_h2": [("rs2a", "down"), ("rs1", "up")],
        "g_w_out": [("rs2b", "down")],
        "ssd_gate_bwd": [("rs1", "out")],
        "ssd_bwd": [("rs2a", "up")],
        "g_w_in": [("rs2b", "up"), ("rs2", "out")],
        "d_h1": [("rs2", "in")],
    }

    def sibling_exchange_in(plan):
        plan.put("r1_in", _run_jobs("rs_sibling_in", [_SiblingJob(plan.get("g_in"))])[0])

    store = {"pos": pos}
    for tag, w in (("w_in", w_in), ("w_out", w_out), ("w_up", w_up), ("w_down", w_down)):
        store["shard_" + tag] = _cast_bf16("cast_" + tag, w[0])
    store["w_in"] = _run_jobs("gather_w_in", [_GatherJob(store["shard_w_in"])])[0]
    plan = _Plan(hosted, store, hooks={"after_g_w_in": sibling_exchange_in}, two_leg=two_leg)
    loss, grad_x, small = _local_step(
        plan, x[0], loss_target[0], mod, conv_w, ssm_conv_b, ssm_dt_bias_f, ssm_dt_bias_b,
        ssm_a_log_f, ssm_a_log_b, ssm_d, ssm_norm_w, sc_w, sc_norm_w, ln1_g, ln1_b, ln2_g, ln2_b)
    for tag in ("down", "up", "out", "in"):
        adamw(plan, tag)

    parts = [small["dmod"]]
    parts += [_pad_lanes(small[n], _round_up(w, 128)) for n, w in _SUMMED]
    parts += [small["ssm_conv_w"].reshape(1, SSM_CONV * D_XBC), small["sc_conv_w"].reshape(1, SC_CONV * D_SC)]
    parts += [loss]
    gvec = jnp.concatenate(parts, axis=1)
    n_vec = _round_up(gvec.shape[1], 8192)
    gall = _gather_vec("gather_small_grads", _pad_lanes(gvec, n_vec))

    def shard_cols(full, k, per):
        return lax.dynamic_slice_in_dim(full.reshape(k, N_DEV, per), me, 1, axis=1).reshape(1, k * per)

    def placed(vals, n_rows=1):
        return jnp.concatenate(vals, axis=1)

    n_mod = N_MOD * D_MODEL
    ws, ms, vs = [b_ada], [m_b_ada], [v_b_ada]
    for n, w in _SUMMED:
        pw = _round_up(w, 128)
        ws.append(_pad_lanes(args[n], pw))
        ms.append(_pad_lanes(args["m_" + n], pw))
        vs.append(_pad_lanes(args["v_" + n], pw))

    def full_rows(shard, k, per):
        z = jnp.zeros((k, N_DEV, per), F32)
        z = lax.dynamic_update_slice_in_dim(z, shard.reshape(k, 1, per), me, axis=1)
        return z.reshape(1, k * N_DEV * per)

    for nm, k, per in (("ssm_conv_w", SSM_CONV, D_XBC // N_DEV), ("sc_conv_w", SC_CONV, D_SC // N_DEV)):
        ws.append(full_rows(args[nm][0], k, per))
        ms.append(full_rows(args["m_" + nm][0], k, per))
        vs.append(full_rows(args["v_" + nm][0], k, per))
    tail = n_vec - sum(a.shape[1] for a in ws)
    ws.append(jnp.zeros((1, tail), F32))
    ms.append(jnp.zeros((1, tail), F32))
    vs.append(jnp.ones((1, tail), F32))
    g_s, d_s, m_s, v_s = _sum8_adamw(gall, placed(ws), placed(ms), placed(vs))

    off = 0

    def take(w):
        nonlocal off
        sl = tuple(a[:, off:off + w] for a in (g_s, d_s, m_s, v_s))
        off += _round_up(w, 128)
        return sl

    out["b_ada"] = take(n_mod)
    for n, w in _SUMMED:
        out[n] = take(w)
    for nm, k, per in (("ssm_conv_w", SSM_CONV, D_XBC // N_DEV), ("sc_conv_w", SC_CONV, D_SC // N_DEV)):
        full = take(k * N_DEV * per)
        out[nm] = tuple(shard_cols(a, k, per).reshape(1, k, per) for a in full)
    loss_total = g_s[0, off]

    dmod_all = gall[:, :n_mod]
    dmod_cols = lax.dynamic_slice_in_dim(dmod_all.reshape(N_DEV, N_DEV, n_ada), me, 1, axis=1)
    dmod16 = jnp.pad(dmod_cols.reshape(N_DEV, n_ada), ((0, 8), (0, 0)))
    out["w_ada"] = tuple(a[None] for a in _ada_bwd_adamw(c16, dmod16, w_ada[0], m_w_ada[0], v_w_ada[0]))

    names = ['w_ada', 'b_ada', 'w_in', 'ssm_conv_w', 'ssm_conv_b', 'ssm_dt_bias_f', 'ssm_dt_bias_b',
             'ssm_a_log_f', 'ssm_a_log_b', 'ssm_d', 'ssm_norm_w', 'sc_conv_w', 'sc_norm_w', 'w_out',
             'ln1_g', 'ln1_b', 'w_up', 'w_down', 'ln2_g', 'ln2_b']
    res = [loss_total, grad_x[None]]
    for k in range(4):
        res += [out[n][k] for n in names]
    return tuple(res)
```

```python
import functools

import jax
import jax.numpy as jnp
from jax import lax
from jax.experimental import pallas as pl
from jax.experimental.pallas import tpu as pltpu

F32 = jnp.float32
BF16 = jnp.bfloat16
MESH = pl.DeviceIdType.MESH

N_DEV = 8
D_MODEL = 4096
D_SSM = 2048
D_SC = 2048
HEADS = 32
HEAD_DIM = 64
GROUPS = 8
GROUP_W = D_SSM // GROUPS
HEADS_PER_GROUP = 4
N_STATE = 128
CHUNK = 128
SSM_CONV = 5
SC_CONV = 3
SC_GROUP_W = 128
D_XBC = 4096
D_FF = 16384
D_IN = 12352
D_IN_SHARD = D_IN // N_DEV
D_MAIN = 12288
N_MOD = 6
ALPHA = (2 * 1) ** 0.25
LN_EPS = 1e-5
RMS_EPS = 1e-5
ADAM_LR = 0.001
ADAM_B1 = 0.9
ADAM_B2 = 0.999
ADAM_EPS = 1e-08
ADAM_WD = 0.01
ADAM_STEP = 10

VMEM_LIMIT = 56 * 1024 * 1024
HALO = 8

_DN = {
    "nn": (((1,), (0,)), ((), ())),
    "nt": (((1,), (1,)), ((), ())),
    "tn": (((0,), (0,)), ((), ())),
}


def _cparams(sem=None):
    return pltpu.CompilerParams(dimension_semantics=sem, vmem_limit_bytes=VMEM_LIMIT)


def _my_pos():
    return lax.axis_index("x"), lax.axis_index("y"), lax.axis_index("c")


def _other_chips(x, y):
    return [(1 - x, y), (x, 1 - y), (1 - x, 1 - y)]


class _GatherJob:
    n_remote = 7

    def __init__(self, shard, space=pl.ANY, rows=None, into=None):
        self.ins = (shard,) if into is None else (shard, into)
        self.alias = None if into is None else 1
        self.out_shapes = (jax.ShapeDtypeStruct((N_DEV,) + shard.shape, shard.dtype),)
        self.space = space
        self.rows = rows

    def _parts(self, ins, outs, send, recv, local):
        x_ref, out_ref = ins[0], outs[0]
        if self.rows is not None:
            x_ref = x_ref.at[pl.ds(*self.rows)]
        x, y, c = _my_pos()
        me, sibling = (x, y, c), (x, y, 1 - c)
        chips = _other_chips(x, y)

        def slab(px, py, pc):
            whole = out_ref.at[4 * px + 2 * py + pc]
            return whole if self.rows is None else whole.at[pl.ds(*self.rows)]

        def copy(k, block, to, src=None):
            return pltpu.make_async_remote_copy(
                src_ref=slab(*block) if src is None else src, dst_ref=slab(*block),
                send_sem=send.at[k], recv_sem=recv.at[k], device_id=to, device_id_type=MESH)

        mine = pltpu.make_async_copy(x_ref, slab(*me), local.at[0])
        own = [copy(0, me, sibling, src=x_ref), copy(1, me, (*chips[0], c), src=x_ref),
               copy(2, me, (*chips[1], c), src=x_ref)]
        relayed = (x + (1 - c) * (1 - 2 * x), y + c * (1 - 2 * y), c)
        relay = copy(3, relayed, (x + c * (1 - 2 * x), y + (1 - c) * (1 - 2 * y), c))
        hand = [copy(4 + j, (*chip, c), sibling) for j, chip in enumerate(chips)]
        landed = [copy(1 + j, (*chip, c), me) for j, chip in enumerate(chips)]
        handed = [copy(0, sibling, me)] + [copy(4 + j, (*chip, 1 - c), me) for j, chip in enumerate(chips)]
        return mine, own, relay, hand, landed, handed

    def start(self, *refs):
        mine, own, _, _, _, _ = self._parts(*refs)
        mine.start()
        for cp in own:
            cp.start()

    def mid(self, *refs):
        _, _, relay, hand, landed, _ = self._parts(*refs)
        landed[0].wait_recv()
        landed[1].wait_recv()
        relay.start()
        hand[0].start()
        hand[1].start()

    def finish(self, *refs):
        mine, own, relay, hand, landed, handed = self._parts(*refs)
        landed[2].wait_recv()
        hand[2].start()
        for cp in handed:
            cp.wait_recv()
        for cp in own + [relay] + hand:
            cp.wait_send()
        mine.wait()


class _SiblingJob:
    n_remote = 4
    space = pl.ANY

    def __init__(self, g):
        self.ins = (g,)
        self.out_shapes = (jax.ShapeDtypeStruct((4,) + g.shape[1:], g.dtype),)

    def _copies(self, ins, outs, send, recv, local):
        x, y, c = _my_pos()
        return [pltpu.make_async_remote_copy(
            src_ref=ins[0].at[2 * j + (1 - c)], dst_ref=outs[0].at[j],
            send_sem=send.at[j], recv_sem=recv.at[j],
            device_id=(x, y, 1 - c), device_id_type=MESH) for j in range(4)]

    def start(self, *refs):
        for cp in self._copies(*refs):
            cp.start()

    def mid(self, *refs):
        pass

    def finish(self, *refs):
        for cp in self._copies(*refs):
            cp.wait()


class _ChipsJob:
    n_remote = 3
    space = pl.ANY

    def __init__(self, p):
        self.ins = (p,)
        self.out_shapes = (jax.ShapeDtypeStruct(p.shape, p.dtype),)

    def _copies(self, ins, outs, send, recv, local):
        x, y, c = _my_pos()
        return [pltpu.make_async_remote_copy(
            src_ref=ins[0].at[k], dst_ref=outs[0].at[k],
            send_sem=send.at[k], recv_sem=recv.at[k],
            device_id=(px, py, c), device_id_type=MESH) for k, (px, py) in enumerate(_other_chips(x, y))]

    def start(self, *refs):
        for cp in self._copies(*refs):
            cp.start()

    def mid(self, *refs):
        pass

    def finish(self, *refs):
        for cp in self._copies(*refs):
            cp.wait()


def _relay_route(x, y, c):
    first = (x + c * (1 - 2 * x), y + (1 - c) * (1 - 2 * y))
    second = (x + (1 - c) * (1 - 2 * x), y + c * (1 - 2 * y))
    return first, second


class _RelayFirstJob:
    n_remote = 2
    space = pl.ANY

    def __init__(self, p):
        self.ins = (p,)
        self.out_shapes = (jax.ShapeDtypeStruct(p.shape, p.dtype),)

    def _copies(self, ins, outs, send, recv, local):
        x, y, c = _my_pos()
        (fx, fy), _ = _relay_route(x, y, c)
        return [pltpu.make_async_remote_copy(
            src_ref=ins[0].at[k], dst_ref=outs[0].at[k], send_sem=send.at[k], recv_sem=recv.at[k],
            device_id=(fx, fy, c), device_id_type=MESH) for k in range(2)]

    def start(self, *refs):
        for cp in self._copies(*refs):
            cp.start()

    def mid(self, *refs):
        pass

    def finish(self, *refs):
        for cp in self._copies(*refs):
            cp.wait()


class _RelaySecondJob:
    n_remote = 1
    space = pl.ANY

    def __init__(self, q):
        self.ins = (q,)
        self.out_shapes = (jax.ShapeDtypeStruct(q.shape, q.dtype),)

    def _copy(self, ins, outs, send, recv, local):
        x, y, c = _my_pos()
        _, (sx, sy) = _relay_route(x, y, c)
        return pltpu.make_async_remote_copy(
            src_ref=ins[0], dst_ref=outs[0], send_sem=send.at[0], recv_sem=recv.at[0],
            device_id=(sx, sy, c), device_id_type=MESH)

    def start(self, *refs):
        self._copy(*refs).start()

    def mid(self, *refs):
        pass

    def finish(self, *refs):
        self._copy(*refs).wait()


MID_STEP_FRACTION = 0.64


def _call(name, body, *, grid, in_specs, out_specs, out_shape, args, scratch_shapes=(), sem=None,
          jobs=(), n_prefetch=0):
    out_shape, out_specs, in_specs = list(out_shape), list(out_specs), list(in_specs)
    scratch_shapes = list(scratch_shapes)
    jobs = list(jobs)
    n_in, n_out, n_scr = len(in_specs), len(out_shape), len(scratch_shapes)
    job_ins = [a for j in jobs for a in j.ins]
    job_outs = [o for j in jobs for o in j.out_shapes]
    steps = 1
    for n in grid:
        steps *= n
    mid_step = min(steps - 1, int(steps * MID_STEP_FRACTION))

    def wrapped(*refs):
        pre, refs = refs[:n_prefetch], refs[n_prefetch:]
        core_in, refs = refs[:n_in], refs[n_in:]
        jin, refs = refs[:len(job_ins)], refs[len(job_ins):]
        core_out, refs = refs[:n_out], refs[n_out:]
        jout, refs = refs[:len(job_outs)], refs[len(job_outs):]
        core_scr, sems = refs[:n_scr], refs[n_scr:]
        lin = 0
        for ax, n in enumerate(grid):
            lin = lin * n + pl.program_id(ax)
        bound = []
        for j in jobs:
            ji, jin = jin[:len(j.ins)], jin[len(j.ins):]
            jo, jout = jout[:len(j.out_shapes)], jout[len(j.out_shapes):]
            (send, recv, local), sems = sems[:3], sems[3:]
            bound.append((j, (ji, jo, send, recv, local)))

        if jobs:
            @pl.when(lin == 0)
            def _():
                for j, r in bound:
                    j.start(*r)

        body(*pre, *core_in, *core_out, *core_scr)

        if jobs:
            @pl.when(lin == mid_step)
            def _():
                for j, r in bound:
                    j.mid(*r)

            @pl.when(lin == steps - 1)
            def _():
                for j, r in bound:
                    j.finish(*r)

    sem_shapes = []
    for j in jobs:
        sem_shapes += [pltpu.SemaphoreType.DMA((j.n_remote,)), pltpu.SemaphoreType.DMA((j.n_remote,)),
                       pltpu.SemaphoreType.DMA((1,))]
    if jobs:
        sem = tuple("arbitrary" for _ in grid)
    aliases = {}
    in_at, out_at = len(args), n_out
    for j in jobs:
        if getattr(j, "alias", None) is not None:
            aliases[in_at + j.alias] = out_at
        in_at, out_at = in_at + len(j.ins), out_at + len(j.out_shapes)
    res = pl.pallas_call(
        wrapped, name=name, input_output_aliases=aliases,
        grid_spec=pltpu.PrefetchScalarGridSpec(
            num_scalar_prefetch=n_prefetch, grid=tuple(grid),
            in_specs=in_specs + [pl.BlockSpec(memory_space=j.space) for j in jobs for _ in j.ins],
            out_specs=out_specs + [pl.BlockSpec(memory_space=j.space) for j in jobs for _ in j.out_shapes],
            scratch_shapes=scratch_shapes + sem_shapes),
        out_shape=out_shape + job_outs,
        compiler_params=_cparams(sem),
    )(*args, *job_ins)
    res = list(res) if isinstance(res, (list, tuple)) else [res]
    return res[:n_out], res[n_out:]


def _run_jobs(name, jobs):
    return _call(name, lambda: None, grid=(1,), in_specs=[], out_specs=[], out_shape=[], args=(),
                 jobs=jobs)[1]


def _matmul(name, a, b, *, mode, grid, a_spec, b_spec, out_shapes, out_specs, acc_shape,
            epilogue=None, extras=(), extra_specs=(), jobs=()):
    nk = grid[2]
    n_extra = len(extras)
    n_out = len(out_shapes)

    def body(*refs):
        a_ref, b_ref = refs[0], refs[1]
        extra_refs = refs[2:2 + n_extra]
        out_refs = refs[2 + n_extra:2 + n_extra + n_out]
        part = lax.dot_general(a_ref[...], b_ref[...], _DN[mode], preferred_element_type=F32)

        def finish(acc):
            outs = epilogue(acc, *[r[...] for r in extra_refs]) if epilogue else (acc,)
            for o_ref, o in zip(out_refs, outs):
                o_ref[...] = o.astype(o_ref.dtype)

        if nk == 1:
            finish(part)
        else:
            acc_ref = refs[-1]
            k = pl.program_id(2)

            @pl.when(k == 0)
            def _():
                acc_ref[...] = part

            @pl.when(k > 0)
            def _():
                acc_ref[...] += part

            @pl.when(k == nk - 1)
            def _():
                finish(acc_ref[...])

    scratch = [pltpu.VMEM(acc_shape, F32)] if nk > 1 else []
    return _call(name, body, grid=grid, in_specs=[a_spec, b_spec, *extra_specs],
                 out_specs=out_specs, out_shape=out_shapes, scratch_shapes=scratch,
                 sem=("parallel", "parallel", "arbitrary"), args=(a, b, *extras), jobs=jobs)


def _tile(n, pref):
    t = min(n, pref)
    assert n % t == 0, (n, t)
    return t


def _mm_nn(name, a, b, out_dtype, tn=1024, tk=None, epilogue=None, out_dtypes=None, jobs=(),
           a_col0=0, extras=()):
    m, k = a.shape[0], b.shape[0]
    n = b.shape[1]
    tm, tn = _tile(m, 1024), _tile(n, tn)
    tk = _tile(k, tk or 4096)
    k0 = a_col0 // tk
    assert a_col0 % tk == 0
    out_dtypes = out_dtypes or (out_dtype,)
    o_spec = pl.BlockSpec((tm, tn), lambda i, j, kk: (i, j))
    return _matmul(
        name, a, b, mode="nn", grid=(m // tm, n // tn, k // tk),
        a_spec=pl.BlockSpec((tm, tk), lambda i, j, kk: (i, k0 + kk)),
        b_spec=pl.BlockSpec((tk, tn), lambda i, j, kk: (kk, j)),
        out_shapes=[jax.ShapeDtypeStruct((m, n), dt) for dt in out_dtypes],
        out_specs=[o_spec for _ in out_dtypes],
        acc_shape=(tm, tn), epilogue=epilogue, jobs=jobs,
        extras=extras, extra_specs=[o_spec for _ in extras])


def _mm_nt(name, a, b, out_dtype, epilogue=None, extras=(), tk=None, jobs=()):
    m, k = a.shape
    n = b.shape[0]
    tm, tn = _tile(m, 1024), _tile(n, 1024)
    tk = _tile(k, tk or 4096)
    o_spec = pl.BlockSpec((tm, tn), lambda i, j, kk: (i, j))
    return _matmul(
        name, a, b, mode="nt", grid=(m // tm, n // tn, k // tk),
        a_spec=pl.BlockSpec((tm, tk), lambda i, j, kk: (i, kk)),
        b_spec=pl.BlockSpec((tn, tk), lambda i, j, kk: (j, kk)),
        out_shapes=[jax.ShapeDtypeStruct((m, n), out_dtype)],
        out_specs=[o_spec], acc_shape=(tm, tn), epilogue=epilogue,
        extras=extras, extra_specs=[o_spec for _ in extras], jobs=jobs)


def _mm_tn(name, a, b, out_dtype, tk=2048, jobs=()):
    k, m = a.shape
    n = b.shape[1]
    tm, tn = _tile(m, 1024), _tile(n, 1024)
    tk = _tile(k, tk)
    return _matmul(
        name, a, b, mode="tn", grid=(m // tm, n // tn, k // tk),
        a_spec=pl.BlockSpec((tk, tm), lambda i, j, kk: (kk, i)),
        b_spec=pl.BlockSpec((tk, tn), lambda i, j, kk: (kk, j)),
        out_shapes=[jax.ShapeDtypeStruct((m, n), out_dtype)],
        out_specs=[pl.BlockSpec((tm, tn), lambda i, j, kk: (i, j))],
        acc_shape=(tm, tn), jobs=jobs)


def _cast_bf16(name, w):
    r, c = w.shape
    tr = _tile(r, 512)

    def body(w_ref, o_ref):
        o_ref[...] = w_ref[...].astype(BF16)

    return pl.pallas_call(
        body, name=name, grid=(r // tr,),
        in_specs=[pl.BlockSpec((tr, c), lambda i: (i, 0))],
        out_specs=pl.BlockSpec((tr, c), lambda i: (i, 0)),
        out_shape=jax.ShapeDtypeStruct((r, c), BF16),
        compiler_params=_cparams(("parallel",)),
    )(w)


def _chip_of(pos, which):
    x, y, c = pos[0], pos[1], pos[2]
    first, second = _relay_route(x, y, c)
    chips = _other_chips(x, y) + [first, second, (x, y)]
    px, py = chips[which]
    return 2 * px + py


def _pair_add(name, g, r1, pos, dests):
    _, r, cdim = g.shape
    tr = _tile(r, 512)

    def chip(k, pos):
        idx = _chip_of(pos, dests[-1])
        for n in range(len(dests) - 2, -1, -1):
            idx = jnp.where(k == n, _chip_of(pos, dests[n]), idx)
        return idx

    def body(pos_ref, g_ref, r_ref, o_ref):
        o_ref[...] = (g_ref[...].astype(F32) + r_ref[...].astype(F32)).astype(o_ref.dtype)

    return pl.pallas_call(
        body, name=name,
        grid_spec=pltpu.PrefetchScalarGridSpec(
            num_scalar_prefetch=1, grid=(len(dests), r // tr),
            in_specs=[pl.BlockSpec((None, tr, cdim), lambda k, i, pos: (2 * chip(k, pos) + pos[2], i, 0)),
                      pl.BlockSpec((None, tr, cdim), lambda k, i, pos: (chip(k, pos), i, 0))],
            out_specs=pl.BlockSpec((None, tr, cdim), lambda k, i, pos: (k, i, 0))),
        out_shape=jax.ShapeDtypeStruct((len(dests), r, cdim), BF16),
        compiler_params=_cparams(("parallel", "parallel")),
    )(pos, g, r1)


def _adamw_math(w, g, m, v):
    m = ADAM_B1 * m + (1.0 - ADAM_B1) * g
    v = ADAM_B2 * v + (1.0 - ADAM_B2) * jnp.square(g)
    m_hat = m / (1.0 - ADAM_B1 ** ADAM_STEP)
    v_hat = v / (1.0 - ADAM_B2 ** ADAM_STEP)
    delta = -ADAM_LR * (m_hat / (jnp.sqrt(v_hat) + ADAM_EPS) + ADAM_WD * w)
    return delta, m, v


def _relay_add(name, g, r1, ra, pos):
    _, r, cdim = g.shape
    tr = _tile(r, 512)

    def body(pos_ref, g_ref, r1_ref, ra_ref, o_ref):
        q = g_ref[...].astype(F32) + r1_ref[...].astype(F32) + ra_ref[...].astype(F32)
        o_ref[...] = q.astype(o_ref.dtype)

    return pl.pallas_call(
        body, name=name,
        grid_spec=pltpu.PrefetchScalarGridSpec(
            num_scalar_prefetch=1, grid=(r // tr,),
            in_specs=[pl.BlockSpec((None, tr, cdim), lambda i, pos: (2 * _chip_of(pos, 4) + pos[2], i, 0)),
                      pl.BlockSpec((None, tr, cdim), lambda i, pos: (_chip_of(pos, 4), i, 0)),
                      pl.BlockSpec((None, tr, cdim), lambda i, pos: (0, i, 0))],
            out_specs=pl.BlockSpec((tr, cdim), lambda i, pos: (i, 0))),
        out_shape=jax.ShapeDtypeStruct((r, cdim), BF16),
        compiler_params=_cparams(("parallel",)),
    )(pos, g, r1, ra)


def _reduce_adamw(name, g8, r1, others, pos, w, m, v, jobs=()):
    r, cdim = w.shape
    tr = _tile(r, 128 if cdim >= D_MODEL else 256)
    blk = pl.BlockSpec((tr, cdim), lambda i, pos: (i, 0))
    n_other = len(others)

    def body(pos_ref, g_ref, r1_ref, *refs):
        other_refs, (w_ref, m_ref, v_ref, g_out, d_out, m_out, v_out) = refs[:n_other], refs[n_other:]
        g = g_ref[...].astype(F32) + r1_ref[...].astype(F32)
        for o_ref in other_refs:
            g = g + o_ref[...].astype(F32)
        d, mn, vn = _adamw_math(w_ref[...], g, m_ref[...], v_ref[...])
        g_out[...] = g
        d_out[...] = d
        m_out[...] = mn
        v_out[...] = vn

    def other_spec(lead):
        if lead is None:
            return blk
        return pl.BlockSpec((None, tr, cdim), lambda i, pos: (lead, i, 0))

    shp = jax.ShapeDtypeStruct((r, cdim), F32)
    return _call(
        name, body, grid=(r // tr,), n_prefetch=1,
        in_specs=[pl.BlockSpec((None, tr, cdim), lambda i, pos: (2 * _chip_of(pos, 5) + pos[2], i, 0)),
                  pl.BlockSpec((None, tr, cdim), lambda i, pos: (_chip_of(pos, 5), i, 0))]
        + [other_spec(lead) for _, lead in others] + [blk, blk, blk],
        out_specs=[blk, blk, blk, blk], out_shape=[shp, shp, shp, shp],
        sem=("parallel",), args=(pos, g8, r1, *[a for a, _ in others], w, m, v), jobs=jobs)


def _row_spec(t, width=D_MODEL):
    return pl.BlockSpec((t, width), lambda i: (i, 0))


def _full_spec(shape):
    return pl.BlockSpec(shape, lambda i: tuple(0 for _ in shape))


def _ln_stats(p):
    mu = jnp.mean(p, axis=-1, keepdims=True)
    xc = p - mu
    var = jnp.mean(xc * xc, axis=-1, keepdims=True)
    rstd = lax.rsqrt(var + LN_EPS)
    return xc * rstd, rstd


def _ln_bwd(dy, xhat, rstd, g):
    dxh = dy * g
    m1 = jnp.mean(dxh, axis=-1, keepdims=True)
    m2 = jnp.mean(dxh * xhat, axis=-1, keepdims=True)
    return rstd * (dxh - m1 - xhat * m2)


def _acc_rows(ref, val, first):
    s = jnp.sum(val, axis=0, keepdims=True)

    @pl.when(first)
    def _():
        ref[...] = s

    @pl.when(jnp.logical_not(first))
    def _():
        ref[...] += s


def _modulate(name, x, mod6):
    s = x.shape[0]
    t = _tile(s, 256)

    def body(x_ref, mod_ref, o_ref):
        o_ref[...] = (x_ref[...] * (1.0 + mod_ref[1:2, :]) + mod_ref[0:1, :]).astype(BF16)

    return pl.pallas_call(
        body, name=name, grid=(s // t,),
        in_specs=[_row_spec(t), _full_spec((N_MOD, D_MODEL))],
        out_specs=_row_spec(t),
        out_shape=jax.ShapeDtypeStruct((s, D_MODEL), BF16),
        compiler_params=_cparams(("parallel",)),
    )(x, mod6)


def _ln1_fwd(x, mix, mod6, g, b, jobs=()):
    s = x.shape[0]
    t = _tile(s, 256)

    def body(x_ref, mix_ref, mod_ref, g_ref, b_ref, x1_ref, h2_ref):
        pre = ALPHA * x_ref[...] + (1.0 + mod_ref[2:3, :]) * mix_ref[...]
        xhat, _ = _ln_stats(pre)
        x1 = xhat * g_ref[...] + b_ref[...]
        x1_ref[...] = x1
        h2_ref[...] = (x1 * (1.0 + mod_ref[4:5, :]) + mod_ref[3:4, :]).astype(BF16)

    vec = _full_spec((1, D_MODEL))
    return _call(
        "ln1_fwd", body, grid=(s // t,),
        in_specs=[_row_spec(t), _row_spec(t), _full_spec((N_MOD, D_MODEL)), vec, vec],
        out_specs=[_row_spec(t), _row_spec(t)],
        out_shape=[jax.ShapeDtypeStruct((s, D_MODEL), F32), jax.ShapeDtypeStruct((s, D_MODEL), BF16)],
        sem=("parallel",), args=(x, mix, mod6, g, b), jobs=jobs)


def _ln2_loss_bwd(x1, f2, tgt, mod6, g, b):
    s = x1.shape[0]
    t = _tile(s, 128)

    def body(x1_ref, f2_ref, tgt_ref, mod_ref, g_ref, b_ref,
             df2_ref, dx1_ref, loss_ref, dg_ref, db_ref, dgate_ref):
        first = pl.program_id(0) == 0
        gate = 1.0 + mod_ref[5:6, :]
        f2v = f2_ref[...]
        pre = ALPHA * x1_ref[...] + gate * f2v
        xhat, rstd = _ln_stats(pre)
        err = xhat * g_ref[...] + b_ref[...] - tgt_ref[...]
        part = 0.5 * jnp.sum(jnp.mean(err * err, axis=-1, keepdims=True), axis=0, keepdims=True)
        dy = err / D_MODEL
        dpre = _ln_bwd(dy, xhat, rstd, g_ref[...])
        df2_ref[...] = (gate * dpre).astype(BF16)
        dx1_ref[...] = ALPHA * dpre
        _acc_rows(loss_ref, jnp.broadcast_to(part, (1, 128)), first)
        _acc_rows(dg_ref, dy * xhat, first)
        _acc_rows(db_ref, dy, first)
        _acc_rows(dgate_ref, dpre * f2v, first)

    vec = _full_spec((1, D_MODEL))
    vshape = jax.ShapeDtypeStruct((1, D_MODEL), F32)
    return pl.pallas_call(
        body, name="ln2_loss_bwd", grid=(s // t,),
        in_specs=[_row_spec(t), _row_spec(t), _row_spec(t), _full_spec((N_MOD, D_MODEL)), vec, vec],
        out_specs=[_row_spec(t), _row_spec(t), _full_spec((1, 128)), vec, vec, vec],
        out_shape=[jax.ShapeDtypeStruct((s, D_MODEL), BF16), jax.ShapeDtypeStruct((s, D_MODEL), F32),
                   jax.ShapeDtypeStruct((1, 128), F32), vshape, vshape, vshape],
        compiler_params=_cparams(("arbitrary",)),
    )(x1, f2, tgt, mod6, g, b)


def _ln1_bwd(dh2, dx1a, x1, x, mix, mod6, g):
    s = x.shape[0]
    t = _tile(s, 128)

    def body(dh2_ref, dx1a_ref, x1_ref, x_ref, mix_ref, mod_ref, g_ref,
             dmix_ref, dxa_ref, dscale_ref, dshift_ref, dg_ref, db_ref, dgate_ref):
        first = pl.program_id(0) == 0
        dh2v = dh2_ref[...]
        dx1 = dx1a_ref[...] + dh2v * (1.0 + mod_ref[4:5, :])
        gate = 1.0 + mod_ref[2:3, :]
        mixv = mix_ref[...]
        pre = ALPHA * x_ref[...] + gate * mixv
        xhat, rstd = _ln_stats(pre)
        dpre = _ln_bwd(dx1, xhat, rstd, g_ref[...])
        dmix_ref[...] = (gate * dpre).astype(BF16)
        dxa_ref[...] = ALPHA * dpre
        _acc_rows(dscale_ref, dh2v * x1_ref[...], first)
        _acc_rows(dshift_ref, dh2v, first)
        _acc_rows(dg_ref, dx1 * xhat, first)
        _acc_rows(db_ref, dx1, first)
        _acc_rows(dgate_ref, dpre * mixv, first)

    vec = _full_spec((1, D_MODEL))
    vshape = jax.ShapeDtypeStruct((1, D_MODEL), F32)
    return pl.pallas_call(
        body, name="ln1_bwd", grid=(s // t,),
        in_specs=[_row_spec(t)] * 5 + [_full_spec((N_MOD, D_MODEL)), vec],
        out_specs=[_row_spec(t), _row_spec(t), vec, vec, vec, vec, vec],
        out_shape=[jax.ShapeDtypeStruct((s, D_MODEL), BF16), jax.ShapeDtypeStruct((s, D_MODEL), F32),
                   vshape, vshape, vshape, vshape, vshape],
        compiler_params=_cparams(("arbitrary",)),
    )(dh2, dx1a, x1, x, mix, mod6, g)


def _grad_x(dxa, dh1, x, mod6):
    s = x.shape[0]
    t = _tile(s, 256)

    def body(dxa_ref, dh1_ref, x_ref, mod_ref, gx_ref, dscale_ref, dshift_ref):
        first = pl.program_id(0) == 0
        dh1v = dh1_ref[...]
        gx_ref[...] = dxa_ref[...] + dh1v * (1.0 + mod_ref[1:2, :])
        _acc_rows(dscale_ref, dh1v * x_ref[...], first)
        _acc_rows(dshift_ref, dh1v, first)

    vec = _full_spec((1, D_MODEL))
    vshape = jax.ShapeDtypeStruct((1, D_MODEL), F32)
    return pl.pallas_call(
        body, name="grad_x", grid=(s // t,),
        in_specs=[_row_spec(t)] * 3 + [_full_spec((N_MOD, D_MODEL))],
        out_specs=[_row_spec(t), vec, vec],
        out_shape=[jax.ShapeDtypeStruct((s, D_MODEL), F32), vshape, vshape],
        compiler_params=_cparams(("arbitrary",)),
    )(dxa, dh1, x, mod6)


def _window(ref, i, t, s):
    r0 = pl.multiple_of(i * t, t)
    cur = ref[pl.ds(r0, t), :]
    lo = pl.multiple_of(jnp.maximum(r0 - HALO, 0), HALO)
    hi = pl.multiple_of(jnp.minimum(r0 + t, s - HALO), HALO)
    before = ref[pl.ds(lo, HALO), :] * (i > 0).astype(F32)
    after = ref[pl.ds(hi, HALO), :] * (i < s // t - 1).astype(F32)
    return jnp.concatenate([before, cur, after], axis=0)


def _tap(ext, shift):
    n = ext.shape[0]
    if shift == 0:
        return ext
    return pltpu.roll(ext, (-shift) % n, 0)


def _centre(ext, t):
    return ext[HALO:HALO + t]


def _conv_taps(ext, w, width):
    acc = None
    for k in range(width):
        term = _tap(ext, k - width // 2) * w[k:k + 1, :]
        acc = term if acc is None else acc + term
    return acc


def _silu(a):
    return a * jax.nn.sigmoid(a)


def _conv_silu_fwd(proj, w, b):
    s = proj.shape[0]
    cb = 256
    t = _tile(s, 256)
    off = D_SSM // cb

    def body(u_ref, w_ref, b_ref, o_ref):
        wv = w_ref[...]
        bv = b_ref[...]

        def step(i, carry):
            ext = _window(u_ref, i, t, s)
            a = _centre(_conv_taps(ext, wv, SSM_CONV), t) + bv
            o_ref[pl.ds(pl.multiple_of(i * t, t), t), :] = _silu(a)
            return carry

        lax.fori_loop(0, s // t, step, 0)

    return pl.pallas_call(
        body, name="conv_silu_fwd", grid=(D_XBC // cb,),
        in_specs=[pl.BlockSpec((s, cb), lambda j: (0, off + j)),
                  pl.BlockSpec((SSM_CONV, cb), lambda j: (0, j)),
                  pl.BlockSpec((1, cb), lambda j: (0, j))],
        out_specs=pl.BlockSpec((s, cb), lambda j: (0, j)),
        out_shape=jax.ShapeDtypeStruct((s, D_XBC), F32),
        compiler_params=_cparams(("parallel",)),
    )(proj, w, b)


def _conv_silu_bwd(name, proj, w, b, col0, ncols, cots, scaled=None):
    s = proj.shape[0]
    cb = 128
    t = _tile(s, 256)
    off = (D_SSM + col0) // cb
    woff = col0 // cb
    n_cot = len(cots)

    def body(*refs):
        u_ref, w_ref, b_ref = refs[:3]
        cot_refs = refs[3:3 + n_cot]
        sc_refs = refs[3 + n_cot:3 + n_cot + (2 if scaled else 0)]
        du_ref, dw_ref, db_ref = refs[-3:]
        wv = w_ref[...]
        bv = b_ref[...]

        def step(i, carry):
            ext = _window(u_ref, i, t, s)
            a = _conv_taps(ext, wv, SSM_CONV) + bv
            cot = None
            for cr in cot_refs:
                term = _window(cr.at[0], i, t, s) + _window(cr.at[1], i, t, s)
                cot = term if cot is None else cot + term
            if scaled:
                cot = cot + _window(sc_refs[0], i, t, s) * sc_refs[1][...]
            sig = jax.nn.sigmoid(a)
            da = cot * (sig * (1.0 + a * (1.0 - sig)))
            du = None
            new = []
            for k in range(SSM_CONV):
                sh = k - SSM_CONV // 2
                term = _tap(da, -sh) * wv[k:k + 1, :]
                du = term if du is None else du + term
                prod = _centre(_tap(ext, sh) * da, t)
                new.append(carry[k] + jnp.sum(prod, axis=0, keepdims=True))
            new.append(carry[SSM_CONV] + jnp.sum(_centre(da, t), axis=0, keepdims=True))
            du_ref[pl.ds(pl.multiple_of(i * t, t), t), :] = _centre(du, t).astype(BF16)
            return tuple(new)

        zero = jnp.zeros((1, cb), F32)
        acc = lax.fori_loop(0, s // t, step, tuple(zero for _ in range(SSM_CONV + 1)))
        for k in range(SSM_CONV):
            dw_ref[k:k + 1, :] = acc[k]
        db_ref[...] = acc[SSM_CONV]

    in_specs = [pl.BlockSpec((s, cb), lambda j: (0, off + j)),
                pl.BlockSpec((SSM_CONV, cb), lambda j: (0, woff + j)),
                pl.BlockSpec((1, cb), lambda j: (0, woff + j))]
    in_specs += [pl.BlockSpec((2, s, cb), lambda j: (0, 0, j)) for _ in cots]
    args = [proj, w, b, *cots]
    if scaled:
        in_specs += [pl.BlockSpec((s, cb), lambda j: (0, j)), pl.BlockSpec((1, cb), lambda j: (0, j))]
        args += list(scaled)
    return pl.pallas_call(
        body, name=name, grid=(ncols // cb,),
        in_specs=in_specs,
        out_specs=[pl.BlockSpec((s, cb), lambda j: (0, j)),
                   pl.BlockSpec((SSM_CONV, cb), lambda j: (0, j)),
                   pl.BlockSpec((1, cb), lambda j: (0, j))],
        out_shape=[jax.ShapeDtypeStruct((s, ncols), BF16),
                   jax.ShapeDtypeStruct((SSM_CONV, ncols), F32),
                   jax.ShapeDtypeStruct((1, ncols), F32)],
        compiler_params=_cparams(("parallel",)),
    )(*args)


_SC_H = (D_SSM + D_XBC) // SC_GROUP_W
_SC_B = _SC_H + D_SC // SC_GROUP_W
_SC_C = _SC_B + D_SC // SC_GROUP_W


def _sc_fwd(proj, w, nw):
    s = proj.shape[0]
    cb = SC_GROUP_W
    t = _tile(s, 256)

    def body(uh_ref, ub_ref, uc_ref, w_ref, nw_ref, o_ref):
        wv = w_ref[...]
        nwv = nw_ref[...]

        def step(i, carry):
            p = _window(uc_ref, i, t, s) * _window(uh_ref, i, t, s)
            cv = _centre(_conv_taps(p, wv, SC_CONV), t)
            rows = pl.ds(pl.multiple_of(i * t, t), t)
            y = ub_ref[rows, :] * cv
            r = lax.rsqrt(jnp.mean(y * y, axis=-1, keepdims=True) + RMS_EPS)
            o_ref[rows, :] = (y * r * nwv).astype(BF16)
            return carry

        lax.fori_loop(0, s // t, step, 0)

    def col(base):
        return pl.BlockSpec((s, cb), lambda j: (0, base + j))

    return pl.pallas_call(
        body, name="sc_fwd", grid=(D_SC // cb,),
        in_specs=[col(_SC_H), col(_SC_B), col(_SC_C),
                  pl.BlockSpec((SC_CONV, cb), lambda j: (0, j)),
                  pl.BlockSpec((1, cb), lambda j: (0, j))],
        out_specs=pl.BlockSpec((s, cb), lambda j: (0, j)),
        out_shape=jax.ShapeDtypeStruct((s, D_SC), BF16),
        compiler_params=_cparams(("parallel",)),
    )(proj, proj, proj, w, nw)


def _sc_bwd(proj, dycat, w, nw):
    s = proj.shape[0]
    cb = SC_GROUP_W
    t = _tile(s, 256)
    dy_off = D_SSM // cb

    def body(uh_ref, ub_ref, uc_ref, dy_ref, w_ref, nw_ref, duh_ref, dub_ref, duc_ref, dw_ref, dnw_ref):
        wv = w_ref[...]
        nwv = nw_ref[...]

        def step(i, carry):
            uh = _window(uh_ref, i, t, s)
            ub = _window(ub_ref, i, t, s)
            uc = _window(uc_ref, i, t, s)
            do = _window(dy_ref, i, t, s)
            p = uc * uh
            cv = _conv_taps(p, wv, SC_CONV)
            y = ub * cv
            r = lax.rsqrt(jnp.mean(y * y, axis=-1, keepdims=True) + RMS_EPS)
            dyr = do * nwv
            dy = r * dyr - y * (r * r * r) * jnp.mean(dyr * y, axis=-1, keepdims=True)
            dcv = dy * ub
            dp = None
            new = []
            for k in range(SC_CONV):
                sh = k - SC_CONV // 2
                term = _tap(dcv, -sh) * wv[k:k + 1, :]
                dp = term if dp is None else dp + term
                new.append(carry[k] + jnp.sum(_centre(_tap(p, sh) * dcv, t), axis=0, keepdims=True))
            new.append(carry[SC_CONV] + jnp.sum(_centre(do * y * r, t), axis=0, keepdims=True))
            rows = pl.ds(pl.multiple_of(i * t, t), t)
            duh_ref[rows, :] = _centre(dp * uc, t).astype(BF16)
            duc_ref[rows, :] = _centre(dp * uh, t).astype(BF16)
            dub_ref[rows, :] = _centre(dy * cv, t).astype(BF16)
            return tuple(new)

        zero = jnp.zeros((1, cb), F32)
        acc = lax.fori_loop(0, s // t, step, tuple(zero for _ in range(SC_CONV + 1)))
        for k in range(SC_CONV):
            dw_ref[k:k + 1, :] = acc[k]
        dnw_ref[...] = acc[SC_CONV]

    def col(base):
        return pl.BlockSpec((s, cb), lambda j: (0, base + j))

    out_col = pl.BlockSpec((s, cb), lambda j: (0, j))
    act = jax.ShapeDtypeStruct((s, D_SC), BF16)
    return pl.pallas_call(
        body, name="sc_bwd", grid=(D_SC // cb,),
        in_specs=[col(_SC_H), col(_SC_B), col(_SC_C), col(dy_off),
                  pl.BlockSpec((SC_CONV, cb), lambda j: (0, j)),
                  pl.BlockSpec((1, cb), lambda j: (0, j))],
        out_specs=[out_col, out_col, out_col,
                   pl.BlockSpec((SC_CONV, cb), lambda j: (0, j)),
                   pl.BlockSpec((1, cb), lambda j: (0, j))],
        out_shape=[act, act, act, jax.ShapeDtypeStruct((SC_CONV, D_SC), F32),
                   jax.ShapeDtypeStruct((1, D_SC), F32)],
        compiler_params=_cparams(("parallel",)),
    )(proj, proj, proj, dycat, w, nw)


def _make_select_dot(differentiable):
    def raw(a, b, mode, const):
        ops = [a, b]
        v = ops[1 - const]
        acc = None
        for _ in range(3):
            piece = v.astype(BF16)
            v = v - piece.astype(F32)
            ops[1 - const] = piece
            part = lax.dot_general(ops[0].astype(BF16), ops[1].astype(BF16), _DN[mode],
                                   preferred_element_type=F32)
            acc = part if acc is None else acc + part
        return acc

    if not differentiable:
        return raw

    @functools.partial(jax.custom_vjp, nondiff_argnums=(2, 3))
    def dot(a, b, mode, const):
        return raw(a, b, mode, const)

    def fwd(a, b, mode, const):
        return raw(a, b, mode, const), (a, b)

    def bwd(mode, const, res, g):
        a, b = res
        assert mode == "nn"
        if const == 1:
            return raw(g, b, "nt", 1), jnp.zeros_like(b)
        return jnp.zeros_like(a), raw(a, g, "tn", 0)

    dot.defvjp(fwd, bwd)
    return dot


def _make_dot(differentiable):
    def raw(a, b, mode):
        return lax.dot_general(a.astype(BF16), b.astype(BF16), _DN[mode], preferred_element_type=F32)

    if not differentiable:
        return raw

    @functools.partial(jax.custom_vjp, nondiff_argnums=(2,))
    def dot(a, b, mode):
        return raw(a, b, mode)

    def fwd(a, b, mode):
        return raw(a, b, mode), (a, b)

    def bwd(mode, res, g):
        a, b = res
        if mode == "nn":
            return raw(g, b, "nt"), raw(a, g, "tn")
        if mode == "nt":
            return raw(g, b, "nn"), raw(g, a, "tn")
        return raw(b, g, "nt"), raw(a, g, "nn")

    dot.defvjp(fwd, bwd)
    return dot


def _make_swap(differentiable):
    def raw(v):
        return pltpu.roll(v, HEAD_DIM, 1)

    if not differentiable:
        return raw
    swap = jax.custom_vjp(raw)
    swap.defvjp(lambda v: (raw(v), None), lambda _, g: (raw(g),))
    return swap


def _ssd_chunk(xs, bm, cm, dtx, acx, ax, prev, tri, differentiable):
    _bdot = _make_dot(differentiable)
    swap = _make_swap(differentiable)
    atx = jnp.sum(dtx * ax, axis=0, keepdims=True)
    xdt = xs * dtx
    mask = tri > 0.0
    scores = _bdot(cm, bm, "nt")
    head = lax.broadcasted_iota(jnp.int32, (1, GROUP_W), 1) // HEAD_DIM
    low = lax.broadcasted_iota(jnp.int32, (1, 128), 1) < HEAD_DIM
    y = _bdot(cm, prev, "nn") * jnp.exp(acx)
    for h in range(HEADS_PER_GROUP):
        pair = acx[:, 128 * (h // 2):128 * (h // 2) + 128]
        other = swap(pair)
        m1 = jnp.where(low, pair, other) if h % 2 == 0 else jnp.where(low, other, pair)
        seg = m1 - m1.T
        decay = jnp.where(mask, jnp.exp(jnp.where(mask, seg, 0.0)), 0.0)
        xh = xdt * (head == h).astype(F32)
        y = y + _bdot(scores * decay, xh, "nn")
    new = prev * jnp.exp(atx) + _bdot(bm, xdt * jnp.exp(atx - acx), "tn")
    return y, new


def _softplus(v):
    return jnp.maximum(v, 0.0) + jnp.log(1.0 + jnp.exp(-jnp.abs(v)))


def _dt_spread(u, bias, a, tri2, exf, differentiable):
    sel = _make_select_dot(differentiable)
    dt = _softplus(u + bias)
    dta = dt * a
    out = []
    for d in range(2):
        acum = sel(tri2[d], dta, "nn", 0)
        out += [sel(dt, exf[d], "nn", 1), sel(acum, exf[d], "nn", 1)]
    return tuple(out)


def _ssd_consts():
    q = CHUNK
    r = lax.broadcasted_iota(jnp.int32, (q, q), 0)
    c = lax.broadcasted_iota(jnp.int32, (q, q), 1)
    tri = jnp.stack([(c <= r), (c >= r)]).astype(F32)
    shp = (2, 128, D_SSM)
    src = lax.broadcasted_iota(jnp.int32, shp, 1)
    d = lax.broadcasted_iota(jnp.int32, shp, 0)
    col = lax.broadcasted_iota(jnp.int32, shp, 2)
    exf = (src == d * HEADS + col // HEAD_DIM).astype(F32)
    return tri, exf


def _dt_prep(proj_dt, bias_all, a_all):
    s = proj_dt.shape[0]
    tri, exf = _ssd_consts()

    def body(u_ref, b_ref, a_ref, tri_ref, exf_ref, dtx_ref, acx_ref):
        dtx0, acx0, dtx1, acx1 = _dt_spread(u_ref[...], b_ref[...], a_ref[...], tri_ref[...],
                                            exf_ref[...], False)
        dtx_ref[0] = dtx0
        dtx_ref[1] = dtx1
        acx_ref[0] = acx0
        acx_ref[1] = acx1

    out = pl.BlockSpec((2, CHUNK, D_SSM), lambda i: (0, i, 0))
    shp = jax.ShapeDtypeStruct((2, s, D_SSM), F32)
    return pl.pallas_call(
        body, name="dt_prep", grid=(s // CHUNK,),
        in_specs=[_row_spec(CHUNK, 128), _full_spec((1, 128)), _full_spec((1, 128)),
                  _full_spec((2, CHUNK, CHUNK)), _full_spec((2, 128, D_SSM))],
        out_specs=[out, out], out_shape=[shp, shp],
        compiler_params=_cparams(("parallel",)),
    )(proj_dt, bias_all, a_all, tri, exf)


def _dt_prep_bwd(proj_dt, bias_all, a_all, d_dtx, d_acx):
    s = proj_dt.shape[0]
    tri, exf = _ssd_consts()

    def body(u_ref, b_ref, a_ref, tri_ref, exf_ref, ddtx_ref, dacx_ref, du_ref, db_ref, da_ref):
        tri_v, exf_v = tri_ref[...], exf_ref[...]

        def f(u, bias, a):
            return _dt_spread(u, bias, a, tri_v, exf_v, True)

        _, vjp = jax.vjp(f, u_ref[...], b_ref[...], a_ref[...])
        du, db, da = vjp((ddtx_ref[0], dacx_ref[0], ddtx_ref[1], dacx_ref[1]))
        du_ref[...] = du.astype(BF16)
        first = pl.program_id(0) == 0
        _acc_rows(db_ref, db, first)
        _acc_rows(da_ref, da, first)

    cot = pl.BlockSpec((2, CHUNK, D_SSM), lambda i: (0, i, 0))
    vec = _full_spec((1, 128))
    return pl.pallas_call(
        body, name="dt_prep_bwd", grid=(s // CHUNK,),
        in_specs=[_row_spec(CHUNK, 128), vec, vec, _full_spec((2, CHUNK, CHUNK)),
                  _full_spec((2, 128, D_SSM)), cot, cot],
        out_specs=[_row_spec(CHUNK, 128), vec, vec],
        out_shape=[jax.ShapeDtypeStruct((s, 128), BF16), jax.ShapeDtypeStruct((1, 128), F32),
                   jax.ShapeDtypeStruct((1, 128), F32)],
        compiler_params=_cparams(("arbitrary",)),
    )(proj_dt, bias_all, a_all, tri, exf, d_dtx, d_acx)


GROUPS_PER_STEP = 2
_PAIR_W = GROUPS_PER_STEP * GROUP_W
_PAIR_N = GROUPS_PER_STEP * N_STATE


def _ssd_specs(chunk_of):
    q = CHUNK
    b0 = D_SSM // _PAIR_N
    xs = pl.BlockSpec((q, _PAIR_W), lambda d, g, ci: (chunk_of(d, ci), g))
    bm = pl.BlockSpec((q, _PAIR_N), lambda d, g, ci: (chunk_of(d, ci), b0 + g))
    cm = pl.BlockSpec((q, _PAIR_N), lambda d, g, ci: (chunk_of(d, ci), b0 + GROUPS // GROUPS_PER_STEP + g))
    spread = pl.BlockSpec((None, q, _PAIR_W), lambda d, g, ci: (d, chunk_of(d, ci), g))
    ax = pl.BlockSpec((None, 1, _PAIR_W), lambda d, g, ci: (d, 0, g))
    tri = pl.BlockSpec((None, q, q), lambda d, g, ci: (d, 0, 0))
    st = pl.BlockSpec((None, None, GROUPS_PER_STEP, N_STATE, GROUP_W),
                      lambda d, g, ci: (d, chunk_of(d, ci), g, 0, 0))
    return xs, bm, cm, spread, ax, tri, st


def _wide(k):
    return slice(k * GROUP_W, (k + 1) * GROUP_W)


def _narrow(k):
    return slice(k * N_STATE, (k + 1) * N_STATE)


def _ssd_fwd(xbc, dtx, acx, ax, jobs=()):
    s = xbc.shape[0]
    nc = s // CHUNK
    tri, _ = _ssd_consts()

    def chunk_of(d, ci):
        return ci + d * (nc - 1 - 2 * ci)

    def body(xs_ref, b_ref, c_ref, dtx_ref, acx_ref, ax_ref, tri_ref, y_ref, st_ref, state):
        @pl.when(pl.program_id(2) == 0)
        def _():
            state[...] = jnp.zeros(state.shape, F32)

        tri_v = tri_ref[...]
        for k in range(GROUPS_PER_STEP):
            prev = state[k]
            st_ref[k] = prev
            y, new = _ssd_chunk(xs_ref[:, _wide(k)], b_ref[:, _narrow(k)], c_ref[:, _narrow(k)],
                                dtx_ref[:, _wide(k)], acx_ref[:, _wide(k)], ax_ref[:, _wide(k)],
                                prev, tri_v, False)
            y_ref[:, _wide(k)] = y
            state[k] = new

    xs, bm, cm, spread, ax_s, tri_s, st = _ssd_specs(chunk_of)
    return _call(
        "ssd_fwd", body, grid=(2, GROUPS // GROUPS_PER_STEP, nc),
        in_specs=[xs, bm, cm, spread, spread, ax_s, tri_s],
        out_specs=[spread, st],
        out_shape=[jax.ShapeDtypeStruct((2, s, D_SSM), F32),
                   jax.ShapeDtypeStruct((2, nc, GROUPS, N_STATE, GROUP_W), F32)],
        scratch_shapes=[pltpu.VMEM((GROUPS_PER_STEP, N_STATE, GROUP_W), F32)],
        sem=("arbitrary", "arbitrary", "arbitrary"),
        args=(xbc, xbc, xbc, dtx, acx, ax, tri), jobs=jobs)


def _ssd_bwd(xbc, dtx, acx, ax, states, dy, jobs=()):
    s = xbc.shape[0]
    nc = s // CHUNK
    tri, _ = _ssd_consts()

    def chunk_of(d, ci):
        return (nc - 1 - ci) + d * (2 * ci - (nc - 1))

    def body(xs_ref, b_ref, c_ref, dtx_ref, acx_ref, ax_ref, tri_ref, st_ref, dy_ref,
             dxs_ref, db_ref, dc_ref, ddtx_ref, dacx_ref, dax_ref, dstate):
        first = pl.program_id(2) == 0

        @pl.when(first)
        def _():
            dstate[...] = jnp.zeros(dstate.shape, F32)

        tri_v = tri_ref[...]

        def f(xs, bm, cm, dtx_v, acx_v, ax_v, prev):
            return _ssd_chunk(xs, bm, cm, dtx_v, acx_v, ax_v, prev, tri_v, True)

        dax_parts = []
        for k in range(GROUPS_PER_STEP):
            _, vjp = jax.vjp(f, xs_ref[:, _wide(k)], b_ref[:, _narrow(k)], c_ref[:, _narrow(k)],
                             dtx_ref[:, _wide(k)], acx_ref[:, _wide(k)], ax_ref[:, _wide(k)], st_ref[k])
            dxs, dbm, dcm, ddtx, dacx, dax, dprev = vjp((dy_ref[:, _wide(k)], dstate[k]))
            dxs_ref[:, _wide(k)] = dxs
            db_ref[:, _narrow(k)] = dbm
            dc_ref[:, _narrow(k)] = dcm
            ddtx_ref[:, _wide(k)] = ddtx
            dacx_ref[:, _wide(k)] = dacx
            dstate[k] = dprev
            dax_parts.append(dax)
        _acc_rows(dax_ref, jnp.concatenate(dax_parts, axis=1), first)

    xs, bm, cm, spread, ax_s, tri_s, st = _ssd_specs(chunk_of)
    dy_s = pl.BlockSpec((CHUNK, _PAIR_W), lambda d, g, ci: (chunk_of(d, ci), g))
    bc_s = pl.BlockSpec((None, CHUNK, _PAIR_N), lambda d, g, ci: (d, chunk_of(d, ci), g))
    wide = jax.ShapeDtypeStruct((2, s, D_SSM), F32)
    narrow = jax.ShapeDtypeStruct((2, s, GROUPS * N_STATE), F32)
    return _call(
        "ssd_bwd", body, grid=(2, GROUPS // GROUPS_PER_STEP, nc),
        in_specs=[xs, bm, cm, spread, spread, ax_s, tri_s, st, dy_s],
        out_specs=[spread, bc_s, bc_s, spread, spread, ax_s],
        out_shape=[wide, narrow, narrow, wide, wide, jax.ShapeDtypeStruct((2, 1, D_SSM), F32)],
        scratch_shapes=[pltpu.VMEM((GROUPS_PER_STEP, N_STATE, GROUP_W), F32)],
        sem=("arbitrary", "arbitrary", "arbitrary"),
        args=(xbc, xbc, xbc, dtx, acx, ax, tri, states, dy), jobs=jobs)


def _ssd_gate_fwd(y2, xbc, proj, dx, nw):
    s = xbc.shape[0]
    t = _tile(s, 512)

    def body(y_ref, xs_ref, z_ref, dx_ref, nw_ref, o_ref):
        y = (y_ref[0] + y_ref[1] + dx_ref[...] * xs_ref[...]) * _silu(z_ref[...])
        r = lax.rsqrt(jnp.mean(y * y, axis=-1, keepdims=True) + RMS_EPS)
        o_ref[...] = (y * r * nw_ref[...]).astype(BF16)

    blk = pl.BlockSpec((t, GROUP_W), lambda j, i: (i, j))
    vec = pl.BlockSpec((1, GROUP_W), lambda j, i: (0, j))
    return pl.pallas_call(
        body, name="ssd_gate_fwd", grid=(GROUPS, s // t),
        in_specs=[pl.BlockSpec((2, t, GROUP_W), lambda j, i: (0, i, j)), blk, blk, vec, vec],
        out_specs=blk,
        out_shape=jax.ShapeDtypeStruct((s, D_SSM), BF16),
        compiler_params=_cparams(("parallel", "parallel")),
    )(y2, xbc, proj, dx, nw)


def _ssd_gate_bwd(y2, xbc, proj, dycat, dx, nw, jobs=()):
    s = xbc.shape[0]
    t = _tile(s, 512)

    def body(y_ref, xs_ref, z_ref, do_ref, dx_ref, nw_ref, dyc_ref, dz_ref, dd_ref, dnw_ref):
        first = pl.program_id(1) == 0
        z = z_ref[...]
        xs = xs_ref[...]
        sig = jax.nn.sigmoid(z)
        gate = z * sig
        yc = y_ref[0] + y_ref[1] + dx_ref[...] * xs
        y = yc * gate
        r = lax.rsqrt(jnp.mean(y * y, axis=-1, keepdims=True) + RMS_EPS)
        do = do_ref[...]
        dyr = do * nw_ref[...]
        dy = r * dyr - y * (r * r * r) * jnp.mean(dyr * y, axis=-1, keepdims=True)
        dyc = dy * gate
        dyc_ref[...] = dyc
        dz_ref[...] = (dy * yc * (sig * (1.0 + z * (1.0 - sig)))).astype(BF16)
        _acc_rows(dd_ref, dyc * xs, first)
        _acc_rows(dnw_ref, do * y * r, first)

    blk = pl.BlockSpec((t, GROUP_W), lambda j, i: (i, j))
    vec = pl.BlockSpec((1, GROUP_W), lambda j, i: (0, j))
    return _call(
        "ssd_gate_bwd", body, grid=(GROUPS, s // t),
        in_specs=[pl.BlockSpec((2, t, GROUP_W), lambda j, i: (0, i, j)), blk, blk, blk, vec, vec],
        out_specs=[blk, blk, vec, vec],
        out_shape=[jax.ShapeDtypeStruct((s, D_SSM), F32), jax.ShapeDtypeStruct((s, D_SSM), BF16),
                   jax.ShapeDtypeStruct((1, D_SSM), F32), jax.ShapeDtypeStruct((1, D_SSM), F32)],
        sem=("parallel", "arbitrary"), args=(y2, xbc, proj, dycat, dx, nw), jobs=jobs)


def _ada_fwd(c16, w_ada):
    k, n = w_ada.shape
    tn = 512

    def body(c_ref, w_ref, o_ref):
        a = _silu(c_ref[...]).astype(BF16)
        o_ref[...] = jnp.dot(a, w_ref[...].astype(BF16), preferred_element_type=F32)

    return pl.pallas_call(
        body, name="ada_fwd", grid=(n // tn,),
        in_specs=[_full_spec((16, k)), pl.BlockSpec((k, tn), lambda j: (0, j))],
        out_specs=pl.BlockSpec((16, tn), lambda j: (0, j)),
        out_shape=jax.ShapeDtypeStruct((16, n), F32),
        compiler_params=_cparams(("parallel",)),
    )(c16, w_ada)


def _ada_bwd_adamw(c16, dmod16, w, m, v):
    k, n = w.shape
    tm, tn = 256, n
    blk = pl.BlockSpec((tm, tn), lambda i, j: (i, j))

    def body(c_ref, d_ref, w_ref, m_ref, v_ref, g_out, d_out, m_out, v_out):
        a = _silu(c_ref[...]).astype(BF16)
        g = lax.dot_general(a, d_ref[...].astype(BF16), _DN["tn"], preferred_element_type=F32)
        d, mn, vn = _adamw_math(w_ref[...], g, m_ref[...], v_ref[...])
        g_out[...] = g
        d_out[...] = d
        m_out[...] = mn
        v_out[...] = vn

    shp = jax.ShapeDtypeStruct((k, n), F32)
    return pl.pallas_call(
        body, name="ada_bwd_adamw", grid=(k // tm, n // tn),
        in_specs=[pl.BlockSpec((16, tm), lambda i, j: (0, i)), pl.BlockSpec((16, tn), lambda i, j: (0, j)),
                  blk, blk, blk],
        out_specs=[blk, blk, blk, blk],
        out_shape=[shp, shp, shp, shp],
        compiler_params=_cparams(("parallel", "parallel")),
    )(c16, dmod16, w, m, v)


def _sum8_adamw(gathered, w, m, v):
    n = w.shape[1]
    tn = _tile(n, 8192)
    vec = pl.BlockSpec((1, tn), lambda j: (0, j))

    def body(g8_ref, w_ref, m_ref, v_ref, g_out, d_out, m_out, v_out):
        g = g8_ref[0:1, :]
        for k in range(1, N_DEV):
            g = g + g8_ref[k:k + 1, :]
        d, mn, vn = _adamw_math(w_ref[...], g, m_ref[...], v_ref[...])
        g_out[...] = g
        d_out[...] = d
        m_out[...] = mn
        v_out[...] = vn

    shp = jax.ShapeDtypeStruct((1, n), F32)
    return pl.pallas_call(
        body, name="sum8_adamw", grid=(n // tn,),
        in_specs=[pl.BlockSpec((N_DEV, tn), lambda j: (0, j)), vec, vec, vec],
        out_specs=[vec, vec, vec, vec],
        out_shape=[shp, shp, shp, shp],
        compiler_params=_cparams(("parallel",)),
    )(gathered, w, m, v)


def _gather_vec(name, v):
    n = v.shape[1]
    out = _run_jobs(name, [_GatherJob(v.reshape(8, n // 8), pltpu.VMEM)])[0]
    return out.reshape(N_DEV, n)


class _Plan:
    _RESULT = {"gather": "", "rs1": "r1_", "rs2": "r2_", "rs2a": "ra_", "rs2b": "rb_"}

    def __init__(self, hosted, store, hooks=None, two_leg=()):
        self.hosted, self.store, self.hooks, self.two_leg = hosted, dict(store), hooks or {}, two_leg

    def get(self, key):
        if key not in self.store and key.startswith("p_"):
            tag = key[2:]
            dests = (2, 3) if tag in self.two_leg else (0, 1, 2)
            self.store[key] = _pair_add("rs_pair_add_" + tag, self.get("g_" + tag), self.get("r1_" + tag),
                                        self.get("pos"), dests)
        if key not in self.store and key.startswith("q_"):
            tag = key[2:]
            self.store[key] = _relay_add("rs_relay_add_" + tag, self.get("g_" + tag), self.get("r1_" + tag),
                                         self.get("ra_" + tag), self.get("pos"))
        return self.store[key]

    def put(self, key, val):
        self.store[key] = val

    def part_job(self, tag, first, count, n):
        shard = self.get("shard_" + tag)
        rows = shard.shape[0] // n
        return _GatherJob(shard, rows=(rows * first, rows * count),
                          into=self.get("part_" + tag) if first else None)

    def jobs(self, host):
        make = {"gather": lambda t: _GatherJob(self.get("shard_" + t)),
                "rs1": lambda t: _SiblingJob(self.get("g_" + t)),
                "rs2": lambda t: _ChipsJob(self.get("p_" + t)),
                "rs2a": lambda t: _RelayFirstJob(self.get("p_" + t)),
                "rs2b": lambda t: _RelaySecondJob(self.get("q_" + t))}
        return [self.part_job(tag, *kind[1:]) if isinstance(kind, tuple) else make[kind](tag)
                for kind, tag in self.hosted.get(host, ())]

    def deliver(self, kind, tag, res):
        if isinstance(kind, tuple):
            _, first, count, n = kind
            self.store["part_" + tag] = res
            if first + count == n:
                self.store[tag] = res
        else:
            self.store[self._RESULT[kind] + tag] = res

    def run(self, host, fn, *args, **kw):
        outs, results = fn(*args, jobs=self.jobs(host), **kw)
        for (kind, tag), res in zip(self.hosted.get(host, ()), results):
            self.deliver(kind, tag, res)
        return outs

    def hook(self, name):
        if name in self.hooks:
            self.hooks[name](self)


def _pad_lanes(v, n):
    return jnp.pad(v, ((0, 0), (0, n - v.shape[1])))


def _local_step(plan, x, tgt, mod, conv_w, conv_b, dt_bias_f, dt_bias_b, a_log_f, a_log_b,
                ssm_d, ssm_nw, sc_w, sc_nw, ln1_g, ln1_b, ln2_g, ln2_b):
    s = x.shape[0]
    run = plan.run
    mod6 = mod.reshape(N_MOD, D_MODEL)
    bias_all = _pad_lanes(jnp.concatenate([dt_bias_f, dt_bias_b], axis=1), 128)
    a_all = _pad_lanes(-jnp.exp(jnp.concatenate([a_log_f, a_log_b], axis=1)), 128)
    a_x = jnp.stack([jnp.repeat(a_all[:, d * HEADS:(d + 1) * HEADS], HEAD_DIM, axis=1) for d in range(2)])
    d_lanes = jnp.repeat(ssm_d, HEAD_DIM, axis=1)

    half = D_MODEL // 2
    main_a, dt_a = _w_in_sections(plan.get("part_w_in")[:, :half])
    h1 = _modulate("mod1", x, mod6)
    part, = run("in_proj_a", _mm_nn, "in_proj_a", h1, main_a, F32)
    w_in_g = plan.get("w_in")
    main_b, dt_b = _w_in_sections(w_in_g[:, half:])
    proj, = run("in_proj_b", _mm_nn, "in_proj_b", h1, main_b, F32, a_col0=half, extras=(part,),
                epilogue=lambda acc, first: (acc + first,))
    proj_dt = _mm_nn("in_proj_dt", h1, jnp.concatenate([dt_a, dt_b], axis=0), F32)[0][0]
    xbc = _conv_silu_fwd(proj, conv_w, conv_b)
    dtx, acx = _dt_prep(proj_dt, bias_all, a_all)
    y2, states = run("ssd_fwd", _ssd_fwd, xbc, dtx, acx, a_x)
    y_ssm = _ssd_gate_fwd(y2, xbc, proj, d_lanes, ssm_nw)
    y_sc = _sc_fwd(proj, sc_w, sc_nw)
    ycat = jnp.concatenate([y_ssm, y_sc], axis=1)
    w_out_g = plan.get("w_out").reshape(D_MODEL, D_MODEL)
    mix, = run("out_proj", _mm_nn, "out_proj", ycat, w_out_g, F32)
    x1, h2 = run("ln1_fwd", _ln1_fwd, x, mix, mod6, ln1_g, ln1_b)

    def relu2(acc):
        u = acc.astype(BF16)
        r = jnp.maximum(acc, 0.0)
        return u, r * r

    w_up3 = plan.get("w_up")
    nper = w_up3.shape[2]
    tm = _tile(s, 1024)
    tn = 1024
    nb = nper // tn
    u_spec = pl.BlockSpec((tm, tn), lambda i, j, kk: (i, j))
    u, ff = run(
        "up_proj", _matmul, "up_proj", h2, w_up3, mode="nn", grid=(s // tm, D_FF // tn, 1),
        a_spec=pl.BlockSpec((tm, D_MODEL), lambda i, j, kk: (i, 0)),
        b_spec=pl.BlockSpec((None, D_MODEL, tn), lambda i, j, kk: (j // nb, 0, j % nb)),
        out_shapes=[jax.ShapeDtypeStruct((s, D_FF), BF16)] * 2, out_specs=[u_spec, u_spec],
        acc_shape=(tm, tn), epilogue=relu2)
    w_down_g = plan.get("w_down").reshape(D_FF, D_MODEL)
    f2 = _mm_nn("down_proj", ff, w_down_g, F32)[0][0]
    df2, dx1a, loss, g_ln2_g, g_ln2_b, dgate2 = _ln2_loss_bwd(x1, f2, tgt, mod6, ln2_g, ln2_b)

    def relu_grad(acc, uu):
        return (acc * (2.0 * jnp.maximum(uu.astype(F32), 0.0)),)

    du = _mm_nt("d_ff", df2, w_down_g, BF16, epilogue=relu_grad, extras=(u,))[0][0]
    plan.put("g_down", _mm_tn("g_w_down", ff, df2, BF16)[0][0].reshape(N_DEV, D_FF // N_DEV, D_MODEL))
    g_up, = run(
        "g_w_up", _matmul, "g_w_up", h2, du, mode="tn",
        grid=(D_MODEL // 1024, D_FF // tn, s // _tile(s, 2048)),
        a_spec=pl.BlockSpec((_tile(s, 2048), 1024), lambda i, j, kk: (kk, i)),
        b_spec=pl.BlockSpec((_tile(s, 2048), tn), lambda i, j, kk: (kk, j)),
        out_shapes=[jax.ShapeDtypeStruct((N_DEV, D_MODEL, nper), BF16)],
        out_specs=[pl.BlockSpec((None, 1024, tn), lambda i, j, kk: (j // nb, i, j % nb))],
        acc_shape=(1024, tn))
    plan.put("g_up", g_up)
    dh2, = run(
        "d_h2", _matmul, "d_h2", du, w_up3, mode="nt", grid=(s // tm, D_MODEL // 1024, D_FF // nper),
        a_spec=pl.BlockSpec((tm, nper), lambda i, j, kk: (i, kk)),
        b_spec=pl.BlockSpec((None, 1024, nper), lambda i, j, kk: (kk, j, 0)),
        out_shapes=[jax.ShapeDtypeStruct((s, D_MODEL), F32)],
        out_specs=[pl.BlockSpec((tm, 1024), lambda i, j, kk: (i, j))],
        acc_shape=(tm, 1024))
    dmix, dxa, dscale2, dshift2, g_ln1_g, g_ln1_b, dgate1 = _ln1_bwd(dh2, dx1a, x1, x, mix, mod6, ln1_g)

    dycat = _mm_nt("d_ycat", dmix, w_out_g, F32)[0][0]
    plan.put("g_out", run("g_w_out", _mm_tn, "g_w_out", ycat, dmix, BF16)[0].reshape(
        N_DEV, D_MODEL // N_DEV, D_MODEL))
    duh, dub, duc, g_sc_w, g_sc_nw = _sc_bwd(proj, dycat, sc_w, sc_nw)
    dyc, dz, dd_lanes, g_ssm_nw = run("ssd_gate_bwd", _ssd_gate_bwd, y2, xbc, proj, dycat, d_lanes, ssm_nw)
    dxs2, db2, dc2, ddtx, dacx, dax = run("ssd_bwd", _ssd_bwd, xbc, dtx, acx, a_x, states, dyc)
    n_bc = GROUPS * N_STATE
    du_xs, gw_xs, gb_xs = _conv_silu_bwd("conv_bwd_x", proj, conv_w, conv_b, 0, D_SSM, [dxs2],
                                         scaled=(dyc, d_lanes))
    du_b, gw_b, gb_b = _conv_silu_bwd("conv_bwd_b", proj, conv_w, conv_b, D_SSM, n_bc, [db2])
    du_c, gw_c, gb_c = _conv_silu_bwd("conv_bwd_c", proj, conv_w, conv_b, D_SSM + n_bc, n_bc, [dc2])
    du_dt, g_bias_all, g_a_sums = _dt_prep_bwd(proj_dt, bias_all, a_all, ddtx, dacx)

    dproj = jnp.concatenate([dz, du_xs, du_b, du_c, du_dt[:, :2 * HEADS], duh, dub, duc], axis=1)
    dproj3 = dproj.reshape(s, N_DEV, D_IN_SHARD).transpose(1, 0, 2)
    tk = _tile(s, 2048)
    g_in, = run(
        "g_w_in", _matmul, "g_w_in", h1, dproj3, mode="tn", grid=(N_DEV, D_MODEL // 1024, s // tk),
        a_spec=pl.BlockSpec((tk, 1024), lambda i, j, kk: (kk, j)),
        b_spec=pl.BlockSpec((None, tk, D_IN_SHARD), lambda i, j, kk: (i, kk, 0)),
        out_shapes=[jax.ShapeDtypeStruct((N_DEV, D_MODEL, D_IN_SHARD), BF16)],
        out_specs=[pl.BlockSpec((None, 1024, D_IN_SHARD), lambda i, j, kk: (i, j, 0))],
        acc_shape=(1024, D_IN_SHARD))
    plan.put("g_in", g_in)
    plan.hook("after_g_w_in")
    dh1, = run(
        "d_h1", _matmul, "d_h1", dproj3, w_in_g, mode="nt", grid=(s // tm, D_MODEL // 1024, N_DEV),
        a_spec=pl.BlockSpec((None, tm, D_IN_SHARD), lambda i, j, kk: (kk, i, 0)),
        b_spec=pl.BlockSpec((None, 1024, D_IN_SHARD), lambda i, j, kk: (kk, j, 0)),
        out_shapes=[jax.ShapeDtypeStruct((s, D_MODEL), F32)],
        out_specs=[pl.BlockSpec((tm, 1024), lambda i, j, kk: (i, j))],
        acc_shape=(tm, 1024))
    grad_x, dscale1, dshift1 = _grad_x(dxa, dh1, x, mod6)

    dmod = jnp.concatenate([dshift1, dscale1, dgate1, dshift2, dscale2, dgate2], axis=1)
    g_a_direct = dax.reshape(2, HEADS, HEAD_DIM).sum(axis=-1).reshape(1, 2 * HEADS)
    g_a_all = g_a_sums + _pad_lanes(g_a_direct, 128)
    small = {
        "dmod": dmod,
        "ssm_conv_w": jnp.concatenate([gw_xs, gw_b, gw_c], axis=1),
        "ssm_conv_b": jnp.concatenate([gb_xs, gb_b, gb_c], axis=1),
        "ssm_dt_bias_f": g_bias_all[:, :HEADS],
        "ssm_dt_bias_b": g_bias_all[:, HEADS:2 * HEADS],
        "ssm_a_log_f": (g_a_all * a_all)[:, :HEADS],
        "ssm_a_log_b": (g_a_all * a_all)[:, HEADS:2 * HEADS],
        "ssm_d": dd_lanes.reshape(HEADS, HEAD_DIM).sum(axis=1).reshape(1, HEADS),
        "ssm_norm_w": g_ssm_nw,
        "sc_conv_w": g_sc_w,
        "sc_norm_w": g_sc_nw,
        "ln1_g": g_ln1_g, "ln1_b": g_ln1_b, "ln2_g": g_ln2_g, "ln2_b": g_ln2_b,
    }
    return loss, grad_x, small


_SUMMED = [("ssm_conv_b", D_XBC), ("ssm_dt_bias_f", HEADS), ("ssm_dt_bias_b", HEADS),
           ("ssm_a_log_f", HEADS), ("ssm_a_log_b", HEADS), ("ssm_d", HEADS),
           ("ssm_norm_w", D_SSM), ("sc_norm_w", D_SC),
           ("ln1_g", D_MODEL), ("ln1_b", D_MODEL), ("ln2_g", D_MODEL), ("ln2_b", D_MODEL)]


def _round_up(n, k):
    return (n + k - 1) // k * k


def _w_in_sections(w_in_g):
    w = w_in_g.transpose(1, 0, 2).reshape(w_in_g.shape[1], D_IN)
    dt_lo = D_SSM + D_XBC
    main = jnp.concatenate([w[:, :dt_lo], w[:, dt_lo + 2 * HEADS:]], axis=1)
    dt = _pad_lanes(w[:, dt_lo:dt_lo + 2 * HEADS], 128)
    return main, dt


def kernel(x, c, w_ada, b_ada, w_in, ssm_conv_w, ssm_conv_b, ssm_dt_bias_f, ssm_dt_bias_b, ssm_a_log_f, ssm_a_log_b, ssm_d, ssm_norm_w, sc_conv_w, sc_norm_w, w_out, ln1_g, ln1_b, w_up, w_down, ln2_g, ln2_b, loss_target, m_w_ada, m_b_ada, m_w_in, m_ssm_conv_w, m_ssm_conv_b, m_ssm_dt_bias_f, m_ssm_dt_bias_b, m_ssm_a_log_f, m_ssm_a_log_b, m_ssm_d, m_ssm_norm_w, m_sc_conv_w, m_sc_norm_w, m_w_out, m_ln1_g, m_ln1_b, m_w_up, m_w_down, m_ln2_g, m_ln2_b, v_w_ada, v_b_ada, v_w_in, v_ssm_conv_w, v_ssm_conv_b, v_ssm_dt_bias_f, v_ssm_dt_bias_b, v_ssm_a_log_f, v_ssm_a_log_b, v_ssm_d, v_ssm_norm_w, v_sc_conv_w, v_sc_norm_w, v_w_out, v_ln1_g, v_ln1_b, v_w_up, v_w_down, v_ln2_g, v_ln2_b):
    args = dict(locals())
    xi, yi, ci = _my_pos()
    me = 4 * xi + 2 * yi + ci
    pos = jnp.stack([xi, yi, ci]).astype(jnp.int32)
    s = x.shape[1]

    n_cw, n_sw = SSM_CONV * D_XBC // N_DEV, SC_CONV * D_SC // N_DEV
    vec = jnp.concatenate([c, ssm_conv_w[0].reshape(1, n_cw), sc_conv_w[0].reshape(1, n_sw)], axis=1)
    vec = _pad_lanes(vec, 8192)
    gath = _gather_vec("gather_c_conv", vec)
    c_all = gath[:, :D_MODEL]
    conv_w = gath[:, D_MODEL:D_MODEL + n_cw].reshape(N_DEV, SSM_CONV, D_XBC // N_DEV)
    conv_w = conv_w.transpose(1, 0, 2).reshape(SSM_CONV, D_XBC)
    sc_w = gath[:, D_MODEL + n_cw:D_MODEL + n_cw + n_sw].reshape(N_DEV, SC_CONV, D_SC // N_DEV)
    sc_w = sc_w.transpose(1, 0, 2).reshape(SC_CONV, D_SC)
    c16 = jnp.pad(c_all, ((0, 8), (0, 0)))

    n_ada = w_ada.shape[2]
    mod_cols = _ada_fwd(c16, w_ada[0])[:N_DEV]
    mod_all = _run_jobs("gather_mod", [_GatherJob(mod_cols, pltpu.VMEM)])[0]
    mod = lax.dynamic_index_in_dim(mod_all, me, axis=1, keepdims=False)
    mod = mod.reshape(1, N_MOD * D_MODEL) + b_ada

    out = {}

    def adamw(plan, tag):
        name = "w_" + tag
        if tag in two_leg:
            others = [(plan.get("ra_" + tag), 1), (plan.get("rb_" + tag), None)]
        else:
            others = [(plan.get("r2_" + tag), k) for k in range(3)]
        res = plan.run("rs_adamw_" + tag, _reduce_adamw, "rs_adamw_" + tag, plan.get("g_" + tag),
                       plan.get("r1_" + tag), others, pos, args[name][0], args["m_" + name][0],
                       args["v_" + name][0])
        out[name] = tuple(a[None] for a in res)

    two_leg = ("down", "up")
    hosted = {
        "in_proj_a": [(("part", 1, 1, 2), "w_in")],
        "in_proj_b": [("gather", "w_out"), (("part", 0, 1, 8), "w_up")],
        "ssd_fwd": [(("part", 1, 4, 8), "w_up")],
        "out_proj": [(("part", 5, 2, 8), "w_up")],
        "ln1_fwd": [(("part", 7, 1, 8), "w_up")],
        "up_proj": [("gather", "w_down")],
        "g_w_up": [("rs1", "down")],
        "d_h2": [("rs2a", "down"), ("rs1", "up")],
        "g_w_out": [("rs2b", "down")],
        "ssd_gate_bwd": [("rs1", "out")],
        "ssd_bwd": [("rs2a", "up")],
        "g_w_in": [("rs2b", "up"), ("rs2", "out")],
        "d_h1": [("rs2", "in")],
    }

    def sibling_exchange_in(plan):
        plan.put("r1_in", _run_jobs("rs_sibling_in", [_SiblingJob(plan.get("g_in"))])[0])

    store = {"pos": pos}
    for tag, w in (("w_in", w_in), ("w_out", w_out), ("w_up", w_up), ("w_down", w_down)):
        store["shard_" + tag] = _cast_bf16("cast_" + tag, w[0])
    store["part_w_in"] = _run_jobs("gather_w_in", [_GatherJob(store["shard_w_in"], rows=(0, D_MODEL // 2))])[0]
    plan = _Plan(hosted, store, hooks={"after_g_w_in": sibling_exchange_in}, two_leg=two_leg)
    loss, grad_x, small = _local_step(
        plan, x[0], loss_target[0], mod, conv_w, ssm_conv_b, ssm_dt_bias_f, ssm_dt_bias_b,
        ssm_a_log_f, ssm_a_log_b, ssm_d, ssm_norm_w, sc_w, sc_norm_w, ln1_g, ln1_b, ln2_g, ln2_b)
    for tag in ("down", "up", "out", "in"):
        adamw(plan, tag)

    parts = [small["dmod"]]
    parts += [_pad_lanes(small[n], _round_up(w, 128)) for n, w in _SUMMED]
    parts += [small["ssm_conv_w"].reshape(1, SSM_CONV * D_XBC), small["sc_conv_w"].reshape(1, SC_CONV * D_SC)]
    parts += [loss]
    gvec = jnp.concatenate(parts, axis=1)
    n_vec = _round_up(gvec.shape[1], 8192)
    gall = _gather_vec("gather_small_grads", _pad_lanes(gvec, n_vec))

    def shard_cols(full, k, per):
        return lax.dynamic_slice_in_dim(full.reshape(k, N_DEV, per), me, 1, axis=1).reshape(1, k * per)

    def placed(vals, n_rows=1):
        return jnp.concatenate(vals, axis=1)

    n_mod = N_MOD * D_MODEL
    ws, ms, vs = [b_ada], [m_b_ada], [v_b_ada]
    for n, w in _SUMMED:
        pw = _round_up(w, 128)
        ws.append(_pad_lanes(args[n], pw))
        ms.append(_pad_lanes(args["m_" + n], pw))
        vs.append(_pad_lanes(args["v_" + n], pw))

    def full_rows(shard, k, per):
        z = jnp.zeros((k, N_DEV, per), F32)
        z = lax.dynamic_update_slice_in_dim(z, shard.reshape(k, 1, per), me, axis=1)
        return z.reshape(1, k * N_DEV * per)

    for nm, k, per in (("ssm_conv_w", SSM_CONV, D_XBC // N_DEV), ("sc_conv_w", SC_CONV, D_SC // N_DEV)):
        ws.append(full_rows(args[nm][0], k, per))
        ms.append(full_rows(args["m_" + nm][0], k, per))
        vs.append(full_rows(args["v_" + nm][0], k, per))
    tail = n_vec - sum(a.shape[1] for a in ws)
    ws.append(jnp.zeros((1, tail), F32))
    ms.append(jnp.zeros((1, tail), F32))
    vs.append(jnp.ones((1, tail), F32))
    g_s, d_s, m_s, v_s = _sum8_adamw(gall, placed(ws), placed(ms), placed(vs))

    off = 0

    def take(w):
        nonlocal off
        sl = tuple(a[:, off:off + w] for a in (g_s, d_s, m_s, v_s))
        off += _round_up(w, 128)
        return sl

    out["b_ada"] = take(n_mod)
    for n, w in _SUMMED:
        out[n] = take(w)
    for nm, k, per in (("ssm_conv_w", SSM_CONV, D_XBC // N_DEV), ("sc_conv_w", SC_CONV, D_SC // N_DEV)):
        full = take(k * N_DEV * per)
        out[nm] = tuple(shard_cols(a, k, per).reshape(1, k, per) for a in full)
    loss_total = g_s[0, off]

    dmod_all = gall[:, :n_mod]
    dmod_cols = lax.dynamic_slice_in_dim(dmod_all.reshape(N_DEV, N_DEV, n_ada), me, 1, axis=1)
    dmod16 = jnp.pad(dmod_cols.reshape(N_DEV, n_ada), ((0, 8), (0, 0)))
    out["w_ada"] = tuple(a[None] for a in _ada_bwd_adamw(c16, dmod16, w_ada[0], m_w_ada[0], v_w_ada[0]))

    names = ['w_ada', 'b_ada', 'w_in', 'ssm_conv_w', 'ssm_conv_b', 'ssm_dt_bias_f', 'ssm_dt_bias_b',
             'ssm_a_log_f', 'ssm_a_log_b', 'ssm_d', 'ssm_norm_w', 'sc_conv_w', 'sc_norm_w', 'w_out',
             'ln1_g', 'ln1_b', 'w_up', 'w_down', 'ln2_g', 'ln2_b']
    res = [loss_total, grad_x[None]]
    for k in range(4):
        res += [out[n][k] for n in names]
    return tuple(res)
```

```python
import functools

import jax
import jax.numpy as jnp
from jax import lax
from jax.experimental import pallas as pl
from jax.experimental.pallas import tpu as pltpu

F32 = jnp.float32
BF16 = jnp.bfloat16
MESH = pl.DeviceIdType.MESH

N_DEV = 8
D_MODEL = 4096
D_SSM = 2048
D_SC = 2048
HEADS = 32
HEAD_DIM = 64
GROUPS = 8
GROUP_W = D_SSM // GROUPS
HEADS_PER_GROUP = 4
N_STATE = 128
CHUNK = 128
SSM_CONV = 5
SC_CONV = 3
SC_GROUP_W = 128
D_XBC = 4096
D_FF = 16384
D_IN = 12352
D_IN_SHARD = D_IN // N_DEV
D_MAIN = 12288
N_MOD = 6
ALPHA = (2 * 1) ** 0.25
LN_EPS = 1e-5
RMS_EPS = 1e-5
ADAM_LR = 0.001
ADAM_B1 = 0.9
ADAM_B2 = 0.999
ADAM_EPS = 1e-08
ADAM_WD = 0.01
ADAM_STEP = 10

VMEM_LIMIT = 56 * 1024 * 1024
HALO = 8

_DN = {
    "nn": (((1,), (0,)), ((), ())),
    "nt": (((1,), (1,)), ((), ())),
    "tn": (((0,), (0,)), ((), ())),
}


def _cparams(sem=None):
    return pltpu.CompilerParams(dimension_semantics=sem, vmem_limit_bytes=VMEM_LIMIT)


def _my_pos():
    return lax.axis_index("x"), lax.axis_index("y"), lax.axis_index("c")


def _other_chips(x, y):
    return [(1 - x, y), (x, 1 - y), (1 - x, 1 - y)]


class _GatherJob:
    n_remote = 7

    def __init__(self, shard, space=pl.ANY, rows=None, into=None):
        self.ins = (shard,) if into is None else (shard, into)
        self.alias = None if into is None else 1
        self.out_shapes = (jax.ShapeDtypeStruct((N_DEV,) + shard.shape, shard.dtype),)
        self.space = space
        self.rows = rows

    def _parts(self, ins, outs, send, recv, local):
        x_ref, out_ref = ins[0], outs[0]
        if self.rows is not None:
            x_ref = x_ref.at[pl.ds(*self.rows)]
        x, y, c = _my_pos()
        me, sibling = (x, y, c), (x, y, 1 - c)
        chips = _other_chips(x, y)

        def slab(px, py, pc):
            whole = out_ref.at[4 * px + 2 * py + pc]
            return whole if self.rows is None else whole.at[pl.ds(*self.rows)]

        def copy(k, block, to, src=None):
            return pltpu.make_async_remote_copy(
                src_ref=slab(*block) if src is None else src, dst_ref=slab(*block),
                send_sem=send.at[k], recv_sem=recv.at[k], device_id=to, device_id_type=MESH)

        mine = pltpu.make_async_copy(x_ref, slab(*me), local.at[0])
        own = [copy(0, me, sibling, src=x_ref), copy(1, me, (*chips[0], c), src=x_ref),
               copy(2, me, (*chips[1], c), src=x_ref)]
        relayed = (x + (1 - c) * (1 - 2 * x), y + c * (1 - 2 * y), c)
        relay = copy(3, relayed, (x + c * (1 - 2 * x), y + (1 - c) * (1 - 2 * y), c))
        hand = [copy(4 + j, (*chip, c), sibling) for j, chip in enumerate(chips)]
        landed = [copy(1 + j, (*chip, c), me) for j, chip in enumerate(chips)]
        handed = [copy(0, sibling, me)] + [copy(4 + j, (*chip, 1 - c), me) for j, chip in enumerate(chips)]
        return mine, own, relay, hand, landed, handed

    def start(self, *refs):
        mine, own, _, _, _, _ = self._parts(*refs)
        mine.start()
        for cp in own:
            cp.start()

    def mid(self, *refs):
        _, _, relay, hand, landed, _ = self._parts(*refs)
        landed[0].wait_recv()
        landed[1].wait_recv()
        relay.start()
        hand[0].start()
        hand[1].start()

    def finish(self, *refs):
        mine, own, relay, hand, landed, handed = self._parts(*refs)
        landed[2].wait_recv()
        hand[2].start()
        for cp in handed:
            cp.wait_recv()
        for cp in own + [relay] + hand:
            cp.wait_send()
        mine.wait()


class _SiblingJob:
    n_remote = 4
    space = pl.ANY

    def __init__(self, g):
        self.ins = (g,)
        self.out_shapes = (jax.ShapeDtypeStruct((4,) + g.shape[1:], g.dtype),)

    def _copies(self, ins, outs, send, recv, local):
        x, y, c = _my_pos()
        return [pltpu.make_async_remote_copy(
            src_ref=ins[0].at[2 * j + (1 - c)], dst_ref=outs[0].at[j],
            send_sem=send.at[j], recv_sem=recv.at[j],
            device_id=(x, y, 1 - c), device_id_type=MESH) for j in range(4)]

    def start(self, *refs):
        for cp in self._copies(*refs):
            cp.start()

    def mid(self, *refs):
        pass

    def finish(self, *refs):
        for cp in self._copies(*refs):
            cp.wait()


class _ChipsJob:
    n_remote = 3
    space = pl.ANY

    def __init__(self, p):
        self.ins = (p,)
        self.out_shapes = (jax.ShapeDtypeStruct(p.shape, p.dtype),)

    def _copies(self, ins, outs, send, recv, local):
        x, y, c = _my_pos()
        return [pltpu.make_async_remote_copy(
            src_ref=ins[0].at[k], dst_ref=outs[0].at[k],
            send_sem=send.at[k], recv_sem=recv.at[k],
            device_id=(px, py, c), device_id_type=MESH) for k, (px, py) in enumerate(_other_chips(x, y))]

    def start(self, *refs):
        for cp in self._copies(*refs):
            cp.start()

    def mid(self, *refs):
        pass

    def finish(self, *refs):
        for cp in self._copies(*refs):
            cp.wait()


def _relay_route(x, y, c):
    first = (x + c * (1 - 2 * x), y + (1 - c) * (1 - 2 * y))
    second = (x + (1 - c) * (1 - 2 * x), y + c * (1 - 2 * y))
    return first, second


class _RelayFirstJob:
    n_remote = 2
    space = pl.ANY

    def __init__(self, p):
        self.ins = (p,)
        self.out_shapes = (jax.ShapeDtypeStruct(p.shape, p.dtype),)

    def _copies(self, ins, outs, send, recv, local):
        x, y, c = _my_pos()
        (fx, fy), _ = _relay_route(x, y, c)
        return [pltpu.make_async_remote_copy(
            src_ref=ins[0].at[k], dst_ref=outs[0].at[k], send_sem=send.at[k], recv_sem=recv.at[k],
            device_id=(fx, fy, c), device_id_type=MESH) for k in range(2)]

    def start(self, *refs):
        for cp in self._copies(*refs):
            cp.start()

    def mid(self, *refs):
        pass

    def finish(self, *refs):
        for cp in self._copies(*refs):
            cp.wait()


class _RelaySecondJob:
    n_remote = 1
    space = pl.ANY

    def __init__(self, q):
        self.ins = (q,)
        self.out_shapes = (jax.ShapeDtypeStruct(q.shape, q.dtype),)

    def _copy(self, ins, outs, send, recv, local):
        x, y, c = _my_pos()
        _, (sx, sy) = _relay_route(x, y, c)
        return pltpu.make_async_remote_copy(
            src_ref=ins[0], dst_ref=outs[0], send_sem=send.at[0], recv_sem=recv.at[0],
            device_id=(sx, sy, c), device_id_type=MESH)

    def start(self, *refs):
        self._copy(*refs).start()

    def mid(self, *refs):
        pass

    def finish(self, *refs):
        self._copy(*refs).wait()


MID_STEP_FRACTION = 0.64


def _call(name, body, *, grid, in_specs, out_specs, out_shape, args, scratch_shapes=(), sem=None,
          jobs=(), n_prefetch=0):
    out_shape, out_specs, in_specs = list(out_shape), list(out_specs), list(in_specs)
    scratch_shapes = list(scratch_shapes)
    jobs = list(jobs)
    n_in, n_out, n_scr = len(in_specs), len(out_shape), len(scratch_shapes)
    job_ins = [a for j in jobs for a in j.ins]
    job_outs = [o for j in jobs for o in j.out_shapes]
    steps = 1
    for n in grid:
        steps *= n
    mid_step = min(steps - 1, int(steps * MID_STEP_FRACTION))

    def wrapped(*refs):
        pre, refs = refs[:n_prefetch], refs[n_prefetch:]
        core_in, refs = refs[:n_in], refs[n_in:]
        jin, refs = refs[:len(job_ins)], refs[len(job_ins):]
        core_out, refs = refs[:n_out], refs[n_out:]
        jout, refs = refs[:len(job_outs)], refs[len(job_outs):]
        core_scr, sems = refs[:n_scr], refs[n_scr:]
        lin = 0
        for ax, n in enumerate(grid):
            lin = lin * n + pl.program_id(ax)
        bound = []
        for j in jobs:
            ji, jin = jin[:len(j.ins)], jin[len(j.ins):]
            jo, jout = jout[:len(j.out_shapes)], jout[len(j.out_shapes):]
            (send, recv, local), sems = sems[:3], sems[3:]
            bound.append((j, (ji, jo, send, recv, local)))

        if jobs:
            @pl.when(lin == 0)
            def _():
                for j, r in bound:
                    j.start(*r)

        body(*pre, *core_in, *core_out, *core_scr)

        if jobs:
            @pl.when(lin == mid_step)
            def _():
                for j, r in bound:
                    j.mid(*r)

            @pl.when(lin == steps - 1)
            def _():
                for j, r in bound:
                    j.finish(*r)

    sem_shapes = []
    for j in jobs:
        sem_shapes += [pltpu.SemaphoreType.DMA((j.n_remote,)), pltpu.SemaphoreType.DMA((j.n_remote,)),
                       pltpu.SemaphoreType.DMA((1,))]
    if jobs:
        sem = tuple("arbitrary" for _ in grid)
    aliases = {}
    in_at, out_at = len(args), n_out
    for j in jobs:
        if getattr(j, "alias", None) is not None:
            aliases[in_at + j.alias] = out_at
        in_at, out_at = in_at + len(j.ins), out_at + len(j.out_shapes)
    res = pl.pallas_call(
        wrapped, name=name, input_output_aliases=aliases,
        grid_spec=pltpu.PrefetchScalarGridSpec(
            num_scalar_prefetch=n_prefetch, grid=tuple(grid),
            in_specs=in_specs + [pl.BlockSpec(memory_space=j.space) for j in jobs for _ in j.ins],
            out_specs=out_specs + [pl.BlockSpec(memory_space=j.space) for j in jobs for _ in j.out_shapes],
            scratch_shapes=scratch_shapes + sem_shapes),
        out_shape=out_shape + job_outs,
        compiler_params=_cparams(sem),
    )(*args, *job_ins)
    res = list(res) if isinstance(res, (list, tuple)) else [res]
    return res[:n_out], res[n_out:]


def _run_jobs(name, jobs):
    return _call(name, lambda: None, grid=(1,), in_specs=[], out_specs=[], out_shape=[], args=(),
                 jobs=jobs)[1]


def _matmul(name, a, b, *, mode, grid, a_spec, b_spec, out_shapes, out_specs, acc_shape,
            epilogue=None, extras=(), extra_specs=(), jobs=()):
    nk = grid[2]
    n_extra = len(extras)
    n_out = len(out_shapes)

    def body(*refs):
        a_ref, b_ref = refs[0], refs[1]
        extra_refs = refs[2:2 + n_extra]
        out_refs = refs[2 + n_extra:2 + n_extra + n_out]
        part = lax.dot_general(a_ref[...], b_ref[...], _DN[mode], preferred_element_type=F32)

        def finish(acc):
            outs = epilogue(acc, *[r[...] for r in extra_refs]) if epilogue else (acc,)
            for o_ref, o in zip(out_refs, outs):
                o_ref[...] = o.astype(o_ref.dtype)

        if nk == 1:
            finish(part)
        else:
            acc_ref = refs[-1]
            k = pl.program_id(2)

            @pl.when(k == 0)
            def _():
                acc_ref[...] = part

            @pl.when(k > 0)
            def _():
                acc_ref[...] += part

            @pl.when(k == nk - 1)
            def _():
                finish(acc_ref[...])

    scratch = [pltpu.VMEM(acc_shape, F32)] if nk > 1 else []
    return _call(name, body, grid=grid, in_specs=[a_spec, b_spec, *extra_specs],
                 out_specs=out_specs, out_shape=out_shapes, scratch_shapes=scratch,
                 sem=("parallel", "parallel", "arbitrary"), args=(a, b, *extras), jobs=jobs)


def _tile(n, pref):
    t = min(n, pref)
    assert n % t == 0, (n, t)
    return t


def _mm_nn(name, a, b, out_dtype, tn=1024, tk=None, epilogue=None, out_dtypes=None, jobs=(),
           a_col0=0, extras=()):
    m, k = a.shape[0], b.shape[0]
    n = b.shape[1]
    tm, tn = _tile(m, 1024), _tile(n, tn)
    tk = _tile(k, tk or 4096)
    k0 = a_col0 // tk
    assert a_col0 % tk == 0
    out_dtypes = out_dtypes or (out_dtype,)
    o_spec = pl.BlockSpec((tm, tn), lambda i, j, kk: (i, j))
    return _matmul(
        name, a, b, mode="nn", grid=(m // tm, n // tn, k // tk),
        a_spec=pl.BlockSpec((tm, tk), lambda i, j, kk: (i, k0 + kk)),
        b_spec=pl.BlockSpec((tk, tn), lambda i, j, kk: (kk, j)),
        out_shapes=[jax.ShapeDtypeStruct((m, n), dt) for dt in out_dtypes],
        out_specs=[o_spec for _ in out_dtypes],
        acc_shape=(tm, tn), epilogue=epilogue, jobs=jobs,
        extras=extras, extra_specs=[o_spec for _ in extras])


def _mm_nt(name, a, b, out_dtype, epilogue=None, extras=(), tk=None, jobs=()):
    m, k = a.shape
    n = b.shape[0]
    tm, tn = _tile(m, 1024), _tile(n, 1024)
    tk = _tile(k, tk or 4096)
    o_spec = pl.BlockSpec((tm, tn), lambda i, j, kk: (i, j))
    return _matmul(
        name, a, b, mode="nt", grid=(m // tm, n // tn, k // tk),
        a_spec=pl.BlockSpec((tm, tk), lambda i, j, kk: (i, kk)),
        b_spec=pl.BlockSpec((tn, tk), lambda i, j, kk: (j, kk)),
        out_shapes=[jax.ShapeDtypeStruct((m, n), out_dtype)],
        out_specs=[o_spec], acc_shape=(tm, tn), epilogue=epilogue,
        extras=extras, extra_specs=[o_spec for _ in extras], jobs=jobs)


def _mm_tn(name, a, b, out_dtype, tk=2048, jobs=()):
    k, m = a.shape
    n = b.shape[1]
    tm, tn = _tile(m, 1024), _tile(n, 1024)
    tk = _tile(k, tk)
    return _matmul(
        name, a, b, mode="tn", grid=(m // tm, n // tn, k // tk),
        a_spec=pl.BlockSpec((tk, tm), lambda i, j, kk: (kk, i)),
        b_spec=pl.BlockSpec((tk, tn), lambda i, j, kk: (kk, j)),
        out_shapes=[jax.ShapeDtypeStruct((m, n), out_dtype)],
        out_specs=[pl.BlockSpec((tm, tn), lambda i, j, kk: (i, j))],
        acc_shape=(tm, tn), jobs=jobs)


def _cast_bf16(name, w):
    r, c = w.shape
    tr = _tile(r, 512)

    def body(w_ref, o_ref):
        o_ref[...] = w_ref[...].astype(BF16)

    return pl.pallas_call(
        body, name=name, grid=(r // tr,),
        in_specs=[pl.BlockSpec((tr, c), lambda i: (i, 0))],
        out_specs=pl.BlockSpec((tr, c), lambda i: (i, 0)),
        out_shape=jax.ShapeDtypeStruct((r, c), BF16),
        compiler_params=_cparams(("parallel",)),
    )(w)


def _chip_of(pos, which):
    x, y, c = pos[0], pos[1], pos[2]
    first, second = _relay_route(x, y, c)
    chips = _other_chips(x, y) + [first, second, (x, y)]
    px, py = chips[which]
    return 2 * px + py


def _pair_add(name, g, r1, pos, dests):
    _, r, cdim = g.shape
    tr = _tile(r, 512)

    def chip(k, pos):
        idx = _chip_of(pos, dests[-1])
        for n in range(len(dests) - 2, -1, -1):
            idx = jnp.where(k == n, _chip_of(pos, dests[n]), idx)
        return idx

    def body(pos_ref, g_ref, r_ref, o_ref):
        o_ref[...] = (g_ref[...].astype(F32) + r_ref[...].astype(F32)).astype(o_ref.dtype)

    return pl.pallas_call(
        body, name=name,
        grid_spec=pltpu.PrefetchScalarGridSpec(
            num_scalar_prefetch=1, grid=(len(dests), r // tr),
            in_specs=[pl.BlockSpec((None, tr, cdim), lambda k, i, pos: (2 * chip(k, pos) + pos[2], i, 0)),
                      pl.BlockSpec((None, tr, cdim), lambda k, i, pos: (chip(k, pos), i, 0))],
            out_specs=pl.BlockSpec((None, tr, cdim), lambda k, i, pos: (k, i, 0))),
        out_shape=jax.ShapeDtypeStruct((len(dests), r, cdim), BF16),
        compiler_params=_cparams(("parallel", "parallel")),
    )(pos, g, r1)


def _adamw_math(w, g, m, v):
    m = ADAM_B1 * m + (1.0 - ADAM_B1) * g
    v = ADAM_B2 * v + (1.0 - ADAM_B2) * jnp.square(g)
    m_hat = m / (1.0 - ADAM_B1 ** ADAM_STEP)
    v_hat = v / (1.0 - ADAM_B2 ** ADAM_STEP)
    delta = -ADAM_LR * (m_hat / (jnp.sqrt(v_hat) + ADAM_EPS) + ADAM_WD * w)
    return delta, m, v


def _relay_add(name, g, r1, ra, pos):
    _, r, cdim = g.shape
    tr = _tile(r, 512)

    def body(pos_ref, g_ref, r1_ref, ra_ref, o_ref):
        q = g_ref[...].astype(F32) + r1_ref[...].astype(F32) + ra_ref[...].astype(F32)
        o_ref[...] = q.astype(o_ref.dtype)

    return pl.pallas_call(
        body, name=name,
        grid_spec=pltpu.PrefetchScalarGridSpec(
            num_scalar_prefetch=1, grid=(r // tr,),
            in_specs=[pl.BlockSpec((None, tr, cdim), lambda i, pos: (2 * _chip_of(pos, 4) + pos[2], i, 0)),
                      pl.BlockSpec((None, tr, cdim), lambda i, pos: (_chip_of(pos, 4), i, 0)),
                      pl.BlockSpec((None, tr, cdim), lambda i, pos: (0, i, 0))],
            out_specs=pl.BlockSpec((tr, cdim), lambda i, pos: (i, 0))),
        out_shape=jax.ShapeDtypeStruct((r, cdim), BF16),
        compiler_params=_cparams(("parallel",)),
    )(pos, g, r1, ra)


def _reduce_adamw(name, g8, r1, others, pos, w, m, v, jobs=()):
    r, cdim = w.shape
    tr = _tile(r, 128 if cdim >= D_MODEL else 256)
    blk = pl.BlockSpec((tr, cdim), lambda i, pos: (i, 0))
    n_other = len(others)

    def body(pos_ref, g_ref, r1_ref, *refs):
        other_refs, (w_ref, m_ref, v_ref, g_out, d_out, m_out, v_out) = refs[:n_other], refs[n_other:]
        g = g_ref[...].astype(F32) + r1_ref[...].astype(F32)
        for o_ref in other_refs:
            g = g + o_ref[...].astype(F32)
        d, mn, vn = _adamw_math(w_ref[...], g, m_ref[...], v_ref[...])
        g_out[...] = g
        d_out[...] = d
        m_out[...] = mn
        v_out[...] = vn

    def other_spec(lead):
        if lead is None:
            return blk
        return pl.BlockSpec((None, tr, cdim), lambda i, pos: (lead, i, 0))

    shp = jax.ShapeDtypeStruct((r, cdim), F32)
    return _call(
        name, body, grid=(r // tr,), n_prefetch=1,
        in_specs=[pl.BlockSpec((None, tr, cdim), lambda i, pos: (2 * _chip_of(pos, 5) + pos[2], i, 0)),
                  pl.BlockSpec((None, tr, cdim), lambda i, pos: (_chip_of(pos, 5), i, 0))]
        + [other_spec(lead) for _, lead in others] + [blk, blk, blk],
        out_specs=[blk, blk, blk, blk], out_shape=[shp, shp, shp, shp],
        sem=("parallel",), args=(pos, g8, r1, *[a for a, _ in others], w, m, v), jobs=jobs)


def _row_spec(t, width=D_MODEL):
    return pl.BlockSpec((t, width), lambda i: (i, 0))


def _full_spec(shape):
    return pl.BlockSpec(shape, lambda i: tuple(0 for _ in shape))


def _ln_stats(p):
    mu = jnp.mean(p, axis=-1, keepdims=True)
    xc = p - mu
    var = jnp.mean(xc * xc, axis=-1, keepdims=True)
    rstd = lax.rsqrt(var + LN_EPS)
    return xc * rstd, rstd


def _ln_bwd(dy, xhat, rstd, g):
    dxh = dy * g
    m1 = jnp.mean(dxh, axis=-1, keepdims=True)
    m2 = jnp.mean(dxh * xhat, axis=-1, keepdims=True)
    return rstd * (dxh - m1 - xhat * m2)


def _acc_rows(ref, val, first):
    s = jnp.sum(val, axis=0, keepdims=True)

    @pl.when(first)
    def _():
        ref[...] = s

    @pl.when(jnp.logical_not(first))
    def _():
        ref[...] += s


def _modulate(name, x, mod6):
    s = x.shape[0]
    t = _tile(s, 256)

    def body(x_ref, mod_ref, o_ref):
        o_ref[...] = (x_ref[...] * (1.0 + mod_ref[1:2, :]) + mod_ref[0:1, :]).astype(BF16)

    return pl.pallas_call(
        body, name=name, grid=(s // t,),
        in_specs=[_row_spec(t), _full_spec((N_MOD, D_MODEL))],
        out_specs=_row_spec(t),
        out_shape=jax.ShapeDtypeStruct((s, D_MODEL), BF16),
        compiler_params=_cparams(("parallel",)),
    )(x, mod6)


def _ln1_fwd(x, mix, mod6, g, b, jobs=()):
    s = x.shape[0]
    t = _tile(s, 256)

    def body(x_ref, mix_ref, mod_ref, g_ref, b_ref, x1_ref, h2_ref):
        pre = ALPHA * x_ref[...] + (1.0 + mod_ref[2:3, :]) * mix_ref[...]
        xhat, _ = _ln_stats(pre)
        x1 = xhat * g_ref[...] + b_ref[...]
        x1_ref[...] = x1
        h2_ref[...] = (x1 * (1.0 + mod_ref[4:5, :]) + mod_ref[3:4, :]).astype(BF16)

    vec = _full_spec((1, D_MODEL))
    return _call(
        "ln1_fwd", body, grid=(s // t,),
        in_specs=[_row_spec(t), _row_spec(t), _full_spec((N_MOD, D_MODEL)), vec, vec],
        out_specs=[_row_spec(t), _row_spec(t)],
        out_shape=[jax.ShapeDtypeStruct((s, D_MODEL), F32), jax.ShapeDtypeStruct((s, D_MODEL), BF16)],
        sem=("parallel",), args=(x, mix, mod6, g, b), jobs=jobs)


def _ln2_loss_bwd(x1, f2, tgt, mod6, g, b):
    s = x1.shape[0]
    t = _tile(s, 128)

    def body(x1_ref, f2_ref, tgt_ref, mod_ref, g_ref, b_ref,
             df2_ref, dx1_ref, loss_ref, dg_ref, db_ref, dgate_ref):
        first = pl.program_id(0) == 0
        gate = 1.0 + mod_ref[5:6, :]
        f2v = f2_ref[...]
        pre = ALPHA * x1_ref[...] + gate * f2v
        xhat, rstd = _ln_stats(pre)
        err = xhat * g_ref[...] + b_ref[...] - tgt_ref[...]
        part = 0.5 * jnp.sum(jnp.mean(err * err, axis=-1, keepdims=True), axis=0, keepdims=True)
        dy = err / D_MODEL
        dpre = _ln_bwd(dy, xhat, rstd, g_ref[...])
        df2_ref[...] = (gate * dpre).astype(BF16)
        dx1_ref[...] = ALPHA * dpre
        _acc_rows(loss_ref, jnp.broadcast_to(part, (1, 128)), first)
        _acc_rows(dg_ref, dy * xhat, first)
        _acc_rows(db_ref, dy, first)
        _acc_rows(dgate_ref, dpre * f2v, first)

    vec = _full_spec((1, D_MODEL))
    vshape = jax.ShapeDtypeStruct((1, D_MODEL), F32)
    return pl.pallas_call(
        body, name="ln2_loss_bwd", grid=(s // t,),
        in_specs=[_row_spec(t), _row_spec(t), _row_spec(t), _full_spec((N_MOD, D_MODEL)), vec, vec],
        out_specs=[_row_spec(t), _row_spec(t), _full_spec((1, 128)), vec, vec, vec],
        out_shape=[jax.ShapeDtypeStruct((s, D_MODEL), BF16), jax.ShapeDtypeStruct((s, D_MODEL), F32),
                   jax.ShapeDtypeStruct((1, 128), F32), vshape, vshape, vshape],
        compiler_params=_cparams(("arbitrary",)),
    )(x1, f2, tgt, mod6, g, b)


def _ln1_bwd(dh2, dx1a, x1, x, mix, mod6, g):
    s = x.shape[0]
    t = _tile(s, 128)

    def body(dh2_ref, dx1a_ref, x1_ref, x_ref, mix_ref, mod_ref, g_ref,
             dmix_ref, dxa_ref, dscale_ref, dshift_ref, dg_ref, db_ref, dgate_ref):
        first = pl.program_id(0) == 0
        dh2v = dh2_ref[...]
        dx1 = dx1a_ref[...] + dh2v * (1.0 + mod_ref[4:5, :])
        gate = 1.0 + mod_ref[2:3, :]
        mixv = mix_ref[...]
        pre = ALPHA * x_ref[...] + gate * mixv
        xhat, rstd = _ln_stats(pre)
        dpre = _ln_bwd(dx1, xhat, rstd, g_ref[...])
        dmix_ref[...] = (gate * dpre).astype(BF16)
        dxa_ref[...] = ALPHA * dpre
        _acc_rows(dscale_ref, dh2v * x1_ref[...], first)
        _acc_rows(dshift_ref, dh2v, first)
        _acc_rows(dg_ref, dx1 * xhat, first)
        _acc_rows(db_ref, dx1, first)
        _acc_rows(dgate_ref, dpre * mixv, first)

    vec = _full_spec((1, D_MODEL))
    vshape = jax.ShapeDtypeStruct((1, D_MODEL), F32)
    return pl.pallas_call(
        body, name="ln1_bwd", grid=(s // t,),
        in_specs=[_row_spec(t)] * 5 + [_full_spec((N_MOD, D_MODEL)), vec],
        out_specs=[_row_spec(t), _row_spec(t), vec, vec, vec, vec, vec],
        out_shape=[jax.ShapeDtypeStruct((s, D_MODEL), BF16), jax.ShapeDtypeStruct((s, D_MODEL), F32),
                   vshape, vshape, vshape, vshape, vshape],
        compiler_params=_cparams(("arbitrary",)),
    )(dh2, dx1a, x1, x, mix, mod6, g)


def _grad_x(dxa, dh1, x, mod6):
    s = x.shape[0]
    t = _tile(s, 256)

    def body(dxa_ref, dh1_ref, x_ref, mod_ref, gx_ref, dscale_ref, dshift_ref):
        first = pl.program_id(0) == 0
        dh1v = dh1_ref[...]
        gx_ref[...] = dxa_ref[...] + dh1v * (1.0 + mod_ref[1:2, :])
        _acc_rows(dscale_ref, dh1v * x_ref[...], first)
        _acc_rows(dshift_ref, dh1v, first)

    vec = _full_spec((1, D_MODEL))
    vshape = jax.ShapeDtypeStruct((1, D_MODEL), F32)
    return pl.pallas_call(
        body, name="grad_x", grid=(s // t,),
        in_specs=[_row_spec(t)] * 3 + [_full_spec((N_MOD, D_MODEL))],
        out_specs=[_row_spec(t), vec, vec],
        out_shape=[jax.ShapeDtypeStruct((s, D_MODEL), F32), vshape, vshape],
        compiler_params=_cparams(("arbitrary",)),
    )(dxa, dh1, x, mod6)


def _window(ref, i, t, s):
    r0 = pl.multiple_of(i * t, t)
    cur = ref[pl.ds(r0, t), :]
    lo = pl.multiple_of(jnp.maximum(r0 - HALO, 0), HALO)
    hi = pl.multiple_of(jnp.minimum(r0 + t, s - HALO), HALO)
    before = ref[pl.ds(lo, HALO), :] * (i > 0).astype(F32)
    after = ref[pl.ds(hi, HALO), :] * (i < s // t - 1).astype(F32)
    return jnp.concatenate([before, cur, after], axis=0)


def _tap(ext, shift):
    n = ext.shape[0]
    if shift == 0:
        return ext
    return pltpu.roll(ext, (-shift) % n, 0)


def _centre(ext, t):
    return ext[HALO:HALO + t]


def _conv_taps(ext, w, width):
    acc = None
    for k in range(width):
        term = _tap(ext, k - width // 2) * w[k:k + 1, :]
        acc = term if acc is None else acc + term
    return acc


def _silu(a):
    return a * jax.nn.sigmoid(a)


def _conv_silu_fwd(proj, w, b):
    s = proj.shape[0]
    cb = 256
    t = _tile(s, 256)
    off = D_SSM // cb

    def body(u_ref, w_ref, b_ref, o_ref):
        wv = w_ref[...]
        bv = b_ref[...]

        def step(i, carry):
            ext = _window(u_ref, i, t, s)
            a = _centre(_conv_taps(ext, wv, SSM_CONV), t) + bv
            o_ref[pl.ds(pl.multiple_of(i * t, t), t), :] = _silu(a)
            return carry

        lax.fori_loop(0, s // t, step, 0)

    return pl.pallas_call(
        body, name="conv_silu_fwd", grid=(D_XBC // cb,),
        in_specs=[pl.BlockSpec((s, cb), lambda j: (0, off + j)),
                  pl.BlockSpec((SSM_CONV, cb), lambda j: (0, j)),
                  pl.BlockSpec((1, cb), lambda j: (0, j))],
        out_specs=pl.BlockSpec((s, cb), lambda j: (0, j)),
        out_shape=jax.ShapeDtypeStruct((s, D_XBC), F32),
        compiler_params=_cparams(("parallel",)),
    )(proj, w, b)


def _conv_silu_bwd(name, proj, w, b, col0, ncols, cots, scaled=None):
    s = proj.shape[0]
    cb = 128
    t = _tile(s, 256)
    off = (D_SSM + col0) // cb
    woff = col0 // cb
    n_cot = len(cots)

    def body(*refs):
        u_ref, w_ref, b_ref = refs[:3]
        cot_refs = refs[3:3 + n_cot]
        sc_refs = refs[3 + n_cot:3 + n_cot + (2 if scaled else 0)]
        du_ref, dw_ref, db_ref = refs[-3:]
        wv = w_ref[...]
        bv = b_ref[...]

        def step(i, carry):
            ext = _window(u_ref, i, t, s)
            a = _conv_taps(ext, wv, SSM_CONV) + bv
            cot = None
            for cr in cot_refs:
                term = _window(cr.at[0], i, t, s) + _window(cr.at[1], i, t, s)
                cot = term if cot is None else cot + term
            if scaled:
                cot = cot + _window(sc_refs[0], i, t, s) * sc_refs[1][...]
            sig = jax.nn.sigmoid(a)
            da = cot * (sig * (1.0 + a * (1.0 - sig)))
            du = None
            new = []
            for k in range(SSM_CONV):
                sh = k - SSM_CONV // 2
                term = _tap(da, -sh) * wv[k:k + 1, :]
                du = term if du is None else du + term
                prod = _centre(_tap(ext, sh) * da, t)
                new.append(carry[k] + jnp.sum(prod, axis=0, keepdims=True))
            new.append(carry[SSM_CONV] + jnp.sum(_centre(da, t), axis=0, keepdims=True))
            du_ref[pl.ds(pl.multiple_of(i * t, t), t), :] = _centre(du, t).astype(BF16)
            return tuple(new)

        zero = jnp.zeros((1, cb), F32)
        acc = lax.fori_loop(0, s // t, step, tuple(zero for _ in range(SSM_CONV + 1)))
        for k in range(SSM_CONV):
            dw_ref[k:k + 1, :] = acc[k]
        db_ref[...] = acc[SSM_CONV]

    in_specs = [pl.BlockSpec((s, cb), lambda j: (0, off + j)),
                pl.BlockSpec((SSM_CONV, cb), lambda j: (0, woff + j)),
                pl.BlockSpec((1, cb), lambda j: (0, woff + j))]
    in_specs += [pl.BlockSpec((2, s, cb), lambda j: (0, 0, j)) for _ in cots]
    args = [proj, w, b, *cots]
    if scaled:
        in_specs += [pl.BlockSpec((s, cb), lambda j: (0, j)), pl.BlockSpec((1, cb), lambda j: (0, j))]
        args += list(scaled)
    return pl.pallas_call(
        body, name=name, grid=(ncols // cb,),
        in_specs=in_specs,
        out_specs=[pl.BlockSpec((s, cb), lambda j: (0, j)),
                   pl.BlockSpec((SSM_CONV, cb), lambda j: (0, j)),
                   pl.BlockSpec((1, cb), lambda j: (0, j))],
        out_shape=[jax.ShapeDtypeStruct((s, ncols), BF16),
                   jax.ShapeDtypeStruct((SSM_CONV, ncols), F32),
                   jax.ShapeDtypeStruct((1, ncols), F32)],
        compiler_params=_cparams(("parallel",)),
    )(*args)


_SC_H = (D_SSM + D_XBC) // SC_GROUP_W
_SC_B = _SC_H + D_SC // SC_GROUP_W
_SC_C = _SC_B + D_SC // SC_GROUP_W


def _sc_fwd(proj, w, nw):
    s = proj.shape[0]
    cb = SC_GROUP_W
    t = _tile(s, 256)

    def body(uh_ref, ub_ref, uc_ref, w_ref, nw_ref, o_ref):
        wv = w_ref[...]
        nwv = nw_ref[...]

        def step(i, carry):
            p = _window(uc_ref, i, t, s) * _window(uh_ref, i, t, s)
            cv = _centre(_conv_taps(p, wv, SC_CONV), t)
            rows = pl.ds(pl.multiple_of(i * t, t), t)
            y = ub_ref[rows, :] * cv
            r = lax.rsqrt(jnp.mean(y * y, axis=-1, keepdims=True) + RMS_EPS)
            o_ref[rows, :] = (y * r * nwv).astype(BF16)
            return carry

        lax.fori_loop(0, s // t, step, 0)

    def col(base):
        return pl.BlockSpec((s, cb), lambda j: (0, base + j))

    return pl.pallas_call(
        body, name="sc_fwd", grid=(D_SC // cb,),
        in_specs=[col(_SC_H), col(_SC_B), col(_SC_C),
                  pl.BlockSpec((SC_CONV, cb), lambda j: (0, j)),
                  pl.BlockSpec((1, cb), lambda j: (0, j))],
        out_specs=pl.BlockSpec((s, cb), lambda j: (0, j)),
        out_shape=jax.ShapeDtypeStruct((s, D_SC), BF16),
        compiler_params=_cparams(("parallel",)),
    )(proj, proj, proj, w, nw)


def _sc_bwd(proj, dycat, w, nw):
    s = proj.shape[0]
    cb = SC_GROUP_W
    t = _tile(s, 256)
    dy_off = D_SSM // cb

    def body(uh_ref, ub_ref, uc_ref, dy_ref, w_ref, nw_ref, duh_ref, dub_ref, duc_ref, dw_ref, dnw_ref):
        wv = w_ref[...]
        nwv = nw_ref[...]

        def step(i, carry):
            uh = _window(uh_ref, i, t, s)
            ub = _window(ub_ref, i, t, s)
            uc = _window(uc_ref, i, t, s)
            do = _window(dy_ref, i, t, s)
            p = uc * uh
            cv = _conv_taps(p, wv, SC_CONV)
            y = ub * cv
            r = lax.rsqrt(jnp.mean(y * y, axis=-1, keepdims=True) + RMS_EPS)
            dyr = do * nwv
            dy = r * dyr - y * (r * r * r) * jnp.mean(dyr * y, axis=-1, keepdims=True)
            dcv = dy * ub
            dp = None
            new = []
            for k in range(SC_CONV):
                sh = k - SC_CONV // 2
                term = _tap(dcv, -sh) * wv[k:k + 1, :]
                dp = term if dp is None else dp + term
                new.append(carry[k] + jnp.sum(_centre(_tap(p, sh) * dcv, t), axis=0, keepdims=True))
            new.append(carry[SC_CONV] + jnp.sum(_centre(do * y * r, t), axis=0, keepdims=True))
            rows = pl.ds(pl.multiple_of(i * t, t), t)
            duh_ref[rows, :] = _centre(dp * uc, t).astype(BF16)
            duc_ref[rows, :] = _centre(dp * uh, t).astype(BF16)
            dub_ref[rows, :] = _centre(dy * cv, t).astype(BF16)
            return tuple(new)

        zero = jnp.zeros((1, cb), F32)
        acc = lax.fori_loop(0, s // t, step, tuple(zero for _ in range(SC_CONV + 1)))
        for k in range(SC_CONV):
            dw_ref[k:k + 1, :] = acc[k]
        dnw_ref[...] = acc[SC_CONV]

    def col(base):
        return pl.BlockSpec((s, cb), lambda j: (0, base + j))

    out_col = pl.BlockSpec((s, cb), lambda j: (0, j))
    act = jax.ShapeDtypeStruct((s, D_SC), BF16)
    return pl.pallas_call(
        body, name="sc_bwd", grid=(D_SC // cb,),
        in_specs=[col(_SC_H), col(_SC_B), col(_SC_C), col(dy_off),
                  pl.BlockSpec((SC_CONV, cb), lambda j: (0, j)),
                  pl.BlockSpec((1, cb), lambda j: (0, j))],
        out_specs=[out_col, out_col, out_col,
                   pl.BlockSpec((SC_CONV, cb), lambda j: (0, j)),
                   pl.BlockSpec((1, cb), lambda j: (0, j))],
        out_shape=[act, act, act, jax.ShapeDtypeStruct((SC_CONV, D_SC), F32),
                   jax.ShapeDtypeStruct((1, D_SC), F32)],
        compiler_params=_cparams(("parallel",)),
    )(proj, proj, proj, dycat, w, nw)


def _make_select_dot(differentiable):
    def raw(a, b, mode, const):
        ops = [a, b]
        v = ops[1 - const]
        acc = None
        for _ in range(3):
            piece = v.astype(BF16)
            v = v - piece.astype(F32)
            ops[1 - const] = piece
            part = lax.dot_general(ops[0].astype(BF16), ops[1].astype(BF16), _DN[mode],
                                   preferred_element_type=F32)
            acc = part if acc is None else acc + part
        return acc

    if not differentiable:
        return raw

    @functools.partial(jax.custom_vjp, nondiff_argnums=(2, 3))
    def dot(a, b, mode, const):
        return raw(a, b, mode, const)

    def fwd(a, b, mode, const):
        return raw(a, b, mode, const), (a, b)

    def bwd(mode, const, res, g):
        a, b = res
        assert mode == "nn"
        if const == 1:
            return raw(g, b, "nt", 1), jnp.zeros_like(b)
        return jnp.zeros_like(a), raw(a, g, "tn", 0)

    dot.defvjp(fwd, bwd)
    return dot


def _make_dot(differentiable):
    def raw(a, b, mode):
        return lax.dot_general(a.astype(BF16), b.astype(BF16), _DN[mode], preferred_element_type=F32)

    if not differentiable:
        return raw

    @functools.partial(jax.custom_vjp, nondiff_argnums=(2,))
    def dot(a, b, mode):
        return raw(a, b, mode)

    def fwd(a, b, mode):
        return raw(a, b, mode), (a, b)

    def bwd(mode, res, g):
        a, b = res
        if mode == "nn":
            return raw(g, b, "nt"), raw(a, g, "tn")
        if mode == "nt":
            return raw(g, b, "nn"), raw(g, a, "tn")
        return raw(b, g, "nt"), raw(a, g, "nn")

    dot.defvjp(fwd, bwd)
    return dot


def _make_swap(differentiable):
    def raw(v):
        return pltpu.roll(v, HEAD_DIM, 1)

    if not differentiable:
        return raw
    swap = jax.custom_vjp(raw)
    swap.defvjp(lambda v: (raw(v), None), lambda _, g: (raw(g),))
    return swap


def _ssd_chunk(xs, bm, cm, dtx, acx, ax, prev, tri, differentiable):
    _bdot = _make_dot(differentiable)
    swap = _make_swap(differentiable)
    atx = jnp.sum(dtx * ax, axis=0, keepdims=True)
    xdt = xs * dtx
    mask = tri > 0.0
    scores = _bdot(cm, bm, "nt")
    head = lax.broadcasted_iota(jnp.int32, (1, GROUP_W), 1) // HEAD_DIM
    low = lax.broadcasted_iota(jnp.int32, (1, 128), 1) < HEAD_DIM
    y = _bdot(cm, prev, "nn") * jnp.exp(acx)
    for h in range(HEADS_PER_GROUP):
        pair = acx[:, 128 * (h // 2):128 * (h // 2) + 128]
        other = swap(pair)
        m1 = jnp.where(low, pair, other) if h % 2 == 0 else jnp.where(low, other, pair)
        seg = m1 - m1.T
        decay = jnp.where(mask, jnp.exp(jnp.where(mask, seg, 0.0)), 0.0)
        xh = xdt * (head == h).astype(F32)
        y = y + _bdot(scores * decay, xh, "nn")
    new = prev * jnp.exp(atx) + _bdot(bm, xdt * jnp.exp(atx - acx), "tn")
    return y, new


def _softplus(v):
    return jnp.maximum(v, 0.0) + jnp.log(1.0 + jnp.exp(-jnp.abs(v)))


def _dt_spread(u, bias, a, tri2, exf, differentiable):
    sel = _make_select_dot(differentiable)
    dt = _softplus(u + bias)
    dta = dt * a
    out = []
    for d in range(2):
        acum = sel(tri2[d], dta, "nn", 0)
        out += [sel(dt, exf[d], "nn", 1), sel(acum, exf[d], "nn", 1)]
    return tuple(out)


def _ssd_consts():
    q = CHUNK
    r = lax.broadcasted_iota(jnp.int32, (q, q), 0)
    c = lax.broadcasted_iota(jnp.int32, (q, q), 1)
    tri = jnp.stack([(c <= r), (c >= r)]).astype(F32)
    shp = (2, 128, D_SSM)
    src = lax.broadcasted_iota(jnp.int32, shp, 1)
    d = lax.broadcasted_iota(jnp.int32, shp, 0)
    col = lax.broadcasted_iota(jnp.int32, shp, 2)
    exf = (src == d * HEADS + col // HEAD_DIM).astype(F32)
    return tri, exf


def _dt_prep(proj_dt, bias_all, a_all, jobs=()):
    s = proj_dt.shape[0]
    tri, exf = _ssd_consts()

    def body(u_ref, b_ref, a_ref, tri_ref, exf_ref, dtx_ref, acx_ref):
        dtx0, acx0, dtx1, acx1 = _dt_spread(u_ref[...], b_ref[...], a_ref[...], tri_ref[...],
                                            exf_ref[...], False)
        dtx_ref[0] = dtx0
        dtx_ref[1] = dtx1
        acx_ref[0] = acx0
        acx_ref[1] = acx1

    out = pl.BlockSpec((2, CHUNK, D_SSM), lambda i: (0, i, 0))
    shp = jax.ShapeDtypeStruct((2, s, D_SSM), F32)
    return _call(
        "dt_prep", body, grid=(s // CHUNK,),
        in_specs=[_row_spec(CHUNK, 128), _full_spec((1, 128)), _full_spec((1, 128)),
                  _full_spec((2, CHUNK, CHUNK)), _full_spec((2, 128, D_SSM))],
        out_specs=[out, out], out_shape=[shp, shp],
        sem=("parallel",), args=(proj_dt, bias_all, a_all, tri, exf), jobs=jobs)


def _dt_prep_bwd(proj_dt, bias_all, a_all, d_dtx, d_acx):
    s = proj_dt.shape[0]
    tri, exf = _ssd_consts()

    def body(u_ref, b_ref, a_ref, tri_ref, exf_ref, ddtx_ref, dacx_ref, du_ref, db_ref, da_ref):
        tri_v, exf_v = tri_ref[...], exf_ref[...]

        def f(u, bias, a):
            return _dt_spread(u, bias, a, tri_v, exf_v, True)

        _, vjp = jax.vjp(f, u_ref[...], b_ref[...], a_ref[...])
        du, db, da = vjp((ddtx_ref[0], dacx_ref[0], ddtx_ref[1], dacx_ref[1]))
        du_ref[...] = du.astype(BF16)
        first = pl.program_id(0) == 0
        _acc_rows(db_ref, db, first)
        _acc_rows(da_ref, da, first)

    cot = pl.BlockSpec((2, CHUNK, D_SSM), lambda i: (0, i, 0))
    vec = _full_spec((1, 128))
    return pl.pallas_call(
        body, name="dt_prep_bwd", grid=(s // CHUNK,),
        in_specs=[_row_spec(CHUNK, 128), vec, vec, _full_spec((2, CHUNK, CHUNK)),
                  _full_spec((2, 128, D_SSM)), cot, cot],
        out_specs=[_row_spec(CHUNK, 128), vec, vec],
        out_shape=[jax.ShapeDtypeStruct((s, 128), BF16), jax.ShapeDtypeStruct((1, 128), F32),
                   jax.ShapeDtypeStruct((1, 128), F32)],
        compiler_params=_cparams(("arbitrary",)),
    )(proj_dt, bias_all, a_all, tri, exf, d_dtx, d_acx)


GROUPS_PER_STEP = 4
_PAIR_W = GROUPS_PER_STEP * GROUP_W
_PAIR_N = GROUPS_PER_STEP * N_STATE


def _ssd_specs(chunk_of):
    q = CHUNK
    b0 = D_SSM // _PAIR_N
    xs = pl.BlockSpec((q, _PAIR_W), lambda d, g, ci: (chunk_of(d, ci), g))
    bm = pl.BlockSpec((q, _PAIR_N), lambda d, g, ci: (chunk_of(d, ci), b0 + g))
    cm = pl.BlockSpec((q, _PAIR_N), lambda d, g, ci: (chunk_of(d, ci), b0 + GROUPS // GROUPS_PER_STEP + g))
    spread = pl.BlockSpec((None, q, _PAIR_W), lambda d, g, ci: (d, chunk_of(d, ci), g))
    ax = pl.BlockSpec((None, 1, _PAIR_W), lambda d, g, ci: (d, 0, g))
    tri = pl.BlockSpec((None, q, q), lambda d, g, ci: (d, 0, 0))
    st = pl.BlockSpec((None, None, GROUPS_PER_STEP, N_STATE, GROUP_W),
                      lambda d, g, ci: (d, chunk_of(d, ci), g, 0, 0))
    return xs, bm, cm, spread, ax, tri, st


def _wide(k):
    return slice(k * GROUP_W, (k + 1) * GROUP_W)


def _narrow(k):
    return slice(k * N_STATE, (k + 1) * N_STATE)


def _ssd_fwd(xbc, dtx, acx, ax, jobs=()):
    s = xbc.shape[0]
    nc = s // CHUNK
    tri, _ = _ssd_consts()

    def chunk_of(d, ci):
        return ci + d * (nc - 1 - 2 * ci)

    def body(xs_ref, b_ref, c_ref, dtx_ref, acx_ref, ax_ref, tri_ref, y_ref, st_ref, state):
        @pl.when(pl.program_id(2) == 0)
        def _():
            state[...] = jnp.zeros(state.shape, F32)

        tri_v = tri_ref[...]
        for k in range(GROUPS_PER_STEP):
            prev = state[k]
            st_ref[k] = prev
            y, new = _ssd_chunk(xs_ref[:, _wide(k)], b_ref[:, _narrow(k)], c_ref[:, _narrow(k)],
                                dtx_ref[:, _wide(k)], acx_ref[:, _wide(k)], ax_ref[:, _wide(k)],
                                prev, tri_v, False)
            y_ref[:, _wide(k)] = y
            state[k] = new

    xs, bm, cm, spread, ax_s, tri_s, st = _ssd_specs(chunk_of)
    return _call(
        "ssd_fwd", body, grid=(2, GROUPS // GROUPS_PER_STEP, nc),
        in_specs=[xs, bm, cm, spread, spread, ax_s, tri_s],
        out_specs=[spread, st],
        out_shape=[jax.ShapeDtypeStruct((2, s, D_SSM), F32),
                   jax.ShapeDtypeStruct((2, nc, GROUPS, N_STATE, GROUP_W), F32)],
        scratch_shapes=[pltpu.VMEM((GROUPS_PER_STEP, N_STATE, GROUP_W), F32)],
        sem=("arbitrary", "arbitrary", "arbitrary"),
        args=(xbc, xbc, xbc, dtx, acx, ax, tri), jobs=jobs)


def _ssd_bwd(xbc, dtx, acx, ax, states, dy, jobs=()):
    s = xbc.shape[0]
    nc = s // CHUNK
    tri, _ = _ssd_consts()

    def chunk_of(d, ci):
        return (nc - 1 - ci) + d * (2 * ci - (nc - 1))

    def body(xs_ref, b_ref, c_ref, dtx_ref, acx_ref, ax_ref, tri_ref, st_ref, dy_ref,
             dxs_ref, db_ref, dc_ref, ddtx_ref, dacx_ref, dax_ref, dstate):
        first = pl.program_id(2) == 0

        @pl.when(first)
        def _():
            dstate[...] = jnp.zeros(dstate.shape, F32)

        tri_v = tri_ref[...]

        def f(xs, bm, cm, dtx_v, acx_v, ax_v, prev):
            return _ssd_chunk(xs, bm, cm, dtx_v, acx_v, ax_v, prev, tri_v, True)

        dax_parts = []
        for k in range(GROUPS_PER_STEP):
            _, vjp = jax.vjp(f, xs_ref[:, _wide(k)], b_ref[:, _narrow(k)], c_ref[:, _narrow(k)],
                             dtx_ref[:, _wide(k)], acx_ref[:, _wide(k)], ax_ref[:, _wide(k)], st_ref[k])
            dxs, dbm, dcm, ddtx, dacx, dax, dprev = vjp((dy_ref[:, _wide(k)], dstate[k]))
            dxs_ref[:, _wide(k)] = dxs
            db_ref[:, _narrow(k)] = dbm
            dc_ref[:, _narrow(k)] = dcm
            ddtx_ref[:, _wide(k)] = ddtx
            dacx_ref[:, _wide(k)] = dacx
            dstate[k] = dprev
            dax_parts.append(dax)
        _acc_rows(dax_ref, jnp.concatenate(dax_parts, axis=1), first)

    xs, bm, cm, spread, ax_s, tri_s, st = _ssd_specs(chunk_of)
    dy_s = pl.BlockSpec((CHUNK, _PAIR_W), lambda d, g, ci: (chunk_of(d, ci), g))
    bc_s = pl.BlockSpec((None, CHUNK, _PAIR_N), lambda d, g, ci: (d, chunk_of(d, ci), g))
    wide = jax.ShapeDtypeStruct((2, s, D_SSM), F32)
    narrow = jax.ShapeDtypeStruct((2, s, GROUPS * N_STATE), F32)
    return _call(
        "ssd_bwd", body, grid=(2, GROUPS // GROUPS_PER_STEP, nc),
        in_specs=[xs, bm, cm, spread, spread, ax_s, tri_s, st, dy_s],
        out_specs=[spread, bc_s, bc_s, spread, spread, ax_s],
        out_shape=[wide, narrow, narrow, wide, wide, jax.ShapeDtypeStruct((2, 1, D_SSM), F32)],
        scratch_shapes=[pltpu.VMEM((GROUPS_PER_STEP, N_STATE, GROUP_W), F32)],
        sem=("arbitrary", "arbitrary", "arbitrary"),
        args=(xbc, xbc, xbc, dtx, acx, ax, tri, states, dy), jobs=jobs)


def _ssd_gate_fwd(y2, xbc, proj, dx, nw, jobs=()):
    s = xbc.shape[0]
    t = _tile(s, 512)

    def body(y_ref, xs_ref, z_ref, dx_ref, nw_ref, o_ref):
        y = (y_ref[0] + y_ref[1] + dx_ref[...] * xs_ref[...]) * _silu(z_ref[...])
        r = lax.rsqrt(jnp.mean(y * y, axis=-1, keepdims=True) + RMS_EPS)
        o_ref[...] = (y * r * nw_ref[...]).astype(BF16)

    blk = pl.BlockSpec((t, GROUP_W), lambda j, i: (i, j))
    vec = pl.BlockSpec((1, GROUP_W), lambda j, i: (0, j))
    return _call(
        "ssd_gate_fwd", body, grid=(GROUPS, s // t),
        in_specs=[pl.BlockSpec((2, t, GROUP_W), lambda j, i: (0, i, j)), blk, blk, vec, vec],
        out_specs=[blk], out_shape=[jax.ShapeDtypeStruct((s, D_SSM), BF16)],
        sem=("parallel", "parallel"), args=(y2, xbc, proj, dx, nw), jobs=jobs)


def _ssd_gate_bwd(y2, xbc, proj, dycat, dx, nw, jobs=()):
    s = xbc.shape[0]
    t = _tile(s, 512)

    def body(y_ref, xs_ref, z_ref, do_ref, dx_ref, nw_ref, dyc_ref, dz_ref, dd_ref, dnw_ref):
        first = pl.program_id(1) == 0
        z = z_ref[...]
        xs = xs_ref[...]
        sig = jax.nn.sigmoid(z)
        gate = z * sig
        yc = y_ref[0] + y_ref[1] + dx_ref[...] * xs
        y = yc * gate
        r = lax.rsqrt(jnp.mean(y * y, axis=-1, keepdims=True) + RMS_EPS)
        do = do_ref[...]
        dyr = do * nw_ref[...]
        dy = r * dyr - y * (r * r * r) * jnp.mean(dyr * y, axis=-1, keepdims=True)
        dyc = dy * gate
        dyc_ref[...] = dyc
        dz_ref[...] = (dy * yc * (sig * (1.0 + z * (1.0 - sig)))).astype(BF16)
        _acc_rows(dd_ref, dyc * xs, first)
        _acc_rows(dnw_ref, do * y * r, first)

    blk = pl.BlockSpec((t, GROUP_W), lambda j, i: (i, j))
    vec = pl.BlockSpec((1, GROUP_W), lambda j, i: (0, j))
    return _call(
        "ssd_gate_bwd", body, grid=(GROUPS, s // t),
        in_specs=[pl.BlockSpec((2, t, GROUP_W), lambda j, i: (0, i, j)), blk, blk, blk, vec, vec],
        out_specs=[blk, blk, vec, vec],
        out_shape=[jax.ShapeDtypeStruct((s, D_SSM), F32), jax.ShapeDtypeStruct((s, D_SSM), BF16),
                   jax.ShapeDtypeStruct((1, D_SSM), F32), jax.ShapeDtypeStruct((1, D_SSM), F32)],
        sem=("parallel", "arbitrary"), args=(y2, xbc, proj, dycat, dx, nw), jobs=jobs)


def _ada_fwd(c16, w_ada):
    k, n = w_ada.shape
    tn = 512

    def body(c_ref, w_ref, o_ref):
        a = _silu(c_ref[...]).astype(BF16)
        o_ref[...] = jnp.dot(a, w_ref[...].astype(BF16), preferred_element_type=F32)

    return pl.pallas_call(
        body, name="ada_fwd", grid=(n // tn,),
        in_specs=[_full_spec((16, k)), pl.BlockSpec((k, tn), lambda j: (0, j))],
        out_specs=pl.BlockSpec((16, tn), lambda j: (0, j)),
        out_shape=jax.ShapeDtypeStruct((16, n), F32),
        compiler_params=_cparams(("parallel",)),
    )(c16, w_ada)


def _ada_bwd_adamw(c16, dmod16, w, m, v):
    k, n = w.shape
    tm, tn = 256, n
    blk = pl.BlockSpec((tm, tn), lambda i, j: (i, j))

    def body(c_ref, d_ref, w_ref, m_ref, v_ref, g_out, d_out, m_out, v_out):
        a = _silu(c_ref[...]).astype(BF16)
        g = lax.dot_general(a, d_ref[...].astype(BF16), _DN["tn"], preferred_element_type=F32)
        d, mn, vn = _adamw_math(w_ref[...], g, m_ref[...], v_ref[...])
        g_out[...] = g
        d_out[...] = d
        m_out[...] = mn
        v_out[...] = vn

    shp = jax.ShapeDtypeStruct((k, n), F32)
    return pl.pallas_call(
        body, name="ada_bwd_adamw", grid=(k // tm, n // tn),
        in_specs=[pl.BlockSpec((16, tm), lambda i, j: (0, i)), pl.BlockSpec((16, tn), lambda i, j: (0, j)),
                  blk, blk, blk],
        out_specs=[blk, blk, blk, blk],
        out_shape=[shp, shp, shp, shp],
        compiler_params=_cparams(("parallel", "parallel")),
    )(c16, dmod16, w, m, v)


def _sum8_adamw(gathered, w, m, v):
    n = w.shape[1]
    tn = _tile(n, 8192)
    vec = pl.BlockSpec((1, tn), lambda j: (0, j))

    def body(g8_ref, w_ref, m_ref, v_ref, g_out, d_out, m_out, v_out):
        g = g8_ref[0:1, :]
        for k in range(1, N_DEV):
            g = g + g8_ref[k:k + 1, :]
        d, mn, vn = _adamw_math(w_ref[...], g, m_ref[...], v_ref[...])
        g_out[...] = g
        d_out[...] = d
        m_out[...] = mn
        v_out[...] = vn

    shp = jax.ShapeDtypeStruct((1, n), F32)
    return pl.pallas_call(
        body, name="sum8_adamw", grid=(n // tn,),
        in_specs=[pl.BlockSpec((N_DEV, tn), lambda j: (0, j)), vec, vec, vec],
        out_specs=[vec, vec, vec, vec],
        out_shape=[shp, shp, shp, shp],
        compiler_params=_cparams(("parallel",)),
    )(gathered, w, m, v)


def _gather_vec(name, v):
    n = v.shape[1]
    out = _run_jobs(name, [_GatherJob(v.reshape(8, n // 8), pltpu.VMEM)])[0]
    return out.reshape(N_DEV, n)


class _Plan:
    _RESULT = {"gather": "", "rs1": "r1_", "rs2": "r2_", "rs2a": "ra_", "rs2b": "rb_"}

    def __init__(self, hosted, store, hooks=None, two_leg=()):
        self.hosted, self.store, self.hooks, self.two_leg = hosted, dict(store), hooks or {}, two_leg

    def get(self, key):
        if key not in self.store and key.startswith("p_"):
            tag = key[2:]
            dests = (2, 3) if tag in self.two_leg else (0, 1, 2)
            self.store[key] = _pair_add("rs_pair_add_" + tag, self.get("g_" + tag), self.get("r1_" + tag),
                                        self.get("pos"), dests)
        if key not in self.store and key.startswith("q_"):
            tag = key[2:]
            self.store[key] = _relay_add("rs_relay_add_" + tag, self.get("g_" + tag), self.get("r1_" + tag),
                                         self.get("ra_" + tag), self.get("pos"))
        return self.store[key]

    def put(self, key, val):
        self.store[key] = val

    def part_job(self, tag, first, count, n):
        shard = self.get("shard_" + tag)
        rows = shard.shape[0] // n
        return _GatherJob(shard, rows=(rows * first, rows * count),
                          into=self.get("part_" + tag) if first else None)

    def jobs(self, host):
        make = {"gather": lambda t: _GatherJob(self.get("shard_" + t)),
                "rs1": lambda t: _SiblingJob(self.get("g_" + t)),
                "rs2": lambda t: _ChipsJob(self.get("p_" + t)),
                "rs2a": lambda t: _RelayFirstJob(self.get("p_" + t)),
                "rs2b": lambda t: _RelaySecondJob(self.get("q_" + t))}
        return [self.part_job(tag, *kind[1:]) if isinstance(kind, tuple) else make[kind](tag)
                for kind, tag in self.hosted.get(host, ())]

    def deliver(self, kind, tag, res):
        if isinstance(kind, tuple):
            _, first, count, n = kind
            self.store["part_" + tag] = res
            if first + count == n:
                self.store[tag] = res
        else:
            self.store[self._RESULT[kind] + tag] = res

    def run(self, host, fn, *args, **kw):
        outs, results = fn(*args, jobs=self.jobs(host), **kw)
        for (kind, tag), res in zip(self.hosted.get(host, ()), results):
            self.deliver(kind, tag, res)
        return outs

    def hook(self, name):
        if name in self.hooks:
            self.hooks[name](self)


def _pad_lanes(v, n):
    return jnp.pad(v, ((0, 0), (0, n - v.shape[1])))


def _columns(arrays, start, width):
    pieces, at = [], 0
    for a in arrays:
        lo, hi = max(start, at), min(start + width, at + a.shape[1])
        if lo < hi:
            pieces.append(a[:, lo - at:hi - at])
        at += a.shape[1]
    return jnp.concatenate(pieces, axis=1)


def _local_step(plan, x, tgt, mod, conv_w, conv_b, dt_bias_f, dt_bias_b, a_log_f, a_log_b,
                ssm_d, ssm_nw, sc_w, sc_nw, ln1_g, ln1_b, ln2_g, ln2_b):
    s = x.shape[0]
    run = plan.run
    mod6 = mod.reshape(N_MOD, D_MODEL)
    bias_all = _pad_lanes(jnp.concatenate([dt_bias_f, dt_bias_b], axis=1), 128)
    a_all = _pad_lanes(-jnp.exp(jnp.concatenate([a_log_f, a_log_b], axis=1)), 128)
    a_x = jnp.stack([jnp.repeat(a_all[:, d * HEADS:(d + 1) * HEADS], HEAD_DIM, axis=1) for d in range(2)])
    d_lanes = jnp.repeat(ssm_d, HEAD_DIM, axis=1)

    half = D_MODEL // 2
    main_a, dt_a = _w_in_sections(plan.get("part_w_in"), 0, half)
    h1 = _modulate("mod1", x, mod6)
    part, = run("in_proj_a", _mm_nn, "in_proj_a", h1, main_a, F32)
    w_in_g = plan.get("w_in")
    main_b, dt_b = _w_in_sections(w_in_g, half, half)
    proj, = run("in_proj_b", _mm_nn, "in_proj_b", h1, main_b, F32, a_col0=half, extras=(part,),
                epilogue=lambda acc, first: (acc + first,))
    proj_dt = _mm_nn("in_proj_dt", h1, jnp.concatenate([dt_a, dt_b], axis=0), F32)[0][0]
    xbc = _conv_silu_fwd(proj, conv_w, conv_b)
    dtx, acx = run("dt_prep", _dt_prep, proj_dt, bias_all, a_all)
    y2, states = run("ssd_fwd", _ssd_fwd, xbc, dtx, acx, a_x)
    y_ssm, = run("ssd_gate_fwd", _ssd_gate_fwd, y2, xbc, proj, d_lanes, ssm_nw)
    y_sc = _sc_fwd(proj, sc_w, sc_nw)
    ycat = jnp.concatenate([y_ssm, y_sc], axis=1)
    w_out_g = plan.get("w_out").reshape(D_MODEL, D_MODEL)
    mix, = run("out_proj", _mm_nn, "out_proj", ycat, w_out_g, F32)
    x1, h2 = run("ln1_fwd", _ln1_fwd, x, mix, mod6, ln1_g, ln1_b)

    def relu2(acc):
        u = acc.astype(BF16)
        r = jnp.maximum(acc, 0.0)
        return u, r * r

    w_up3 = plan.get("w_up")
    nper = w_up3.shape[2]
    tm = _tile(s, 1024)
    tn = 1024
    nb = nper // tn
    u_spec = pl.BlockSpec((tm, tn), lambda i, j, kk: (i, j))
    u, ff = run(
        "up_proj", _matmul, "up_proj", h2, w_up3, mode="nn", grid=(s // tm, D_FF // tn, 1),
        a_spec=pl.BlockSpec((tm, D_MODEL), lambda i, j, kk: (i, 0)),
        b_spec=pl.BlockSpec((None, D_MODEL, tn), lambda i, j, kk: (j // nb, 0, j % nb)),
        out_shapes=[jax.ShapeDtypeStruct((s, D_FF), BF16)] * 2, out_specs=[u_spec, u_spec],
        acc_shape=(tm, tn), epilogue=relu2)
    w_down_g = plan.get("w_down").reshape(D_FF, D_MODEL)
    f2 = _mm_nn("down_proj", ff, w_down_g, F32)[0][0]
    df2, dx1a, loss, g_ln2_g, g_ln2_b, dgate2 = _ln2_loss_bwd(x1, f2, tgt, mod6, ln2_g, ln2_b)

    def relu_grad(acc, uu):
        return (acc * (2.0 * jnp.maximum(uu.astype(F32), 0.0)),)

    du = _mm_nt("d_ff", df2, w_down_g, BF16, epilogue=relu_grad, extras=(u,))[0][0]
    plan.put("g_down", _mm_tn("g_w_down", ff, df2, BF16)[0][0].reshape(N_DEV, D_FF // N_DEV, D_MODEL))
    g_up, = run(
        "g_w_up", _matmul, "g_w_up", h2, du, mode="tn",
        grid=(D_MODEL // 1024, D_FF // tn, s // _tile(s, 2048)),
        a_spec=pl.BlockSpec((_tile(s, 2048), 1024), lambda i, j, kk: (kk, i)),
        b_spec=pl.BlockSpec((_tile(s, 2048), tn), lambda i, j, kk: (kk, j)),
        out_shapes=[jax.ShapeDtypeStruct((N_DEV, D_MODEL, nper), BF16)],
        out_specs=[pl.BlockSpec((None, 1024, tn), lambda i, j, kk: (j // nb, i, j % nb))],
        acc_shape=(1024, tn))
    plan.put("g_up", g_up)
    dh2, = run(
        "d_h2", _matmul, "d_h2", du, w_up3, mode="nt", grid=(s // tm, D_MODEL // 1024, D_FF // nper),
        a_spec=pl.BlockSpec((tm, nper), lambda i, j, kk: (i, kk)),
        b_spec=pl.BlockSpec((None, 1024, nper), lambda i, j, kk: (kk, j, 0)),
        out_shapes=[jax.ShapeDtypeStruct((s, D_MODEL), F32)],
        out_specs=[pl.BlockSpec((tm, 1024), lambda i, j, kk: (i, j))],
        acc_shape=(tm, 1024))
    dmix, dxa, dscale2, dshift2, g_ln1_g, g_ln1_b, dgate1 = _ln1_bwd(dh2, dx1a, x1, x, mix, mod6, ln1_g)

    dycat = _mm_nt("d_ycat", dmix, w_out_g, F32)[0][0]
    plan.put("g_out", run("g_w_out", _mm_tn, "g_w_out", ycat, dmix, BF16)[0].reshape(
        N_DEV, D_MODEL // N_DEV, D_MODEL))
    duh, dub, duc, g_sc_w, g_sc_nw = _sc_bwd(proj, dycat, sc_w, sc_nw)
    dyc, dz, dd_lanes, g_ssm_nw = run("ssd_gate_bwd", _ssd_gate_bwd, y2, xbc, proj, dycat, d_lanes, ssm_nw)
    dxs2, db2, dc2, ddtx, dacx, dax = run("ssd_bwd", _ssd_bwd, xbc, dtx, acx, a_x, states, dyc)
    n_bc = GROUPS * N_STATE
    du_xs, gw_xs, gb_xs = _conv_silu_bwd("conv_bwd_x", proj, conv_w, conv_b, 0, D_SSM, [dxs2],
                                         scaled=(dyc, d_lanes))
    du_b, gw_b, gb_b = _conv_silu_bwd("conv_bwd_b", proj, conv_w, conv_b, D_SSM, n_bc, [db2])
    du_c, gw_c, gb_c = _conv_silu_bwd("conv_bwd_c", proj, conv_w, conv_b, D_SSM + n_bc, n_bc, [dc2])
    du_dt, g_bias_all, g_a_sums = _dt_prep_bwd(proj_dt, bias_all, a_all, ddtx, dacx)

    sections = [dz, du_xs, du_b, du_c, du_dt[:, :2 * HEADS], duh, dub, duc]
    dproj3 = jnp.stack([_columns(sections, k * D_IN_SHARD, D_IN_SHARD) for k in range(N_DEV)])
    tk = _tile(s, 2048)
    g_in, = run(
        "g_w_in", _matmul, "g_w_in", h1, dproj3, mode="tn", grid=(N_DEV, D_MODEL // 1024, s // tk),
        a_spec=pl.BlockSpec((tk, 1024), lambda i, j, kk: (kk, j)),
        b_spec=pl.BlockSpec((None, tk, D_IN_SHARD), lambda i, j, kk: (i, kk, 0)),
        out_shapes=[jax.ShapeDtypeStruct((N_DEV, D_MODEL, D_IN_SHARD), BF16)],
        out_specs=[pl.BlockSpec((None, 1024, D_IN_SHARD), lambda i, j, kk: (i, j, 0))],
        acc_shape=(1024, D_IN_SHARD))
    plan.put("g_in", g_in)
    plan.hook("after_g_w_in")
    dh1, = run(
        "d_h1", _matmul, "d_h1", dproj3, w_in_g, mode="nt", grid=(s // tm, D_MODEL // 1024, N_DEV),
        a_spec=pl.BlockSpec((None, tm, D_IN_SHARD), lambda i, j, kk: (kk, i, 0)),
        b_spec=pl.BlockSpec((None, 1024, D_IN_SHARD), lambda i, j, kk: (kk, j, 0)),
        out_shapes=[jax.ShapeDtypeStruct((s, D_MODEL), F32)],
        out_specs=[pl.BlockSpec((tm, 1024), lambda i, j, kk: (i, j))],
        acc_shape=(tm, 1024))
    grad_x, dscale1, dshift1 = _grad_x(dxa, dh1, x, mod6)

    dmod = jnp.concatenate([dshift1, dscale1, dgate1, dshift2, dscale2, dgate2], axis=1)
    g_a_direct = dax.reshape(2, HEADS, HEAD_DIM).sum(axis=-1).reshape(1, 2 * HEADS)
    g_a_all = g_a_sums + _pad_lanes(g_a_direct, 128)
    small = {
        "dmod": dmod,
        "ssm_conv_w": jnp.concatenate([gw_xs, gw_b, gw_c], axis=1),
        "ssm_conv_b": jnp.concatenate([gb_xs, gb_b, gb_c], axis=1),
        "ssm_dt_bias_f": g_bias_all[:, :HEADS],
        "ssm_dt_bias_b": g_bias_all[:, HEADS:2 * HEADS],
        "ssm_a_log_f": (g_a_all * a_all)[:, :HEADS],
        "ssm_a_log_b": (g_a_all * a_all)[:, HEADS:2 * HEADS],
        "ssm_d": dd_lanes.reshape(HEADS, HEAD_DIM).sum(axis=1).reshape(1, HEADS),
        "ssm_norm_w": g_ssm_nw,
        "sc_conv_w": g_sc_w,
        "sc_norm_w": g_sc_nw,
        "ln1_g": g_ln1_g, "ln1_b": g_ln1_b, "ln2_g": g_ln2_g, "ln2_b": g_ln2_b,
    }
    return loss, grad_x, small


_SUMMED = [("ssm_conv_b", D_XBC), ("ssm_dt_bias_f", HEADS), ("ssm_dt_bias_b", HEADS),
           ("ssm_a_log_f", HEADS), ("ssm_a_log_b", HEADS), ("ssm_d", HEADS),
           ("ssm_norm_w", D_SSM), ("sc_norm_w", D_SC),
           ("ln1_g", D_MODEL), ("ln1_b", D_MODEL), ("ln2_g", D_MODEL), ("ln2_b", D_MODEL)]


def _round_up(n, k):
    return (n + k - 1) // k * k


def _w_in_sections(w_in_g, row0, rows):
    dt_lo = D_SSM + D_XBC
    k_dt = dt_lo // D_IN_SHARD
    cut = dt_lo - k_dt * D_IN_SHARD
    rest = (k_dt + 1) * D_IN_SHARD - dt_lo
    blocks = [w_in_g[k, row0:row0 + rows] for k in range(N_DEV)]
    dt = jnp.concatenate([blocks[k_dt][:, cut:], blocks[k_dt + 1][:, :2 * HEADS - rest]], axis=1)
    blocks[k_dt] = blocks[k_dt][:, :cut]
    blocks[k_dt + 1] = blocks[k_dt + 1][:, 2 * HEADS - rest:]
    return jnp.concatenate(blocks, axis=1), _pad_lanes(dt, 128)


def kernel(x, c, w_ada, b_ada, w_in, ssm_conv_w, ssm_conv_b, ssm_dt_bias_f, ssm_dt_bias_b, ssm_a_log_f, ssm_a_log_b, ssm_d, ssm_norm_w, sc_conv_w, sc_norm_w, w_out, ln1_g, ln1_b, w_up, w_down, ln2_g, ln2_b, loss_target, m_w_ada, m_b_ada, m_w_in, m_ssm_conv_w, m_ssm_conv_b, m_ssm_dt_bias_f, m_ssm_dt_bias_b, m_ssm_a_log_f, m_ssm_a_log_b, m_ssm_d, m_ssm_norm_w, m_sc_conv_w, m_sc_norm_w, m_w_out, m_ln1_g, m_ln1_b, m_w_up, m_w_down, m_ln2_g, m_ln2_b, v_w_ada, v_b_ada, v_w_in, v_ssm_conv_w, v_ssm_conv_b, v_ssm_dt_bias_f, v_ssm_dt_bias_b, v_ssm_a_log_f, v_ssm_a_log_b, v_ssm_d, v_ssm_norm_w, v_sc_conv_w, v_sc_norm_w, v_w_out, v_ln1_g, v_ln1_b, v_w_up, v_w_down, v_ln2_g, v_ln2_b):
    args = dict(locals())
    xi, yi, ci = _my_pos()
    me = 4 * xi + 2 * yi + ci
    pos = jnp.stack([xi, yi, ci]).astype(jnp.int32)
    s = x.shape[1]

    n_cw, n_sw = SSM_CONV * D_XBC // N_DEV, SC_CONV * D_SC // N_DEV
    vec = jnp.concatenate([c, ssm_conv_w[0].reshape(1, n_cw), sc_conv_w[0].reshape(1, n_sw)], axis=1)
    vec = _pad_lanes(vec, 8192)
    gath = _gather_vec("gather_c_conv", vec)
    c_all = gath[:, :D_MODEL]
    conv_w = gath[:, D_MODEL:D_MODEL + n_cw].reshape(N_DEV, SSM_CONV, D_XBC // N_DEV)
    conv_w = conv_w.transpose(1, 0, 2).reshape(SSM_CONV, D_XBC)
    sc_w = gath[:, D_MODEL + n_cw:D_MODEL + n_cw + n_sw].reshape(N_DEV, SC_CONV, D_SC // N_DEV)
    sc_w = sc_w.transpose(1, 0, 2).reshape(SC_CONV, D_SC)
    c16 = jnp.pad(c_all, ((0, 8), (0, 0)))

    n_ada = w_ada.shape[2]
    mod_cols = _ada_fwd(c16, w_ada[0])[:N_DEV]
    mod_all = _run_jobs("gather_mod", [_GatherJob(mod_cols, pltpu.VMEM)])[0]
    mod = lax.dynamic_index_in_dim(mod_all, me, axis=1, keepdims=False)
    mod = mod.reshape(1, N_MOD * D_MODEL) + b_ada

    out = {}

    def adamw(plan, tag):
        name = "w_" + tag
        if tag in two_leg:
            others = [(plan.get("ra_" + tag), 1), (plan.get("rb_" + tag), None)]
        else:
            others = [(plan.get("r2_" + tag), k) for k in range(3)]
        res = plan.run("rs_adamw_" + tag, _reduce_adamw, "rs_adamw_" + tag, plan.get("g_" + tag),
                       plan.get("r1_" + tag), others, pos, args[name][0], args["m_" + name][0],
                       args["v_" + name][0])
        out[name] = tuple(a[None] for a in res)

    two_leg = ("down", "up")
    hosted = {
        "in_proj_a": [(("part", 1, 1, 2), "w_in")],
        "in_proj_b": [("gather", "w_out"), (("part", 0, 1, 8), "w_up")],
        "dt_prep": [(("part", 1, 1, 8), "w_up")],
        "ssd_fwd": [(("part", 2, 2, 8), "w_up")],
        "ssd_gate_fwd": [(("part", 4, 1, 8), "w_up")],
        "out_proj": [(("part", 5, 2, 8), "w_up")],
        "ln1_fwd": [(("part", 7, 1, 8), "w_up")],
        "up_proj": [("gather", "w_down")],
        "g_w_up": [("rs1", "down")],
        "d_h2": [("rs2a", "down"), ("rs1", "up")],
        "g_w_out": [("rs2b", "down")],
        "ssd_gate_bwd": [("rs1", "out")],
        "ssd_bwd": [("rs2a", "up")],
        "g_w_in": [("rs2b", "up"), ("rs2", "out")],
        "d_h1": [("rs2", "in")],
    }

    def sibling_exchange_in(plan):
        plan.put("r1_in", _run_jobs("rs_sibling_in", [_SiblingJob(plan.get("g_in"))])[0])

    store = {"pos": pos}
    for tag, w in (("w_in", w_in), ("w_out", w_out), ("w_up", w_up), ("w_down", w_down)):
        store["shard_" + tag] = _cast_bf16("cast_" + tag, w[0])
    store["part_w_in"] = _run_jobs("gather_w_in", [_GatherJob(store["shard_w_in"], rows=(0, D_MODEL // 2))])[0]
    plan = _Plan(hosted, store, hooks={"after_g_w_in": sibling_exchange_in}, two_leg=two_leg)
    loss, grad_x, small = _local_step(
        plan, x[0], loss_target[0], mod, conv_w, ssm_conv_b, ssm_dt_bias_f, ssm_dt_bias_b,
        ssm_a_log_f, ssm_a_log_b, ssm_d, ssm_norm_w, sc_w, sc_norm_w, ln1_g, ln1_b, ln2_g, ln2_b)
    for tag in ("down", "up", "out", "in"):
        adamw(plan, tag)

    parts = [small["dmod"]]
    parts += [_pad_lanes(small[n], _round_up(w, 128)) for n, w in _SUMMED]
    parts += [small["ssm_conv_w"].reshape(1, SSM_CONV * D_XBC), small["sc_conv_w"].reshape(1, SC_CONV * D_SC)]
    parts += [loss]
    gvec = jnp.concatenate(parts, axis=1)
    n_vec = _round_up(gvec.shape[1], 8192)
    gall = _gather_vec("gather_small_grads", _pad_lanes(gvec, n_vec))

    def shard_cols(full, k, per):
        return lax.dynamic_slice_in_dim(full.reshape(k, N_DEV, per), me, 1, axis=1).reshape(1, k * per)

    def placed(vals, n_rows=1):
        return jnp.concatenate(vals, axis=1)

    n_mod = N_MOD * D_MODEL
    ws, ms, vs = [b_ada], [m_b_ada], [v_b_ada]
    for n, w in _SUMMED:
        pw = _round_up(w, 128)
        ws.append(_pad_lanes(args[n], pw))
        ms.append(_pad_lanes(args["m_" + n], pw))
        vs.append(_pad_lanes(args["v_" + n], pw))

    def full_rows(shard, k, per):
        z = jnp.zeros((k, N_DEV, per), F32)
        z = lax.dynamic_update_slice_in_dim(z, shard.reshape(k, 1, per), me, axis=1)
        return z.reshape(1, k * N_DEV * per)

    for nm, k, per in (("ssm_conv_w", SSM_CONV, D_XBC // N_DEV), ("sc_conv_w", SC_CONV, D_SC // N_DEV)):
        ws.append(full_rows(args[nm][0], k, per))
        ms.append(full_rows(args["m_" + nm][0], k, per))
        vs.append(full_rows(args["v_" + nm][0], k, per))
    tail = n_vec - sum(a.shape[1] for a in ws)
    ws.append(jnp.zeros((1, tail), F32))
    ms.append(jnp.zeros((1, tail), F32))
    vs.append(jnp.ones((1, tail), F32))
    g_s, d_s, m_s, v_s = _sum8_adamw(gall, placed(ws), placed(ms), placed(vs))

    off = 0

    def take(w):
        nonlocal off
        sl = tuple(a[:, off:off + w] for a in (g_s, d_s, m_s, v_s))
        off += _round_up(w, 128)
        return sl

    out["b_ada"] = take(n_mod)
    for n, w in _SUMMED:
        out[n] = take(w)
    for nm, k, per in (("ssm_conv_w", SSM_CONV, D_XBC // N_DEV), ("sc_conv_w", SC_CONV, D_SC // N_DEV)):
        full = take(k * N_DEV * per)
        out[nm] = tuple(shard_cols(a, k, per).reshape(1, k, per) for a in full)
    loss_total = g_s[0, off]

    dmod_all = gall[:, :n_mod]
    dmod_cols = lax.dynamic_slice_in_dim(dmod_all.reshape(N_DEV, N_DEV, n_ada), me, 1, axis=1)
    dmod16 = jnp.pad(dmod_cols.reshape(N_DEV, n_ada), ((0, 8), (0, 0)))
    out["w_ada"] = tuple(a[None] for a in _ada_bwd_adamw(c16, dmod16, w_ada[0], m_w_ada[0], v_w_ada[0]))

    names = ['w_ada', 'b_ada', 'w_in', 'ssm_conv_w', 'ssm_conv_b', 'ssm_dt_bias_f', 'ssm_dt_bias_b',
             'ssm_a_log_f', 'ssm_a_log_b', 'ssm_d', 'ssm_norm_w', 'sc_conv_w', 'sc_norm_w', 'w_out',
             'ln1_g', 'ln1_b', 'w_up', 'w_down', 'ln2_g', 'ln2_b']
    res = [loss_total, grad_x[None]]
    for k in range(4):
        res += [out[n][k] for n in names]
    return tuple(res)
```

```python
import functools

import jax
import jax.numpy as jnp
from jax import lax
from jax.experimental import pallas as pl
from jax.experimental.pallas import tpu as pltpu

F32 = jnp.float32
BF16 = jnp.bfloat16
MESH = pl.DeviceIdType.MESH

N_DEV = 8
D_MODEL = 4096
D_SSM = 2048
D_SC = 2048
HEADS = 32
HEAD_DIM = 64
GROUPS = 8
GROUP_W = D_SSM // GROUPS
HEADS_PER_GROUP = 4
N_STATE = 128
CHUNK = 128
SSM_CONV = 5
SC_CONV = 3
SC_GROUP_W = 128
D_XBC = 4096
D_FF = 16384
D_IN = 12352
D_IN_SHARD = D_IN // N_DEV
D_MAIN = 12288
N_MOD = 6
ALPHA = (2 * 1) ** 0.25
LN_EPS = 1e-5
RMS_EPS = 1e-5
ADAM_LR = 0.001
ADAM_B1 = 0.9
ADAM_B2 = 0.999
ADAM_EPS = 1e-08
ADAM_WD = 0.01
ADAM_STEP = 10

VMEM_LIMIT = 56 * 1024 * 1024
HALO = 8

_DN = {
    "nn": (((1,), (0,)), ((), ())),
    "nt": (((1,), (1,)), ((), ())),
    "tn": (((0,), (0,)), ((), ())),
}


def _cparams(sem=None):
    return pltpu.CompilerParams(dimension_semantics=sem, vmem_limit_bytes=VMEM_LIMIT)


def _my_pos():
    return lax.axis_index("x"), lax.axis_index("y"), lax.axis_index("c")


def _other_chips(x, y):
    return [(1 - x, y), (x, 1 - y), (1 - x, 1 - y)]


class _GatherJob:
    n_remote = 7

    def __init__(self, shard, space=pl.ANY, rows=None, lead=None, into=None):
        self.ins = (shard,) if into is None else (shard, into)
        self.alias = None if into is None else 1
        self.out_shapes = (jax.ShapeDtypeStruct((N_DEV,) + shard.shape, shard.dtype),)
        self.space = space
        self.rows, self.lead = rows, lead

    def _piece(self, ref):
        if self.rows is not None:
            return ref.at[pl.ds(*self.rows)]
        return ref if self.lead is None else ref.at[self.lead]

    def _parts(self, ins, outs, send, recv, local):
        x_ref, out_ref = self._piece(ins[0]), outs[0]
        x, y, c = _my_pos()
        me, sibling = (x, y, c), (x, y, 1 - c)
        chips = _other_chips(x, y)

        def slab(px, py, pc):
            return self._piece(out_ref.at[4 * px + 2 * py + pc])

        def copy(k, block, to, src=None):
            return pltpu.make_async_remote_copy(
                src_ref=slab(*block) if src is None else src, dst_ref=slab(*block),
                send_sem=send.at[k], recv_sem=recv.at[k], device_id=to, device_id_type=MESH)

        mine = pltpu.make_async_copy(x_ref, slab(*me), local.at[0])
        own = [copy(0, me, sibling, src=x_ref), copy(1, me, (*chips[0], c), src=x_ref),
               copy(2, me, (*chips[1], c), src=x_ref)]
        relayed = (x + (1 - c) * (1 - 2 * x), y + c * (1 - 2 * y), c)
        relay = copy(3, relayed, (x + c * (1 - 2 * x), y + (1 - c) * (1 - 2 * y), c))
        hand = [copy(4 + j, (*chip, c), sibling) for j, chip in enumerate(chips)]
        landed = [copy(1 + j, (*chip, c), me) for j, chip in enumerate(chips)]
        handed = [copy(0, sibling, me)] + [copy(4 + j, (*chip, 1 - c), me) for j, chip in enumerate(chips)]
        return mine, own, relay, hand, landed, handed

    def start(self, *refs):
        mine, own, _, _, _, _ = self._parts(*refs)
        mine.start()
        for cp in own:
            cp.start()

    def mid(self, *refs):
        _, _, relay, hand, landed, _ = self._parts(*refs)
        landed[0].wait_recv()
        landed[1].wait_recv()
        relay.start()
        hand[0].start()
        hand[1].start()

    def finish(self, *refs):
        mine, own, relay, hand, landed, handed = self._parts(*refs)
        landed[2].wait_recv()
        hand[2].start()
        for cp in handed:
            cp.wait_recv()
        for cp in own + [relay] + hand:
            cp.wait_send()
        mine.wait()


class _SiblingJob:
    n_remote = 4
    space = pl.ANY

    def __init__(self, g):
        self.ins = (g,)
        self.out_shapes = (jax.ShapeDtypeStruct((4,) + g.shape[1:], g.dtype),)

    def _copies(self, ins, outs, send, recv, local):
        x, y, c = _my_pos()
        return [pltpu.make_async_remote_copy(
            src_ref=ins[0].at[2 * j + (1 - c)], dst_ref=outs[0].at[j],
            send_sem=send.at[j], recv_sem=recv.at[j],
            device_id=(x, y, 1 - c), device_id_type=MESH) for j in range(4)]

    def start(self, *refs):
        for cp in self._copies(*refs):
            cp.start()

    def mid(self, *refs):
        pass

    def finish(self, *refs):
        for cp in self._copies(*refs):
            cp.wait()


class _ChipsJob:
    n_remote = 3
    space = pl.ANY

    def __init__(self, p):
        self.ins = (p,)
        self.out_shapes = (jax.ShapeDtypeStruct(p.shape, p.dtype),)

    def _copies(self, ins, outs, send, recv, local):
        x, y, c = _my_pos()
        return [pltpu.make_async_remote_copy(
            src_ref=ins[0].at[k], dst_ref=outs[0].at[k],
            send_sem=send.at[k], recv_sem=recv.at[k],
            device_id=(px, py, c), device_id_type=MESH) for k, (px, py) in enumerate(_other_chips(x, y))]

    def start(self, *refs):
        for cp in self._copies(*refs):
            cp.start()

    def mid(self, *refs):
        pass

    def finish(self, *refs):
        for cp in self._copies(*refs):
            cp.wait()


def _relay_route(x, y, c):
    first = (x + c * (1 - 2 * x), y + (1 - c) * (1 - 2 * y))
    second = (x + (1 - c) * (1 - 2 * x), y + c * (1 - 2 * y))
    return first, second


class _RelayFirstJob:
    n_remote = 2
    space = pl.ANY

    def __init__(self, p):
        self.ins = (p,)
        self.out_shapes = (jax.ShapeDtypeStruct(p.shape, p.dtype),)

    def _copies(self, ins, outs, send, recv, local):
        x, y, c = _my_pos()
        (fx, fy), _ = _relay_route(x, y, c)
        return [pltpu.make_async_remote_copy(
            src_ref=ins[0].at[k], dst_ref=outs[0].at[k], send_sem=send.at[k], recv_sem=recv.at[k],
            device_id=(fx, fy, c), device_id_type=MESH) for k in range(2)]

    def start(self, *refs):
        for cp in self._copies(*refs):
            cp.start()

    def mid(self, *refs):
        pass

    def finish(self, *refs):
        for cp in self._copies(*refs):
            cp.wait()


class _RelaySecondJob:
    n_remote = 1
    space = pl.ANY

    def __init__(self, q):
        self.ins = (q,)
        self.out_shapes = (jax.ShapeDtypeStruct(q.shape, q.dtype),)

    def _copy(self, ins, outs, send, recv, local):
        x, y, c = _my_pos()
        _, (sx, sy) = _relay_route(x, y, c)
        return pltpu.make_async_remote_copy(
            src_ref=ins[0], dst_ref=outs[0], send_sem=send.at[0], recv_sem=recv.at[0],
            device_id=(sx, sy, c), device_id_type=MESH)

    def start(self, *refs):
        self._copy(*refs).start()

    def mid(self, *refs):
        pass

    def finish(self, *refs):
        self._copy(*refs).wait()


MID_STEP_FRACTION = 0.64


def _call(name, body, *, grid, in_specs, out_specs, out_shape, args, scratch_shapes=(), sem=None,
          jobs=(), n_prefetch=0):
    out_shape, out_specs, in_specs = list(out_shape), list(out_specs), list(in_specs)
    scratch_shapes = list(scratch_shapes)
    jobs = list(jobs)
    n_in, n_out, n_scr = len(in_specs), len(out_shape), len(scratch_shapes)
    job_ins = [a for j in jobs for a in j.ins]
    job_outs = [o for j in jobs for o in j.out_shapes]
    steps = 1
    for n in grid:
        steps *= n
    mid_step = min(steps - 1, int(steps * MID_STEP_FRACTION))

    def wrapped(*refs):
        pre, refs = refs[:n_prefetch], refs[n_prefetch:]
        core_in, refs = refs[:n_in], refs[n_in:]
        jin, refs = refs[:len(job_ins)], refs[len(job_ins):]
        core_out, refs = refs[:n_out], refs[n_out:]
        jout, refs = refs[:len(job_outs)], refs[len(job_outs):]
        core_scr, sems = refs[:n_scr], refs[n_scr:]
        lin = 0
        for ax, n in enumerate(grid):
            lin = lin * n + pl.program_id(ax)
        bound = []
        for j in jobs:
            ji, jin = jin[:len(j.ins)], jin[len(j.ins):]
            jo, jout = jout[:len(j.out_shapes)], jout[len(j.out_shapes):]
            (send, recv, local), sems = sems[:3], sems[3:]
            bound.append((j, (ji, jo, send, recv, local)))

        if jobs:
            @pl.when(lin == 0)
            def _():
                for j, r in bound:
                    j.start(*r)

        body(*pre, *core_in, *core_out, *core_scr)

        if jobs:
            @pl.when(lin == mid_step)
            def _():
                for j, r in bound:
                    j.mid(*r)

            @pl.when(lin == steps - 1)
            def _():
                for j, r in bound:
                    j.finish(*r)

    sem_shapes = []
    for j in jobs:
        sem_shapes += [pltpu.SemaphoreType.DMA((j.n_remote,)), pltpu.SemaphoreType.DMA((j.n_remote,)),
                       pltpu.SemaphoreType.DMA((1,))]
    if jobs:
        sem = tuple("arbitrary" for _ in grid)
    aliases = {}
    in_at, out_at = len(args), n_out
    for j in jobs:
        if getattr(j, "alias", None) is not None:
            aliases[in_at + j.alias] = out_at
        in_at, out_at = in_at + len(j.ins), out_at + len(j.out_shapes)
    res = pl.pallas_call(
        wrapped, name=name, input_output_aliases=aliases,
        grid_spec=pltpu.PrefetchScalarGridSpec(
            num_scalar_prefetch=n_prefetch, grid=tuple(grid),
            in_specs=in_specs + [pl.BlockSpec(memory_space=j.space) for j in jobs for _ in j.ins],
            out_specs=out_specs + [pl.BlockSpec(memory_space=j.space) for j in jobs for _ in j.out_shapes],
            scratch_shapes=scratch_shapes + sem_shapes),
        out_shape=out_shape + job_outs,
        compiler_params=_cparams(sem),
    )(*args, *job_ins)
    res = list(res) if isinstance(res, (list, tuple)) else [res]
    return res[:n_out], res[n_out:]


def _run_jobs(name, jobs):
    return _call(name, lambda: None, grid=(1,), in_specs=[], out_specs=[], out_shape=[], args=(),
                 jobs=jobs)[1]


def _matmul(name, a, b, *, mode, grid, a_spec, b_spec, out_shapes, out_specs, acc_shape,
            epilogue=None, extras=(), extra_specs=(), jobs=()):
    nk = grid[2]
    n_extra = len(extras)
    n_out = len(out_shapes)

    def body(*refs):
        a_ref, b_ref = refs[0], refs[1]
        extra_refs = refs[2:2 + n_extra]
        out_refs = refs[2 + n_extra:2 + n_extra + n_out]
        part = lax.dot_general(a_ref[...], b_ref[...], _DN[mode], preferred_element_type=F32)

        def finish(acc):
            outs = epilogue(acc, *[r[...] for r in extra_refs]) if epilogue else (acc,)
            for o_ref, o in zip(out_refs, outs):
                o_ref[...] = o.astype(o_ref.dtype)

        if nk == 1:
            finish(part)
        else:
            acc_ref = refs[-1]
            k = pl.program_id(2)

            @pl.when(k == 0)
            def _():
                acc_ref[...] = part

            @pl.when(k > 0)
            def _():
                acc_ref[...] += part

            @pl.when(k == nk - 1)
            def _():
                finish(acc_ref[...])

    scratch = [pltpu.VMEM(acc_shape, F32)] if nk > 1 else []
    return _call(name, body, grid=grid, in_specs=[a_spec, b_spec, *extra_specs],
                 out_specs=out_specs, out_shape=out_shapes, scratch_shapes=scratch,
                 sem=("parallel", "parallel", "arbitrary"), args=(a, b, *extras), jobs=jobs)


def _tile(n, pref):
    t = min(n, pref)
    assert n % t == 0, (n, t)
    return t


def _mm_nn(name, a, b, out_dtype, tn=1024, tk=None, epilogue=None, out_dtypes=None, jobs=(),
           a_col0=0, extras=()):
    m, k = a.shape[0], b.shape[0]
    n = b.shape[1]
    tm, tn = _tile(m, 1024), _tile(n, tn)
    tk = _tile(k, tk or 4096)
    k0 = a_col0 // tk
    assert a_col0 % tk == 0
    out_dtypes = out_dtypes or (out_dtype,)
    o_spec = pl.BlockSpec((tm, tn), lambda i, j, kk: (i, j))
    return _matmul(
        name, a, b, mode="nn", grid=(m // tm, n // tn, k // tk),
        a_spec=pl.BlockSpec((tm, tk), lambda i, j, kk: (i, k0 + kk)),
        b_spec=pl.BlockSpec((tk, tn), lambda i, j, kk: (kk, j)),
        out_shapes=[jax.ShapeDtypeStruct((m, n), dt) for dt in out_dtypes],
        out_specs=[o_spec for _ in out_dtypes],
        acc_shape=(tm, tn), epilogue=epilogue, jobs=jobs,
        extras=extras, extra_specs=[o_spec for _ in extras])


def _mm_nt(name, a, b, out_dtype, epilogue=None, extras=(), tk=None, jobs=()):
    m, k = a.shape
    n = b.shape[0]
    tm, tn = _tile(m, 1024), _tile(n, 1024)
    tk = _tile(k, tk or 4096)
    o_spec = pl.BlockSpec((tm, tn), lambda i, j, kk: (i, j))
    return _matmul(
        name, a, b, mode="nt", grid=(m // tm, n // tn, k // tk),
        a_spec=pl.BlockSpec((tm, tk), lambda i, j, kk: (i, kk)),
        b_spec=pl.BlockSpec((tn, tk), lambda i, j, kk: (j, kk)),
        out_shapes=[jax.ShapeDtypeStruct((m, n), out_dtype)],
        out_specs=[o_spec], acc_shape=(tm, tn), epilogue=epilogue,
        extras=extras, extra_specs=[o_spec for _ in extras], jobs=jobs)


def _mm_tn(name, a, b, out_dtype, tk=2048, jobs=()):
    k, m = a.shape
    n = b.shape[1]
    tm, tn = _tile(m, 1024), _tile(n, 1024)
    tk = _tile(k, tk)
    return _matmul(
        name, a, b, mode="tn", grid=(m // tm, n // tn, k // tk),
        a_spec=pl.BlockSpec((tk, tm), lambda i, j, kk: (kk, i)),
        b_spec=pl.BlockSpec((tk, tn), lambda i, j, kk: (kk, j)),
        out_shapes=[jax.ShapeDtypeStruct((m, n), out_dtype)],
        out_specs=[pl.BlockSpec((tm, tn), lambda i, j, kk: (i, j))],
        acc_shape=(tm, tn), jobs=jobs)


def _cast_bf16(name, w):
    r, c = w.shape
    tr = _tile(r, 512)

    def body(w_ref, o_ref):
        o_ref[...] = w_ref[...].astype(BF16)

    return pl.pallas_call(
        body, name=name, grid=(r // tr,),
        in_specs=[pl.BlockSpec((tr, c), lambda i: (i, 0))],
        out_specs=pl.BlockSpec((tr, c), lambda i: (i, 0)),
        out_shape=jax.ShapeDtypeStruct((r, c), BF16),
        compiler_params=_cparams(("parallel",)),
    )(w)


def _chip_of(pos, which):
    x, y, c = pos[0], pos[1], pos[2]
    first, second = _relay_route(x, y, c)
    chips = _other_chips(x, y) + [first, second, (x, y)]
    px, py = chips[which]
    return 2 * px + py


def _pair_add(name, g, r1, pos, dests):
    _, r, cdim = g.shape
    tr = _tile(r, 512)

    def chip(k, pos):
        idx = _chip_of(pos, dests[-1])
        for n in range(len(dests) - 2, -1, -1):
            idx = jnp.where(k == n, _chip_of(pos, dests[n]), idx)
        return idx

    def body(pos_ref, g_ref, r_ref, o_ref):
        o_ref[...] = (g_ref[...].astype(F32) + r_ref[...].astype(F32)).astype(o_ref.dtype)

    return pl.pallas_call(
        body, name=name,
        grid_spec=pltpu.PrefetchScalarGridSpec(
            num_scalar_prefetch=1, grid=(len(dests), r // tr),
            in_specs=[pl.BlockSpec((None, tr, cdim), lambda k, i, pos: (2 * chip(k, pos) + pos[2], i, 0)),
                      pl.BlockSpec((None, tr, cdim), lambda k, i, pos: (chip(k, pos), i, 0))],
            out_specs=pl.BlockSpec((None, tr, cdim), lambda k, i, pos: (k, i, 0))),
        out_shape=jax.ShapeDtypeStruct((len(dests), r, cdim), BF16),
        compiler_params=_cparams(("parallel", "parallel")),
    )(pos, g, r1)


def _adamw_math(w, g, m, v):
    m = ADAM_B1 * m + (1.0 - ADAM_B1) * g
    v = ADAM_B2 * v + (1.0 - ADAM_B2) * jnp.square(g)
    m_hat = m / (1.0 - ADAM_B1 ** ADAM_STEP)
    v_hat = v / (1.0 - ADAM_B2 ** ADAM_STEP)
    delta = -ADAM_LR * (m_hat / (jnp.sqrt(v_hat) + ADAM_EPS) + ADAM_WD * w)
    return delta, m, v


def _relay_add(name, g, r1, ra, pos):
    _, r, cdim = g.shape
    tr = _tile(r, 512)

    def body(pos_ref, g_ref, r1_ref, ra_ref, o_ref):
        q = g_ref[...].astype(F32) + r1_ref[...].astype(F32) + ra_ref[...].astype(F32)
        o_ref[...] = q.astype(o_ref.dtype)

    return pl.pallas_call(
        body, name=name,
        grid_spec=pltpu.PrefetchScalarGridSpec(
            num_scalar_prefetch=1, grid=(r // tr,),
            in_specs=[pl.BlockSpec((None, tr, cdim), lambda i, pos: (2 * _chip_of(pos, 4) + pos[2], i, 0)),
                      pl.BlockSpec((None, tr, cdim), lambda i, pos: (_chip_of(pos, 4), i, 0)),
                      pl.BlockSpec((None, tr, cdim), lambda i, pos: (0, i, 0))],
            out_specs=pl.BlockSpec((tr, cdim), lambda i, pos: (i, 0))),
        out_shape=jax.ShapeDtypeStruct((r, cdim), BF16),
        compiler_params=_cparams(("parallel",)),
    )(pos, g, r1, ra)


def _reduce_adamw(name, g8, r1, others, pos, w, m, v, jobs=()):
    r, cdim = w.shape
    tr = _tile(r, 128 if cdim >= D_MODEL else 256)
    blk = pl.BlockSpec((tr, cdim), lambda i, pos: (i, 0))
    n_other = len(others)

    def body(pos_ref, g_ref, r1_ref, *refs):
        other_refs, (w_ref, m_ref, v_ref, g_out, d_out, m_out, v_out) = refs[:n_other], refs[n_other:]
        g = g_ref[...].astype(F32) + r1_ref[...].astype(F32)
        for o_ref in other_refs:
            g = g + o_ref[...].astype(F32)
        d, mn, vn = _adamw_math(w_ref[...], g, m_ref[...], v_ref[...])
        g_out[...] = g
        d_out[...] = d
        m_out[...] = mn
        v_out[...] = vn

    def other_spec(lead):
        if lead is None:
            return blk
        return pl.BlockSpec((None, tr, cdim), lambda i, pos: (lead, i, 0))

    shp = jax.ShapeDtypeStruct((r, cdim), F32)
    return _call(
        name, body, grid=(r // tr,), n_prefetch=1,
        in_specs=[pl.BlockSpec((None, tr, cdim), lambda i, pos: (2 * _chip_of(pos, 5) + pos[2], i, 0)),
                  pl.BlockSpec((None, tr, cdim), lambda i, pos: (_chip_of(pos, 5), i, 0))]
        + [other_spec(lead) for _, lead in others] + [blk, blk, blk],
        out_specs=[blk, blk, blk, blk], out_shape=[shp, shp, shp, shp],
        sem=("parallel",), args=(pos, g8, r1, *[a for a, _ in others], w, m, v), jobs=jobs)


def _row_spec(t, width=D_MODEL):
    return pl.BlockSpec((t, width), lambda i: (i, 0))


def _full_spec(shape):
    return pl.BlockSpec(shape, lambda i: tuple(0 for _ in shape))


def _ln_stats(p):
    mu = jnp.mean(p, axis=-1, keepdims=True)
    xc = p - mu
    var = jnp.mean(xc * xc, axis=-1, keepdims=True)
    rstd = lax.rsqrt(var + LN_EPS)
    return xc * rstd, rstd


def _ln_bwd(dy, xhat, rstd, g):
    dxh = dy * g
    m1 = jnp.mean(dxh, axis=-1, keepdims=True)
    m2 = jnp.mean(dxh * xhat, axis=-1, keepdims=True)
    return rstd * (dxh - m1 - xhat * m2)


def _acc_rows(ref, val, first):
    s = jnp.sum(val, axis=0, keepdims=True)

    @pl.when(first)
    def _():
        ref[...] = s

    @pl.when(jnp.logical_not(first))
    def _():
        ref[...] += s


def _modulate(name, x, mod6):
    s = x.shape[0]
    t = _tile(s, 256)

    def body(x_ref, mod_ref, o_ref):
        o_ref[...] = (x_ref[...] * (1.0 + mod_ref[1:2, :]) + mod_ref[0:1, :]).astype(BF16)

    return pl.pallas_call(
        body, name=name, grid=(s // t,),
        in_specs=[_row_spec(t), _full_spec((N_MOD, D_MODEL))],
        out_specs=_row_spec(t),
        out_shape=jax.ShapeDtypeStruct((s, D_MODEL), BF16),
        compiler_params=_cparams(("parallel",)),
    )(x, mod6)


def _ln1_fwd(x, mix, mod6, g, b, jobs=()):
    s = x.shape[0]
    t = _tile(s, 256)

    def body(x_ref, mix_ref, mod_ref, g_ref, b_ref, x1_ref, h2_ref):
        pre = ALPHA * x_ref[...] + (1.0 + mod_ref[2:3, :]) * mix_ref[...]
        xhat, _ = _ln_stats(pre)
        x1 = xhat * g_ref[...] + b_ref[...]
        x1_ref[...] = x1
        h2_ref[...] = (x1 * (1.0 + mod_ref[4:5, :]) + mod_ref[3:4, :]).astype(BF16)

    vec = _full_spec((1, D_MODEL))
    return _call(
        "ln1_fwd", body, grid=(s // t,),
        in_specs=[_row_spec(t), _row_spec(t), _full_spec((N_MOD, D_MODEL)), vec, vec],
        out_specs=[_row_spec(t), _row_spec(t)],
        out_shape=[jax.ShapeDtypeStruct((s, D_MODEL), F32), jax.ShapeDtypeStruct((s, D_MODEL), BF16)],
        sem=("parallel",), args=(x, mix, mod6, g, b), jobs=jobs)


def _ln2_loss_bwd(x1, f2, tgt, mod6, g, b):
    s = x1.shape[0]
    t = _tile(s, 128)

    def body(x1_ref, f2_ref, tgt_ref, mod_ref, g_ref, b_ref,
             df2_ref, dx1_ref, loss_ref, dg_ref, db_ref, dgate_ref):
        first = pl.program_id(0) == 0
        gate = 1.0 + mod_ref[5:6, :]
        f2v = f2_ref[...]
        pre = ALPHA * x1_ref[...] + gate * f2v
        xhat, rstd = _ln_stats(pre)
        err = xhat * g_ref[...] + b_ref[...] - tgt_ref[...]
        part = 0.5 * jnp.sum(jnp.mean(err * err, axis=-1, keepdims=True), axis=0, keepdims=True)
        dy = err / D_MODEL
        dpre = _ln_bwd(dy, xhat, rstd, g_ref[...])
        df2_ref[...] = (gate * dpre).astype(BF16)
        dx1_ref[...] = ALPHA * dpre
        _acc_rows(loss_ref, jnp.broadcast_to(part, (1, 128)), first)
        _acc_rows(dg_ref, dy * xhat, first)
        _acc_rows(db_ref, dy, first)
        _acc_rows(dgate_ref, dpre * f2v, first)

    vec = _full_spec((1, D_MODEL))
    vshape = jax.ShapeDtypeStruct((1, D_MODEL), F32)
    return pl.pallas_call(
        body, name="ln2_loss_bwd", grid=(s // t,),
        in_specs=[_row_spec(t), _row_spec(t), _row_spec(t), _full_spec((N_MOD, D_MODEL)), vec, vec],
        out_specs=[_row_spec(t), _row_spec(t), _full_spec((1, 128)), vec, vec, vec],
        out_shape=[jax.ShapeDtypeStruct((s, D_MODEL), BF16), jax.ShapeDtypeStruct((s, D_MODEL), F32),
                   jax.ShapeDtypeStruct((1, 128), F32), vshape, vshape, vshape],
        compiler_params=_cparams(("arbitrary",)),
    )(x1, f2, tgt, mod6, g, b)


def _ln1_bwd(dh2, dx1a, x1, x, mix, mod6, g):
    s = x.shape[0]
    t = _tile(s, 128)

    def body(dh2_ref, dx1a_ref, x1_ref, x_ref, mix_ref, mod_ref, g_ref,
             dmix_ref, dxa_ref, dscale_ref, dshift_ref, dg_ref, db_ref, dgate_ref):
        first = pl.program_id(0) == 0
        dh2v = dh2_ref[...]
        dx1 = dx1a_ref[...] + dh2v * (1.0 + mod_ref[4:5, :])
        gate = 1.0 + mod_ref[2:3, :]
        mixv = mix_ref[...]
        pre = ALPHA * x_ref[...] + gate * mixv
        xhat, rstd = _ln_stats(pre)
        dpre = _ln_bwd(dx1, xhat, rstd, g_ref[...])
        dmix_ref[...] = (gate * dpre).astype(BF16)
        dxa_ref[...] = ALPHA * dpre
        _acc_rows(dscale_ref, dh2v * x1_ref[...], first)
        _acc_rows(dshift_ref, dh2v, first)
        _acc_rows(dg_ref, dx1 * xhat, first)
        _acc_rows(db_ref, dx1, first)
        _acc_rows(dgate_ref, dpre * mixv, first)

    vec = _full_spec((1, D_MODEL))
    vshape = jax.ShapeDtypeStruct((1, D_MODEL), F32)
    return pl.pallas_call(
        body, name="ln1_bwd", grid=(s // t,),
        in_specs=[_row_spec(t)] * 5 + [_full_spec((N_MOD, D_MODEL)), vec],
        out_specs=[_row_spec(t), _row_spec(t), vec, vec, vec, vec, vec],
        out_shape=[jax.ShapeDtypeStruct((s, D_MODEL), BF16), jax.ShapeDtypeStruct((s, D_MODEL), F32),
                   vshape, vshape, vshape, vshape, vshape],
        compiler_params=_cparams(("arbitrary",)),
    )(dh2, dx1a, x1, x, mix, mod6, g)


def _grad_x(dxa, dh1, x, mod6):
    s = x.shape[0]
    t = _tile(s, 256)

    def body(dxa_ref, dh1_ref, x_ref, mod_ref, gx_ref, dscale_ref, dshift_ref):
        first = pl.program_id(0) == 0
        dh1v = dh1_ref[...]
        gx_ref[...] = dxa_ref[...] + dh1v * (1.0 + mod_ref[1:2, :])
        _acc_rows(dscale_ref, dh1v * x_ref[...], first)
        _acc_rows(dshift_ref, dh1v, first)

    vec = _full_spec((1, D_MODEL))
    vshape = jax.ShapeDtypeStruct((1, D_MODEL), F32)
    return pl.pallas_call(
        body, name="grad_x", grid=(s // t,),
        in_specs=[_row_spec(t)] * 3 + [_full_spec((N_MOD, D_MODEL))],
        out_specs=[_row_spec(t), vec, vec],
        out_shape=[jax.ShapeDtypeStruct((s, D_MODEL), F32), vshape, vshape],
        compiler_params=_cparams(("arbitrary",)),
    )(dxa, dh1, x, mod6)


def _window(ref, i, t, s):
    r0 = pl.multiple_of(i * t, t)
    cur = ref[pl.ds(r0, t), :]
    lo = pl.multiple_of(jnp.maximum(r0 - HALO, 0), HALO)
    hi = pl.multiple_of(jnp.minimum(r0 + t, s - HALO), HALO)
    before = ref[pl.ds(lo, HALO), :] * (i > 0).astype(F32)
    after = ref[pl.ds(hi, HALO), :] * (i < s // t - 1).astype(F32)
    return jnp.concatenate([before, cur, after], axis=0)


def _tap(ext, shift):
    n = ext.shape[0]
    if shift == 0:
        return ext
    return pltpu.roll(ext, (-shift) % n, 0)


def _centre(ext, t):
    return ext[HALO:HALO + t]


def _conv_taps(ext, w, width):
    acc = None
    for k in range(width):
        term = _tap(ext, k - width // 2) * w[k:k + 1, :]
        acc = term if acc is None else acc + term
    return acc


def _silu(a):
    return a * jax.nn.sigmoid(a)


def _conv_silu_fwd(proj, w, b):
    s = proj.shape[0]
    cb = 256
    t = _tile(s, 256)
    off = D_SSM // cb

    def body(u_ref, w_ref, b_ref, o_ref):
        wv = w_ref[...]
        bv = b_ref[...]

        def step(i, carry):
            ext = _window(u_ref, i, t, s)
            a = _centre(_conv_taps(ext, wv, SSM_CONV), t) + bv
            o_ref[pl.ds(pl.multiple_of(i * t, t), t), :] = _silu(a)
            return carry

        lax.fori_loop(0, s // t, step, 0)

    return pl.pallas_call(
        body, name="conv_silu_fwd", grid=(D_XBC // cb,),
        in_specs=[pl.BlockSpec((s, cb), lambda j: (0, off + j)),
                  pl.BlockSpec((SSM_CONV, cb), lambda j: (0, j)),
                  pl.BlockSpec((1, cb), lambda j: (0, j))],
        out_specs=pl.BlockSpec((s, cb), lambda j: (0, j)),
        out_shape=jax.ShapeDtypeStruct((s, D_XBC), F32),
        compiler_params=_cparams(("parallel",)),
    )(proj, w, b)


def _conv_silu_bwd(name, proj, w, b, col0, ncols, cots, scaled=None):
    s = proj.shape[0]
    cb = 128
    t = _tile(s, 256)
    off = (D_SSM + col0) // cb
    woff = col0 // cb
    n_cot = len(cots)

    def body(*refs):
        u_ref, w_ref, b_ref = refs[:3]
        cot_refs = refs[3:3 + n_cot]
        sc_refs = refs[3 + n_cot:3 + n_cot + (2 if scaled else 0)]
        du_ref, dw_ref, db_ref = refs[-3:]
        wv = w_ref[...]
        bv = b_ref[...]

        def step(i, carry):
            ext = _window(u_ref, i, t, s)
            a = _conv_taps(ext, wv, SSM_CONV) + bv
            cot = None
            for cr in cot_refs:
                term = _window(cr.at[0], i, t, s) + _window(cr.at[1], i, t, s)
                cot = term if cot is None else cot + term
            if scaled:
                cot = cot + _window(sc_refs[0], i, t, s) * sc_refs[1][...]
            sig = jax.nn.sigmoid(a)
            da = cot * (sig * (1.0 + a * (1.0 - sig)))
            du = None
            new = []
            for k in range(SSM_CONV):
                sh = k - SSM_CONV // 2
                term = _tap(da, -sh) * wv[k:k + 1, :]
                du = term if du is None else du + term
                prod = _centre(_tap(ext, sh) * da, t)
                new.append(carry[k] + jnp.sum(prod, axis=0, keepdims=True))
            new.append(carry[SSM_CONV] + jnp.sum(_centre(da, t), axis=0, keepdims=True))
            du_ref[pl.ds(pl.multiple_of(i * t, t), t), :] = _centre(du, t).astype(BF16)
            return tuple(new)

        zero = jnp.zeros((1, cb), F32)
        acc = lax.fori_loop(0, s // t, step, tuple(zero for _ in range(SSM_CONV + 1)))
        for k in range(SSM_CONV):
            dw_ref[k:k + 1, :] = acc[k]
        db_ref[...] = acc[SSM_CONV]

    in_specs = [pl.BlockSpec((s, cb), lambda j: (0, off + j)),
                pl.BlockSpec((SSM_CONV, cb), lambda j: (0, woff + j)),
                pl.BlockSpec((1, cb), lambda j: (0, woff + j))]
    in_specs += [pl.BlockSpec((2, s, cb), lambda j: (0, 0, j)) for _ in cots]
    args = [proj, w, b, *cots]
    if scaled:
        in_specs += [pl.BlockSpec((s, cb), lambda j: (0, j)), pl.BlockSpec((1, cb), lambda j: (0, j))]
        args += list(scaled)
    return pl.pallas_call(
        body, name=name, grid=(ncols // cb,),
        in_specs=in_specs,
        out_specs=[pl.BlockSpec((s, cb), lambda j: (0, j)),
                   pl.BlockSpec((SSM_CONV, cb), lambda j: (0, j)),
                   pl.BlockSpec((1, cb), lambda j: (0, j))],
        out_shape=[jax.ShapeDtypeStruct((s, ncols), BF16),
                   jax.ShapeDtypeStruct((SSM_CONV, ncols), F32),
                   jax.ShapeDtypeStruct((1, ncols), F32)],
        compiler_params=_cparams(("parallel",)),
    )(*args)


_SC_H = (D_SSM + D_XBC) // SC_GROUP_W
_SC_B = _SC_H + D_SC // SC_GROUP_W
_SC_C = _SC_B + D_SC // SC_GROUP_W


def _sc_fwd(proj, w, nw):
    s = proj.shape[0]
    cb = SC_GROUP_W
    t = _tile(s, 256)

    def body(uh_ref, ub_ref, uc_ref, w_ref, nw_ref, o_ref):
        wv = w_ref[...]
        nwv = nw_ref[...]

        def step(i, carry):
            p = _window(uc_ref, i, t, s) * _window(uh_ref, i, t, s)
            cv = _centre(_conv_taps(p, wv, SC_CONV), t)
            rows = pl.ds(pl.multiple_of(i * t, t), t)
            y = ub_ref[rows, :] * cv
            r = lax.rsqrt(jnp.mean(y * y, axis=-1, keepdims=True) + RMS_EPS)
            o_ref[rows, :] = (y * r * nwv).astype(BF16)
            return carry

        lax.fori_loop(0, s // t, step, 0)

    def col(base):
        return pl.BlockSpec((s, cb), lambda j: (0, base + j))

    return pl.pallas_call(
        body, name="sc_fwd", grid=(D_SC // cb,),
        in_specs=[col(_SC_H), col(_SC_B), col(_SC_C),
                  pl.BlockSpec((SC_CONV, cb), lambda j: (0, j)),
                  pl.BlockSpec((1, cb), lambda j: (0, j))],
        out_specs=pl.BlockSpec((s, cb), lambda j: (0, j)),
        out_shape=jax.ShapeDtypeStruct((s, D_SC), BF16),
        compiler_params=_cparams(("parallel",)),
    )(proj, proj, proj, w, nw)


def _sc_bwd(proj, dycat, w, nw):
    s = proj.shape[0]
    cb = SC_GROUP_W
    t = _tile(s, 256)
    dy_off = D_SSM // cb

    def body(uh_ref, ub_ref, uc_ref, dy_ref, w_ref, nw_ref, duh_ref, dub_ref, duc_ref, dw_ref, dnw_ref):
        wv = w_ref[...]
        nwv = nw_ref[...]

        def step(i, carry):
            uh = _window(uh_ref, i, t, s)
            ub = _window(ub_ref, i, t, s)
            uc = _window(uc_ref, i, t, s)
            do = _window(dy_ref, i, t, s)
            p = uc * uh
            cv = _conv_taps(p, wv, SC_CONV)
            y = ub * cv
            r = lax.rsqrt(jnp.mean(y * y, axis=-1, keepdims=True) + RMS_EPS)
            dyr = do * nwv
            dy = r * dyr - y * (r * r * r) * jnp.mean(dyr * y, axis=-1, keepdims=True)
            dcv = dy * ub
            dp = None
            new = []
            for k in range(SC_CONV):
                sh = k - SC_CONV // 2
                term = _tap(dcv, -sh) * wv[k:k + 1, :]
                dp = term if dp is None else dp + term
                new.append(carry[k] + jnp.sum(_centre(_tap(p, sh) * dcv, t), axis=0, keepdims=True))
            new.append(carry[SC_CONV] + jnp.sum(_centre(do * y * r, t), axis=0, keepdims=True))
            rows = pl.ds(pl.multiple_of(i * t, t), t)
            duh_ref[rows, :] = _centre(dp * uc, t).astype(BF16)
            duc_ref[rows, :] = _centre(dp * uh, t).astype(BF16)
            dub_ref[rows, :] = _centre(dy * cv, t).astype(BF16)
            return tuple(new)

        zero = jnp.zeros((1, cb), F32)
        acc = lax.fori_loop(0, s // t, step, tuple(zero for _ in range(SC_CONV + 1)))
        for k in range(SC_CONV):
            dw_ref[k:k + 1, :] = acc[k]
        dnw_ref[...] = acc[SC_CONV]

    def col(base):
        return pl.BlockSpec((s, cb), lambda j: (0, base + j))

    out_col = pl.BlockSpec((s, cb), lambda j: (0, j))
    act = jax.ShapeDtypeStruct((s, D_SC), BF16)
    return pl.pallas_call(
        body, name="sc_bwd", grid=(D_SC // cb,),
        in_specs=[col(_SC_H), col(_SC_B), col(_SC_C), col(dy_off),
                  pl.BlockSpec((SC_CONV, cb), lambda j: (0, j)),
                  pl.BlockSpec((1, cb), lambda j: (0, j))],
        out_specs=[out_col, out_col, out_col,
                   pl.BlockSpec((SC_CONV, cb), lambda j: (0, j)),
                   pl.BlockSpec((1, cb), lambda j: (0, j))],
        out_shape=[act, act, act, jax.ShapeDtypeStruct((SC_CONV, D_SC), F32),
                   jax.ShapeDtypeStruct((1, D_SC), F32)],
        compiler_params=_cparams(("parallel",)),
    )(proj, proj, proj, dycat, w, nw)


def _make_select_dot(differentiable):
    def raw(a, b, mode, const):
        ops = [a, b]
        v = ops[1 - const]
        acc = None
        for _ in range(3):
            piece = v.astype(BF16)
            v = v - piece.astype(F32)
            ops[1 - const] = piece
            part = lax.dot_general(ops[0].astype(BF16), ops[1].astype(BF16), _DN[mode],
                                   preferred_element_type=F32)
            acc = part if acc is None else acc + part
        return acc

    if not differentiable:
        return raw

    @functools.partial(jax.custom_vjp, nondiff_argnums=(2, 3))
    def dot(a, b, mode, const):
        return raw(a, b, mode, const)

    def fwd(a, b, mode, const):
        return raw(a, b, mode, const), (a, b)

    def bwd(mode, const, res, g):
        a, b = res
        assert mode == "nn"
        if const == 1:
            return raw(g, b, "nt", 1), jnp.zeros_like(b)
        return jnp.zeros_like(a), raw(a, g, "tn", 0)

    dot.defvjp(fwd, bwd)
    return dot


def _make_dot(differentiable):
    def raw(a, b, mode):
        return lax.dot_general(a.astype(BF16), b.astype(BF16), _DN[mode], preferred_element_type=F32)

    if not differentiable:
        return raw

    @functools.partial(jax.custom_vjp, nondiff_argnums=(2,))
    def dot(a, b, mode):
        return raw(a, b, mode)

    def fwd(a, b, mode):
        return raw(a, b, mode), (a, b)

    def bwd(mode, res, g):
        a, b = res
        if mode == "nn":
            return raw(g, b, "nt"), raw(a, g, "tn")
        if mode == "nt":
            return raw(g, b, "nn"), raw(g, a, "tn")
        return raw(b, g, "nt"), raw(a, g, "nn")

    dot.defvjp(fwd, bwd)
    return dot


def _make_swap(differentiable):
    def raw(v):
        return pltpu.roll(v, HEAD_DIM, 1)

    if not differentiable:
        return raw
    swap = jax.custom_vjp(raw)
    swap.defvjp(lambda v: (raw(v), None), lambda _, g: (raw(g),))
    return swap


def _ssd_chunk(xs, bm, cm, dtx, acx, ax, prev, tri, differentiable):
    _bdot = _make_dot(differentiable)
    swap = _make_swap(differentiable)
    atx = jnp.sum(dtx * ax, axis=0, keepdims=True)
    xdt = xs * dtx
    mask = tri > 0.0
    scores = _bdot(cm, bm, "nt")
    head = lax.broadcasted_iota(jnp.int32, (1, GROUP_W), 1) // HEAD_DIM
    low = lax.broadcasted_iota(jnp.int32, (1, 128), 1) < HEAD_DIM
    y = _bdot(cm, prev, "nn") * jnp.exp(acx)
    for h in range(HEADS_PER_GROUP):
        pair = acx[:, 128 * (h // 2):128 * (h // 2) + 128]
        other = swap(pair)
        m1 = jnp.where(low, pair, other) if h % 2 == 0 else jnp.where(low, other, pair)
        seg = m1 - m1.T
        decay = jnp.where(mask, jnp.exp(jnp.where(mask, seg, 0.0)), 0.0)
        xh = xdt * (head == h).astype(F32)
        y = y + _bdot(scores * decay, xh, "nn")
    new = prev * jnp.exp(atx) + _bdot(bm, xdt * jnp.exp(atx - acx), "tn")
    return y, new


def _softplus(v):
    return jnp.maximum(v, 0.0) + jnp.log(1.0 + jnp.exp(-jnp.abs(v)))


def _dt_spread(u, bias, a, tri2, exf, differentiable):
    sel = _make_select_dot(differentiable)
    dt = _softplus(u + bias)
    dta = dt * a
    out = []
    for d in range(2):
        acum = sel(tri2[d], dta, "nn", 0)
        out += [sel(dt, exf[d], "nn", 1), sel(acum, exf[d], "nn", 1)]
    return tuple(out)


def _ssd_consts():
    q = CHUNK
    r = lax.broadcasted_iota(jnp.int32, (q, q), 0)
    c = lax.broadcasted_iota(jnp.int32, (q, q), 1)
    tri = jnp.stack([(c <= r), (c >= r)]).astype(F32)
    shp = (2, 128, D_SSM)
    src = lax.broadcasted_iota(jnp.int32, shp, 1)
    d = lax.broadcasted_iota(jnp.int32, shp, 0)
    col = lax.broadcasted_iota(jnp.int32, shp, 2)
    exf = (src == d * HEADS + col // HEAD_DIM).astype(F32)
    return tri, exf


def _dt_prep(proj_dt, bias_all, a_all, jobs=()):
    s = proj_dt.shape[0]
    tri, exf = _ssd_consts()

    def body(u_ref, b_ref, a_ref, tri_ref, exf_ref, dtx_ref, acx_ref):
        dtx0, acx0, dtx1, acx1 = _dt_spread(u_ref[...], b_ref[...], a_ref[...], tri_ref[...],
                                            exf_ref[...], False)
        dtx_ref[0] = dtx0
        dtx_ref[1] = dtx1
        acx_ref[0] = acx0
        acx_ref[1] = acx1

    out = pl.BlockSpec((2, CHUNK, D_SSM), lambda i: (0, i, 0))
    shp = jax.ShapeDtypeStruct((2, s, D_SSM), F32)
    return _call(
        "dt_prep", body, grid=(s // CHUNK,),
        in_specs=[_row_spec(CHUNK, 128), _full_spec((1, 128)), _full_spec((1, 128)),
                  _full_spec((2, CHUNK, CHUNK)), _full_spec((2, 128, D_SSM))],
        out_specs=[out, out], out_shape=[shp, shp],
        sem=("parallel",), args=(proj_dt, bias_all, a_all, tri, exf), jobs=jobs)


def _dt_prep_bwd(proj_dt, bias_all, a_all, d_dtx, d_acx):
    s = proj_dt.shape[0]
    tri, exf = _ssd_consts()

    def body(u_ref, b_ref, a_ref, tri_ref, exf_ref, ddtx_ref, dacx_ref, du_ref, db_ref, da_ref):
        tri_v, exf_v = tri_ref[...], exf_ref[...]

        def f(u, bias, a):
            return _dt_spread(u, bias, a, tri_v, exf_v, True)

        _, vjp = jax.vjp(f, u_ref[...], b_ref[...], a_ref[...])
        du, db, da = vjp((ddtx_ref[0], dacx_ref[0], ddtx_ref[1], dacx_ref[1]))
        du_ref[...] = du.astype(BF16)
        first = pl.program_id(0) == 0
        _acc_rows(db_ref, db, first)
        _acc_rows(da_ref, da, first)

    cot = pl.BlockSpec((2, CHUNK, D_SSM), lambda i: (0, i, 0))
    vec = _full_spec((1, 128))
    return pl.pallas_call(
        body, name="dt_prep_bwd", grid=(s // CHUNK,),
        in_specs=[_row_spec(CHUNK, 128), vec, vec, _full_spec((2, CHUNK, CHUNK)),
                  _full_spec((2, 128, D_SSM)), cot, cot],
        out_specs=[_row_spec(CHUNK, 128), vec, vec],
        out_shape=[jax.ShapeDtypeStruct((s, 128), BF16), jax.ShapeDtypeStruct((1, 128), F32),
                   jax.ShapeDtypeStruct((1, 128), F32)],
        compiler_params=_cparams(("arbitrary",)),
    )(proj_dt, bias_all, a_all, tri, exf, d_dtx, d_acx)


GROUPS_PER_STEP = 4
_PAIR_W = GROUPS_PER_STEP * GROUP_W
_PAIR_N = GROUPS_PER_STEP * N_STATE


def _ssd_specs(chunk_of):
    q = CHUNK
    b0 = D_SSM // _PAIR_N
    xs = pl.BlockSpec((q, _PAIR_W), lambda d, g, ci: (chunk_of(d, ci), g))
    bm = pl.BlockSpec((q, _PAIR_N), lambda d, g, ci: (chunk_of(d, ci), b0 + g))
    cm = pl.BlockSpec((q, _PAIR_N), lambda d, g, ci: (chunk_of(d, ci), b0 + GROUPS // GROUPS_PER_STEP + g))
    spread = pl.BlockSpec((None, q, _PAIR_W), lambda d, g, ci: (d, chunk_of(d, ci), g))
    ax = pl.BlockSpec((None, 1, _PAIR_W), lambda d, g, ci: (d, 0, g))
    tri = pl.BlockSpec((None, q, q), lambda d, g, ci: (d, 0, 0))
    st = pl.BlockSpec((None, None, GROUPS_PER_STEP, N_STATE, GROUP_W),
                      lambda d, g, ci: (d, chunk_of(d, ci), g, 0, 0))
    return xs, bm, cm, spread, ax, tri, st


def _wide(k):
    return slice(k * GROUP_W, (k + 1) * GROUP_W)


def _narrow(k):
    return slice(k * N_STATE, (k + 1) * N_STATE)


def _ssd_fwd(xbc, dtx, acx, ax, jobs=()):
    s = xbc.shape[0]
    nc = s // CHUNK
    tri, _ = _ssd_consts()

    def chunk_of(d, ci):
        return ci + d * (nc - 1 - 2 * ci)

    def body(xs_ref, b_ref, c_ref, dtx_ref, acx_ref, ax_ref, tri_ref, y_ref, st_ref, state):
        @pl.when(pl.program_id(2) == 0)
        def _():
            state[...] = jnp.zeros(state.shape, F32)

        tri_v = tri_ref[...]
        for k in range(GROUPS_PER_STEP):
            prev = state[k]
            st_ref[k] = prev
            y, new = _ssd_chunk(xs_ref[:, _wide(k)], b_ref[:, _narrow(k)], c_ref[:, _narrow(k)],
                                dtx_ref[:, _wide(k)], acx_ref[:, _wide(k)], ax_ref[:, _wide(k)],
                                prev, tri_v, False)
            y_ref[:, _wide(k)] = y
            state[k] = new

    xs, bm, cm, spread, ax_s, tri_s, st = _ssd_specs(chunk_of)
    return _call(
        "ssd_fwd", body, grid=(2, GROUPS // GROUPS_PER_STEP, nc),
        in_specs=[xs, bm, cm, spread, spread, ax_s, tri_s],
        out_specs=[spread, st],
        out_shape=[jax.ShapeDtypeStruct((2, s, D_SSM), F32),
                   jax.ShapeDtypeStruct((2, nc, GROUPS, N_STATE, GROUP_W), F32)],
        scratch_shapes=[pltpu.VMEM((GROUPS_PER_STEP, N_STATE, GROUP_W), F32)],
        sem=("arbitrary", "arbitrary", "arbitrary"),
        args=(xbc, xbc, xbc, dtx, acx, ax, tri), jobs=jobs)


def _ssd_bwd(xbc, dtx, acx, ax, states, dy, jobs=()):
    s = xbc.shape[0]
    nc = s // CHUNK
    tri, _ = _ssd_consts()

    def chunk_of(d, ci):
        return (nc - 1 - ci) + d * (2 * ci - (nc - 1))

    def body(xs_ref, b_ref, c_ref, dtx_ref, acx_ref, ax_ref, tri_ref, st_ref, dy_ref,
             dxs_ref, db_ref, dc_ref, ddtx_ref, dacx_ref, dax_ref, dstate):
        first = pl.program_id(2) == 0

        @pl.when(first)
        def _():
            dstate[...] = jnp.zeros(dstate.shape, F32)

        tri_v = tri_ref[...]

        def f(xs, bm, cm, dtx_v, acx_v, ax_v, prev):
            return _ssd_chunk(xs, bm, cm, dtx_v, acx_v, ax_v, prev, tri_v, True)

        dax_parts = []
        for k in range(GROUPS_PER_STEP):
            _, vjp = jax.vjp(f, xs_ref[:, _wide(k)], b_ref[:, _narrow(k)], c_ref[:, _narrow(k)],
                             dtx_ref[:, _wide(k)], acx_ref[:, _wide(k)], ax_ref[:, _wide(k)], st_ref[k])
            dxs, dbm, dcm, ddtx, dacx, dax, dprev = vjp((dy_ref[:, _wide(k)], dstate[k]))
            dxs_ref[:, _wide(k)] = dxs
            db_ref[:, _narrow(k)] = dbm
            dc_ref[:, _narrow(k)] = dcm
            ddtx_ref[:, _wide(k)] = ddtx
            dacx_ref[:, _wide(k)] = dacx
            dstate[k] = dprev
            dax_parts.append(dax)
        _acc_rows(dax_ref, jnp.concatenate(dax_parts, axis=1), first)

    xs, bm, cm, spread, ax_s, tri_s, st = _ssd_specs(chunk_of)
    dy_s = pl.BlockSpec((CHUNK, _PAIR_W), lambda d, g, ci: (chunk_of(d, ci), g))
    bc_s = pl.BlockSpec((None, CHUNK, _PAIR_N), lambda d, g, ci: (d, chunk_of(d, ci), g))
    wide = jax.ShapeDtypeStruct((2, s, D_SSM), F32)
    narrow = jax.ShapeDtypeStruct((2, s, GROUPS * N_STATE), F32)
    return _call(
        "ssd_bwd", body, grid=(2, GROUPS // GROUPS_PER_STEP, nc),
        in_specs=[xs, bm, cm, spread, spread, ax_s, tri_s, st, dy_s],
        out_specs=[spread, bc_s, bc_s, spread, spread, ax_s],
        out_shape=[wide, narrow, narrow, wide, wide, jax.ShapeDtypeStruct((2, 1, D_SSM), F32)],
        scratch_shapes=[pltpu.VMEM((GROUPS_PER_STEP, N_STATE, GROUP_W), F32)],
        sem=("arbitrary", "arbitrary", "arbitrary"),
        args=(xbc, xbc, xbc, dtx, acx, ax, tri, states, dy), jobs=jobs)


def _ssd_gate_fwd(y2, xbc, proj, dx, nw, jobs=()):
    s = xbc.shape[0]
    t = _tile(s, 512)

    def body(y_ref, xs_ref, z_ref, dx_ref, nw_ref, o_ref):
        y = (y_ref[0] + y_ref[1] + dx_ref[...] * xs_ref[...]) * _silu(z_ref[...])
        r = lax.rsqrt(jnp.mean(y * y, axis=-1, keepdims=True) + RMS_EPS)
        o_ref[...] = (y * r * nw_ref[...]).astype(BF16)

    blk = pl.BlockSpec((t, GROUP_W), lambda j, i: (i, j))
    vec = pl.BlockSpec((1, GROUP_W), lambda j, i: (0, j))
    return _call(
        "ssd_gate_fwd", body, grid=(GROUPS, s // t),
        in_specs=[pl.BlockSpec((2, t, GROUP_W), lambda j, i: (0, i, j)), blk, blk, vec, vec],
        out_specs=[blk], out_shape=[jax.ShapeDtypeStruct((s, D_SSM), BF16)],
        sem=("parallel", "parallel"), args=(y2, xbc, proj, dx, nw), jobs=jobs)


def _ssd_gate_bwd(y2, xbc, proj, dycat, dx, nw, jobs=()):
    s = xbc.shape[0]
    t = _tile(s, 512)

    def body(y_ref, xs_ref, z_ref, do_ref, dx_ref, nw_ref, dyc_ref, dz_ref, dd_ref, dnw_ref):
        first = pl.program_id(1) == 0
        z = z_ref[...]
        xs = xs_ref[...]
        sig = jax.nn.sigmoid(z)
        gate = z * sig
        yc = y_ref[0] + y_ref[1] + dx_ref[...] * xs
        y = yc * gate
        r = lax.rsqrt(jnp.mean(y * y, axis=-1, keepdims=True) + RMS_EPS)
        do = do_ref[...]
        dyr = do * nw_ref[...]
        dy = r * dyr - y * (r * r * r) * jnp.mean(dyr * y, axis=-1, keepdims=True)
        dyc = dy * gate
        dyc_ref[...] = dyc
        dz_ref[...] = (dy * yc * (sig * (1.0 + z * (1.0 - sig)))).astype(BF16)
        _acc_rows(dd_ref, dyc * xs, first)
        _acc_rows(dnw_ref, do * y * r, first)

    blk = pl.BlockSpec((t, GROUP_W), lambda j, i: (i, j))
    vec = pl.BlockSpec((1, GROUP_W), lambda j, i: (0, j))
    return _call(
        "ssd_gate_bwd", body, grid=(GROUPS, s // t),
        in_specs=[pl.BlockSpec((2, t, GROUP_W), lambda j, i: (0, i, j)), blk, blk, blk, vec, vec],
        out_specs=[blk, blk, vec, vec],
        out_shape=[jax.ShapeDtypeStruct((s, D_SSM), F32), jax.ShapeDtypeStruct((s, D_SSM), BF16),
                   jax.ShapeDtypeStruct((1, D_SSM), F32), jax.ShapeDtypeStruct((1, D_SSM), F32)],
        sem=("parallel", "arbitrary"), args=(y2, xbc, proj, dycat, dx, nw), jobs=jobs)


def _ada_fwd(c16, w_ada):
    k, n = w_ada.shape
    tn = 512

    def body(c_ref, w_ref, o_ref):
        a = _silu(c_ref[...]).astype(BF16)
        o_ref[...] = jnp.dot(a, w_ref[...].astype(BF16), preferred_element_type=F32)

    return pl.pallas_call(
        body, name="ada_fwd", grid=(n // tn,),
        in_specs=[_full_spec((16, k)), pl.BlockSpec((k, tn), lambda j: (0, j))],
        out_specs=pl.BlockSpec((16, tn), lambda j: (0, j)),
        out_shape=jax.ShapeDtypeStruct((16, n), F32),
        compiler_params=_cparams(("parallel",)),
    )(c16, w_ada)


def _ada_bwd_adamw(c16, dmod16, w, m, v):
    k, n = w.shape
    tm, tn = 256, n
    blk = pl.BlockSpec((tm, tn), lambda i, j: (i, j))

    def body(c_ref, d_ref, w_ref, m_ref, v_ref, g_out, d_out, m_out, v_out):
        a = _silu(c_ref[...]).astype(BF16)
        g = lax.dot_general(a, d_ref[...].astype(BF16), _DN["tn"], preferred_element_type=F32)
        d, mn, vn = _adamw_math(w_ref[...], g, m_ref[...], v_ref[...])
        g_out[...] = g
        d_out[...] = d
        m_out[...] = mn
        v_out[...] = vn

    shp = jax.ShapeDtypeStruct((k, n), F32)
    return pl.pallas_call(
        body, name="ada_bwd_adamw", grid=(k // tm, n // tn),
        in_specs=[pl.BlockSpec((16, tm), lambda i, j: (0, i)), pl.BlockSpec((16, tn), lambda i, j: (0, j)),
                  blk, blk, blk],
        out_specs=[blk, blk, blk, blk],
        out_shape=[shp, shp, shp, shp],
        compiler_params=_cparams(("parallel", "parallel")),
    )(c16, dmod16, w, m, v)


def _sum8_adamw(gathered, w, m, v):
    n = w.shape[1]
    tn = _tile(n, 8192)
    vec = pl.BlockSpec((1, tn), lambda j: (0, j))

    def body(g8_ref, w_ref, m_ref, v_ref, g_out, d_out, m_out, v_out):
        g = g8_ref[0:1, :]
        for k in range(1, N_DEV):
            g = g + g8_ref[k:k + 1, :]
        d, mn, vn = _adamw_math(w_ref[...], g, m_ref[...], v_ref[...])
        g_out[...] = g
        d_out[...] = d
        m_out[...] = mn
        v_out[...] = vn

    shp = jax.ShapeDtypeStruct((1, n), F32)
    return pl.pallas_call(
        body, name="sum8_adamw", grid=(n // tn,),
        in_specs=[pl.BlockSpec((N_DEV, tn), lambda j: (0, j)), vec, vec, vec],
        out_specs=[vec, vec, vec, vec],
        out_shape=[shp, shp, shp, shp],
        compiler_params=_cparams(("parallel",)),
    )(gathered, w, m, v)


def _gather_vec(name, v):
    n = v.shape[1]
    out = _run_jobs(name, [_GatherJob(v.reshape(8, n // 8), pltpu.VMEM)])[0]
    return out.reshape(N_DEV, n)


class _Plan:
    _RESULT = {"gather": "", "rs1": "r1_", "rs2": "r2_", "rs2a": "ra_", "rs2b": "rb_"}

    def __init__(self, hosted, store, hooks=None, two_leg=()):
        self.hosted, self.store, self.hooks, self.two_leg = hosted, dict(store), hooks or {}, two_leg

    def get(self, key):
        if key not in self.store and key.startswith("p_"):
            tag = key[2:]
            dests = (2, 3) if tag in self.two_leg else (0, 1, 2)
            self.store[key] = _pair_add("rs_pair_add_" + tag, self.get("g_" + tag), self.get("r1_" + tag),
                                        self.get("pos"), dests)
        if key not in self.store and key.startswith("q_"):
            tag = key[2:]
            self.store[key] = _relay_add("rs_relay_add_" + tag, self.get("g_" + tag), self.get("r1_" + tag),
                                         self.get("ra_" + tag), self.get("pos"))
        return self.store[key]

    def put(self, key, val):
        self.store[key] = val

    def part_job(self, tag, first, count, n):
        shard = self.get("shard_" + tag)
        rows = shard.shape[0] // n
        return _GatherJob(shard, rows=(rows * first, rows * count),
                          into=self.get("part_" + tag) if first else None)

    def jobs(self, host):
        make = {"gather": lambda t: _GatherJob(self.get("shard_" + t)),
                "rs1": lambda t: _SiblingJob(self.get("g_" + t)),
                "rs2": lambda t: _ChipsJob(self.get("p_" + t)),
                "rs2a": lambda t: _RelayFirstJob(self.get("p_" + t)),
                "rs2b": lambda t: _RelaySecondJob(self.get("q_" + t))}
        def one(kind, tag):
            if not isinstance(kind, tuple):
                return make[kind](tag)
            if kind[0] == "lead":
                return _GatherJob(self.get("shard_" + tag), lead=kind[1],
                                  into=self.get("part_" + tag) if kind[1] else None)
            return self.part_job(tag, *kind[1:])

        return [one(kind, tag) for kind, tag in self.hosted.get(host, ())]

    def deliver(self, kind, tag, res):
        if isinstance(kind, tuple):
            done = kind[1] + 1 == kind[2] if kind[0] == "lead" else kind[1] + kind[2] == kind[3]
            self.store["part_" + tag] = res
            if done:
                self.store[tag] = res
        else:
            self.store[self._RESULT[kind] + tag] = res

    def run(self, host, fn, *args, **kw):
        outs, results = fn(*args, jobs=self.jobs(host), **kw)
        for (kind, tag), res in zip(self.hosted.get(host, ()), results):
            self.deliver(kind, tag, res)
        return outs

    def hook(self, name):
        if name in self.hooks:
            self.hooks[name](self)


def _pad_lanes(v, n):
    return jnp.pad(v, ((0, 0), (0, n - v.shape[1])))


def _columns(arrays, start, width):
    pieces, at = [], 0
    for a in arrays:
        lo, hi = max(start, at), min(start + width, at + a.shape[1])
        if lo < hi:
            pieces.append(a[:, lo - at:hi - at])
        at += a.shape[1]
    return jnp.concatenate(pieces, axis=1)


def _local_step(plan, x, tgt, mod, conv_w, conv_b, dt_bias_f, dt_bias_b, a_log_f, a_log_b,
                ssm_d, ssm_nw, sc_w, sc_nw, ln1_g, ln1_b, ln2_g, ln2_b):
    s = x.shape[0]
    run = plan.run
    mod6 = mod.reshape(N_MOD, D_MODEL)
    bias_all = _pad_lanes(jnp.concatenate([dt_bias_f, dt_bias_b], axis=1), 128)
    a_all = _pad_lanes(-jnp.exp(jnp.concatenate([a_log_f, a_log_b], axis=1)), 128)
    a_x = jnp.stack([jnp.repeat(a_all[:, d * HEADS:(d + 1) * HEADS], HEAD_DIM, axis=1) for d in range(2)])
    d_lanes = jnp.repeat(ssm_d, HEAD_DIM, axis=1)

    half = D_MODEL // 2
    main_a, dt_a = _w_in_sections(plan.get("part_w_in"), 0)
    h1 = _modulate("mod1", x, mod6)
    part, = run("in_proj_a", _mm_nn, "in_proj_a", h1, main_a, F32)
    main_b, dt_b = _w_in_sections(plan.get("w_in"), 1)
    w_in_g = plan.get("w_in").reshape(N_DEV, D_MODEL, D_IN_SHARD)
    proj, = run("in_proj_b", _mm_nn, "in_proj_b", h1, main_b, F32, a_col0=half, extras=(part,),
                epilogue=lambda acc, first: (acc + first,))
    proj_dt = _mm_nn("in_proj_dt", h1, jnp.concatenate([dt_a, dt_b], axis=0), F32)[0][0]
    xbc = _conv_silu_fwd(proj, conv_w, conv_b)
    dtx, acx = run("dt_prep", _dt_prep, proj_dt, bias_all, a_all)
    y2, states = run("ssd_fwd", _ssd_fwd, xbc, dtx, acx, a_x)
    y_ssm, = run("ssd_gate_fwd", _ssd_gate_fwd, y2, xbc, proj, d_lanes, ssm_nw)
    y_sc = _sc_fwd(proj, sc_w, sc_nw)
    ycat = jnp.concatenate([y_ssm, y_sc], axis=1)
    w_out_g = plan.get("w_out").reshape(D_MODEL, D_MODEL)
    mix, = run("out_proj", _mm_nn, "out_proj", ycat, w_out_g, F32)
    x1, h2 = run("ln1_fwd", _ln1_fwd, x, mix, mod6, ln1_g, ln1_b)

    def relu2(acc):
        u = acc.astype(BF16)
        r = jnp.maximum(acc, 0.0)
        return u, r * r

    w_up3 = plan.get("w_up")
    nper = w_up3.shape[2]
    tm = _tile(s, 1024)
    tn = 1024
    nb = nper // tn
    u_spec = pl.BlockSpec((tm, tn), lambda i, j, kk: (i, j))
    u, ff = run(
        "up_proj", _matmul, "up_proj", h2, w_up3, mode="nn", grid=(s // tm, D_FF // tn, 1),
        a_spec=pl.BlockSpec((tm, D_MODEL), lambda i, j, kk: (i, 0)),
        b_spec=pl.BlockSpec((None, D_MODEL, tn), lambda i, j, kk: (j // nb, 0, j % nb)),
        out_shapes=[jax.ShapeDtypeStruct((s, D_FF), BF16)] * 2, out_specs=[u_spec, u_spec],
        acc_shape=(tm, tn), epilogue=relu2)
    w_down_g = plan.get("w_down").reshape(D_FF, D_MODEL)
    f2 = _mm_nn("down_proj", ff, w_down_g, F32)[0][0]
    df2, dx1a, loss, g_ln2_g, g_ln2_b, dgate2 = _ln2_loss_bwd(x1, f2, tgt, mod6, ln2_g, ln2_b)

    def relu_grad(acc, uu):
        return (acc * (2.0 * jnp.maximum(uu.astype(F32), 0.0)),)

    du = _mm_nt("d_ff", df2, w_down_g, BF16, epilogue=relu_grad, extras=(u,))[0][0]
    plan.put("g_down", _mm_tn("g_w_down", ff, df2, BF16)[0][0].reshape(N_DEV, D_FF // N_DEV, D_MODEL))
    g_up, = run(
        "g_w_up", _matmul, "g_w_up", h2, du, mode="tn",
        grid=(D_MODEL // 1024, D_FF // tn, s // _tile(s, 2048)),
        a_spec=pl.BlockSpec((_tile(s, 2048), 1024), lambda i, j, kk: (kk, i)),
        b_spec=pl.BlockSpec((_tile(s, 2048), tn), lambda i, j, kk: (kk, j)),
        out_shapes=[jax.ShapeDtypeStruct((N_DEV, D_MODEL, nper), BF16)],
        out_specs=[pl.BlockSpec((None, 1024, tn), lambda i, j, kk: (j // nb, i, j % nb))],
        acc_shape=(1024, tn))
    plan.put("g_up", g_up)
    dh2, = run(
        "d_h2", _matmul, "d_h2", du, w_up3, mode="nt", grid=(s // tm, D_MODEL // 1024, D_FF // nper),
        a_spec=pl.BlockSpec((tm, nper), lambda i, j, kk: (i, kk)),
        b_spec=pl.BlockSpec((None, 1024, nper), lambda i, j, kk: (kk, j, 0)),
        out_shapes=[jax.ShapeDtypeStruct((s, D_MODEL), F32)],
        out_specs=[pl.BlockSpec((tm, 1024), lambda i, j, kk: (i, j))],
        acc_shape=(tm, 1024))
    dmix, dxa, dscale2, dshift2, g_ln1_g, g_ln1_b, dgate1 = _ln1_bwd(dh2, dx1a, x1, x, mix, mod6, ln1_g)

    dycat = _mm_nt("d_ycat", dmix, w_out_g, F32)[0][0]
    plan.put("g_out", run("g_w_out", _mm_tn, "g_w_out", ycat, dmix, BF16)[0].reshape(
        N_DEV, D_MODEL // N_DEV, D_MODEL))
    duh, dub, duc, g_sc_w, g_sc_nw = _sc_bwd(proj, dycat, sc_w, sc_nw)
    dyc, dz, dd_lanes, g_ssm_nw = run("ssd_gate_bwd", _ssd_gate_bwd, y2, xbc, proj, dycat, d_lanes, ssm_nw)
    dxs2, db2, dc2, ddtx, dacx, dax = run("ssd_bwd", _ssd_bwd, xbc, dtx, acx, a_x, states, dyc)
    n_bc = GROUPS * N_STATE
    du_xs, gw_xs, gb_xs = _conv_silu_bwd("conv_bwd_x", proj, conv_w, conv_b, 0, D_SSM, [dxs2],
                                         scaled=(dyc, d_lanes))
    du_b, gw_b, gb_b = _conv_silu_bwd("conv_bwd_b", proj, conv_w, conv_b, D_SSM, n_bc, [db2])
    du_c, gw_c, gb_c = _conv_silu_bwd("conv_bwd_c", proj, conv_w, conv_b, D_SSM + n_bc, n_bc, [dc2])
    du_dt, g_bias_all, g_a_sums = _dt_prep_bwd(proj_dt, bias_all, a_all, ddtx, dacx)

    sections = [dz, du_xs, du_b, du_c, du_dt[:, :2 * HEADS], duh, dub, duc]
    dproj3 = jnp.stack([_columns(sections, k * D_IN_SHARD, D_IN_SHARD) for k in range(N_DEV)])
    tk = _tile(s, 2048)
    g_in, = run(
        "g_w_in", _matmul, "g_w_in", h1, dproj3, mode="tn", grid=(N_DEV, D_MODEL // 1024, s // tk),
        a_spec=pl.BlockSpec((tk, 1024), lambda i, j, kk: (kk, j)),
        b_spec=pl.BlockSpec((None, tk, D_IN_SHARD), lambda i, j, kk: (i, kk, 0)),
        out_shapes=[jax.ShapeDtypeStruct((N_DEV, D_MODEL, D_IN_SHARD), BF16)],
        out_specs=[pl.BlockSpec((None, 1024, D_IN_SHARD), lambda i, j, kk: (i, j, 0))],
        acc_shape=(1024, D_IN_SHARD))
    plan.put("g_in", g_in)
    plan.hook("after_g_w_in")
    dh1, = run(
        "d_h1", _matmul, "d_h1", dproj3, w_in_g, mode="nt", grid=(s // tm, D_MODEL // 1024, N_DEV),
        a_spec=pl.BlockSpec((None, tm, D_IN_SHARD), lambda i, j, kk: (kk, i, 0)),
        b_spec=pl.BlockSpec((None, 1024, D_IN_SHARD), lambda i, j, kk: (kk, j, 0)),
        out_shapes=[jax.ShapeDtypeStruct((s, D_MODEL), F32)],
        out_specs=[pl.BlockSpec((tm, 1024), lambda i, j, kk: (i, j))],
        acc_shape=(tm, 1024))
    grad_x, dscale1, dshift1 = _grad_x(dxa, dh1, x, mod6)

    dmod = jnp.concatenate([dshift1, dscale1, dgate1, dshift2, dscale2, dgate2], axis=1)
    g_a_direct = dax.reshape(2, HEADS, HEAD_DIM).sum(axis=-1).reshape(1, 2 * HEADS)
    g_a_all = g_a_sums + _pad_lanes(g_a_direct, 128)
    small = {
        "dmod": dmod,
        "ssm_conv_w": jnp.concatenate([gw_xs, gw_b, gw_c], axis=1),
        "ssm_conv_b": jnp.concatenate([gb_xs, gb_b, gb_c], axis=1),
        "ssm_dt_bias_f": g_bias_all[:, :HEADS],
        "ssm_dt_bias_b": g_bias_all[:, HEADS:2 * HEADS],
        "ssm_a_log_f": (g_a_all * a_all)[:, :HEADS],
        "ssm_a_log_b": (g_a_all * a_all)[:, HEADS:2 * HEADS],
        "ssm_d": dd_lanes.reshape(HEADS, HEAD_DIM).sum(axis=1).reshape(1, HEADS),
        "ssm_norm_w": g_ssm_nw,
        "sc_conv_w": g_sc_w,
        "sc_norm_w": g_sc_nw,
        "ln1_g": g_ln1_g, "ln1_b": g_ln1_b, "ln2_g": g_ln2_g, "ln2_b": g_ln2_b,
    }
    return loss, grad_x, small


_SUMMED = [("ssm_conv_b", D_XBC), ("ssm_dt_bias_f", HEADS), ("ssm_dt_bias_b", HEADS),
           ("ssm_a_log_f", HEADS), ("ssm_a_log_b", HEADS), ("ssm_d", HEADS),
           ("ssm_norm_w", D_SSM), ("sc_norm_w", D_SC),
           ("ln1_g", D_MODEL), ("ln1_b", D_MODEL), ("ln2_g", D_MODEL), ("ln2_b", D_MODEL)]


def _round_up(n, k):
    return (n + k - 1) // k * k


def _w_in_sections(w_in_g, half):
    dt_lo = D_SSM + D_XBC
    k_dt = dt_lo // D_IN_SHARD
    cut = dt_lo - k_dt * D_IN_SHARD
    rest = (k_dt + 1) * D_IN_SHARD - dt_lo
    blocks = [w_in_g[k, half] for k in range(N_DEV)]
    dt = jnp.concatenate([blocks[k_dt][:, cut:], blocks[k_dt + 1][:, :2 * HEADS - rest]], axis=1)
    blocks[k_dt] = blocks[k_dt][:, :cut]
    blocks[k_dt + 1] = blocks[k_dt + 1][:, 2 * HEADS - rest:]
    return jnp.concatenate(blocks, axis=1), _pad_lanes(dt, 128)


def kernel(x, c, w_ada, b_ada, w_in, ssm_conv_w, ssm_conv_b, ssm_dt_bias_f, ssm_dt_bias_b, ssm_a_log_f, ssm_a_log_b, ssm_d, ssm_norm_w, sc_conv_w, sc_norm_w, w_out, ln1_g, ln1_b, w_up, w_down, ln2_g, ln2_b, loss_target, m_w_ada, m_b_ada, m_w_in, m_ssm_conv_w, m_ssm_conv_b, m_ssm_dt_bias_f, m_ssm_dt_bias_b, m_ssm_a_log_f, m_ssm_a_log_b, m_ssm_d, m_ssm_norm_w, m_sc_conv_w, m_sc_norm_w, m_w_out, m_ln1_g, m_ln1_b, m_w_up, m_w_down, m_ln2_g, m_ln2_b, v_w_ada, v_b_ada, v_w_in, v_ssm_conv_w, v_ssm_conv_b, v_ssm_dt_bias_f, v_ssm_dt_bias_b, v_ssm_a_log_f, v_ssm_a_log_b, v_ssm_d, v_ssm_norm_w, v_sc_conv_w, v_sc_norm_w, v_w_out, v_ln1_g, v_ln1_b, v_w_up, v_w_down, v_ln2_g, v_ln2_b):
    args = dict(locals())
    xi, yi, ci = _my_pos()
    me = 4 * xi + 2 * yi + ci
    pos = jnp.stack([xi, yi, ci]).astype(jnp.int32)
    s = x.shape[1]

    n_cw, n_sw = SSM_CONV * D_XBC // N_DEV, SC_CONV * D_SC // N_DEV
    vec = jnp.concatenate([c, ssm_conv_w[0].reshape(1, n_cw), sc_conv_w[0].reshape(1, n_sw)], axis=1)
    vec = _pad_lanes(vec, 8192)
    gath = _gather_vec("gather_c_conv", vec)
    c_all = gath[:, :D_MODEL]
    conv_w = gath[:, D_MODEL:D_MODEL + n_cw].reshape(N_DEV, SSM_CONV, D_XBC // N_DEV)
    conv_w = conv_w.transpose(1, 0, 2).reshape(SSM_CONV, D_XBC)
    sc_w = gath[:, D_MODEL + n_cw:D_MODEL + n_cw + n_sw].reshape(N_DEV, SC_CONV, D_SC // N_DEV)
    sc_w = sc_w.transpose(1, 0, 2).reshape(SC_CONV, D_SC)
    c16 = jnp.pad(c_all, ((0, 8), (0, 0)))

    n_ada = w_ada.shape[2]
    mod_cols = _ada_fwd(c16, w_ada[0])[:N_DEV]
    mod_all = _run_jobs("gather_mod", [_GatherJob(mod_cols, pltpu.VMEM)])[0]
    mod = lax.dynamic_index_in_dim(mod_all, me, axis=1, keepdims=False)
    mod = mod.reshape(1, N_MOD * D_MODEL) + b_ada

    out = {}

    def adamw(plan, tag):
        name = "w_" + tag
        if tag in two_leg:
            others = [(plan.get("ra_" + tag), 1), (plan.get("rb_" + tag), None)]
        else:
            others = [(plan.get("r2_" + tag), k) for k in range(3)]
        res = plan.run("rs_adamw_" + tag, _reduce_adamw, "rs_adamw_" + tag, plan.get("g_" + tag),
                       plan.get("r1_" + tag), others, pos, args[name][0], args["m_" + name][0],
                       args["v_" + name][0])
        out[name] = tuple(a[None] for a in res)

    two_leg = ("down", "up")
    hosted = {
        "in_proj_a": [(("lead", 1, 2), "w_in")],
        "in_proj_b": [("gather", "w_out"), (("part", 0, 1, 8), "w_up")],
        "dt_prep": [(("part", 1, 1, 8), "w_up")],
        "ssd_fwd": [(("part", 2, 2, 8), "w_up")],
        "ssd_gate_fwd": [(("part", 4, 1, 8), "w_up")],
        "out_proj": [(("part", 5, 2, 8), "w_up")],
        "ln1_fwd": [(("part", 7, 1, 8), "w_up")],
        "up_proj": [("gather", "w_down")],
        "g_w_up": [("rs1", "down")],
        "d_h2": [("rs2a", "down"), ("rs1", "up")],
        "g_w_out": [("rs2b", "down")],
        "ssd_gate_bwd": [("rs1", "out")],
        "ssd_bwd": [("rs2a", "up")],
        "g_w_in": [("rs2b", "up"), ("rs2", "out")],
        "d_h1": [("rs2", "in")],
    }

    def sibling_exchange_in(plan):
        plan.put("r1_in", _run_jobs("rs_sibling_in", [_SiblingJob(plan.get("g_in"))])[0])

    store = {"pos": pos}
    for tag, w in (("w_in", w_in), ("w_out", w_out), ("w_up", w_up), ("w_down", w_down)):
        store["shard_" + tag] = _cast_bf16("cast_" + tag, w[0])
    store["shard_w_in"] = store["shard_w_in"].reshape(2, D_MODEL // 2, D_IN_SHARD)
    store["part_w_in"] = _run_jobs("gather_w_in", [_GatherJob(store["shard_w_in"], lead=0)])[0]
    plan = _Plan(hosted, store, hooks={"after_g_w_in": sibling_exchange_in}, two_leg=two_leg)
    loss, grad_x, small = _local_step(
        plan, x[0], loss_target[0], mod, conv_w, ssm_conv_b, ssm_dt_bias_f, ssm_dt_bias_b,
        ssm_a_log_f, ssm_a_log_b, ssm_d, ssm_norm_w, sc_w, sc_norm_w, ln1_g, ln1_b, ln2_g, ln2_b)
    for tag in ("down", "up", "out", "in"):
        adamw(plan, tag)

    parts = [small["dmod"]]
    parts += [_pad_lanes(small[n], _round_up(w, 128)) for n, w in _SUMMED]
    parts += [small["ssm_conv_w"].reshape(1, SSM_CONV * D_XBC), small["sc_conv_w"].reshape(1, SC_CONV * D_SC)]
    parts += [loss]
    gvec = jnp.concatenate(parts, axis=1)
    n_vec = _round_up(gvec.shape[1], 8192)
    gall = _gather_vec("gather_small_grads", _pad_lanes(gvec, n_vec))

    def shard_cols(full, k, per):
        return lax.dynamic_slice_in_dim(full.reshape(k, N_DEV, per), me, 1, axis=1).reshape(1, k * per)

    def placed(vals, n_rows=1):
        return jnp.concatenate(vals, axis=1)

    n_mod = N_MOD * D_MODEL
    ws, ms, vs = [b_ada], [m_b_ada], [v_b_ada]
    for n, w in _SUMMED:
        pw = _round_up(w, 128)
        ws.append(_pad_lanes(args[n], pw))
        ms.append(_pad_lanes(args["m_" + n], pw))
        vs.append(_pad_lanes(args["v_" + n], pw))

    def full_rows(shard, k, per):
        z = jnp.zeros((k, N_DEV, per), F32)
        z = lax.dynamic_update_slice_in_dim(z, shard.reshape(k, 1, per), me, axis=1)
        return z.reshape(1, k * N_DEV * per)

    for nm, k, per in (("ssm_conv_w", SSM_CONV, D_XBC // N_DEV), ("sc_conv_w", SC_CONV, D_SC // N_DEV)):
        ws.append(full_rows(args[nm][0], k, per))
        ms.append(full_rows(args["m_" + nm][0], k, per))
        vs.append(full_rows(args["v_" + nm][0], k, per))
    tail = n_vec - sum(a.shape[1] for a in ws)
    ws.append(jnp.zeros((1, tail), F32))
    ms.append(jnp.zeros((1, tail), F32))
    vs.append(jnp.ones((1, tail), F32))
    g_s, d_s, m_s, v_s = _sum8_adamw(gall, placed(ws), placed(ms), placed(vs))

    off = 0

    def take(w):
        nonlocal off
        sl = tuple(a[:, off:off + w] for a in (g_s, d_s, m_s, v_s))
        off += _round_up(w, 128)
        return sl

    out["b_ada"] = take(n_mod)
    for n, w in _SUMMED:
        out[n] = take(w)
    for nm, k, per in (("ssm_conv_w", SSM_CONV, D_XBC // N_DEV), ("sc_conv_w", SC_CONV, D_SC // N_DEV)):
        full = take(k * N_DEV * per)
        out[nm] = tuple(shard_cols(a, k, per).reshape(1, k, per) for a in full)
    loss_total = g_s[0, off]

    dmod_all = gall[:, :n_mod]
    dmod_cols = lax.dynamic_slice_in_dim(dmod_all.reshape(N_DEV, N_DEV, n_ada), me, 1, axis=1)
    dmod16 = jnp.pad(dmod_cols.reshape(N_DEV, n_ada), ((0, 8), (0, 0)))
    out["w_ada"] = tuple(a[None] for a in _ada_bwd_adamw(c16, dmod16, w_ada[0], m_w_ada[0], v_w_ada[0]))

    names = ['w_ada', 'b_ada', 'w_in', 'ssm_conv_w', 'ssm_conv_b', 'ssm_dt_bias_f', 'ssm_dt_bias_b',
             'ssm_a_log_f', 'ssm_a_log_b', 'ssm_d', 'ssm_norm_w', 'sc_conv_w', 'sc_norm_w', 'w_out',
             'ln1_g', 'ln1_b', 'w_up', 'w_down', 'ln2_g', 'ln2_b']
    res = [loss_total, grad_x[None]]
    for k in range(4):
        res += [out[n][k] for n in names]
    return tuple(res)
```

```python
import functools

import jax
import jax.numpy as jnp
from jax import lax
from jax.experimental import pallas as pl
from jax.experimental.pallas import tpu as pltpu

F32 = jnp.float32
BF16 = jnp.bfloat16
MESH = pl.DeviceIdType.MESH

N_DEV = 8
D_MODEL = 4096
D_SSM = 2048
D_SC = 2048
HEADS = 32
HEAD_DIM = 64
GROUPS = 8
GROUP_W = D_SSM // GROUPS
HEADS_PER_GROUP = 4
N_STATE = 128
CHUNK = 128
SSM_CONV = 5
SC_CONV = 3
SC_GROUP_W = 128
D_XBC = 4096
D_FF = 16384
D_IN = 12352
D_IN_SHARD = D_IN // N_DEV
D_MAIN = 12288
N_MOD = 6
ALPHA = (2 * 1) ** 0.25
LN_EPS = 1e-5
RMS_EPS = 1e-5
ADAM_LR = 0.001
ADAM_B1 = 0.9
ADAM_B2 = 0.999
ADAM_EPS = 1e-08
ADAM_WD = 0.01
ADAM_STEP = 10

VMEM_LIMIT = 56 * 1024 * 1024
HALO = 8

_DN = {
    "nn": (((1,), (0,)), ((), ())),
    "nt": (((1,), (1,)), ((), ())),
    "tn": (((0,), (0,)), ((), ())),
}


def _cparams(sem=None):
    return pltpu.CompilerParams(dimension_semantics=sem, vmem_limit_bytes=VMEM_LIMIT)


def _my_pos():
    return lax.axis_index("x"), lax.axis_index("y"), lax.axis_index("c")


def _other_chips(x, y):
    return [(1 - x, y), (x, 1 - y), (1 - x, 1 - y)]


class _GatherJob:
    n_remote = 7

    def __init__(self, shard, space=pl.ANY, rows=None, lead=None, into=None):
        self.ins = (shard,) if into is None else (shard, into)
        self.alias = None if into is None else 1
        self.out_shapes = (jax.ShapeDtypeStruct((N_DEV,) + shard.shape, shard.dtype),)
        self.space = space
        self.rows, self.lead = rows, lead

    def _piece(self, ref):
        if self.rows is not None:
            return ref.at[pl.ds(*self.rows)]
        return ref if self.lead is None else ref.at[self.lead]

    def _parts(self, ins, outs, send, recv, local):
        x_ref, out_ref = self._piece(ins[0]), outs[0]
        x, y, c = _my_pos()
        me, sibling = (x, y, c), (x, y, 1 - c)
        chips = _other_chips(x, y)

        def slab(px, py, pc):
            return self._piece(out_ref.at[4 * px + 2 * py + pc])

        def copy(k, block, to, src=None):
            return pltpu.make_async_remote_copy(
                src_ref=slab(*block) if src is None else src, dst_ref=slab(*block),
                send_sem=send.at[k], recv_sem=recv.at[k], device_id=to, device_id_type=MESH)

        mine = pltpu.make_async_copy(x_ref, slab(*me), local.at[0])
        own = [copy(0, me, sibling, src=x_ref), copy(1, me, (*chips[0], c), src=x_ref),
               copy(2, me, (*chips[1], c), src=x_ref)]
        relayed = (x + (1 - c) * (1 - 2 * x), y + c * (1 - 2 * y), c)
        relay = copy(3, relayed, (x + c * (1 - 2 * x), y + (1 - c) * (1 - 2 * y), c))
        hand = [copy(4 + j, (*chip, c), sibling) for j, chip in enumerate(chips)]
        landed = [copy(1 + j, (*chip, c), me) for j, chip in enumerate(chips)]
        handed = [copy(0, sibling, me)] + [copy(4 + j, (*chip, 1 - c), me) for j, chip in enumerate(chips)]
        return mine, own, relay, hand, landed, handed

    def start(self, *refs):
        mine, own, _, _, _, _ = self._parts(*refs)
        mine.start()
        for cp in own:
            cp.start()

    def mid(self, *refs):
        _, _, relay, hand, landed, _ = self._parts(*refs)
        landed[0].wait_recv()
        landed[1].wait_recv()
        relay.start()
        hand[0].start()
        hand[1].start()

    def finish(self, *refs):
        mine, own, relay, hand, landed, handed = self._parts(*refs)
        landed[2].wait_recv()
        hand[2].start()
        for cp in handed:
            cp.wait_recv()
        for cp in own + [relay] + hand:
            cp.wait_send()
        mine.wait()


class _SiblingJob:
    n_remote = 4
    space = pl.ANY

    def __init__(self, g):
        self.ins = (g,)
        self.out_shapes = (jax.ShapeDtypeStruct((4,) + g.shape[1:], g.dtype),)

    def _copies(self, ins, outs, send, recv, local):
        x, y, c = _my_pos()
        return [pltpu.make_async_remote_copy(
            src_ref=ins[0].at[2 * j + (1 - c)], dst_ref=outs[0].at[j],
            send_sem=send.at[j], recv_sem=recv.at[j],
            device_id=(x, y, 1 - c), device_id_type=MESH) for j in range(4)]

    def start(self, *refs):
        for cp in self._copies(*refs):
            cp.start()

    def mid(self, *refs):
        pass

    def finish(self, *refs):
        for cp in self._copies(*refs):
            cp.wait()


class _ChipsJob:
    n_remote = 3
    space = pl.ANY

    def __init__(self, p):
        self.ins = (p,)
        self.out_shapes = (jax.ShapeDtypeStruct(p.shape, p.dtype),)

    def _copies(self, ins, outs, send, recv, local):
        x, y, c = _my_pos()
        return [pltpu.make_async_remote_copy(
            src_ref=ins[0].at[k], dst_ref=outs[0].at[k],
            send_sem=send.at[k], recv_sem=recv.at[k],
            device_id=(px, py, c), device_id_type=MESH) for k, (px, py) in enumerate(_other_chips(x, y))]

    def start(self, *refs):
        for cp in self._copies(*refs):
            cp.start()

    def mid(self, *refs):
        pass

    def finish(self, *refs):
        for cp in self._copies(*refs):
            cp.wait()


def _relay_route(x, y, c):
    first = (x + c * (1 - 2 * x), y + (1 - c) * (1 - 2 * y))
    second = (x + (1 - c) * (1 - 2 * x), y + c * (1 - 2 * y))
    return first, second


class _RelayFirstJob:
    n_remote = 2
    space = pl.ANY

    def __init__(self, p):
        self.ins = (p,)
        self.out_shapes = (jax.ShapeDtypeStruct(p.shape, p.dtype),)

    def _copies(self, ins, outs, send, recv, local):
        x, y, c = _my_pos()
        (fx, fy), _ = _relay_route(x, y, c)
        return [pltpu.make_async_remote_copy(
            src_ref=ins[0].at[k], dst_ref=outs[0].at[k], send_sem=send.at[k], recv_sem=recv.at[k],
            device_id=(fx, fy, c), device_id_type=MESH) for k in range(2)]

    def start(self, *refs):
        for cp in self._copies(*refs):
            cp.start()

    def mid(self, *refs):
        pass

    def finish(self, *refs):
        for cp in self._copies(*refs):
            cp.wait()


class _RelaySecondJob:
    n_remote = 1
    space = pl.ANY

    def __init__(self, q):
        self.ins = (q,)
        self.out_shapes = (jax.ShapeDtypeStruct(q.shape, q.dtype),)

    def _copy(self, ins, outs, send, recv, local):
        x, y, c = _my_pos()
        _, (sx, sy) = _relay_route(x, y, c)
        return pltpu.make_async_remote_copy(
            src_ref=ins[0], dst_ref=outs[0], send_sem=send.at[0], recv_sem=recv.at[0],
            device_id=(sx, sy, c), device_id_type=MESH)

    def start(self, *refs):
        self._copy(*refs).start()

    def mid(self, *refs):
        pass

    def finish(self, *refs):
        self._copy(*refs).wait()


MID_STEP_FRACTION = 0.64


def _call(name, body, *, grid, in_specs, out_specs, out_shape, args, scratch_shapes=(), sem=None,
          jobs=(), n_prefetch=0):
    out_shape, out_specs, in_specs = list(out_shape), list(out_specs), list(in_specs)
    scratch_shapes = list(scratch_shapes)
    jobs = list(jobs)
    n_in, n_out, n_scr = len(in_specs), len(out_shape), len(scratch_shapes)
    job_ins = [a for j in jobs for a in j.ins]
    job_outs = [o for j in jobs for o in j.out_shapes]
    steps = 1
    for n in grid:
        steps *= n
    mid_step = min(steps - 1, int(steps * MID_STEP_FRACTION))

    def wrapped(*refs):
        pre, refs = refs[:n_prefetch], refs[n_prefetch:]
        core_in, refs = refs[:n_in], refs[n_in:]
        jin, refs = refs[:len(job_ins)], refs[len(job_ins):]
        core_out, refs = refs[:n_out], refs[n_out:]
        jout, refs = refs[:len(job_outs)], refs[len(job_outs):]
        core_scr, sems = refs[:n_scr], refs[n_scr:]
        lin = 0
        for ax, n in enumerate(grid):
            lin = lin * n + pl.program_id(ax)
        bound = []
        for j in jobs:
            ji, jin = jin[:len(j.ins)], jin[len(j.ins):]
            jo, jout = jout[:len(j.out_shapes)], jout[len(j.out_shapes):]
            (send, recv, local), sems = sems[:3], sems[3:]
            bound.append((j, (ji, jo, send, recv, local)))

        if jobs:
            @pl.when(lin == 0)
            def _():
                for j, r in bound:
                    j.start(*r)

        body(*pre, *core_in, *core_out, *core_scr)

        if jobs:
            @pl.when(lin == mid_step)
            def _():
                for j, r in bound:
                    j.mid(*r)

            @pl.when(lin == steps - 1)
            def _():
                for j, r in bound:
                    j.finish(*r)

    sem_shapes = []
    for j in jobs:
        sem_shapes += [pltpu.SemaphoreType.DMA((j.n_remote,)), pltpu.SemaphoreType.DMA((j.n_remote,)),
                       pltpu.SemaphoreType.DMA((1,))]
    if jobs:
        sem = tuple("arbitrary" for _ in grid)
    aliases = {}
    in_at, out_at = len(args), n_out
    for j in jobs:
        if getattr(j, "alias", None) is not None:
            aliases[in_at + j.alias] = out_at
        in_at, out_at = in_at + len(j.ins), out_at + len(j.out_shapes)
    res = pl.pallas_call(
        wrapped, name=name, input_output_aliases=aliases,
        grid_spec=pltpu.PrefetchScalarGridSpec(
            num_scalar_prefetch=n_prefetch, grid=tuple(grid),
            in_specs=in_specs + [pl.BlockSpec(memory_space=j.space) for j in jobs for _ in j.ins],
            out_specs=out_specs + [pl.BlockSpec(memory_space=j.space) for j in jobs for _ in j.out_shapes],
            scratch_shapes=scratch_shapes + sem_shapes),
        out_shape=out_shape + job_outs,
        compiler_params=_cparams(sem),
    )(*args, *job_ins)
    res = list(res) if isinstance(res, (list, tuple)) else [res]
    return res[:n_out], res[n_out:]


def _run_jobs(name, jobs):
    return _call(name, lambda: None, grid=(1,), in_specs=[], out_specs=[], out_shape=[], args=(),
                 jobs=jobs)[1]


def _matmul(name, a, b, *, mode, grid, a_spec, b_spec, out_shapes, out_specs, acc_shape,
            epilogue=None, extras=(), extra_specs=(), jobs=()):
    nk = grid[2]
    n_extra = len(extras)
    n_out = len(out_shapes)

    def body(*refs):
        a_ref, b_ref = refs[0], refs[1]
        extra_refs = refs[2:2 + n_extra]
        out_refs = refs[2 + n_extra:2 + n_extra + n_out]
        part = lax.dot_general(a_ref[...], b_ref[...], _DN[mode], preferred_element_type=F32)

        def finish(acc):
            outs = epilogue(acc, *[r[...] for r in extra_refs]) if epilogue else (acc,)
            for o_ref, o in zip(out_refs, outs):
                o_ref[...] = o.astype(o_ref.dtype)

        if nk == 1:
            finish(part)
        else:
            acc_ref = refs[-1]
            k = pl.program_id(2)

            @pl.when(k == 0)
            def _():
                acc_ref[...] = part

            @pl.when(k > 0)
            def _():
                acc_ref[...] += part

            @pl.when(k == nk - 1)
            def _():
                finish(acc_ref[...])

    scratch = [pltpu.VMEM(acc_shape, F32)] if nk > 1 else []
    return _call(name, body, grid=grid, in_specs=[a_spec, b_spec, *extra_specs],
                 out_specs=out_specs, out_shape=out_shapes, scratch_shapes=scratch,
                 sem=("parallel", "parallel", "arbitrary"), args=(a, b, *extras), jobs=jobs)


def _tile(n, pref):
    t = min(n, pref)
    assert n % t == 0, (n, t)
    return t


def _mm_nn(name, a, b, out_dtype, tn=1024, tk=None, epilogue=None, out_dtypes=None, jobs=(),
           a_col0=0, extras=()):
    m, k = a.shape[0], b.shape[0]
    n = b.shape[1]
    tm, tn = _tile(m, 1024), _tile(n, tn)
    tk = _tile(k, tk or 4096)
    k0 = a_col0 // tk
    assert a_col0 % tk == 0
    out_dtypes = out_dtypes or (out_dtype,)
    o_spec = pl.BlockSpec((tm, tn), lambda i, j, kk: (i, j))
    return _matmul(
        name, a, b, mode="nn", grid=(m // tm, n // tn, k // tk),
        a_spec=pl.BlockSpec((tm, tk), lambda i, j, kk: (i, k0 + kk)),
        b_spec=pl.BlockSpec((tk, tn), lambda i, j, kk: (kk, j)),
        out_shapes=[jax.ShapeDtypeStruct((m, n), dt) for dt in out_dtypes],
        out_specs=[o_spec for _ in out_dtypes],
        acc_shape=(tm, tn), epilogue=epilogue, jobs=jobs,
        extras=extras, extra_specs=[o_spec for _ in extras])


def _mm_nt(name, a, b, out_dtype, epilogue=None, extras=(), tk=None, jobs=()):
    m, k = a.shape
    n = b.shape[0]
    tm, tn = _tile(m, 1024), _tile(n, 1024)
    tk = _tile(k, tk or 4096)
    o_spec = pl.BlockSpec((tm, tn), lambda i, j, kk: (i, j))
    return _matmul(
        name, a, b, mode="nt", grid=(m // tm, n // tn, k // tk),
        a_spec=pl.BlockSpec((tm, tk), lambda i, j, kk: (i, kk)),
        b_spec=pl.BlockSpec((tn, tk), lambda i, j, kk: (j, kk)),
        out_shapes=[jax.ShapeDtypeStruct((m, n), out_dtype)],
        out_specs=[o_spec], acc_shape=(tm, tn), epilogue=epilogue,
        extras=extras, extra_specs=[o_spec for _ in extras], jobs=jobs)


def _mm_tn(name, a, b, out_dtype, tk=2048, jobs=()):
    k, m = a.shape
    n = b.shape[1]
    tm, tn = _tile(m, 1024), _tile(n, 1024)
    tk = _tile(k, tk)
    return _matmul(
        name, a, b, mode="tn", grid=(m // tm, n // tn, k // tk),
        a_spec=pl.BlockSpec((tk, tm), lambda i, j, kk: (kk, i)),
        b_spec=pl.BlockSpec((tk, tn), lambda i, j, kk: (kk, j)),
        out_shapes=[jax.ShapeDtypeStruct((m, n), out_dtype)],
        out_specs=[pl.BlockSpec((tm, tn), lambda i, j, kk: (i, j))],
        acc_shape=(tm, tn), jobs=jobs)


def _cast_bf16(name, w):
    r, c = w.shape
    tr = _tile(r, 512)

    def body(w_ref, o_ref):
        o_ref[...] = w_ref[...].astype(BF16)

    return pl.pallas_call(
        body, name=name, grid=(r // tr,),
        in_specs=[pl.BlockSpec((tr, c), lambda i: (i, 0))],
        out_specs=pl.BlockSpec((tr, c), lambda i: (i, 0)),
        out_shape=jax.ShapeDtypeStruct((r, c), BF16),
        compiler_params=_cparams(("parallel",)),
    )(w)


def _cast_many(name, ws, steps=8, jobs=()):
    def body(*refs):
        for w_ref, o_ref in zip(refs[:len(ws)], refs[len(ws):]):
            o_ref[...] = w_ref[...].astype(BF16)

    specs = [pl.BlockSpec((w.shape[0] // steps, w.shape[1]), lambda i: (i, 0)) for w in ws]
    return _call(name, body, grid=(steps,), in_specs=specs, out_specs=specs,
                 out_shape=[jax.ShapeDtypeStruct(w.shape, BF16) for w in ws],
                 sem=("parallel",), args=tuple(ws), jobs=jobs)


def _chip_of(pos, which):
    x, y, c = pos[0], pos[1], pos[2]
    first, second = _relay_route(x, y, c)
    chips = _other_chips(x, y) + [first, second, (x, y)]
    px, py = chips[which]
    return 2 * px + py


def _pair_add(name, g, r1, pos, dests):
    _, r, cdim = g.shape
    tr = _tile(r, 512)

    def chip(k, pos):
        idx = _chip_of(pos, dests[-1])
        for n in range(len(dests) - 2, -1, -1):
            idx = jnp.where(k == n, _chip_of(pos, dests[n]), idx)
        return idx

    def body(pos_ref, g_ref, r_ref, o_ref):
        o_ref[...] = (g_ref[...].astype(F32) + r_ref[...].astype(F32)).astype(o_ref.dtype)

    return pl.pallas_call(
        body, name=name,
        grid_spec=pltpu.PrefetchScalarGridSpec(
            num_scalar_prefetch=1, grid=(len(dests), r // tr),
            in_specs=[pl.BlockSpec((None, tr, cdim), lambda k, i, pos: (2 * chip(k, pos) + pos[2], i, 0)),
                      pl.BlockSpec((None, tr, cdim), lambda k, i, pos: (chip(k, pos), i, 0))],
            out_specs=pl.BlockSpec((None, tr, cdim), lambda k, i, pos: (k, i, 0))),
        out_shape=jax.ShapeDtypeStruct((len(dests), r, cdim), BF16),
        compiler_params=_cparams(("parallel", "parallel")),
    )(pos, g, r1)


def _adamw_math(w, g, m, v):
    m = ADAM_B1 * m + (1.0 - ADAM_B1) * g
    v = ADAM_B2 * v + (1.0 - ADAM_B2) * jnp.square(g)
    m_hat = m / (1.0 - ADAM_B1 ** ADAM_STEP)
    v_hat = v / (1.0 - ADAM_B2 ** ADAM_STEP)
    delta = -ADAM_LR * (m_hat / (jnp.sqrt(v_hat) + ADAM_EPS) + ADAM_WD * w)
    return delta, m, v


def _relay_add(name, g, r1, ra, pos):
    _, r, cdim = g.shape
    tr = _tile(r, 512)

    def body(pos_ref, g_ref, r1_ref, ra_ref, o_ref):
        q = g_ref[...].astype(F32) + r1_ref[...].astype(F32) + ra_ref[...].astype(F32)
        o_ref[...] = q.astype(o_ref.dtype)

    return pl.pallas_call(
        body, name=name,
        grid_spec=pltpu.PrefetchScalarGridSpec(
            num_scalar_prefetch=1, grid=(r // tr,),
            in_specs=[pl.BlockSpec((None, tr, cdim), lambda i, pos: (2 * _chip_of(pos, 4) + pos[2], i, 0)),
                      pl.BlockSpec((None, tr, cdim), lambda i, pos: (_chip_of(pos, 4), i, 0)),
                      pl.BlockSpec((None, tr, cdim), lambda i, pos: (0, i, 0))],
            out_specs=pl.BlockSpec((tr, cdim), lambda i, pos: (i, 0))),
        out_shape=jax.ShapeDtypeStruct((r, cdim), BF16),
        compiler_params=_cparams(("parallel",)),
    )(pos, g, r1, ra)


def _reduce_adamw(name, g8, r1, others, pos, w, m, v, jobs=()):
    r, cdim = w.shape
    tr = _tile(r, 128 if cdim >= D_MODEL else 256)
    blk = pl.BlockSpec((tr, cdim), lambda i, pos: (i, 0))
    n_other = len(others)

    def body(pos_ref, g_ref, r1_ref, *refs):
        other_refs, (w_ref, m_ref, v_ref, g_out, d_out, m_out, v_out) = refs[:n_other], refs[n_other:]
        g = g_ref[...].astype(F32) + r1_ref[...].astype(F32)
        for o_ref in other_refs:
            g = g + o_ref[...].astype(F32)
        d, mn, vn = _adamw_math(w_ref[...], g, m_ref[...], v_ref[...])
        g_out[...] = g
        d_out[...] = d
        m_out[...] = mn
        v_out[...] = vn

    def other_spec(lead):
        if lead is None:
            return blk
        return pl.BlockSpec((None, tr, cdim), lambda i, pos: (lead, i, 0))

    shp = jax.ShapeDtypeStruct((r, cdim), F32)
    return _call(
        name, body, grid=(r // tr,), n_prefetch=1,
        in_specs=[pl.BlockSpec((None, tr, cdim), lambda i, pos: (2 * _chip_of(pos, 5) + pos[2], i, 0)),
                  pl.BlockSpec((None, tr, cdim), lambda i, pos: (_chip_of(pos, 5), i, 0))]
        + [other_spec(lead) for _, lead in others] + [blk, blk, blk],
        out_specs=[blk, blk, blk, blk], out_shape=[shp, shp, shp, shp],
        sem=("parallel",), args=(pos, g8, r1, *[a for a, _ in others], w, m, v), jobs=jobs)


def _row_spec(t, width=D_MODEL):
    return pl.BlockSpec((t, width), lambda i: (i, 0))


def _full_spec(shape):
    return pl.BlockSpec(shape, lambda i: tuple(0 for _ in shape))


def _ln_stats(p):
    mu = jnp.mean(p, axis=-1, keepdims=True)
    xc = p - mu
    var = jnp.mean(xc * xc, axis=-1, keepdims=True)
    rstd = lax.rsqrt(var + LN_EPS)
    return xc * rstd, rstd


def _ln_bwd(dy, xhat, rstd, g):
    dxh = dy * g
    m1 = jnp.mean(dxh, axis=-1, keepdims=True)
    m2 = jnp.mean(dxh * xhat, axis=-1, keepdims=True)
    return rstd * (dxh - m1 - xhat * m2)


def _acc_rows(ref, val, first):
    s = jnp.sum(val, axis=0, keepdims=True)

    @pl.when(first)
    def _():
        ref[...] = s

    @pl.when(jnp.logical_not(first))
    def _():
        ref[...] += s


def _modulate(name, x, mod6):
    s = x.shape[0]
    t = _tile(s, 256)

    def body(x_ref, mod_ref, o_ref):
        o_ref[...] = (x_ref[...] * (1.0 + mod_ref[1:2, :]) + mod_ref[0:1, :]).astype(BF16)

    return pl.pallas_call(
        body, name=name, grid=(s // t,),
        in_specs=[_row_spec(t), _full_spec((N_MOD, D_MODEL))],
        out_specs=_row_spec(t),
        out_shape=jax.ShapeDtypeStruct((s, D_MODEL), BF16),
        compiler_params=_cparams(("parallel",)),
    )(x, mod6)


def _ln1_fwd(x, mix, mod6, g, b, jobs=()):
    s = x.shape[0]
    t = _tile(s, 256)

    def body(x_ref, mix_ref, mod_ref, g_ref, b_ref, x1_ref, h2_ref):
        pre = ALPHA * x_ref[...] + (1.0 + mod_ref[2:3, :]) * mix_ref[...]
        xhat, _ = _ln_stats(pre)
        x1 = xhat * g_ref[...] + b_ref[...]
        x1_ref[...] = x1
        h2_ref[...] = (x1 * (1.0 + mod_ref[4:5, :]) + mod_ref[3:4, :]).astype(BF16)

    vec = _full_spec((1, D_MODEL))
    return _call(
        "ln1_fwd", body, grid=(s // t,),
        in_specs=[_row_spec(t), _row_spec(t), _full_spec((N_MOD, D_MODEL)), vec, vec],
        out_specs=[_row_spec(t), _row_spec(t)],
        out_shape=[jax.ShapeDtypeStruct((s, D_MODEL), F32), jax.ShapeDtypeStruct((s, D_MODEL), BF16)],
        sem=("parallel",), args=(x, mix, mod6, g, b), jobs=jobs)


def _ln2_loss_bwd(x1, f2, tgt, mod6, g, b):
    s = x1.shape[0]
    t = _tile(s, 128)

    def body(x1_ref, f2_ref, tgt_ref, mod_ref, g_ref, b_ref,
             df2_ref, dx1_ref, loss_ref, dg_ref, db_ref, dgate_ref):
        first = pl.program_id(0) == 0
        gate = 1.0 + mod_ref[5:6, :]
        f2v = f2_ref[...]
        pre = ALPHA * x1_ref[...] + gate * f2v
        xhat, rstd = _ln_stats(pre)
        err = xhat * g_ref[...] + b_ref[...] - tgt_ref[...]
        part = 0.5 * jnp.sum(jnp.mean(err * err, axis=-1, keepdims=True), axis=0, keepdims=True)
        dy = err / D_MODEL
        dpre = _ln_bwd(dy, xhat, rstd, g_ref[...])
        df2_ref[...] = (gate * dpre).astype(BF16)
        dx1_ref[...] = ALPHA * dpre
        _acc_rows(loss_ref, jnp.broadcast_to(part, (1, 128)), first)
        _acc_rows(dg_ref, dy * xhat, first)
        _acc_rows(db_ref, dy, first)
        _acc_rows(dgate_ref, dpre * f2v, first)

    vec = _full_spec((1, D_MODEL))
    vshape = jax.ShapeDtypeStruct((1, D_MODEL), F32)
    return pl.pallas_call(
        body, name="ln2_loss_bwd", grid=(s // t,),
        in_specs=[_row_spec(t), _row_spec(t), _row_spec(t), _full_spec((N_MOD, D_MODEL)), vec, vec],
        out_specs=[_row_spec(t), _row_spec(t), _full_spec((1, 128)), vec, vec, vec],
        out_shape=[jax.ShapeDtypeStruct((s, D_MODEL), BF16), jax.ShapeDtypeStruct((s, D_MODEL), F32),
                   jax.ShapeDtypeStruct((1, 128), F32), vshape, vshape, vshape],
        compiler_params=_cparams(("arbitrary",)),
    )(x1, f2, tgt, mod6, g, b)


def _ln1_bwd(dh2, dx1a, x1, x, mix, mod6, g):
    s = x.shape[0]
    t = _tile(s, 128)

    def body(dh2_ref, dx1a_ref, x1_ref, x_ref, mix_ref, mod_ref, g_ref,
             dmix_ref, dxa_ref, dscale_ref, dshift_ref, dg_ref, db_ref, dgate_ref):
        first = pl.program_id(0) == 0
        dh2v = dh2_ref[...]
        dx1 = dx1a_ref[...] + dh2v * (1.0 + mod_ref[4:5, :])
        gate = 1.0 + mod_ref[2:3, :]
        mixv = mix_ref[...]
        pre = ALPHA * x_ref[...] + gate * mixv
        xhat, rstd = _ln_stats(pre)
        dpre = _ln_bwd(dx1, xhat, rstd, g_ref[...])
        dmix_ref[...] = (gate * dpre).astype(BF16)
        dxa_ref[...] = ALPHA * dpre
        _acc_rows(dscale_ref, dh2v * x1_ref[...], first)
        _acc_rows(dshift_ref, dh2v, first)
        _acc_rows(dg_ref, dx1 * xhat, first)
        _acc_rows(db_ref, dx1, first)
        _acc_rows(dgate_ref, dpre * mixv, first)

    vec = _full_spec((1, D_MODEL))
    vshape = jax.ShapeDtypeStruct((1, D_MODEL), F32)
    return pl.pallas_call(
        body, name="ln1_bwd", grid=(s // t,),
        in_specs=[_row_spec(t)] * 5 + [_full_spec((N_MOD, D_MODEL)), vec],
        out_specs=[_row_spec(t), _row_spec(t), vec, vec, vec, vec, vec],
        out_shape=[jax.ShapeDtypeStruct((s, D_MODEL), BF16), jax.ShapeDtypeStruct((s, D_MODEL), F32),
                   vshape, vshape, vshape, vshape, vshape],
        compiler_params=_cparams(("arbitrary",)),
    )(dh2, dx1a, x1, x, mix, mod6, g)


def _grad_x(dxa, dh1, x, mod6):
    s = x.shape[0]
    t = _tile(s, 256)

    def body(dxa_ref, dh1_ref, x_ref, mod_ref, gx_ref, dscale_ref, dshift_ref):
        first = pl.program_id(0) == 0
        dh1v = dh1_ref[...]
        gx_ref[...] = dxa_ref[...] + dh1v * (1.0 + mod_ref[1:2, :])
        _acc_rows(dscale_ref, dh1v * x_ref[...], first)
        _acc_rows(dshift_ref, dh1v, first)

    vec = _full_spec((1, D_MODEL))
    vshape = jax.ShapeDtypeStruct((1, D_MODEL), F32)
    return pl.pallas_call(
        body, name="grad_x", grid=(s // t,),
        in_specs=[_row_spec(t)] * 3 + [_full_spec((N_MOD, D_MODEL))],
        out_specs=[_row_spec(t), vec, vec],
        out_shape=[jax.ShapeDtypeStruct((s, D_MODEL), F32), vshape, vshape],
        compiler_params=_cparams(("arbitrary",)),
    )(dxa, dh1, x, mod6)


def _window(ref, i, t, s):
    r0 = pl.multiple_of(i * t, t)
    cur = ref[pl.ds(r0, t), :]
    lo = pl.multiple_of(jnp.maximum(r0 - HALO, 0), HALO)
    hi = pl.multiple_of(jnp.minimum(r0 + t, s - HALO), HALO)
    before = ref[pl.ds(lo, HALO), :] * (i > 0).astype(F32)
    after = ref[pl.ds(hi, HALO), :] * (i < s // t - 1).astype(F32)
    return jnp.concatenate([before, cur, after], axis=0)


def _tap(ext, shift):
    n = ext.shape[0]
    if shift == 0:
        return ext
    return pltpu.roll(ext, (-shift) % n, 0)


def _centre(ext, t):
    return ext[HALO:HALO + t]


def _conv_taps(ext, w, width):
    acc = None
    for k in range(width):
        term = _tap(ext, k - width // 2) * w[k:k + 1, :]
        acc = term if acc is None else acc + term
    return acc


def _silu(a):
    return a * jax.nn.sigmoid(a)


def _conv_silu_fwd(proj, w, b, jobs=()):
    s = proj.shape[0]
    cb = 256
    t = _tile(s, 256)
    off = D_SSM // cb

    def body(u_ref, w_ref, b_ref, o_ref):
        wv = w_ref[...]
        bv = b_ref[...]

        def step(i, carry):
            ext = _window(u_ref, i, t, s)
            a = _centre(_conv_taps(ext, wv, SSM_CONV), t) + bv
            o_ref[pl.ds(pl.multiple_of(i * t, t), t), :] = _silu(a)
            return carry

        lax.fori_loop(0, s // t, step, 0)

    return _call(
        "conv_silu_fwd", body, grid=(D_XBC // cb,),
        in_specs=[pl.BlockSpec((s, cb), lambda j: (0, off + j)),
                  pl.BlockSpec((SSM_CONV, cb), lambda j: (0, j)),
                  pl.BlockSpec((1, cb), lambda j: (0, j))],
        out_specs=[pl.BlockSpec((s, cb), lambda j: (0, j))],
        out_shape=[jax.ShapeDtypeStruct((s, D_XBC), F32)],
        sem=("parallel",), args=(proj, w, b), jobs=jobs)


def _conv_silu_bwd(name, proj, w, b, col0, ncols, cots, scaled=None):
    s = proj.shape[0]
    cb = 128
    t = _tile(s, 256)
    off = (D_SSM + col0) // cb
    woff = col0 // cb
    n_cot = len(cots)

    def body(*refs):
        u_ref, w_ref, b_ref = refs[:3]
        cot_refs = refs[3:3 + n_cot]
        sc_refs = refs[3 + n_cot:3 + n_cot + (2 if scaled else 0)]
        du_ref, dw_ref, db_ref = refs[-3:]
        wv = w_ref[...]
        bv = b_ref[...]

        def step(i, carry):
            ext = _window(u_ref, i, t, s)
            a = _conv_taps(ext, wv, SSM_CONV) + bv
            cot = None
            for cr in cot_refs:
                term = _window(cr.at[0], i, t, s) + _window(cr.at[1], i, t, s)
                cot = term if cot is None else cot + term
            if scaled:
                cot = cot + _window(sc_refs[0], i, t, s) * sc_refs[1][...]
            sig = jax.nn.sigmoid(a)
            da = cot * (sig * (1.0 + a * (1.0 - sig)))
            du = None
            new = []
            for k in range(SSM_CONV):
                sh = k - SSM_CONV // 2
                term = _tap(da, -sh) * wv[k:k + 1, :]
                du = term if du is None else du + term
                prod = _centre(_tap(ext, sh) * da, t)
                new.append(carry[k] + jnp.sum(prod, axis=0, keepdims=True))
            new.append(carry[SSM_CONV] + jnp.sum(_centre(da, t), axis=0, keepdims=True))
            du_ref[pl.ds(pl.multiple_of(i * t, t), t), :] = _centre(du, t).astype(BF16)
            return tuple(new)

        zero = jnp.zeros((1, cb), F32)
        acc = lax.fori_loop(0, s // t, step, tuple(zero for _ in range(SSM_CONV + 1)))
        for k in range(SSM_CONV):
            dw_ref[k:k + 1, :] = acc[k]
        db_ref[...] = acc[SSM_CONV]

    in_specs = [pl.BlockSpec((s, cb), lambda j: (0, off + j)),
                pl.BlockSpec((SSM_CONV, cb), lambda j: (0, woff + j)),
                pl.BlockSpec((1, cb), lambda j: (0, woff + j))]
    in_specs += [pl.BlockSpec((2, s, cb), lambda j: (0, 0, j)) for _ in cots]
    args = [proj, w, b, *cots]
    if scaled:
        in_specs += [pl.BlockSpec((s, cb), lambda j: (0, j)), pl.BlockSpec((1, cb), lambda j: (0, j))]
        args += list(scaled)
    return pl.pallas_call(
        body, name=name, grid=(ncols // cb,),
        in_specs=in_specs,
        out_specs=[pl.BlockSpec((s, cb), lambda j: (0, j)),
                   pl.BlockSpec((SSM_CONV, cb), lambda j: (0, j)),
                   pl.BlockSpec((1, cb), lambda j: (0, j))],
        out_shape=[jax.ShapeDtypeStruct((s, ncols), BF16),
                   jax.ShapeDtypeStruct((SSM_CONV, ncols), F32),
                   jax.ShapeDtypeStruct((1, ncols), F32)],
        compiler_params=_cparams(("parallel",)),
    )(*args)


_SC_H = (D_SSM + D_XBC) // SC_GROUP_W
_SC_B = _SC_H + D_SC // SC_GROUP_W
_SC_C = _SC_B + D_SC // SC_GROUP_W


def _sc_fwd(proj, w, nw):
    s = proj.shape[0]
    cb = SC_GROUP_W
    t = _tile(s, 256)

    def body(uh_ref, ub_ref, uc_ref, w_ref, nw_ref, o_ref):
        wv = w_ref[...]
        nwv = nw_ref[...]

        def step(i, carry):
            p = _window(uc_ref, i, t, s) * _window(uh_ref, i, t, s)
            cv = _centre(_conv_taps(p, wv, SC_CONV), t)
            rows = pl.ds(pl.multiple_of(i * t, t), t)
            y = ub_ref[rows, :] * cv
            r = lax.rsqrt(jnp.mean(y * y, axis=-1, keepdims=True) + RMS_EPS)
            o_ref[rows, :] = (y * r * nwv).astype(BF16)
            return carry

        lax.fori_loop(0, s // t, step, 0)

    def col(base):
        return pl.BlockSpec((s, cb), lambda j: (0, base + j))

    return pl.pallas_call(
        body, name="sc_fwd", grid=(D_SC // cb,),
        in_specs=[col(_SC_H), col(_SC_B), col(_SC_C),
                  pl.BlockSpec((SC_CONV, cb), lambda j: (0, j)),
                  pl.BlockSpec((1, cb), lambda j: (0, j))],
        out_specs=pl.BlockSpec((s, cb), lambda j: (0, j)),
        out_shape=jax.ShapeDtypeStruct((s, D_SC), BF16),
        compiler_params=_cparams(("parallel",)),
    )(proj, proj, proj, w, nw)


def _sc_bwd(proj, dycat, w, nw):
    s = proj.shape[0]
    cb = SC_GROUP_W
    t = _tile(s, 256)
    dy_off = D_SSM // cb

    def body(uh_ref, ub_ref, uc_ref, dy_ref, w_ref, nw_ref, duh_ref, dub_ref, duc_ref, dw_ref, dnw_ref):
        wv = w_ref[...]
        nwv = nw_ref[...]

        def step(i, carry):
            uh = _window(uh_ref, i, t, s)
            ub = _window(ub_ref, i, t, s)
            uc = _window(uc_ref, i, t, s)
            do = _window(dy_ref, i, t, s)
            p = uc * uh
            cv = _conv_taps(p, wv, SC_CONV)
            y = ub * cv
            r = lax.rsqrt(jnp.mean(y * y, axis=-1, keepdims=True) + RMS_EPS)
            dyr = do * nwv
            dy = r * dyr - y * (r * r * r) * jnp.mean(dyr * y, axis=-1, keepdims=True)
            dcv = dy * ub
            dp = None
            new = []
            for k in range(SC_CONV):
                sh = k - SC_CONV // 2
                term = _tap(dcv, -sh) * wv[k:k + 1, :]
                dp = term if dp is None else dp + term
                new.append(carry[k] + jnp.sum(_centre(_tap(p, sh) * dcv, t), axis=0, keepdims=True))
            new.append(carry[SC_CONV] + jnp.sum(_centre(do * y * r, t), axis=0, keepdims=True))
            rows = pl.ds(pl.multiple_of(i * t, t), t)
            duh_ref[rows, :] = _centre(dp * uc, t).astype(BF16)
            duc_ref[rows, :] = _centre(dp * uh, t).astype(BF16)
            dub_ref[rows, :] = _centre(dy * cv, t).astype(BF16)
            return tuple(new)

        zero = jnp.zeros((1, cb), F32)
        acc = lax.fori_loop(0, s // t, step, tuple(zero for _ in range(SC_CONV + 1)))
        for k in range(SC_CONV):
            dw_ref[k:k + 1, :] = acc[k]
        dnw_ref[...] = acc[SC_CONV]

    def col(base):
        return pl.BlockSpec((s, cb), lambda j: (0, base + j))

    out_col = pl.BlockSpec((s, cb), lambda j: (0, j))
    act = jax.ShapeDtypeStruct((s, D_SC), BF16)
    return pl.pallas_call(
        body, name="sc_bwd", grid=(D_SC // cb,),
        in_specs=[col(_SC_H), col(_SC_B), col(_SC_C), col(dy_off),
                  pl.BlockSpec((SC_CONV, cb), lambda j: (0, j)),
                  pl.BlockSpec((1, cb), lambda j: (0, j))],
        out_specs=[out_col, out_col, out_col,
                   pl.BlockSpec((SC_CONV, cb), lambda j: (0, j)),
                   pl.BlockSpec((1, cb), lambda j: (0, j))],
        out_shape=[act, act, act, jax.ShapeDtypeStruct((SC_CONV, D_SC), F32),
                   jax.ShapeDtypeStruct((1, D_SC), F32)],
        compiler_params=_cparams(("parallel",)),
    )(proj, proj, proj, dycat, w, nw)


def _make_select_dot(differentiable):
    def raw(a, b, mode, const):
        ops = [a, b]
        v = ops[1 - const]
        acc = None
        for _ in range(3):
            piece = v.astype(BF16)
            v = v - piece.astype(F32)
            ops[1 - const] = piece
            part = lax.dot_general(ops[0].astype(BF16), ops[1].astype(BF16), _DN[mode],
                                   preferred_element_type=F32)
            acc = part if acc is None else acc + part
        return acc

    if not differentiable:
        return raw

    @functools.partial(jax.custom_vjp, nondiff_argnums=(2, 3))
    def dot(a, b, mode, const):
        return raw(a, b, mode, const)

    def fwd(a, b, mode, const):
        return raw(a, b, mode, const), (a, b)

    def bwd(mode, const, res, g):
        a, b = res
        assert mode == "nn"
        if const == 1:
            return raw(g, b, "nt", 1), jnp.zeros_like(b)
        return jnp.zeros_like(a), raw(a, g, "tn", 0)

    dot.defvjp(fwd, bwd)
    return dot


def _make_dot(differentiable):
    def raw(a, b, mode):
        return lax.dot_general(a.astype(BF16), b.astype(BF16), _DN[mode], preferred_element_type=F32)

    if not differentiable:
        return raw

    @functools.partial(jax.custom_vjp, nondiff_argnums=(2,))
    def dot(a, b, mode):
        return raw(a, b, mode)

    def fwd(a, b, mode):
        return raw(a, b, mode), (a, b)

    def bwd(mode, res, g):
        a, b = res
        if mode == "nn":
            return raw(g, b, "nt"), raw(a, g, "tn")
        if mode == "nt":
            return raw(g, b, "nn"), raw(g, a, "tn")
        return raw(b, g, "nt"), raw(a, g, "nn")

    dot.defvjp(fwd, bwd)
    return dot


def _make_swap(differentiable):
    def raw(v):
        return pltpu.roll(v, HEAD_DIM, 1)

    if not differentiable:
        return raw
    swap = jax.custom_vjp(raw)
    swap.defvjp(lambda v: (raw(v), None), lambda _, g: (raw(g),))
    return swap


def _ssd_chunk(xs, bm, cm, dtx, acx, ax, prev, tri, differentiable):
    _bdot = _make_dot(differentiable)
    swap = _make_swap(differentiable)
    atx = jnp.sum(dtx * ax, axis=0, keepdims=True)
    xdt = xs * dtx
    mask = tri > 0.0
    scores = _bdot(cm, bm, "nt")
    head = lax.broadcasted_iota(jnp.int32, (1, GROUP_W), 1) // HEAD_DIM
    low = lax.broadcasted_iota(jnp.int32, (1, 128), 1) < HEAD_DIM
    y = _bdot(cm, prev, "nn") * jnp.exp(acx)
    for h in range(HEADS_PER_GROUP):
        pair = acx[:, 128 * (h // 2):128 * (h // 2) + 128]
        other = swap(pair)
        m1 = jnp.where(low, pair, other) if h % 2 == 0 else jnp.where(low, other, pair)
        seg = m1 - m1.T
        decay = jnp.where(mask, jnp.exp(jnp.where(mask, seg, 0.0)), 0.0)
        xh = xdt * (head == h).astype(F32)
        y = y + _bdot(scores * decay, xh, "nn")
    new = prev * jnp.exp(atx) + _bdot(bm, xdt * jnp.exp(atx - acx), "tn")
    return y, new


def _softplus(v):
    return jnp.maximum(v, 0.0) + jnp.log(1.0 + jnp.exp(-jnp.abs(v)))


def _dt_spread(u, bias, a, tri2, exf, differentiable):
    sel = _make_select_dot(differentiable)
    dt = _softplus(u + bias)
    dta = dt * a
    out = []
    for d in range(2):
        acum = sel(tri2[d], dta, "nn", 0)
        out += [sel(dt, exf[d], "nn", 1), sel(acum, exf[d], "nn", 1)]
    return tuple(out)


def _ssd_consts():
    q = CHUNK
    r = lax.broadcasted_iota(jnp.int32, (q, q), 0)
    c = lax.broadcasted_iota(jnp.int32, (q, q), 1)
    tri = jnp.stack([(c <= r), (c >= r)]).astype(F32)
    shp = (2, 128, D_SSM)
    src = lax.broadcasted_iota(jnp.int32, shp, 1)
    d = lax.broadcasted_iota(jnp.int32, shp, 0)
    col = lax.broadcasted_iota(jnp.int32, shp, 2)
    exf = (src == d * HEADS + col // HEAD_DIM).astype(F32)
    return tri, exf


def _dt_prep(proj_dt, bias_all, a_all, jobs=()):
    s = proj_dt.shape[0]
    tri, exf = _ssd_consts()

    def body(u_ref, b_ref, a_ref, tri_ref, exf_ref, dtx_ref, acx_ref):
        dtx0, acx0, dtx1, acx1 = _dt_spread(u_ref[...], b_ref[...], a_ref[...], tri_ref[...],
                                            exf_ref[...], False)
        dtx_ref[0] = dtx0
        dtx_ref[1] = dtx1
        acx_ref[0] = acx0
        acx_ref[1] = acx1

    out = pl.BlockSpec((2, CHUNK, D_SSM), lambda i: (0, i, 0))
    shp = jax.ShapeDtypeStruct((2, s, D_SSM), F32)
    return _call(
        "dt_prep", body, grid=(s // CHUNK,),
        in_specs=[_row_spec(CHUNK, 128), _full_spec((1, 128)), _full_spec((1, 128)),
                  _full_spec((2, CHUNK, CHUNK)), _full_spec((2, 128, D_SSM))],
        out_specs=[out, out], out_shape=[shp, shp],
        sem=("parallel",), args=(proj_dt, bias_all, a_all, tri, exf), jobs=jobs)


def _dt_prep_bwd(proj_dt, bias_all, a_all, d_dtx, d_acx):
    s = proj_dt.shape[0]
    tri, exf = _ssd_consts()

    def body(u_ref, b_ref, a_ref, tri_ref, exf_ref, ddtx_ref, dacx_ref, du_ref, db_ref, da_ref):
        tri_v, exf_v = tri_ref[...], exf_ref[...]

        def f(u, bias, a):
            return _dt_spread(u, bias, a, tri_v, exf_v, True)

        _, vjp = jax.vjp(f, u_ref[...], b_ref[...], a_ref[...])
        du, db, da = vjp((ddtx_ref[0], dacx_ref[0], ddtx_ref[1], dacx_ref[1]))
        du_ref[...] = du.astype(BF16)
        first = pl.program_id(0) == 0
        _acc_rows(db_ref, db, first)
        _acc_rows(da_ref, da, first)

    cot = pl.BlockSpec((2, CHUNK, D_SSM), lambda i: (0, i, 0))
    vec = _full_spec((1, 128))
    return pl.pallas_call(
        body, name="dt_prep_bwd", grid=(s // CHUNK,),
        in_specs=[_row_spec(CHUNK, 128), vec, vec, _full_spec((2, CHUNK, CHUNK)),
                  _full_spec((2, 128, D_SSM)), cot, cot],
        out_specs=[_row_spec(CHUNK, 128), vec, vec],
        out_shape=[jax.ShapeDtypeStruct((s, 128), BF16), jax.ShapeDtypeStruct((1, 128), F32),
                   jax.ShapeDtypeStruct((1, 128), F32)],
        compiler_params=_cparams(("arbitrary",)),
    )(proj_dt, bias_all, a_all, tri, exf, d_dtx, d_acx)


GROUPS_PER_STEP = 4
_PAIR_W = GROUPS_PER_STEP * GROUP_W
_PAIR_N = GROUPS_PER_STEP * N_STATE


def _ssd_specs(chunk_of):
    q = CHUNK
    b0 = D_SSM // _PAIR_N
    xs = pl.BlockSpec((q, _PAIR_W), lambda d, g, ci: (chunk_of(d, ci), g))
    bm = pl.BlockSpec((q, _PAIR_N), lambda d, g, ci: (chunk_of(d, ci), b0 + g))
    cm = pl.BlockSpec((q, _PAIR_N), lambda d, g, ci: (chunk_of(d, ci), b0 + GROUPS // GROUPS_PER_STEP + g))
    spread = pl.BlockSpec((None, q, _PAIR_W), lambda d, g, ci: (d, chunk_of(d, ci), g))
    ax = pl.BlockSpec((None, 1, _PAIR_W), lambda d, g, ci: (d, 0, g))
    tri = pl.BlockSpec((None, q, q), lambda d, g, ci: (d, 0, 0))
    st = pl.BlockSpec((None, None, GROUPS_PER_STEP, N_STATE, GROUP_W),
                      lambda d, g, ci: (d, chunk_of(d, ci), g, 0, 0))
    return xs, bm, cm, spread, ax, tri, st


def _wide(k):
    return slice(k * GROUP_W, (k + 1) * GROUP_W)


def _narrow(k):
    return slice(k * N_STATE, (k + 1) * N_STATE)


def _ssd_fwd(xbc, dtx, acx, ax, jobs=()):
    s = xbc.shape[0]
    nc = s // CHUNK
    tri, _ = _ssd_consts()

    def chunk_of(d, ci):
        return ci + d * (nc - 1 - 2 * ci)

    def body(xs_ref, b_ref, c_ref, dtx_ref, acx_ref, ax_ref, tri_ref, y_ref, st_ref, state):
        @pl.when(pl.program_id(2) == 0)
        def _():
            state[...] = jnp.zeros(state.shape, F32)

        tri_v = tri_ref[...]
        for k in range(GROUPS_PER_STEP):
            prev = state[k]
            st_ref[k] = prev
            y, new = _ssd_chunk(xs_ref[:, _wide(k)], b_ref[:, _narrow(k)], c_ref[:, _narrow(k)],
                                dtx_ref[:, _wide(k)], acx_ref[:, _wide(k)], ax_ref[:, _wide(k)],
                                prev, tri_v, False)
            y_ref[:, _wide(k)] = y
            state[k] = new

    xs, bm, cm, spread, ax_s, tri_s, st = _ssd_specs(chunk_of)
    return _call(
        "ssd_fwd", body, grid=(2, GROUPS // GROUPS_PER_STEP, nc),
        in_specs=[xs, bm, cm, spread, spread, ax_s, tri_s],
        out_specs=[spread, st],
        out_shape=[jax.ShapeDtypeStruct((2, s, D_SSM), F32),
                   jax.ShapeDtypeStruct((2, nc, GROUPS, N_STATE, GROUP_W), F32)],
        scratch_shapes=[pltpu.VMEM((GROUPS_PER_STEP, N_STATE, GROUP_W), F32)],
        sem=("arbitrary", "arbitrary", "arbitrary"),
        args=(xbc, xbc, xbc, dtx, acx, ax, tri), jobs=jobs)


def _ssd_bwd(xbc, dtx, acx, ax, states, dy, jobs=()):
    s = xbc.shape[0]
    nc = s // CHUNK
    tri, _ = _ssd_consts()

    def chunk_of(d, ci):
        return (nc - 1 - ci) + d * (2 * ci - (nc - 1))

    def body(xs_ref, b_ref, c_ref, dtx_ref, acx_ref, ax_ref, tri_ref, st_ref, dy_ref,
             dxs_ref, db_ref, dc_ref, ddtx_ref, dacx_ref, dax_ref, dstate):
        first = pl.program_id(2) == 0

        @pl.when(first)
        def _():
            dstate[...] = jnp.zeros(dstate.shape, F32)

        tri_v = tri_ref[...]

        def f(xs, bm, cm, dtx_v, acx_v, ax_v, prev):
            return _ssd_chunk(xs, bm, cm, dtx_v, acx_v, ax_v, prev, tri_v, True)

        dax_parts = []
        for k in range(GROUPS_PER_STEP):
            _, vjp = jax.vjp(f, xs_ref[:, _wide(k)], b_ref[:, _narrow(k)], c_ref[:, _narrow(k)],
                             dtx_ref[:, _wide(k)], acx_ref[:, _wide(k)], ax_ref[:, _wide(k)], st_ref[k])
            dxs, dbm, dcm, ddtx, dacx, dax, dprev = vjp((dy_ref[:, _wide(k)], dstate[k]))
            dxs_ref[:, _wide(k)] = dxs
            db_ref[:, _narrow(k)] = dbm
            dc_ref[:, _narrow(k)] = dcm
            ddtx_ref[:, _wide(k)] = ddtx
            dacx_ref[:, _wide(k)] = dacx
            dstate[k] = dprev
            dax_parts.append(dax)
        _acc_rows(dax_ref, jnp.concatenate(dax_parts, axis=1), first)

    xs, bm, cm, spread, ax_s, tri_s, st = _ssd_specs(chunk_of)
    dy_s = pl.BlockSpec((CHUNK, _PAIR_W), lambda d, g, ci: (chunk_of(d, ci), g))
    bc_s = pl.BlockSpec((None, CHUNK, _PAIR_N), lambda d, g, ci: (d, chunk_of(d, ci), g))
    wide = jax.ShapeDtypeStruct((2, s, D_SSM), F32)
    narrow = jax.ShapeDtypeStruct((2, s, GROUPS * N_STATE), F32)
    return _call(
        "ssd_bwd", body, grid=(2, GROUPS // GROUPS_PER_STEP, nc),
        in_specs=[xs, bm, cm, spread, spread, ax_s, tri_s, st, dy_s],
        out_specs=[spread, bc_s, bc_s, spread, spread, ax_s],
        out_shape=[wide, narrow, narrow, wide, wide, jax.ShapeDtypeStruct((2, 1, D_SSM), F32)],
        scratch_shapes=[pltpu.VMEM((GROUPS_PER_STEP, N_STATE, GROUP_W), F32)],
        sem=("arbitrary", "arbitrary", "arbitrary"),
        args=(xbc, xbc, xbc, dtx, acx, ax, tri, states, dy), jobs=jobs)


def _ssd_gate_fwd(y2, xbc, proj, dx, nw, jobs=()):
    s = xbc.shape[0]
    t = _tile(s, 512)

    def body(y_ref, xs_ref, z_ref, dx_ref, nw_ref, o_ref):
        y = (y_ref[0] + y_ref[1] + dx_ref[...] * xs_ref[...]) * _silu(z_ref[...])
        r = lax.rsqrt(jnp.mean(y * y, axis=-1, keepdims=True) + RMS_EPS)
        o_ref[...] = (y * r * nw_ref[...]).astype(BF16)

    blk = pl.BlockSpec((t, GROUP_W), lambda j, i: (i, j))
    vec = pl.BlockSpec((1, GROUP_W), lambda j, i: (0, j))
    return _call(
        "ssd_gate_fwd", body, grid=(GROUPS, s // t),
        in_specs=[pl.BlockSpec((2, t, GROUP_W), lambda j, i: (0, i, j)), blk, blk, vec, vec],
        out_specs=[blk], out_shape=[jax.ShapeDtypeStruct((s, D_SSM), BF16)],
        sem=("parallel", "parallel"), args=(y2, xbc, proj, dx, nw), jobs=jobs)


def _ssd_gate_bwd(y2, xbc, proj, dycat, dx, nw, jobs=()):
    s = xbc.shape[0]
    t = _tile(s, 512)

    def body(y_ref, xs_ref, z_ref, do_ref, dx_ref, nw_ref, dyc_ref, dz_ref, dd_ref, dnw_ref):
        first = pl.program_id(1) == 0
        z = z_ref[...]
        xs = xs_ref[...]
        sig = jax.nn.sigmoid(z)
        gate = z * sig
        yc = y_ref[0] + y_ref[1] + dx_ref[...] * xs
        y = yc * gate
        r = lax.rsqrt(jnp.mean(y * y, axis=-1, keepdims=True) + RMS_EPS)
        do = do_ref[...]
        dyr = do * nw_ref[...]
        dy = r * dyr - y * (r * r * r) * jnp.mean(dyr * y, axis=-1, keepdims=True)
        dyc = dy * gate
        dyc_ref[...] = dyc
        dz_ref[...] = (dy * yc * (sig * (1.0 + z * (1.0 - sig)))).astype(BF16)
        _acc_rows(dd_ref, dyc * xs, first)
        _acc_rows(dnw_ref, do * y * r, first)

    blk = pl.BlockSpec((t, GROUP_W), lambda j, i: (i, j))
    vec = pl.BlockSpec((1, GROUP_W), lambda j, i: (0, j))
    return _call(
        "ssd_gate_bwd", body, grid=(GROUPS, s // t),
        in_specs=[pl.BlockSpec((2, t, GROUP_W), lambda j, i: (0, i, j)), blk, blk, blk, vec, vec],
        out_specs=[blk, blk, vec, vec],
        out_shape=[jax.ShapeDtypeStruct((s, D_SSM), F32), jax.ShapeDtypeStruct((s, D_SSM), BF16),
                   jax.ShapeDtypeStruct((1, D_SSM), F32), jax.ShapeDtypeStruct((1, D_SSM), F32)],
        sem=("parallel", "arbitrary"), args=(y2, xbc, proj, dycat, dx, nw), jobs=jobs)


def _ada_fwd(c16, w_ada):
    k, n = w_ada.shape
    tn = 512

    def body(c_ref, w_ref, o_ref):
        a = _silu(c_ref[...]).astype(BF16)
        o_ref[...] = jnp.dot(a, w_ref[...].astype(BF16), preferred_element_type=F32)

    return pl.pallas_call(
        body, name="ada_fwd", grid=(n // tn,),
        in_specs=[_full_spec((16, k)), pl.BlockSpec((k, tn), lambda j: (0, j))],
        out_specs=pl.BlockSpec((16, tn), lambda j: (0, j)),
        out_shape=jax.ShapeDtypeStruct((16, n), F32),
        compiler_params=_cparams(("parallel",)),
    )(c16, w_ada)


def _ada_bwd_adamw(c16, dmod16, w, m, v):
    k, n = w.shape
    tm, tn = 256, n
    blk = pl.BlockSpec((tm, tn), lambda i, j: (i, j))

    def body(c_ref, d_ref, w_ref, m_ref, v_ref, g_out, d_out, m_out, v_out):
        a = _silu(c_ref[...]).astype(BF16)
        g = lax.dot_general(a, d_ref[...].astype(BF16), _DN["tn"], preferred_element_type=F32)
        d, mn, vn = _adamw_math(w_ref[...], g, m_ref[...], v_ref[...])
        g_out[...] = g
        d_out[...] = d
        m_out[...] = mn
        v_out[...] = vn

    shp = jax.ShapeDtypeStruct((k, n), F32)
    return pl.pallas_call(
        body, name="ada_bwd_adamw", grid=(k // tm, n // tn),
        in_specs=[pl.BlockSpec((16, tm), lambda i, j: (0, i)), pl.BlockSpec((16, tn), lambda i, j: (0, j)),
                  blk, blk, blk],
        out_specs=[blk, blk, blk, blk],
        out_shape=[shp, shp, shp, shp],
        compiler_params=_cparams(("parallel", "parallel")),
    )(c16, dmod16, w, m, v)


def _sum8_adamw(gathered, w, m, v):
    n = w.shape[1]
    tn = _tile(n, 8192)
    vec = pl.BlockSpec((1, tn), lambda j: (0, j))

    def body(g8_ref, w_ref, m_ref, v_ref, g_out, d_out, m_out, v_out):
        g = g8_ref[0:1, :]
        for k in range(1, N_DEV):
            g = g + g8_ref[k:k + 1, :]
        d, mn, vn = _adamw_math(w_ref[...], g, m_ref[...], v_ref[...])
        g_out[...] = g
        d_out[...] = d
        m_out[...] = mn
        v_out[...] = vn

    shp = jax.ShapeDtypeStruct((1, n), F32)
    return pl.pallas_call(
        body, name="sum8_adamw", grid=(n // tn,),
        in_specs=[pl.BlockSpec((N_DEV, tn), lambda j: (0, j)), vec, vec, vec],
        out_specs=[vec, vec, vec, vec],
        out_shape=[shp, shp, shp, shp],
        compiler_params=_cparams(("parallel",)),
    )(gathered, w, m, v)


def _gather_vec(name, v):
    n = v.shape[1]
    out = _run_jobs(name, [_GatherJob(v.reshape(8, n // 8), pltpu.VMEM)])[0]
    return out.reshape(N_DEV, n)


class _Plan:
    _RESULT = {"gather": "", "rs1": "r1_", "rs2": "r2_", "rs2a": "ra_", "rs2b": "rb_"}

    def __init__(self, hosted, store, hooks=None, two_leg=()):
        self.hosted, self.store, self.hooks, self.two_leg = hosted, dict(store), hooks or {}, two_leg

    def get(self, key):
        if key not in self.store and key.startswith("p_"):
            tag = key[2:]
            dests = (2, 3) if tag in self.two_leg else (0, 1, 2)
            self.store[key] = _pair_add("rs_pair_add_" + tag, self.get("g_" + tag), self.get("r1_" + tag),
                                        self.get("pos"), dests)
        if key not in self.store and key.startswith("q_"):
            tag = key[2:]
            self.store[key] = _relay_add("rs_relay_add_" + tag, self.get("g_" + tag), self.get("r1_" + tag),
                                         self.get("ra_" + tag), self.get("pos"))
        return self.store[key]

    def put(self, key, val):
        self.store[key] = val

    def part_job(self, tag, first, count, n):
        shard = self.get("shard_" + tag)
        rows = shard.shape[0] // n
        return _GatherJob(shard, rows=(rows * first, rows * count),
                          into=self.get("part_" + tag) if first else None)

    def jobs(self, host):
        make = {"gather": lambda t: _GatherJob(self.get("shard_" + t)),
                "rs1": lambda t: _SiblingJob(self.get("g_" + t)),
                "rs2": lambda t: _ChipsJob(self.get("p_" + t)),
                "rs2a": lambda t: _RelayFirstJob(self.get("p_" + t)),
                "rs2b": lambda t: _RelaySecondJob(self.get("q_" + t))}
        def one(kind, tag):
            if not isinstance(kind, tuple):
                return make[kind](tag)
            if kind[0] == "lead":
                return _GatherJob(self.get("shard_" + tag), lead=kind[1],
                                  into=self.get("part_" + tag) if kind[1] else None)
            return self.part_job(tag, *kind[1:])

        return [one(kind, tag) for kind, tag in self.hosted.get(host, ())]

    def deliver(self, kind, tag, res):
        if isinstance(kind, tuple):
            done = kind[1] + 1 == kind[2] if kind[0] == "lead" else kind[1] + kind[2] == kind[3]
            self.store["part_" + tag] = res
            if done:
                self.store[tag] = res
        else:
            self.store[self._RESULT[kind] + tag] = res

    def run(self, host, fn, *args, **kw):
        outs, results = fn(*args, jobs=self.jobs(host), **kw)
        for (kind, tag), res in zip(self.hosted.get(host, ()), results):
            self.deliver(kind, tag, res)
        return outs

    def hook(self, name):
        if name in self.hooks:
            self.hooks[name](self)


def _pad_lanes(v, n):
    return jnp.pad(v, ((0, 0), (0, n - v.shape[1])))


def _columns(arrays, start, width):
    pieces, at = [], 0
    for a in arrays:
        lo, hi = max(start, at), min(start + width, at + a.shape[1])
        if lo < hi:
            pieces.append(a[:, lo - at:hi - at])
        at += a.shape[1]
    return jnp.concatenate(pieces, axis=1)


def _local_step(plan, x, tgt, mod, conv_w, conv_b, dt_bias_f, dt_bias_b, a_log_f, a_log_b,
                ssm_d, ssm_nw, sc_w, sc_nw, ln1_g, ln1_b, ln2_g, ln2_b):
    s = x.shape[0]
    run = plan.run
    mod6 = mod.reshape(N_MOD, D_MODEL)
    bias_all = _pad_lanes(jnp.concatenate([dt_bias_f, dt_bias_b], axis=1), 128)
    a_all = _pad_lanes(-jnp.exp(jnp.concatenate([a_log_f, a_log_b], axis=1)), 128)
    a_x = jnp.stack([jnp.repeat(a_all[:, d * HEADS:(d + 1) * HEADS], HEAD_DIM, axis=1) for d in range(2)])
    d_lanes = jnp.repeat(ssm_d, HEAD_DIM, axis=1)

    half = D_MODEL // 2
    main_a, dt_a = _w_in_sections(plan.get("part_w_in"), 0)
    h1 = _modulate("mod1", x, mod6)
    part, = run("in_proj_a", _mm_nn, "in_proj_a", h1, main_a, F32)
    main_b, dt_b = _w_in_sections(plan.get("w_in"), 1)
    w_in_g = plan.get("w_in").reshape(N_DEV, D_MODEL, D_IN_SHARD)
    proj, = run("in_proj_b", _mm_nn, "in_proj_b", h1, main_b, F32, a_col0=half, extras=(part,),
                epilogue=lambda acc, first: (acc + first,))
    proj_dt = _mm_nn("in_proj_dt", h1, jnp.concatenate([dt_a, dt_b], axis=0), F32)[0][0]
    xbc, = run("conv_silu_fwd", _conv_silu_fwd, proj, conv_w, conv_b)
    dtx, acx = run("dt_prep", _dt_prep, proj_dt, bias_all, a_all)
    y2, states = run("ssd_fwd", _ssd_fwd, xbc, dtx, acx, a_x)
    y_ssm, = run("ssd_gate_fwd", _ssd_gate_fwd, y2, xbc, proj, d_lanes, ssm_nw)
    y_sc = _sc_fwd(proj, sc_w, sc_nw)
    ycat = jnp.concatenate([y_ssm, y_sc], axis=1)
    w_out_g = plan.get("w_out").reshape(D_MODEL, D_MODEL)
    mix, = run("out_proj", _mm_nn, "out_proj", ycat, w_out_g, F32)
    x1, h2 = run("ln1_fwd", _ln1_fwd, x, mix, mod6, ln1_g, ln1_b)

    def relu2(acc):
        u = acc.astype(BF16)
        r = jnp.maximum(acc, 0.0)
        return u, r * r

    w_up3 = plan.get("w_up")
    nper = w_up3.shape[2]
    tm = _tile(s, 1024)
    tn = 1024
    nb = nper // tn
    u_spec = pl.BlockSpec((tm, tn), lambda i, j, kk: (i, j))
    u, ff = run(
        "up_proj", _matmul, "up_proj", h2, w_up3, mode="nn", grid=(s // tm, D_FF // tn, 1),
        a_spec=pl.BlockSpec((tm, D_MODEL), lambda i, j, kk: (i, 0)),
        b_spec=pl.BlockSpec((None, D_MODEL, tn), lambda i, j, kk: (j // nb, 0, j % nb)),
        out_shapes=[jax.ShapeDtypeStruct((s, D_FF), BF16)] * 2, out_specs=[u_spec, u_spec],
        acc_shape=(tm, tn), epilogue=relu2)
    w_down_g = plan.get("w_down").reshape(D_FF, D_MODEL)
    f2 = _mm_nn("down_proj", ff, w_down_g, F32)[0][0]
    df2, dx1a, loss, g_ln2_g, g_ln2_b, dgate2 = _ln2_loss_bwd(x1, f2, tgt, mod6, ln2_g, ln2_b)

    def relu_grad(acc, uu):
        return (acc * (2.0 * jnp.maximum(uu.astype(F32), 0.0)),)

    du = _mm_nt("d_ff", df2, w_down_g, BF16, epilogue=relu_grad, extras=(u,))[0][0]
    plan.put("g_down", _mm_tn("g_w_down", ff, df2, BF16)[0][0].reshape(N_DEV, D_FF // N_DEV, D_MODEL))
    g_up, = run(
        "g_w_up", _matmul, "g_w_up", h2, du, mode="tn",
        grid=(D_MODEL // 1024, D_FF // tn, s // _tile(s, 2048)),
        a_spec=pl.BlockSpec((_tile(s, 2048), 1024), lambda i, j, kk: (kk, i)),
        b_spec=pl.BlockSpec((_tile(s, 2048), tn), lambda i, j, kk: (kk, j)),
        out_shapes=[jax.ShapeDtypeStruct((N_DEV, D_MODEL, nper), BF16)],
        out_specs=[pl.BlockSpec((None, 1024, tn), lambda i, j, kk: (j // nb, i, j % nb))],
        acc_shape=(1024, tn))
    plan.put("g_up", g_up)
    dh2, = run(
        "d_h2", _matmul, "d_h2", du, w_up3, mode="nt", grid=(s // tm, D_MODEL // 1024, D_FF // nper),
        a_spec=pl.BlockSpec((tm, nper), lambda i, j, kk: (i, kk)),
        b_spec=pl.BlockSpec((None, 1024, nper), lambda i, j, kk: (kk, j, 0)),
        out_shapes=[jax.ShapeDtypeStruct((s, D_MODEL), F32)],
        out_specs=[pl.BlockSpec((tm, 1024), lambda i, j, kk: (i, j))],
        acc_shape=(tm, 1024))
    dmix, dxa, dscale2, dshift2, g_ln1_g, g_ln1_b, dgate1 = _ln1_bwd(dh2, dx1a, x1, x, mix, mod6, ln1_g)

    dycat = _mm_nt("d_ycat", dmix, w_out_g, F32)[0][0]
    plan.put("g_out", run("g_w_out", _mm_tn, "g_w_out", ycat, dmix, BF16)[0].reshape(
        N_DEV, D_MODEL // N_DEV, D_MODEL))
    duh, dub, duc, g_sc_w, g_sc_nw = _sc_bwd(proj, dycat, sc_w, sc_nw)
    dyc, dz, dd_lanes, g_ssm_nw = run("ssd_gate_bwd", _ssd_gate_bwd, y2, xbc, proj, dycat, d_lanes, ssm_nw)
    dxs2, db2, dc2, ddtx, dacx, dax = run("ssd_bwd", _ssd_bwd, xbc, dtx, acx, a_x, states, dyc)
    n_bc = GROUPS * N_STATE
    du_xs, gw_xs, gb_xs = _conv_silu_bwd("conv_bwd_x", proj, conv_w, conv_b, 0, D_SSM, [dxs2],
                                         scaled=(dyc, d_lanes))
    du_b, gw_b, gb_b = _conv_silu_bwd("conv_bwd_b", proj, conv_w, conv_b, D_SSM, n_bc, [db2])
    du_c, gw_c, gb_c = _conv_silu_bwd("conv_bwd_c", proj, conv_w, conv_b, D_SSM + n_bc, n_bc, [dc2])
    du_dt, g_bias_all, g_a_sums = _dt_prep_bwd(proj_dt, bias_all, a_all, ddtx, dacx)

    sections = [dz, du_xs, du_b, du_c, du_dt[:, :2 * HEADS], duh, dub, duc]
    dproj3 = jnp.stack([_columns(sections, k * D_IN_SHARD, D_IN_SHARD) for k in range(N_DEV)])
    tk = _tile(s, 2048)
    g_in, = run(
        "g_w_in", _matmul, "g_w_in", h1, dproj3, mode="tn", grid=(N_DEV, D_MODEL // 1024, s // tk),
        a_spec=pl.BlockSpec((tk, 1024), lambda i, j, kk: (kk, j)),
        b_spec=pl.BlockSpec((None, tk, D_IN_SHARD), lambda i, j, kk: (i, kk, 0)),
        out_shapes=[jax.ShapeDtypeStruct((N_DEV, D_MODEL, D_IN_SHARD), BF16)],
        out_specs=[pl.BlockSpec((None, 1024, D_IN_SHARD), lambda i, j, kk: (i, j, 0))],
        acc_shape=(1024, D_IN_SHARD))
    plan.put("g_in", g_in)
    plan.hook("after_g_w_in")
    dh1, = run(
        "d_h1", _matmul, "d_h1", dproj3, w_in_g, mode="nt", grid=(s // tm, D_MODEL // 1024, N_DEV),
        a_spec=pl.BlockSpec((None, tm, D_IN_SHARD), lambda i, j, kk: (kk, i, 0)),
        b_spec=pl.BlockSpec((None, 1024, D_IN_SHARD), lambda i, j, kk: (kk, j, 0)),
        out_shapes=[jax.ShapeDtypeStruct((s, D_MODEL), F32)],
        out_specs=[pl.BlockSpec((tm, 1024), lambda i, j, kk: (i, j))],
        acc_shape=(tm, 1024))
    grad_x, dscale1, dshift1 = _grad_x(dxa, dh1, x, mod6)

    dmod = jnp.concatenate([dshift1, dscale1, dgate1, dshift2, dscale2, dgate2], axis=1)
    g_a_direct = dax.reshape(2, HEADS, HEAD_DIM).sum(axis=-1).reshape(1, 2 * HEADS)
    g_a_all = g_a_sums + _pad_lanes(g_a_direct, 128)
    small = {
        "dmod": dmod,
        "ssm_conv_w": jnp.concatenate([gw_xs, gw_b, gw_c], axis=1),
        "ssm_conv_b": jnp.concatenate([gb_xs, gb_b, gb_c], axis=1),
        "ssm_dt_bias_f": g_bias_all[:, :HEADS],
        "ssm_dt_bias_b": g_bias_all[:, HEADS:2 * HEADS],
        "ssm_a_log_f": (g_a_all * a_all)[:, :HEADS],
        "ssm_a_log_b": (g_a_all * a_all)[:, HEADS:2 * HEADS],
        "ssm_d": dd_lanes.reshape(HEADS, HEAD_DIM).sum(axis=1).reshape(1, HEADS),
        "ssm_norm_w": g_ssm_nw,
        "sc_conv_w": g_sc_w,
        "sc_norm_w": g_sc_nw,
        "ln1_g": g_ln1_g, "ln1_b": g_ln1_b, "ln2_g": g_ln2_g, "ln2_b": g_ln2_b,
    }
    return loss, grad_x, small


_SUMMED = [("ssm_conv_b", D_XBC), ("ssm_dt_bias_f", HEADS), ("ssm_dt_bias_b", HEADS),
           ("ssm_a_log_f", HEADS), ("ssm_a_log_b", HEADS), ("ssm_d", HEADS),
           ("ssm_norm_w", D_SSM), ("sc_norm_w", D_SC),
           ("ln1_g", D_MODEL), ("ln1_b", D_MODEL), ("ln2_g", D_MODEL), ("ln2_b", D_MODEL)]


def _round_up(n, k):
    return (n + k - 1) // k * k


def _w_in_sections(w_in_g, half):
    dt_lo = D_SSM + D_XBC
    k_dt = dt_lo // D_IN_SHARD
    cut = dt_lo - k_dt * D_IN_SHARD
    rest = (k_dt + 1) * D_IN_SHARD - dt_lo
    blocks = [w_in_g[k, half] for k in range(N_DEV)]
    dt = jnp.concatenate([blocks[k_dt][:, cut:], blocks[k_dt + 1][:, :2 * HEADS - rest]], axis=1)
    blocks[k_dt] = blocks[k_dt][:, :cut]
    blocks[k_dt + 1] = blocks[k_dt + 1][:, 2 * HEADS - rest:]
    return jnp.concatenate(blocks, axis=1), _pad_lanes(dt, 128)


def kernel(x, c, w_ada, b_ada, w_in, ssm_conv_w, ssm_conv_b, ssm_dt_bias_f, ssm_dt_bias_b, ssm_a_log_f, ssm_a_log_b, ssm_d, ssm_norm_w, sc_conv_w, sc_norm_w, w_out, ln1_g, ln1_b, w_up, w_down, ln2_g, ln2_b, loss_target, m_w_ada, m_b_ada, m_w_in, m_ssm_conv_w, m_ssm_conv_b, m_ssm_dt_bias_f, m_ssm_dt_bias_b, m_ssm_a_log_f, m_ssm_a_log_b, m_ssm_d, m_ssm_norm_w, m_sc_conv_w, m_sc_norm_w, m_w_out, m_ln1_g, m_ln1_b, m_w_up, m_w_down, m_ln2_g, m_ln2_b, v_w_ada, v_b_ada, v_w_in, v_ssm_conv_w, v_ssm_conv_b, v_ssm_dt_bias_f, v_ssm_dt_bias_b, v_ssm_a_log_f, v_ssm_a_log_b, v_ssm_d, v_ssm_norm_w, v_sc_conv_w, v_sc_norm_w, v_w_out, v_ln1_g, v_ln1_b, v_w_up, v_w_down, v_ln2_g, v_ln2_b):
    args = dict(locals())
    xi, yi, ci = _my_pos()
    me = 4 * xi + 2 * yi + ci
    pos = jnp.stack([xi, yi, ci]).astype(jnp.int32)
    s = x.shape[1]

    n_cw, n_sw = SSM_CONV * D_XBC // N_DEV, SC_CONV * D_SC // N_DEV
    vec = jnp.concatenate([c, ssm_conv_w[0].reshape(1, n_cw), sc_conv_w[0].reshape(1, n_sw)], axis=1)
    vec = _pad_lanes(vec, 8192)
    gath = _gather_vec("gather_c_conv", vec)
    c_all = gath[:, :D_MODEL]
    conv_w = gath[:, D_MODEL:D_MODEL + n_cw].reshape(N_DEV, SSM_CONV, D_XBC // N_DEV)
    conv_w = conv_w.transpose(1, 0, 2).reshape(SSM_CONV, D_XBC)
    sc_w = gath[:, D_MODEL + n_cw:D_MODEL + n_cw + n_sw].reshape(N_DEV, SC_CONV, D_SC // N_DEV)
    sc_w = sc_w.transpose(1, 0, 2).reshape(SC_CONV, D_SC)
    c16 = jnp.pad(c_all, ((0, 8), (0, 0)))

    n_ada = w_ada.shape[2]
    mod_cols = _ada_fwd(c16, w_ada[0])[:N_DEV]
    mod_all = _run_jobs("gather_mod", [_GatherJob(mod_cols, pltpu.VMEM)])[0]
    mod = lax.dynamic_index_in_dim(mod_all, me, axis=1, keepdims=False)
    mod = mod.reshape(1, N_MOD * D_MODEL) + b_ada

    out = {}

    def adamw(plan, tag):
        name = "w_" + tag
        if tag in two_leg:
            others = [(plan.get("ra_" + tag), 1), (plan.get("rb_" + tag), None)]
        else:
            others = [(plan.get("r2_" + tag), k) for k in range(3)]
        res = plan.run("rs_adamw_" + tag, _reduce_adamw, "rs_adamw_" + tag, plan.get("g_" + tag),
                       plan.get("r1_" + tag), others, pos, args[name][0], args["m_" + name][0],
                       args["v_" + name][0])
        out[name] = tuple(a[None] for a in res)

    two_leg = ("down", "up")
    hosted = {
        "in_proj_a": [(("lead", 1, 2), "w_in")],
        "in_proj_b": [("gather", "w_out")],
        "conv_silu_fwd": [(("part", 0, 1, 8), "w_up")],
        "dt_prep": [(("part", 1, 1, 8), "w_up")],
        "ssd_fwd": [(("part", 2, 2, 8), "w_up")],
        "ssd_gate_fwd": [(("part", 4, 1, 8), "w_up")],
        "out_proj": [(("part", 5, 2, 8), "w_up")],
        "ln1_fwd": [(("part", 7, 1, 8), "w_up")],
        "up_proj": [("gather", "w_down")],
        "g_w_up": [("rs1", "down")],
        "d_h2": [("rs2a", "down"), ("rs1", "up")],
        "g_w_out": [("rs2b", "down")],
        "ssd_gate_bwd": [("rs1", "out")],
        "ssd_bwd": [("rs2a", "up")],
        "g_w_in": [("rs2b", "up"), ("rs2", "out")],
        "d_h1": [("rs2", "in")],
    }

    def sibling_exchange_in(plan):
        plan.put("r1_in", _run_jobs("rs_sibling_in", [_SiblingJob(plan.get("g_in"))])[0])

    store = {"pos": pos}
    shard_in = _cast_bf16("cast_w_in", w_in[0]).reshape(2, D_MODEL // 2, D_IN_SHARD)
    casts, (part_in,) = _cast_many("cast_w_rest", [w_out[0], w_up[0], w_down[0]],
                                   jobs=[_GatherJob(shard_in, lead=0)])
    store.update(shard_w_in=shard_in, part_w_in=part_in, shard_w_out=casts[0], shard_w_up=casts[1],
                 shard_w_down=casts[2])
    plan = _Plan(hosted, store, hooks={"after_g_w_in": sibling_exchange_in}, two_leg=two_leg)
    loss, grad_x, small = _local_step(
        plan, x[0], loss_target[0], mod, conv_w, ssm_conv_b, ssm_dt_bias_f, ssm_dt_bias_b,
        ssm_a_log_f, ssm_a_log_b, ssm_d, ssm_norm_w, sc_w, sc_norm_w, ln1_g, ln1_b, ln2_g, ln2_b)
    for tag in ("down", "up", "out", "in"):
        adamw(plan, tag)

    parts = [small["dmod"]]
    parts += [_pad_lanes(small[n], _round_up(w, 128)) for n, w in _SUMMED]
    parts += [small["ssm_conv_w"].reshape(1, SSM_CONV * D_XBC), small["sc_conv_w"].reshape(1, SC_CONV * D_SC)]
    parts += [loss]
    gvec = jnp.concatenate(parts, axis=1)
    n_vec = _round_up(gvec.shape[1], 8192)
    gall = _gather_vec("gather_small_grads", _pad_lanes(gvec, n_vec))

    def shard_cols(full, k, per):
        return lax.dynamic_slice_in_dim(full.reshape(k, N_DEV, per), me, 1, axis=1).reshape(1, k * per)

    def placed(vals, n_rows=1):
        return jnp.concatenate(vals, axis=1)

    n_mod = N_MOD * D_MODEL
    ws, ms, vs = [b_ada], [m_b_ada], [v_b_ada]
    for n, w in _SUMMED:
        pw = _round_up(w, 128)
        ws.append(_pad_lanes(args[n], pw))
        ms.append(_pad_lanes(args["m_" + n], pw))
        vs.append(_pad_lanes(args["v_" + n], pw))

    def full_rows(shard, k, per):
        z = jnp.zeros((k, N_DEV, per), F32)
        z = lax.dynamic_update_slice_in_dim(z, shard.reshape(k, 1, per), me, axis=1)
        return z.reshape(1, k * N_DEV * per)

    for nm, k, per in (("ssm_conv_w", SSM_CONV, D_XBC // N_DEV), ("sc_conv_w", SC_CONV, D_SC // N_DEV)):
        ws.append(full_rows(args[nm][0], k, per))
        ms.append(full_rows(args["m_" + nm][0], k, per))
        vs.append(full_rows(args["v_" + nm][0], k, per))
    tail = n_vec - sum(a.shape[1] for a in ws)
    ws.append(jnp.zeros((1, tail), F32))
    ms.append(jnp.zeros((1, tail), F32))
    vs.append(jnp.ones((1, tail), F32))
    g_s, d_s, m_s, v_s = _sum8_adamw(gall, placed(ws), placed(ms), placed(vs))

    off = 0

    def take(w):
        nonlocal off
        sl = tuple(a[:, off:off + w] for a in (g_s, d_s, m_s, v_s))
        off += _round_up(w, 128)
        return sl

    out["b_ada"] = take(n_mod)
    for n, w in _SUMMED:
        out[n] = take(w)
    for nm, k, per in (("ssm_conv_w", SSM_CONV, D_XBC // N_DEV), ("sc_conv_w", SC_CONV, D_SC // N_DEV)):
        full = take(k * N_DEV * per)
        out[nm] = tuple(shard_cols(a, k, per).reshape(1, k, per) for a in full)
    loss_total = g_s[0, off]

    dmod_all = gall[:, :n_mod]
    dmod_cols = lax.dynamic_slice_in_dim(dmod_all.reshape(N_DEV, N_DEV, n_ada), me, 1, axis=1)
    dmod16 = jnp.pad(dmod_cols.reshape(N_DEV, n_ada), ((0, 8), (0, 0)))
    out["w_ada"] = tuple(a[None] for a in _ada_bwd_adamw(c16, dmod16, w_ada[0], m_w_ada[0], v_w_ada[0]))

    names = ['w_ada', 'b_ada', 'w_in', 'ssm_conv_w', 'ssm_conv_b', 'ssm_dt_bias_f', 'ssm_dt_bias_b',
             'ssm_a_log_f', 'ssm_a_log_b', 'ssm_d', 'ssm_norm_w', 'sc_conv_w', 'sc_norm_w', 'w_out',
             'ln1_g', 'ln1_b', 'w_up', 'w_down', 'ln2_g', 'ln2_b']
    res = [loss_total, grad_x[None]]
    for k in range(4):
        res += [out[n][k] for n in names]
    return tuple(res)
```

```python
import functools

import jax
import jax.numpy as jnp
from jax import lax
from jax.experimental import pallas as pl
from jax.experimental.pallas import tpu as pltpu

F32 = jnp.float32
BF16 = jnp.bfloat16
MESH = pl.DeviceIdType.MESH

N_DEV = 8
D_MODEL = 4096
D_SSM = 2048
D_SC = 2048
HEADS = 32
HEAD_DIM = 64
GROUPS = 8
GROUP_W = D_SSM // GROUPS
HEADS_PER_GROUP = 4
N_STATE = 128
CHUNK = 128
SSM_CONV = 5
SC_CONV = 3
SC_GROUP_W = 128
D_XBC = 4096
D_FF = 16384
D_IN = 12352
D_IN_SHARD = D_IN // N_DEV
D_MAIN = 12288
N_MOD = 6
ALPHA = (2 * 1) ** 0.25
LN_EPS = 1e-5
RMS_EPS = 1e-5
ADAM_LR = 0.001
ADAM_B1 = 0.9
ADAM_B2 = 0.999
ADAM_EPS = 1e-08
ADAM_WD = 0.01
ADAM_STEP = 10

VMEM_LIMIT = 56 * 1024 * 1024
HALO = 8

_DN = {
    "nn": (((1,), (0,)), ((), ())),
    "nt": (((1,), (1,)), ((), ())),
    "tn": (((0,), (0,)), ((), ())),
}


def _cparams(sem=None):
    return pltpu.CompilerParams(dimension_semantics=sem, vmem_limit_bytes=VMEM_LIMIT)


def _my_pos():
    return lax.axis_index("x"), lax.axis_index("y"), lax.axis_index("c")


def _other_chips(x, y):
    return [(1 - x, y), (x, 1 - y), (1 - x, 1 - y)]


class _GatherJob:
    n_remote = 7

    def __init__(self, shard, space=pl.ANY, rows=None, lead=None, into=None):
        self.ins = (shard,) if into is None else (shard, into)
        self.alias = None if into is None else 1
        self.out_shapes = (jax.ShapeDtypeStruct((N_DEV,) + shard.shape, shard.dtype),)
        self.space = space
        self.rows, self.lead = rows, lead

    def _piece(self, ref):
        if self.rows is not None:
            return ref.at[pl.ds(*self.rows)]
        return ref if self.lead is None else ref.at[self.lead]

    def _parts(self, ins, outs, send, recv, local):
        x_ref, out_ref = self._piece(ins[0]), outs[0]
        x, y, c = _my_pos()
        me, sibling = (x, y, c), (x, y, 1 - c)
        chips = _other_chips(x, y)

        def slab(px, py, pc):
            return self._piece(out_ref.at[4 * px + 2 * py + pc])

        def copy(k, block, to, src=None):
            return pltpu.make_async_remote_copy(
                src_ref=slab(*block) if src is None else src, dst_ref=slab(*block),
                send_sem=send.at[k], recv_sem=recv.at[k], device_id=to, device_id_type=MESH)

        mine = pltpu.make_async_copy(x_ref, slab(*me), local.at[0])
        own = [copy(0, me, sibling, src=x_ref), copy(1, me, (*chips[0], c), src=x_ref),
               copy(2, me, (*chips[1], c), src=x_ref)]
        relayed = (x + (1 - c) * (1 - 2 * x), y + c * (1 - 2 * y), c)
        relay = copy(3, relayed, (x + c * (1 - 2 * x), y + (1 - c) * (1 - 2 * y), c))
        hand = [copy(4 + j, (*chip, c), sibling) for j, chip in enumerate(chips)]
        landed = [copy(1 + j, (*chip, c), me) for j, chip in enumerate(chips)]
        handed = [copy(0, sibling, me)] + [copy(4 + j, (*chip, 1 - c), me) for j, chip in enumerate(chips)]
        return mine, own, relay, hand, landed, handed

    def start(self, *refs):
        mine, own, _, _, _, _ = self._parts(*refs)
        mine.start()
        for cp in own:
            cp.start()

    def mid(self, *refs):
        _, _, relay, hand, landed, _ = self._parts(*refs)
        landed[0].wait_recv()
        landed[1].wait_recv()
        relay.start()
        hand[0].start()
        hand[1].start()

    def finish(self, *refs):
        mine, own, relay, hand, landed, handed = self._parts(*refs)
        landed[2].wait_recv()
        hand[2].start()
        for cp in handed:
            cp.wait_recv()
        for cp in own + [relay] + hand:
            cp.wait_send()
        mine.wait()


class _SiblingJob:
    n_remote = 4
    space = pl.ANY

    def __init__(self, g):
        self.ins = (g,)
        self.out_shapes = (jax.ShapeDtypeStruct((4,) + g.shape[1:], g.dtype),)

    def _copies(self, ins, outs, send, recv, local):
        x, y, c = _my_pos()
        return [pltpu.make_async_remote_copy(
            src_ref=ins[0].at[2 * j + (1 - c)], dst_ref=outs[0].at[j],
            send_sem=send.at[j], recv_sem=recv.at[j],
            device_id=(x, y, 1 - c), device_id_type=MESH) for j in range(4)]

    def start(self, *refs):
        for cp in self._copies(*refs):
            cp.start()

    def mid(self, *refs):
        pass

    def finish(self, *refs):
        for cp in self._copies(*refs):
            cp.wait()


class _ChipsJob:
    n_remote = 3
    space = pl.ANY

    def __init__(self, p):
        self.ins = (p,)
        self.out_shapes = (jax.ShapeDtypeStruct(p.shape, p.dtype),)

    def _copies(self, ins, outs, send, recv, local):
        x, y, c = _my_pos()
        return [pltpu.make_async_remote_copy(
            src_ref=ins[0].at[k], dst_ref=outs[0].at[k],
            send_sem=send.at[k], recv_sem=recv.at[k],
            device_id=(px, py, c), device_id_type=MESH) for k, (px, py) in enumerate(_other_chips(x, y))]

    def start(self, *refs):
        for cp in self._copies(*refs):
            cp.start()

    def mid(self, *refs):
        pass

    def finish(self, *refs):
        for cp in self._copies(*refs):
            cp.wait()


def _relay_route(x, y, c):
    first = (x + c * (1 - 2 * x), y + (1 - c) * (1 - 2 * y))
    second = (x + (1 - c) * (1 - 2 * x), y + c * (1 - 2 * y))
    return first, second


class _RelayFirstJob:
    n_remote = 2
    space = pl.ANY

    def __init__(self, p):
        self.ins = (p,)
        self.out_shapes = (jax.ShapeDtypeStruct(p.shape, p.dtype),)

    def _copies(self, ins, outs, send, recv, local):
        x, y, c = _my_pos()
        (fx, fy), _ = _relay_route(x, y, c)
        return [pltpu.make_async_remote_copy(
            src_ref=ins[0].at[k], dst_ref=outs[0].at[k], send_sem=send.at[k], recv_sem=recv.at[k],
            device_id=(fx, fy, c), device_id_type=MESH) for k in range(2)]

    def start(self, *refs):
        for cp in self._copies(*refs):
            cp.start()

    def mid(self, *refs):
        pass

    def finish(self, *refs):
        for cp in self._copies(*refs):
            cp.wait()


class _RelaySecondJob:
    n_remote = 1
    space = pl.ANY

    def __init__(self, q):
        self.ins = (q,)
        self.out_shapes = (jax.ShapeDtypeStruct(q.shape, q.dtype),)

    def _copy(self, ins, outs, send, recv, local):
        x, y, c = _my_pos()
        _, (sx, sy) = _relay_route(x, y, c)
        return pltpu.make_async_remote_copy(
            src_ref=ins[0], dst_ref=outs[0], send_sem=send.at[0], recv_sem=recv.at[0],
            device_id=(sx, sy, c), device_id_type=MESH)

    def start(self, *refs):
        self._copy(*refs).start()

    def mid(self, *refs):
        pass

    def finish(self, *refs):
        self._copy(*refs).wait()


MID_STEP_FRACTION = 0.64


def _call(name, body, *, grid, in_specs, out_specs, out_shape, args, scratch_shapes=(), sem=None,
          jobs=(), n_prefetch=0):
    out_shape, out_specs, in_specs = list(out_shape), list(out_specs), list(in_specs)
    scratch_shapes = list(scratch_shapes)
    jobs = list(jobs)
    n_in, n_out, n_scr = len(in_specs), len(out_shape), len(scratch_shapes)
    job_ins = [a for j in jobs for a in j.ins]
    job_outs = [o for j in jobs for o in j.out_shapes]
    steps = 1
    for n in grid:
        steps *= n
    mid_step = min(steps - 1, int(steps * MID_STEP_FRACTION))

    def wrapped(*refs):
        pre, refs = refs[:n_prefetch], refs[n_prefetch:]
        core_in, refs = refs[:n_in], refs[n_in:]
        jin, refs = refs[:len(job_ins)], refs[len(job_ins):]
        core_out, refs = refs[:n_out], refs[n_out:]
        jout, refs = refs[:len(job_outs)], refs[len(job_outs):]
        core_scr, sems = refs[:n_scr], refs[n_scr:]
        lin = 0
        for ax, n in enumerate(grid):
            lin = lin * n + pl.program_id(ax)
        bound = []
        for j in jobs:
            ji, jin = jin[:len(j.ins)], jin[len(j.ins):]
            jo, jout = jout[:len(j.out_shapes)], jout[len(j.out_shapes):]
            (send, recv, local), sems = sems[:3], sems[3:]
            bound.append((j, (ji, jo, send, recv, local)))

        if jobs:
            @pl.when(lin == 0)
            def _():
                for j, r in bound:
                    j.start(*r)

        body(*pre, *core_in, *core_out, *core_scr)

        if jobs:
            @pl.when(lin == mid_step)
            def _():
                for j, r in bound:
                    j.mid(*r)

            @pl.when(lin == steps - 1)
            def _():
                for j, r in bound:
                    j.finish(*r)

    sem_shapes = []
    for j in jobs:
        sem_shapes += [pltpu.SemaphoreType.DMA((j.n_remote,)), pltpu.SemaphoreType.DMA((j.n_remote,)),
                       pltpu.SemaphoreType.DMA((1,))]
    if jobs:
        sem = tuple("arbitrary" for _ in grid)
    aliases = {}
    in_at, out_at = len(args), n_out
    for j in jobs:
        if getattr(j, "alias", None) is not None:
            aliases[in_at + j.alias] = out_at
        in_at, out_at = in_at + len(j.ins), out_at + len(j.out_shapes)
    res = pl.pallas_call(
        wrapped, name=name, input_output_aliases=aliases,
        grid_spec=pltpu.PrefetchScalarGridSpec(
            num_scalar_prefetch=n_prefetch, grid=tuple(grid),
            in_specs=in_specs + [pl.BlockSpec(memory_space=j.space) for j in jobs for _ in j.ins],
            out_specs=out_specs + [pl.BlockSpec(memory_space=j.space) for j in jobs for _ in j.out_shapes],
            scratch_shapes=scratch_shapes + sem_shapes),
        out_shape=out_shape + job_outs,
        compiler_params=_cparams(sem),
    )(*args, *job_ins)
    res = list(res) if isinstance(res, (list, tuple)) else [res]
    return res[:n_out], res[n_out:]


def _run_jobs(name, jobs):
    return _call(name, lambda: None, grid=(1,), in_specs=[], out_specs=[], out_shape=[], args=(),
                 jobs=jobs)[1]


def _matmul(name, a, b, *, mode, grid, a_spec, b_spec, out_shapes, out_specs, acc_shape,
            epilogue=None, extras=(), extra_specs=(), jobs=()):
    nk = grid[2]
    n_extra = len(extras)
    n_out = len(out_shapes)

    def body(*refs):
        a_ref, b_ref = refs[0], refs[1]
        extra_refs = refs[2:2 + n_extra]
        out_refs = refs[2 + n_extra:2 + n_extra + n_out]
        part = lax.dot_general(a_ref[...], b_ref[...], _DN[mode], preferred_element_type=F32)

        def finish(acc):
            outs = epilogue(acc, *[r[...] for r in extra_refs]) if epilogue else (acc,)
            for o_ref, o in zip(out_refs, outs):
                o_ref[...] = o.astype(o_ref.dtype)

        if nk == 1:
            finish(part)
        else:
            acc_ref = refs[-1]
            k = pl.program_id(2)

            @pl.when(k == 0)
            def _():
                acc_ref[...] = part

            @pl.when(k > 0)
            def _():
                acc_ref[...] += part

            @pl.when(k == nk - 1)
            def _():
                finish(acc_ref[...])

    scratch = [pltpu.VMEM(acc_shape, F32)] if nk > 1 else []
    return _call(name, body, grid=grid, in_specs=[a_spec, b_spec, *extra_specs],
                 out_specs=out_specs, out_shape=out_shapes, scratch_shapes=scratch,
                 sem=("parallel", "parallel", "arbitrary"), args=(a, b, *extras), jobs=jobs)


def _tile(n, pref):
    t = min(n, pref)
    assert n % t == 0, (n, t)
    return t


def _mm_nn(name, a, b, out_dtype, tn=1024, tk=None, epilogue=None, out_dtypes=None, jobs=(),
           a_col0=0, extras=()):
    m, k = a.shape[0], b.shape[0]
    n = b.shape[1]
    tm, tn = _tile(m, 1024), _tile(n, tn)
    tk = _tile(k, tk or 4096)
    k0 = a_col0 // tk
    assert a_col0 % tk == 0
    out_dtypes = out_dtypes or (out_dtype,)
    o_spec = pl.BlockSpec((tm, tn), lambda i, j, kk: (i, j))
    return _matmul(
        name, a, b, mode="nn", grid=(m // tm, n // tn, k // tk),
        a_spec=pl.BlockSpec((tm, tk), lambda i, j, kk: (i, k0 + kk)),
        b_spec=pl.BlockSpec((tk, tn), lambda i, j, kk: (kk, j)),
        out_shapes=[jax.ShapeDtypeStruct((m, n), dt) for dt in out_dtypes],
        out_specs=[o_spec for _ in out_dtypes],
        acc_shape=(tm, tn), epilogue=epilogue, jobs=jobs,
        extras=extras, extra_specs=[o_spec for _ in extras])


def _mm_nt(name, a, b, out_dtype, epilogue=None, extras=(), tk=None, jobs=()):
    m, k = a.shape
    n = b.shape[0]
    tm, tn = _tile(m, 1024), _tile(n, 1024)
    tk = _tile(k, tk or 4096)
    o_spec = pl.BlockSpec((tm, tn), lambda i, j, kk: (i, j))
    return _matmul(
        name, a, b, mode="nt", grid=(m // tm, n // tn, k // tk),
        a_spec=pl.BlockSpec((tm, tk), lambda i, j, kk: (i, kk)),
        b_spec=pl.BlockSpec((tn, tk), lambda i, j, kk: (j, kk)),
        out_shapes=[jax.ShapeDtypeStruct((m, n), out_dtype)],
        out_specs=[o_spec], acc_shape=(tm, tn), epilogue=epilogue,
        extras=extras, extra_specs=[o_spec for _ in extras], jobs=jobs)


def _mm_tn(name, a, b, out_dtype, tk=2048, jobs=()):
    k, m = a.shape
    n = b.shape[1]
    tm, tn = _tile(m, 1024), _tile(n, 1024)
    tk = _tile(k, tk)
    return _matmul(
        name, a, b, mode="tn", grid=(m // tm, n // tn, k // tk),
        a_spec=pl.BlockSpec((tk, tm), lambda i, j, kk: (kk, i)),
        b_spec=pl.BlockSpec((tk, tn), lambda i, j, kk: (kk, j)),
        out_shapes=[jax.ShapeDtypeStruct((m, n), out_dtype)],
        out_specs=[pl.BlockSpec((tm, tn), lambda i, j, kk: (i, j))],
        acc_shape=(tm, tn), jobs=jobs)


def _cast_bf16(name, w):
    r, c = w.shape
    tr = _tile(r, 512)

    def body(w_ref, o_ref):
        o_ref[...] = w_ref[...].astype(BF16)

    return pl.pallas_call(
        body, name=name, grid=(r // tr,),
        in_specs=[pl.BlockSpec((tr, c), lambda i: (i, 0))],
        out_specs=pl.BlockSpec((tr, c), lambda i: (i, 0)),
        out_shape=jax.ShapeDtypeStruct((r, c), BF16),
        compiler_params=_cparams(("parallel",)),
    )(w)


def _cast_many(name, ws, steps=8, jobs=()):
    def body(*refs):
        for w_ref, o_ref in zip(refs[:len(ws)], refs[len(ws):]):
            o_ref[...] = w_ref[...].astype(BF16)

    specs = [pl.BlockSpec((w.shape[0] // steps, w.shape[1]), lambda i: (i, 0)) for w in ws]
    return _call(name, body, grid=(steps,), in_specs=specs, out_specs=specs,
                 out_shape=[jax.ShapeDtypeStruct(w.shape, BF16) for w in ws],
                 sem=("parallel",), args=tuple(ws), jobs=jobs)


def _chip_of(pos, which):
    x, y, c = pos[0], pos[1], pos[2]
    first, second = _relay_route(x, y, c)
    chips = _other_chips(x, y) + [first, second, (x, y)]
    px, py = chips[which]
    return 2 * px + py


def _pair_add(name, g, r1, pos, dests):
    _, r, cdim = g.shape
    tr = _tile(r, 512)

    def chip(k, pos):
        idx = _chip_of(pos, dests[-1])
        for n in range(len(dests) - 2, -1, -1):
            idx = jnp.where(k == n, _chip_of(pos, dests[n]), idx)
        return idx

    def body(pos_ref, g_ref, r_ref, o_ref):
        o_ref[...] = (g_ref[...].astype(F32) + r_ref[...].astype(F32)).astype(o_ref.dtype)

    return pl.pallas_call(
        body, name=name,
        grid_spec=pltpu.PrefetchScalarGridSpec(
            num_scalar_prefetch=1, grid=(len(dests), r // tr),
            in_specs=[pl.BlockSpec((None, tr, cdim), lambda k, i, pos: (2 * chip(k, pos) + pos[2], i, 0)),
                      pl.BlockSpec((None, tr, cdim), lambda k, i, pos: (chip(k, pos), i, 0))],
            out_specs=pl.BlockSpec((None, tr, cdim), lambda k, i, pos: (k, i, 0))),
        out_shape=jax.ShapeDtypeStruct((len(dests), r, cdim), BF16),
        compiler_params=_cparams(("parallel", "parallel")),
    )(pos, g, r1)


def _adamw_math(w, g, m, v):
    m = ADAM_B1 * m + (1.0 - ADAM_B1) * g
    v = ADAM_B2 * v + (1.0 - ADAM_B2) * jnp.square(g)
    m_hat = m / (1.0 - ADAM_B1 ** ADAM_STEP)
    v_hat = v / (1.0 - ADAM_B2 ** ADAM_STEP)
    delta = -ADAM_LR * (m_hat / (jnp.sqrt(v_hat) + ADAM_EPS) + ADAM_WD * w)
    return delta, m, v


def _relay_add(name, g, r1, ra, pos):
    _, r, cdim = g.shape
    tr = _tile(r, 512)

    def body(pos_ref, g_ref, r1_ref, ra_ref, o_ref):
        q = g_ref[...].astype(F32) + r1_ref[...].astype(F32) + ra_ref[...].astype(F32)
        o_ref[...] = q.astype(o_ref.dtype)

    return pl.pallas_call(
        body, name=name,
        grid_spec=pltpu.PrefetchScalarGridSpec(
            num_scalar_prefetch=1, grid=(r // tr,),
            in_specs=[pl.BlockSpec((None, tr, cdim), lambda i, pos: (2 * _chip_of(pos, 4) + pos[2], i, 0)),
                      pl.BlockSpec((None, tr, cdim), lambda i, pos: (_chip_of(pos, 4), i, 0)),
                      pl.BlockSpec((None, tr, cdim), lambda i, pos: (0, i, 0))],
            out_specs=pl.BlockSpec((tr, cdim), lambda i, pos: (i, 0))),
        out_shape=jax.ShapeDtypeStruct((r, cdim), BF16),
        compiler_params=_cparams(("parallel",)),
    )(pos, g, r1, ra)


def _reduce_adamw(name, g8, r1, others, pos, w, m, v, jobs=()):
    r, cdim = w.shape
    tr = _tile(r, 128 if cdim >= D_MODEL else 256)
    blk = pl.BlockSpec((tr, cdim), lambda i, pos: (i, 0))
    n_other = len(others)

    def body(pos_ref, g_ref, r1_ref, *refs):
        other_refs, (w_ref, m_ref, v_ref, g_out, d_out, m_out, v_out) = refs[:n_other], refs[n_other:]
        g = g_ref[...].astype(F32) + r1_ref[...].astype(F32)
        for o_ref in other_refs:
            g = g + o_ref[...].astype(F32)
        d, mn, vn = _adamw_math(w_ref[...], g, m_ref[...], v_ref[...])
        g_out[...] = g
        d_out[...] = d
        m_out[...] = mn
        v_out[...] = vn

    def other_spec(lead):
        if lead is None:
            return blk
        return pl.BlockSpec((None, tr, cdim), lambda i, pos: (lead, i, 0))

    shp = jax.ShapeDtypeStruct((r, cdim), F32)
    return _call(
        name, body, grid=(r // tr,), n_prefetch=1,
        in_specs=[pl.BlockSpec((None, tr, cdim), lambda i, pos: (2 * _chip_of(pos, 5) + pos[2], i, 0)),
                  pl.BlockSpec((None, tr, cdim), lambda i, pos: (_chip_of(pos, 5), i, 0))]
        + [other_spec(lead) for _, lead in others] + [blk, blk, blk],
        out_specs=[blk, blk, blk, blk], out_shape=[shp, shp, shp, shp],
        sem=("parallel",), args=(pos, g8, r1, *[a for a, _ in others], w, m, v), jobs=jobs)


def _row_spec(t, width=D_MODEL):
    return pl.BlockSpec((t, width), lambda i: (i, 0))


def _full_spec(shape):
    return pl.BlockSpec(shape, lambda i: tuple(0 for _ in shape))


def _ln_stats(p):
    mu = jnp.mean(p, axis=-1, keepdims=True)
    xc = p - mu
    var = jnp.mean(xc * xc, axis=-1, keepdims=True)
    rstd = lax.rsqrt(var + LN_EPS)
    return xc * rstd, rstd


def _ln_bwd(dy, xhat, rstd, g):
    dxh = dy * g
    m1 = jnp.mean(dxh, axis=-1, keepdims=True)
    m2 = jnp.mean(dxh * xhat, axis=-1, keepdims=True)
    return rstd * (dxh - m1 - xhat * m2)


def _acc_rows(ref, val, first):
    s = jnp.sum(val, axis=0, keepdims=True)

    @pl.when(first)
    def _():
        ref[...] = s

    @pl.when(jnp.logical_not(first))
    def _():
        ref[...] += s


def _modulate(name, x, mod6):
    s = x.shape[0]
    t = _tile(s, 256)

    def body(x_ref, mod_ref, o_ref):
        o_ref[...] = (x_ref[...] * (1.0 + mod_ref[1:2, :]) + mod_ref[0:1, :]).astype(BF16)

    return pl.pallas_call(
        body, name=name, grid=(s // t,),
        in_specs=[_row_spec(t), _full_spec((N_MOD, D_MODEL))],
        out_specs=_row_spec(t),
        out_shape=jax.ShapeDtypeStruct((s, D_MODEL), BF16),
        compiler_params=_cparams(("parallel",)),
    )(x, mod6)


def _ln1_fwd(x, mix, mod6, g, b, jobs=()):
    s = x.shape[0]
    t = _tile(s, 256)

    def body(x_ref, mix_ref, mod_ref, g_ref, b_ref, x1_ref, h2_ref):
        pre = ALPHA * x_ref[...] + (1.0 + mod_ref[2:3, :]) * mix_ref[...]
        xhat, _ = _ln_stats(pre)
        x1 = xhat * g_ref[...] + b_ref[...]
        x1_ref[...] = x1
        h2_ref[...] = (x1 * (1.0 + mod_ref[4:5, :]) + mod_ref[3:4, :]).astype(BF16)

    vec = _full_spec((1, D_MODEL))
    return _call(
        "ln1_fwd", body, grid=(s // t,),
        in_specs=[_row_spec(t), _row_spec(t), _full_spec((N_MOD, D_MODEL)), vec, vec],
        out_specs=[_row_spec(t), _row_spec(t)],
        out_shape=[jax.ShapeDtypeStruct((s, D_MODEL), F32), jax.ShapeDtypeStruct((s, D_MODEL), BF16)],
        sem=("parallel",), args=(x, mix, mod6, g, b), jobs=jobs)


def _ln2_loss_bwd(x1, f2, tgt, mod6, g, b):
    s = x1.shape[0]
    t = _tile(s, 128)

    def body(x1_ref, f2_ref, tgt_ref, mod_ref, g_ref, b_ref,
             df2_ref, dx1_ref, loss_ref, dg_ref, db_ref, dgate_ref):
        first = pl.program_id(0) == 0
        gate = 1.0 + mod_ref[5:6, :]
        f2v = f2_ref[...]
        pre = ALPHA * x1_ref[...] + gate * f2v
        xhat, rstd = _ln_stats(pre)
        err = xhat * g_ref[...] + b_ref[...] - tgt_ref[...]
        part = 0.5 * jnp.sum(jnp.mean(err * err, axis=-1, keepdims=True), axis=0, keepdims=True)
        dy = err / D_MODEL
        dpre = _ln_bwd(dy, xhat, rstd, g_ref[...])
        df2_ref[...] = (gate * dpre).astype(BF16)
        dx1_ref[...] = (ALPHA * dpre).astype(BF16)
        _acc_rows(loss_ref, jnp.broadcast_to(part, (1, 128)), first)
        _acc_rows(dg_ref, dy * xhat, first)
        _acc_rows(db_ref, dy, first)
        _acc_rows(dgate_ref, dpre * f2v, first)

    vec = _full_spec((1, D_MODEL))
    vshape = jax.ShapeDtypeStruct((1, D_MODEL), F32)
    return pl.pallas_call(
        body, name="ln2_loss_bwd", grid=(s // t,),
        in_specs=[_row_spec(t), _row_spec(t), _row_spec(t), _full_spec((N_MOD, D_MODEL)), vec, vec],
        out_specs=[_row_spec(t), _row_spec(t), _full_spec((1, 128)), vec, vec, vec],
        out_shape=[jax.ShapeDtypeStruct((s, D_MODEL), BF16), jax.ShapeDtypeStruct((s, D_MODEL), BF16),
                   jax.ShapeDtypeStruct((1, 128), F32), vshape, vshape, vshape],
        compiler_params=_cparams(("arbitrary",)),
    )(x1, f2, tgt, mod6, g, b)


def _ln1_bwd(dh2, dx1a, x1, x, mix, mod6, g):
    s = x.shape[0]
    t = _tile(s, 128)

    def body(dh2_ref, dx1a_ref, x1_ref, x_ref, mix_ref, mod_ref, g_ref,
             dmix_ref, dxa_ref, dscale_ref, dshift_ref, dg_ref, db_ref, dgate_ref):
        first = pl.program_id(0) == 0
        dh2v = dh2_ref[...]
        dx1 = dx1a_ref[...] + dh2v * (1.0 + mod_ref[4:5, :])
        gate = 1.0 + mod_ref[2:3, :]
        mixv = mix_ref[...]
        pre = ALPHA * x_ref[...] + gate * mixv
        xhat, rstd = _ln_stats(pre)
        dpre = _ln_bwd(dx1, xhat, rstd, g_ref[...])
        dmix_ref[...] = (gate * dpre).astype(BF16)
        dxa_ref[...] = (ALPHA * dpre).astype(BF16)
        _acc_rows(dscale_ref, dh2v * x1_ref[...], first)
        _acc_rows(dshift_ref, dh2v, first)
        _acc_rows(dg_ref, dx1 * xhat, first)
        _acc_rows(db_ref, dx1, first)
        _acc_rows(dgate_ref, dpre * mixv, first)

    vec = _full_spec((1, D_MODEL))
    vshape = jax.ShapeDtypeStruct((1, D_MODEL), F32)
    return pl.pallas_call(
        body, name="ln1_bwd", grid=(s // t,),
        in_specs=[_row_spec(t)] * 5 + [_full_spec((N_MOD, D_MODEL)), vec],
        out_specs=[_row_spec(t), _row_spec(t), vec, vec, vec, vec, vec],
        out_shape=[jax.ShapeDtypeStruct((s, D_MODEL), BF16), jax.ShapeDtypeStruct((s, D_MODEL), BF16),
                   vshape, vshape, vshape, vshape, vshape],
        compiler_params=_cparams(("arbitrary",)),
    )(dh2, dx1a, x1, x, mix, mod6, g)


def _grad_x(dxa, dh1, x, mod6):
    s = x.shape[0]
    t = _tile(s, 256)

    def body(dxa_ref, dh1_ref, x_ref, mod_ref, gx_ref, dscale_ref, dshift_ref):
        first = pl.program_id(0) == 0
        dh1v = dh1_ref[...]
        gx_ref[...] = dxa_ref[...] + dh1v * (1.0 + mod_ref[1:2, :])
        _acc_rows(dscale_ref, dh1v * x_ref[...], first)
        _acc_rows(dshift_ref, dh1v, first)

    vec = _full_spec((1, D_MODEL))
    vshape = jax.ShapeDtypeStruct((1, D_MODEL), F32)
    return pl.pallas_call(
        body, name="grad_x", grid=(s // t,),
        in_specs=[_row_spec(t)] * 3 + [_full_spec((N_MOD, D_MODEL))],
        out_specs=[_row_spec(t), vec, vec],
        out_shape=[jax.ShapeDtypeStruct((s, D_MODEL), F32), vshape, vshape],
        compiler_params=_cparams(("arbitrary",)),
    )(dxa, dh1, x, mod6)


def _window(ref, i, t, s):
    r0 = pl.multiple_of(i * t, t)
    cur = ref[pl.ds(r0, t), :]
    lo = pl.multiple_of(jnp.maximum(r0 - HALO, 0), HALO)
    hi = pl.multiple_of(jnp.minimum(r0 + t, s - HALO), HALO)
    before = ref[pl.ds(lo, HALO), :] * (i > 0).astype(F32)
    after = ref[pl.ds(hi, HALO), :] * (i < s // t - 1).astype(F32)
    return jnp.concatenate([before, cur, after], axis=0)


def _tap(ext, shift):
    n = ext.shape[0]
    if shift == 0:
        return ext
    return pltpu.roll(ext, (-shift) % n, 0)


def _centre(ext, t):
    return ext[HALO:HALO + t]


def _conv_taps(ext, w, width):
    acc = None
    for k in range(width):
        term = _tap(ext, k - width // 2) * w[k:k + 1, :]
        acc = term if acc is None else acc + term
    return acc


def _silu(a):
    return a * jax.nn.sigmoid(a)


def _conv_silu_fwd(proj, w, b, jobs=()):
    s = proj.shape[0]
    cb = 256
    t = _tile(s, 256)
    off = D_SSM // cb

    def body(u_ref, w_ref, b_ref, o_ref):
        wv = w_ref[...]
        bv = b_ref[...]

        def step(i, carry):
            ext = _window(u_ref, i, t, s)
            a = _centre(_conv_taps(ext, wv, SSM_CONV), t) + bv
            o_ref[pl.ds(pl.multiple_of(i * t, t), t), :] = _silu(a)
            return carry

        lax.fori_loop(0, s // t, step, 0)

    return _call(
        "conv_silu_fwd", body, grid=(D_XBC // cb,),
        in_specs=[pl.BlockSpec((s, cb), lambda j: (0, off + j)),
                  pl.BlockSpec((SSM_CONV, cb), lambda j: (0, j)),
                  pl.BlockSpec((1, cb), lambda j: (0, j))],
        out_specs=[pl.BlockSpec((s, cb), lambda j: (0, j))],
        out_shape=[jax.ShapeDtypeStruct((s, D_XBC), F32)],
        sem=("parallel",), args=(proj, w, b), jobs=jobs)


def _conv_silu_bwd(name, proj, w, b, col0, ncols, cots, scaled=None):
    s = proj.shape[0]
    cb = 128
    t = _tile(s, 256)
    off = (D_SSM + col0) // cb
    woff = col0 // cb
    n_cot = len(cots)

    def body(*refs):
        u_ref, w_ref, b_ref = refs[:3]
        cot_refs = refs[3:3 + n_cot]
        sc_refs = refs[3 + n_cot:3 + n_cot + (2 if scaled else 0)]
        du_ref, dw_ref, db_ref = refs[-3:]
        wv = w_ref[...]
        bv = b_ref[...]

        def step(i, carry):
            ext = _window(u_ref, i, t, s)
            a = _conv_taps(ext, wv, SSM_CONV) + bv
            cot = None
            for cr in cot_refs:
                term = _window(cr.at[0], i, t, s) + _window(cr.at[1], i, t, s)
                cot = term if cot is None else cot + term
            if scaled:
                cot = cot + _window(sc_refs[0], i, t, s) * sc_refs[1][...]
            sig = jax.nn.sigmoid(a)
            da = cot * (sig * (1.0 + a * (1.0 - sig)))
            du = None
            new = []
            for k in range(SSM_CONV):
                sh = k - SSM_CONV // 2
                term = _tap(da, -sh) * wv[k:k + 1, :]
                du = term if du is None else du + term
                prod = _centre(_tap(ext, sh) * da, t)
                new.append(carry[k] + jnp.sum(prod, axis=0, keepdims=True))
            new.append(carry[SSM_CONV] + jnp.sum(_centre(da, t), axis=0, keepdims=True))
            du_ref[pl.ds(pl.multiple_of(i * t, t), t), :] = _centre(du, t).astype(BF16)
            return tuple(new)

        zero = jnp.zeros((1, cb), F32)
        acc = lax.fori_loop(0, s // t, step, tuple(zero for _ in range(SSM_CONV + 1)))
        for k in range(SSM_CONV):
            dw_ref[k:k + 1, :] = acc[k]
        db_ref[...] = acc[SSM_CONV]

    in_specs = [pl.BlockSpec((s, cb), lambda j: (0, off + j)),
                pl.BlockSpec((SSM_CONV, cb), lambda j: (0, woff + j)),
                pl.BlockSpec((1, cb), lambda j: (0, woff + j))]
    in_specs += [pl.BlockSpec((2, s, cb), lambda j: (0, 0, j)) for _ in cots]
    args = [proj, w, b, *cots]
    if scaled:
        in_specs += [pl.BlockSpec((s, cb), lambda j: (0, j)), pl.BlockSpec((1, cb), lambda j: (0, j))]
        args += list(scaled)
    return pl.pallas_call(
        body, name=name, grid=(ncols // cb,),
        in_specs=in_specs,
        out_specs=[pl.BlockSpec((s, cb), lambda j: (0, j)),
                   pl.BlockSpec((SSM_CONV, cb), lambda j: (0, j)),
                   pl.BlockSpec((1, cb), lambda j: (0, j))],
        out_shape=[jax.ShapeDtypeStruct((s, ncols), BF16),
                   jax.ShapeDtypeStruct((SSM_CONV, ncols), F32),
                   jax.ShapeDtypeStruct((1, ncols), F32)],
        compiler_params=_cparams(("parallel",)),
    )(*args)


_SC_H = (D_SSM + D_XBC) // SC_GROUP_W
_SC_B = _SC_H + D_SC // SC_GROUP_W
_SC_C = _SC_B + D_SC // SC_GROUP_W


def _sc_fwd(proj, w, nw):
    s = proj.shape[0]
    cb = SC_GROUP_W
    t = _tile(s, 256)

    def body(uh_ref, ub_ref, uc_ref, w_ref, nw_ref, o_ref):
        wv = w_ref[...]
        nwv = nw_ref[...]

        def step(i, carry):
            p = _window(uc_ref, i, t, s) * _window(uh_ref, i, t, s)
            cv = _centre(_conv_taps(p, wv, SC_CONV), t)
            rows = pl.ds(pl.multiple_of(i * t, t), t)
            y = ub_ref[rows, :] * cv
            r = lax.rsqrt(jnp.mean(y * y, axis=-1, keepdims=True) + RMS_EPS)
            o_ref[rows, :] = (y * r * nwv).astype(BF16)
            return carry

        lax.fori_loop(0, s // t, step, 0)

    def col(base):
        return pl.BlockSpec((s, cb), lambda j: (0, base + j))

    return pl.pallas_call(
        body, name="sc_fwd", grid=(D_SC // cb,),
        in_specs=[col(_SC_H), col(_SC_B), col(_SC_C),
                  pl.BlockSpec((SC_CONV, cb), lambda j: (0, j)),
                  pl.BlockSpec((1, cb), lambda j: (0, j))],
        out_specs=pl.BlockSpec((s, cb), lambda j: (0, j)),
        out_shape=jax.ShapeDtypeStruct((s, D_SC), BF16),
        compiler_params=_cparams(("parallel",)),
    )(proj, proj, proj, w, nw)


def _sc_bwd(proj, dycat, w, nw):
    s = proj.shape[0]
    cb = SC_GROUP_W
    t = _tile(s, 256)
    dy_off = D_SSM // cb

    def body(uh_ref, ub_ref, uc_ref, dy_ref, w_ref, nw_ref, duh_ref, dub_ref, duc_ref, dw_ref, dnw_ref):
        wv = w_ref[...]
        nwv = nw_ref[...]

        def step(i, carry):
            uh = _window(uh_ref, i, t, s)
            ub = _window(ub_ref, i, t, s)
            uc = _window(uc_ref, i, t, s)
            do = _window(dy_ref, i, t, s)
            p = uc * uh
            cv = _conv_taps(p, wv, SC_CONV)
            y = ub * cv
            r = lax.rsqrt(jnp.mean(y * y, axis=-1, keepdims=True) + RMS_EPS)
            dyr = do * nwv
            dy = r * dyr - y * (r * r * r) * jnp.mean(dyr * y, axis=-1, keepdims=True)
            dcv = dy * ub
            dp = None
            new = []
            for k in range(SC_CONV):
                sh = k - SC_CONV // 2
                term = _tap(dcv, -sh) * wv[k:k + 1, :]
                dp = term if dp is None else dp + term
                new.append(carry[k] + jnp.sum(_centre(_tap(p, sh) * dcv, t), axis=0, keepdims=True))
            new.append(carry[SC_CONV] + jnp.sum(_centre(do * y * r, t), axis=0, keepdims=True))
            rows = pl.ds(pl.multiple_of(i * t, t), t)
            duh_ref[rows, :] = _centre(dp * uc, t).astype(BF16)
            duc_ref[rows, :] = _centre(dp * uh, t).astype(BF16)
            dub_ref[rows, :] = _centre(dy * cv, t).astype(BF16)
            return tuple(new)

        zero = jnp.zeros((1, cb), F32)
        acc = lax.fori_loop(0, s // t, step, tuple(zero for _ in range(SC_CONV + 1)))
        for k in range(SC_CONV):
            dw_ref[k:k + 1, :] = acc[k]
        dnw_ref[...] = acc[SC_CONV]

    def col(base):
        return pl.BlockSpec((s, cb), lambda j: (0, base + j))

    out_col = pl.BlockSpec((s, cb), lambda j: (0, j))
    act = jax.ShapeDtypeStruct((s, D_SC), BF16)
    return pl.pallas_call(
        body, name="sc_bwd", grid=(D_SC // cb,),
        in_specs=[col(_SC_H), col(_SC_B), col(_SC_C), col(dy_off),
                  pl.BlockSpec((SC_CONV, cb), lambda j: (0, j)),
                  pl.BlockSpec((1, cb), lambda j: (0, j))],
        out_specs=[out_col, out_col, out_col,
                   pl.BlockSpec((SC_CONV, cb), lambda j: (0, j)),
                   pl.BlockSpec((1, cb), lambda j: (0, j))],
        out_shape=[act, act, act, jax.ShapeDtypeStruct((SC_CONV, D_SC), F32),
                   jax.ShapeDtypeStruct((1, D_SC), F32)],
        compiler_params=_cparams(("parallel",)),
    )(proj, proj, proj, dycat, w, nw)


def _make_select_dot(differentiable):
    def raw(a, b, mode, const):
        ops = [a, b]
        v = ops[1 - const]
        acc = None
        for _ in range(3):
            piece = v.astype(BF16)
            v = v - piece.astype(F32)
            ops[1 - const] = piece
            part = lax.dot_general(ops[0].astype(BF16), ops[1].astype(BF16), _DN[mode],
                                   preferred_element_type=F32)
            acc = part if acc is None else acc + part
        return acc

    if not differentiable:
        return raw

    @functools.partial(jax.custom_vjp, nondiff_argnums=(2, 3))
    def dot(a, b, mode, const):
        return raw(a, b, mode, const)

    def fwd(a, b, mode, const):
        return raw(a, b, mode, const), (a, b)

    def bwd(mode, const, res, g):
        a, b = res
        assert mode == "nn"
        if const == 1:
            return raw(g, b, "nt", 1), jnp.zeros_like(b)
        return jnp.zeros_like(a), raw(a, g, "tn", 0)

    dot.defvjp(fwd, bwd)
    return dot


def _make_dot(differentiable):
    def raw(a, b, mode):
        return lax.dot_general(a.astype(BF16), b.astype(BF16), _DN[mode], preferred_element_type=F32)

    if not differentiable:
        return raw

    @functools.partial(jax.custom_vjp, nondiff_argnums=(2,))
    def dot(a, b, mode):
        return raw(a, b, mode)

    def fwd(a, b, mode):
        return raw(a, b, mode), (a, b)

    def bwd(mode, res, g):
        a, b = res
        if mode == "nn":
            return raw(g, b, "nt"), raw(a, g, "tn")
        if mode == "nt":
            return raw(g, b, "nn"), raw(g, a, "tn")
        return raw(b, g, "nt"), raw(a, g, "nn")

    dot.defvjp(fwd, bwd)
    return dot


def _make_swap(differentiable):
    def raw(v):
        return pltpu.roll(v, HEAD_DIM, 1)

    if not differentiable:
        return raw
    swap = jax.custom_vjp(raw)
    swap.defvjp(lambda v: (raw(v), None), lambda _, g: (raw(g),))
    return swap


def _ssd_chunk(xs, bm, cm, dtx, acx, ax, prev, tri, differentiable):
    _bdot = _make_dot(differentiable)
    swap = _make_swap(differentiable)
    atx = jnp.sum(dtx * ax, axis=0, keepdims=True)
    xdt = xs * dtx
    mask = tri > 0.0
    scores = _bdot(cm, bm, "nt")
    head = lax.broadcasted_iota(jnp.int32, (1, GROUP_W), 1) // HEAD_DIM
    low = lax.broadcasted_iota(jnp.int32, (1, 128), 1) < HEAD_DIM
    y = _bdot(cm, prev, "nn") * jnp.exp(acx)
    for h in range(HEADS_PER_GROUP):
        pair = acx[:, 128 * (h // 2):128 * (h // 2) + 128]
        other = swap(pair)
        m1 = jnp.where(low, pair, other) if h % 2 == 0 else jnp.where(low, other, pair)
        seg = m1 - m1.T
        decay = jnp.where(mask, jnp.exp(jnp.where(mask, seg, 0.0)), 0.0)
        xh = xdt * (head == h).astype(F32)
        y = y + _bdot(scores * decay, xh, "nn")
    new = prev * jnp.exp(atx) + _bdot(bm, xdt * jnp.exp(atx - acx), "tn")
    return y, new


def _softplus(v):
    return jnp.maximum(v, 0.0) + jnp.log(1.0 + jnp.exp(-jnp.abs(v)))


def _dt_spread(u, bias, a, tri2, exf, differentiable):
    sel = _make_select_dot(differentiable)
    dt = _softplus(u + bias)
    dta = dt * a
    out = []
    for d in range(2):
        acum = sel(tri2[d], dta, "nn", 0)
        out += [sel(dt, exf[d], "nn", 1), sel(acum, exf[d], "nn", 1)]
    return tuple(out)


def _ssd_consts():
    q = CHUNK
    r = lax.broadcasted_iota(jnp.int32, (q, q), 0)
    c = lax.broadcasted_iota(jnp.int32, (q, q), 1)
    tri = jnp.stack([(c <= r), (c >= r)]).astype(F32)
    shp = (2, 128, D_SSM)
    src = lax.broadcasted_iota(jnp.int32, shp, 1)
    d = lax.broadcasted_iota(jnp.int32, shp, 0)
    col = lax.broadcasted_iota(jnp.int32, shp, 2)
    exf = (src == d * HEADS + col // HEAD_DIM).astype(F32)
    return tri, exf


def _dt_prep(proj_dt, bias_all, a_all, jobs=()):
    s = proj_dt.shape[0]
    tri, exf = _ssd_consts()

    def body(u_ref, b_ref, a_ref, tri_ref, exf_ref, dtx_ref, acx_ref):
        dtx0, acx0, dtx1, acx1 = _dt_spread(u_ref[...], b_ref[...], a_ref[...], tri_ref[...],
                                            exf_ref[...], False)
        dtx_ref[0] = dtx0
        dtx_ref[1] = dtx1
        acx_ref[0] = acx0
        acx_ref[1] = acx1

    out = pl.BlockSpec((2, CHUNK, D_SSM), lambda i: (0, i, 0))
    shp = jax.ShapeDtypeStruct((2, s, D_SSM), F32)
    return _call(
        "dt_prep", body, grid=(s // CHUNK,),
        in_specs=[_row_spec(CHUNK, 128), _full_spec((1, 128)), _full_spec((1, 128)),
                  _full_spec((2, CHUNK, CHUNK)), _full_spec((2, 128, D_SSM))],
        out_specs=[out, out], out_shape=[shp, shp],
        sem=("parallel",), args=(proj_dt, bias_all, a_all, tri, exf), jobs=jobs)


def _dt_prep_bwd(proj_dt, bias_all, a_all, d_dtx, d_acx):
    s = proj_dt.shape[0]
    tri, exf = _ssd_consts()

    def body(u_ref, b_ref, a_ref, tri_ref, exf_ref, ddtx_ref, dacx_ref, du_ref, db_ref, da_ref):
        tri_v, exf_v = tri_ref[...], exf_ref[...]

        def f(u, bias, a):
            return _dt_spread(u, bias, a, tri_v, exf_v, True)

        _, vjp = jax.vjp(f, u_ref[...], b_ref[...], a_ref[...])
        du, db, da = vjp((ddtx_ref[0], dacx_ref[0], ddtx_ref[1], dacx_ref[1]))
        du_ref[...] = du.astype(BF16)
        first = pl.program_id(0) == 0
        _acc_rows(db_ref, db, first)
        _acc_rows(da_ref, da, first)

    cot = pl.BlockSpec((2, CHUNK, D_SSM), lambda i: (0, i, 0))
    vec = _full_spec((1, 128))
    return pl.pallas_call(
        body, name="dt_prep_bwd", grid=(s // CHUNK,),
        in_specs=[_row_spec(CHUNK, 128), vec, vec, _full_spec((2, CHUNK, CHUNK)),
                  _full_spec((2, 128, D_SSM)), cot, cot],
        out_specs=[_row_spec(CHUNK, 128), vec, vec],
        out_shape=[jax.ShapeDtypeStruct((s, 128), BF16), jax.ShapeDtypeStruct((1, 128), F32),
                   jax.ShapeDtypeStruct((1, 128), F32)],
        compiler_params=_cparams(("arbitrary",)),
    )(proj_dt, bias_all, a_all, tri, exf, d_dtx, d_acx)


GROUPS_PER_STEP = 4
_PAIR_W = GROUPS_PER_STEP * GROUP_W
_PAIR_N = GROUPS_PER_STEP * N_STATE


def _ssd_specs(chunk_of):
    q = CHUNK
    b0 = D_SSM // _PAIR_N
    xs = pl.BlockSpec((q, _PAIR_W), lambda d, g, ci: (chunk_of(d, ci), g))
    bm = pl.BlockSpec((q, _PAIR_N), lambda d, g, ci: (chunk_of(d, ci), b0 + g))
    cm = pl.BlockSpec((q, _PAIR_N), lambda d, g, ci: (chunk_of(d, ci), b0 + GROUPS // GROUPS_PER_STEP + g))
    spread = pl.BlockSpec((None, q, _PAIR_W), lambda d, g, ci: (d, chunk_of(d, ci), g))
    ax = pl.BlockSpec((None, 1, _PAIR_W), lambda d, g, ci: (d, 0, g))
    tri = pl.BlockSpec((None, q, q), lambda d, g, ci: (d, 0, 0))
    st = pl.BlockSpec((None, None, GROUPS_PER_STEP, N_STATE, GROUP_W),
                      lambda d, g, ci: (d, chunk_of(d, ci), g, 0, 0))
    return xs, bm, cm, spread, ax, tri, st


def _wide(k):
    return slice(k * GROUP_W, (k + 1) * GROUP_W)


def _narrow(k):
    return slice(k * N_STATE, (k + 1) * N_STATE)


def _ssd_fwd(xbc, dtx, acx, ax, jobs=()):
    s = xbc.shape[0]
    nc = s // CHUNK
    tri, _ = _ssd_consts()

    def chunk_of(d, ci):
        return ci + d * (nc - 1 - 2 * ci)

    def body(xs_ref, b_ref, c_ref, dtx_ref, acx_ref, ax_ref, tri_ref, y_ref, st_ref, state):
        @pl.when(pl.program_id(2) == 0)
        def _():
            state[...] = jnp.zeros(state.shape, F32)

        tri_v = tri_ref[...]
        for k in range(GROUPS_PER_STEP):
            prev = state[k]
            st_ref[k] = prev
            y, new = _ssd_chunk(xs_ref[:, _wide(k)], b_ref[:, _narrow(k)], c_ref[:, _narrow(k)],
                                dtx_ref[:, _wide(k)], acx_ref[:, _wide(k)], ax_ref[:, _wide(k)],
                                prev, tri_v, False)
            y_ref[:, _wide(k)] = y
            state[k] = new

    xs, bm, cm, spread, ax_s, tri_s, st = _ssd_specs(chunk_of)
    return _call(
        "ssd_fwd", body, grid=(2, GROUPS // GROUPS_PER_STEP, nc),
        in_specs=[xs, bm, cm, spread, spread, ax_s, tri_s],
        out_specs=[spread, st],
        out_shape=[jax.ShapeDtypeStruct((2, s, D_SSM), F32),
                   jax.ShapeDtypeStruct((2, nc, GROUPS, N_STATE, GROUP_W), F32)],
        scratch_shapes=[pltpu.VMEM((GROUPS_PER_STEP, N_STATE, GROUP_W), F32)],
        sem=("arbitrary", "arbitrary", "arbitrary"),
        args=(xbc, xbc, xbc, dtx, acx, ax, tri), jobs=jobs)


def _ssd_bwd(xbc, dtx, acx, ax, states, dy, jobs=()):
    s = xbc.shape[0]
    nc = s // CHUNK
    tri, _ = _ssd_consts()

    def chunk_of(d, ci):
        return (nc - 1 - ci) + d * (2 * ci - (nc - 1))

    def body(xs_ref, b_ref, c_ref, dtx_ref, acx_ref, ax_ref, tri_ref, st_ref, dy_ref,
             dxs_ref, db_ref, dc_ref, ddtx_ref, dacx_ref, dax_ref, dstate):
        first = pl.program_id(2) == 0

        @pl.when(first)
        def _():
            dstate[...] = jnp.zeros(dstate.shape, F32)

        tri_v = tri_ref[...]

        def f(xs, bm, cm, dtx_v, acx_v, ax_v, prev):
            return _ssd_chunk(xs, bm, cm, dtx_v, acx_v, ax_v, prev, tri_v, True)

        dax_parts = []
        for k in range(GROUPS_PER_STEP):
            _, vjp = jax.vjp(f, xs_ref[:, _wide(k)], b_ref[:, _narrow(k)], c_ref[:, _narrow(k)],
                             dtx_ref[:, _wide(k)], acx_ref[:, _wide(k)], ax_ref[:, _wide(k)], st_ref[k])
            dxs, dbm, dcm, ddtx, dacx, dax, dprev = vjp((dy_ref[:, _wide(k)], dstate[k]))
            dxs_ref[:, _wide(k)] = dxs
            db_ref[:, _narrow(k)] = dbm
            dc_ref[:, _narrow(k)] = dcm
            ddtx_ref[:, _wide(k)] = ddtx
            dacx_ref[:, _wide(k)] = dacx
            dstate[k] = dprev
            dax_parts.append(dax)
        _acc_rows(dax_ref, jnp.concatenate(dax_parts, axis=1), first)

    xs, bm, cm, spread, ax_s, tri_s, st = _ssd_specs(chunk_of)
    dy_s = pl.BlockSpec((CHUNK, _PAIR_W), lambda d, g, ci: (chunk_of(d, ci), g))
    bc_s = pl.BlockSpec((None, CHUNK, _PAIR_N), lambda d, g, ci: (d, chunk_of(d, ci), g))
    wide = jax.ShapeDtypeStruct((2, s, D_SSM), F32)
    narrow = jax.ShapeDtypeStruct((2, s, GROUPS * N_STATE), F32)
    return _call(
        "ssd_bwd", body, grid=(2, GROUPS // GROUPS_PER_STEP, nc),
        in_specs=[xs, bm, cm, spread, spread, ax_s, tri_s, st, dy_s],
        out_specs=[spread, bc_s, bc_s, spread, spread, ax_s],
        out_shape=[wide, narrow, narrow, wide, wide, jax.ShapeDtypeStruct((2, 1, D_SSM), F32)],
        scratch_shapes=[pltpu.VMEM((GROUPS_PER_STEP, N_STATE, GROUP_W), F32)],
        sem=("arbitrary", "arbitrary", "arbitrary"),
        args=(xbc, xbc, xbc, dtx, acx, ax, tri, states, dy), jobs=jobs)


def _ssd_gate_fwd(y2, xbc, proj, dx, nw, jobs=()):
    s = xbc.shape[0]
    t = _tile(s, 512)

    def body(y_ref, xs_ref, z_ref, dx_ref, nw_ref, o_ref):
        y = (y_ref[0] + y_ref[1] + dx_ref[...] * xs_ref[...]) * _silu(z_ref[...])
        r = lax.rsqrt(jnp.mean(y * y, axis=-1, keepdims=True) + RMS_EPS)
        o_ref[...] = (y * r * nw_ref[...]).astype(BF16)

    blk = pl.BlockSpec((t, GROUP_W), lambda j, i: (i, j))
    vec = pl.BlockSpec((1, GROUP_W), lambda j, i: (0, j))
    return _call(
        "ssd_gate_fwd", body, grid=(GROUPS, s // t),
        in_specs=[pl.BlockSpec((2, t, GROUP_W), lambda j, i: (0, i, j)), blk, blk, vec, vec],
        out_specs=[blk], out_shape=[jax.ShapeDtypeStruct((s, D_SSM), BF16)],
        sem=("parallel", "parallel"), args=(y2, xbc, proj, dx, nw), jobs=jobs)


def _ssd_gate_bwd(y2, xbc, proj, dycat, dx, nw, jobs=()):
    s = xbc.shape[0]
    t = _tile(s, 512)

    def body(y_ref, xs_ref, z_ref, do_ref, dx_ref, nw_ref, dyc_ref, dz_ref, dd_ref, dnw_ref):
        first = pl.program_id(1) == 0
        z = z_ref[...]
        xs = xs_ref[...]
        sig = jax.nn.sigmoid(z)
        gate = z * sig
        yc = y_ref[0] + y_ref[1] + dx_ref[...] * xs
        y = yc * gate
        r = lax.rsqrt(jnp.mean(y * y, axis=-1, keepdims=True) + RMS_EPS)
        do = do_ref[...]
        dyr = do * nw_ref[...]
        dy = r * dyr - y * (r * r * r) * jnp.mean(dyr * y, axis=-1, keepdims=True)
        dyc = dy * gate
        dyc_ref[...] = dyc
        dz_ref[...] = (dy * yc * (sig * (1.0 + z * (1.0 - sig)))).astype(BF16)
        _acc_rows(dd_ref, dyc * xs, first)
        _acc_rows(dnw_ref, do * y * r, first)

    blk = pl.BlockSpec((t, GROUP_W), lambda j, i: (i, j))
    vec = pl.BlockSpec((1, GROUP_W), lambda j, i: (0, j))
    return _call(
        "ssd_gate_bwd", body, grid=(GROUPS, s // t),
        in_specs=[pl.BlockSpec((2, t, GROUP_W), lambda j, i: (0, i, j)), blk, blk, blk, vec, vec],
        out_specs=[blk, blk, vec, vec],
        out_shape=[jax.ShapeDtypeStruct((s, D_SSM), F32), jax.ShapeDtypeStruct((s, D_SSM), BF16),
                   jax.ShapeDtypeStruct((1, D_SSM), F32), jax.ShapeDtypeStruct((1, D_SSM), F32)],
        sem=("parallel", "arbitrary"), args=(y2, xbc, proj, dycat, dx, nw), jobs=jobs)


def _ada_fwd(c16, w_ada):
    k, n = w_ada.shape
    tn = 512

    def body(c_ref, w_ref, o_ref):
        a = _silu(c_ref[...]).astype(BF16)
        o_ref[...] = jnp.dot(a, w_ref[...].astype(BF16), preferred_element_type=F32)

    return pl.pallas_call(
        body, name="ada_fwd", grid=(n // tn,),
        in_specs=[_full_spec((16, k)), pl.BlockSpec((k, tn), lambda j: (0, j))],
        out_specs=pl.BlockSpec((16, tn), lambda j: (0, j)),
        out_shape=jax.ShapeDtypeStruct((16, n), F32),
        compiler_params=_cparams(("parallel",)),
    )(c16, w_ada)


def _ada_bwd_adamw(c16, dmod16, w, m, v):
    k, n = w.shape
    tm, tn = 256, n
    blk = pl.BlockSpec((tm, tn), lambda i, j: (i, j))

    def body(c_ref, d_ref, w_ref, m_ref, v_ref, g_out, d_out, m_out, v_out):
        a = _silu(c_ref[...]).astype(BF16)
        g = lax.dot_general(a, d_ref[...].astype(BF16), _DN["tn"], preferred_element_type=F32)
        d, mn, vn = _adamw_math(w_ref[...], g, m_ref[...], v_ref[...])
        g_out[...] = g
        d_out[...] = d
        m_out[...] = mn
        v_out[...] = vn

    shp = jax.ShapeDtypeStruct((k, n), F32)
    return pl.pallas_call(
        body, name="ada_bwd_adamw", grid=(k // tm, n // tn),
        in_specs=[pl.BlockSpec((16, tm), lambda i, j: (0, i)), pl.BlockSpec((16, tn), lambda i, j: (0, j)),
                  blk, blk, blk],
        out_specs=[blk, blk, blk, blk],
        out_shape=[shp, shp, shp, shp],
        compiler_params=_cparams(("parallel", "parallel")),
    )(c16, dmod16, w, m, v)


def _sum8_adamw(gathered, w, m, v):
    n = w.shape[1]
    tn = _tile(n, 8192)
    vec = pl.BlockSpec((1, tn), lambda j: (0, j))

    def body(g8_ref, w_ref, m_ref, v_ref, g_out, d_out, m_out, v_out):
        g = g8_ref[0:1, :]
        for k in range(1, N_DEV):
            g = g + g8_ref[k:k + 1, :]
        d, mn, vn = _adamw_math(w_ref[...], g, m_ref[...], v_ref[...])
        g_out[...] = g
        d_out[...] = d
        m_out[...] = mn
        v_out[...] = vn

    shp = jax.ShapeDtypeStruct((1, n), F32)
    return pl.pallas_call(
        body, name="sum8_adamw", grid=(n // tn,),
        in_specs=[pl.BlockSpec((N_DEV, tn), lambda j: (0, j)), vec, vec, vec],
        out_specs=[vec, vec, vec, vec],
        out_shape=[shp, shp, shp, shp],
        compiler_params=_cparams(("parallel",)),
    )(gathered, w, m, v)


def _gather_vec(name, v):
    n = v.shape[1]
    out = _run_jobs(name, [_GatherJob(v.reshape(8, n // 8), pltpu.VMEM)])[0]
    return out.reshape(N_DEV, n)


class _Plan:
    _RESULT = {"gather": "", "rs1": "r1_", "rs2": "r2_", "rs2a": "ra_", "rs2b": "rb_"}

    def __init__(self, hosted, store, hooks=None, two_leg=()):
        self.hosted, self.store, self.hooks, self.two_leg = hosted, dict(store), hooks or {}, two_leg

    def get(self, key):
        if key not in self.store and key.startswith("p_"):
            tag = key[2:]
            dests = (2, 3) if tag in self.two_leg else (0, 1, 2)
            self.store[key] = _pair_add("rs_pair_add_" + tag, self.get("g_" + tag), self.get("r1_" + tag),
                                        self.get("pos"), dests)
        if key not in self.store and key.startswith("q_"):
            tag = key[2:]
            self.store[key] = _relay_add("rs_relay_add_" + tag, self.get("g_" + tag), self.get("r1_" + tag),
                                         self.get("ra_" + tag), self.get("pos"))
        return self.store[key]

    def put(self, key, val):
        self.store[key] = val

    def part_job(self, tag, first, count, n):
        shard = self.get("shard_" + tag)
        rows = shard.shape[0] // n
        return _GatherJob(shard, rows=(rows * first, rows * count),
                          into=self.get("part_" + tag) if first else None)

    def jobs(self, host):
        make = {"gather": lambda t: _GatherJob(self.get("shard_" + t)),
                "rs1": lambda t: _SiblingJob(self.get("g_" + t)),
                "rs2": lambda t: _ChipsJob(self.get("p_" + t)),
                "rs2a": lambda t: _RelayFirstJob(self.get("p_" + t)),
                "rs2b": lambda t: _RelaySecondJob(self.get("q_" + t))}
        def one(kind, tag):
            if not isinstance(kind, tuple):
                return make[kind](tag)
            if kind[0] == "lead":
                return _GatherJob(self.get("shard_" + tag), lead=kind[1],
                                  into=self.get("part_" + tag) if kind[1] else None)
            return self.part_job(tag, *kind[1:])

        return [one(kind, tag) for kind, tag in self.hosted.get(host, ())]

    def deliver(self, kind, tag, res):
        if isinstance(kind, tuple):
            done = kind[1] + 1 == kind[2] if kind[0] == "lead" else kind[1] + kind[2] == kind[3]
            self.store["part_" + tag] = res
            if done:
                self.store[tag] = res
        else:
            self.store[self._RESULT[kind] + tag] = res

    def run(self, host, fn, *args, **kw):
        outs, results = fn(*args, jobs=self.jobs(host), **kw)
        for (kind, tag), res in zip(self.hosted.get(host, ()), results):
            self.deliver(kind, tag, res)
        return outs

    def hook(self, name):
        if name in self.hooks:
            self.hooks[name](self)


def _pad_lanes(v, n):
    return jnp.pad(v, ((0, 0), (0, n - v.shape[1])))


def _columns(arrays, start, width):
    pieces, at = [], 0
    for a in arrays:
        lo, hi = max(start, at), min(start + width, at + a.shape[1])
        if lo < hi:
            pieces.append(a[:, lo - at:hi - at])
        at += a.shape[1]
    return jnp.concatenate(pieces, axis=1)


def _local_step(plan, x, tgt, mod, conv_w, conv_b, dt_bias_f, dt_bias_b, a_log_f, a_log_b,
                ssm_d, ssm_nw, sc_w, sc_nw, ln1_g, ln1_b, ln2_g, ln2_b):
    s = x.shape[0]
    run = plan.run
    mod6 = mod.reshape(N_MOD, D_MODEL)
    bias_all = _pad_lanes(jnp.concatenate([dt_bias_f, dt_bias_b], axis=1), 128)
    a_all = _pad_lanes(-jnp.exp(jnp.concatenate([a_log_f, a_log_b], axis=1)), 128)
    a_x = jnp.stack([jnp.repeat(a_all[:, d * HEADS:(d + 1) * HEADS], HEAD_DIM, axis=1) for d in range(2)])
    d_lanes = jnp.repeat(ssm_d, HEAD_DIM, axis=1)

    half = D_MODEL // 2
    main_a, dt_a = _w_in_sections(plan.get("part_w_in"), 0)
    h1 = _modulate("mod1", x, mod6)
    part, = run("in_proj_a", _mm_nn, "in_proj_a", h1, main_a, F32)
    main_b, dt_b = _w_in_sections(plan.get("w_in"), 1)
    w_in_g = plan.get("w_in").reshape(N_DEV, D_MODEL, D_IN_SHARD)
    proj, = run("in_proj_b", _mm_nn, "in_proj_b", h1, main_b, F32, a_col0=half, extras=(part,),
                epilogue=lambda acc, first: (acc + first,))
    proj_dt = _mm_nn("in_proj_dt", h1, jnp.concatenate([dt_a, dt_b], axis=0), F32)[0][0]
    xbc, = run("conv_silu_fwd", _conv_silu_fwd, proj, conv_w, conv_b)
    dtx, acx = run("dt_prep", _dt_prep, proj_dt, bias_all, a_all)
    y2, states = run("ssd_fwd", _ssd_fwd, xbc, dtx, acx, a_x)
    y_ssm, = run("ssd_gate_fwd", _ssd_gate_fwd, y2, xbc, proj, d_lanes, ssm_nw)
    y_sc = _sc_fwd(proj, sc_w, sc_nw)
    ycat = jnp.concatenate([y_ssm, y_sc], axis=1)
    w_out_g = plan.get("w_out").reshape(D_MODEL, D_MODEL)
    mix, = run("out_proj", _mm_nn, "out_proj", ycat, w_out_g, F32)
    x1, h2 = run("ln1_fwd", _ln1_fwd, x, mix, mod6, ln1_g, ln1_b)

    def relu2(acc):
        u = acc.astype(BF16)
        r = jnp.maximum(acc, 0.0)
        return u, r * r

    w_up3 = plan.get("w_up")
    nper = w_up3.shape[2]
    tm = _tile(s, 1024)
    tn = 1024
    nb = nper // tn
    u_spec = pl.BlockSpec((tm, tn), lambda i, j, kk: (i, j))
    u, ff = run(
        "up_proj", _matmul, "up_proj", h2, w_up3, mode="nn", grid=(s // tm, D_FF // tn, 1),
        a_spec=pl.BlockSpec((tm, D_MODEL), lambda i, j, kk: (i, 0)),
        b_spec=pl.BlockSpec((None, D_MODEL, tn), lambda i, j, kk: (j // nb, 0, j % nb)),
        out_shapes=[jax.ShapeDtypeStruct((s, D_FF), BF16)] * 2, out_specs=[u_spec, u_spec],
        acc_shape=(tm, tn), epilogue=relu2)
    w_down_g = plan.get("w_down").reshape(D_FF, D_MODEL)
    f2 = _mm_nn("down_proj", ff, w_down_g, F32)[0][0]
    df2, dx1a, loss, g_ln2_g, g_ln2_b, dgate2 = _ln2_loss_bwd(x1, f2, tgt, mod6, ln2_g, ln2_b)

    def relu_grad(acc, uu):
        return (acc * (2.0 * jnp.maximum(uu.astype(F32), 0.0)),)

    du = _mm_nt("d_ff", df2, w_down_g, BF16, epilogue=relu_grad, extras=(u,))[0][0]
    plan.put("g_down", _mm_tn("g_w_down", ff, df2, BF16)[0][0].reshape(N_DEV, D_FF // N_DEV, D_MODEL))
    g_up, = run(
        "g_w_up", _matmul, "g_w_up", h2, du, mode="tn",
        grid=(D_MODEL // 1024, D_FF // tn, s // _tile(s, 2048)),
        a_spec=pl.BlockSpec((_tile(s, 2048), 1024), lambda i, j, kk: (kk, i)),
        b_spec=pl.BlockSpec((_tile(s, 2048), tn), lambda i, j, kk: (kk, j)),
        out_shapes=[jax.ShapeDtypeStruct((N_DEV, D_MODEL, nper), BF16)],
        out_specs=[pl.BlockSpec((None, 1024, tn), lambda i, j, kk: (j // nb, i, j % nb))],
        acc_shape=(1024, tn))
    plan.put("g_up", g_up)
    dh2, = run(
        "d_h2", _matmul, "d_h2", du, w_up3, mode="nt", grid=(s // tm, D_MODEL // 1024, D_FF // nper),
        a_spec=pl.BlockSpec((tm, nper), lambda i, j, kk: (i, kk)),
        b_spec=pl.BlockSpec((None, 1024, nper), lambda i, j, kk: (kk, j, 0)),
        out_shapes=[jax.ShapeDtypeStruct((s, D_MODEL), F32)],
        out_specs=[pl.BlockSpec((tm, 1024), lambda i, j, kk: (i, j))],
        acc_shape=(tm, 1024))
    dmix, dxa, dscale2, dshift2, g_ln1_g, g_ln1_b, dgate1 = _ln1_bwd(dh2, dx1a, x1, x, mix, mod6, ln1_g)

    dycat = _mm_nt("d_ycat", dmix, w_out_g, F32)[0][0]
    plan.put("g_out", run("g_w_out", _mm_tn, "g_w_out", ycat, dmix, BF16)[0].reshape(
        N_DEV, D_MODEL // N_DEV, D_MODEL))
    duh, dub, duc, g_sc_w, g_sc_nw = _sc_bwd(proj, dycat, sc_w, sc_nw)
    dyc, dz, dd_lanes, g_ssm_nw = run("ssd_gate_bwd", _ssd_gate_bwd, y2, xbc, proj, dycat, d_lanes, ssm_nw)
    dxs2, db2, dc2, ddtx, dacx, dax = run("ssd_bwd", _ssd_bwd, xbc, dtx, acx, a_x, states, dyc)
    n_bc = GROUPS * N_STATE
    du_xs, gw_xs, gb_xs = _conv_silu_bwd("conv_bwd_x", proj, conv_w, conv_b, 0, D_SSM, [dxs2],
                                         scaled=(dyc, d_lanes))
    du_b, gw_b, gb_b = _conv_silu_bwd("conv_bwd_b", proj, conv_w, conv_b, D_SSM, n_bc, [db2])
    du_c, gw_c, gb_c = _conv_silu_bwd("conv_bwd_c", proj, conv_w, conv_b, D_SSM + n_bc, n_bc, [dc2])
    du_dt, g_bias_all, g_a_sums = _dt_prep_bwd(proj_dt, bias_all, a_all, ddtx, dacx)

    sections = [dz, du_xs, du_b, du_c, du_dt[:, :2 * HEADS], duh, dub, duc]
    dproj3 = jnp.stack([_columns(sections, k * D_IN_SHARD, D_IN_SHARD) for k in range(N_DEV)])
    tk = _tile(s, 2048)
    g_in, = run(
        "g_w_in", _matmul, "g_w_in", h1, dproj3, mode="tn", grid=(N_DEV, D_MODEL // 1024, s // tk),
        a_spec=pl.BlockSpec((tk, 1024), lambda i, j, kk: (kk, j)),
        b_spec=pl.BlockSpec((None, tk, D_IN_SHARD), lambda i, j, kk: (i, kk, 0)),
        out_shapes=[jax.ShapeDtypeStruct((N_DEV, D_MODEL, D_IN_SHARD), BF16)],
        out_specs=[pl.BlockSpec((None, 1024, D_IN_SHARD), lambda i, j, kk: (i, j, 0))],
        acc_shape=(1024, D_IN_SHARD))
    plan.put("g_in", g_in)
    plan.hook("after_g_w_in")
    dh1, = run(
        "d_h1", _matmul, "d_h1", dproj3, w_in_g, mode="nt", grid=(s // tm, D_MODEL // 1024, N_DEV),
        a_spec=pl.BlockSpec((None, tm, D_IN_SHARD), lambda i, j, kk: (kk, i, 0)),
        b_spec=pl.BlockSpec((None, 1024, D_IN_SHARD), lambda i, j, kk: (kk, j, 0)),
        out_shapes=[jax.ShapeDtypeStruct((s, D_MODEL), F32)],
        out_specs=[pl.BlockSpec((tm, 1024), lambda i, j, kk: (i, j))],
        acc_shape=(tm, 1024))
    grad_x, dscale1, dshift1 = _grad_x(dxa, dh1, x, mod6)

    dmod = jnp.concatenate([dshift1, dscale1, dgate1, dshift2, dscale2, dgate2], axis=1)
    g_a_direct = dax.reshape(2, HEADS, HEAD_DIM).sum(axis=-1).reshape(1, 2 * HEADS)
    g_a_all = g_a_sums + _pad_lanes(g_a_direct, 128)
    small = {
        "dmod": dmod,
        "ssm_conv_w": jnp.concatenate([gw_xs, gw_b, gw_c], axis=1),
        "ssm_conv_b": jnp.concatenate([gb_xs, gb_b, gb_c], axis=1),
        "ssm_dt_bias_f": g_bias_all[:, :HEADS],
        "ssm_dt_bias_b": g_bias_all[:, HEADS:2 * HEADS],
        "ssm_a_log_f": (g_a_all * a_all)[:, :HEADS],
        "ssm_a_log_b": (g_a_all * a_all)[:, HEADS:2 * HEADS],
        "ssm_d": dd_lanes.reshape(HEADS, HEAD_DIM).sum(axis=1).reshape(1, HEADS),
        "ssm_norm_w": g_ssm_nw,
        "sc_conv_w": g_sc_w,
        "sc_norm_w": g_sc_nw,
        "ln1_g": g_ln1_g, "ln1_b": g_ln1_b, "ln2_g": g_ln2_g, "ln2_b": g_ln2_b,
    }
    return loss, grad_x, small


_SUMMED = [("ssm_conv_b", D_XBC), ("ssm_dt_bias_f", HEADS), ("ssm_dt_bias_b", HEADS),
           ("ssm_a_log_f", HEADS), ("ssm_a_log_b", HEADS), ("ssm_d", HEADS),
           ("ssm_norm_w", D_SSM), ("sc_norm_w", D_SC),
           ("ln1_g", D_MODEL), ("ln1_b", D_MODEL), ("ln2_g", D_MODEL), ("ln2_b", D_MODEL)]


def _round_up(n, k):
    return (n + k - 1) // k * k


def _w_in_sections(w_in_g, half):
    dt_lo = D_SSM + D_XBC
    k_dt = dt_lo // D_IN_SHARD
    cut = dt_lo - k_dt * D_IN_SHARD
    rest = (k_dt + 1) * D_IN_SHARD - dt_lo
    blocks = [w_in_g[k, half] for k in range(N_DEV)]
    dt = jnp.concatenate([blocks[k_dt][:, cut:], blocks[k_dt + 1][:, :2 * HEADS - rest]], axis=1)
    blocks[k_dt] = blocks[k_dt][:, :cut]
    blocks[k_dt + 1] = blocks[k_dt + 1][:, 2 * HEADS - rest:]
    return jnp.concatenate(blocks, axis=1), _pad_lanes(dt, 128)


def kernel(x, c, w_ada, b_ada, w_in, ssm_conv_w, ssm_conv_b, ssm_dt_bias_f, ssm_dt_bias_b, ssm_a_log_f, ssm_a_log_b, ssm_d, ssm_norm_w, sc_conv_w, sc_norm_w, w_out, ln1_g, ln1_b, w_up, w_down, ln2_g, ln2_b, loss_target, m_w_ada, m_b_ada, m_w_in, m_ssm_conv_w, m_ssm_conv_b, m_ssm_dt_bias_f, m_ssm_dt_bias_b, m_ssm_a_log_f, m_ssm_a_log_b, m_ssm_d, m_ssm_norm_w, m_sc_conv_w, m_sc_norm_w, m_w_out, m_ln1_g, m_ln1_b, m_w_up, m_w_down, m_ln2_g, m_ln2_b, v_w_ada, v_b_ada, v_w_in, v_ssm_conv_w, v_ssm_conv_b, v_ssm_dt_bias_f, v_ssm_dt_bias_b, v_ssm_a_log_f, v_ssm_a_log_b, v_ssm_d, v_ssm_norm_w, v_sc_conv_w, v_sc_norm_w, v_w_out, v_ln1_g, v_ln1_b, v_w_up, v_w_down, v_ln2_g, v_ln2_b):
    args = dict(locals())
    xi, yi, ci = _my_pos()
    me = 4 * xi + 2 * yi + ci
    pos = jnp.stack([xi, yi, ci]).astype(jnp.int32)
    s = x.shape[1]

    n_cw, n_sw = SSM_CONV * D_XBC // N_DEV, SC_CONV * D_SC // N_DEV
    vec = jnp.concatenate([c, ssm_conv_w[0].reshape(1, n_cw), sc_conv_w[0].reshape(1, n_sw)], axis=1)
    vec = _pad_lanes(vec, 8192)
    gath = _gather_vec("gather_c_conv", vec)
    c_all = gath[:, :D_MODEL]
    conv_w = gath[:, D_MODEL:D_MODEL + n_cw].reshape(N_DEV, SSM_CONV, D_XBC // N_DEV)
    conv_w = conv_w.transpose(1, 0, 2).reshape(SSM_CONV, D_XBC)
    sc_w = gath[:, D_MODEL + n_cw:D_MODEL + n_cw + n_sw].reshape(N_DEV, SC_CONV, D_SC // N_DEV)
    sc_w = sc_w.transpose(1, 0, 2).reshape(SC_CONV, D_SC)
    c16 = jnp.pad(c_all, ((0, 8), (0, 0)))

    n_ada = w_ada.shape[2]
    mod_cols = _ada_fwd(c16, w_ada[0])[:N_DEV]
    mod_all = _run_jobs("gather_mod", [_GatherJob(mod_cols, pltpu.VMEM)])[0]
    mod = lax.dynamic_index_in_dim(mod_all, me, axis=1, keepdims=False)
    mod = mod.reshape(1, N_MOD * D_MODEL) + b_ada

    out = {}

    def adamw(plan, tag):
        name = "w_" + tag
        if tag in two_leg:
            others = [(plan.get("ra_" + tag), 1), (plan.get("rb_" + tag), None)]
        else:
            others = [(plan.get("r2_" + tag), k) for k in range(3)]
        res = plan.run("rs_adamw_" + tag, _reduce_adamw, "rs_adamw_" + tag, plan.get("g_" + tag),
                       plan.get("r1_" + tag), others, pos, args[name][0], args["m_" + name][0],
                       args["v_" + name][0])
        out[name] = tuple(a[None] for a in res)

    two_leg = ("down", "up")
    hosted = {
        "in_proj_a": [(("lead", 1, 2), "w_in")],
        "in_proj_b": [("gather", "w_out")],
        "conv_silu_fwd": [(("part", 0, 1, 8), "w_up")],
        "dt_prep": [(("part", 1, 1, 8), "w_up")],
        "ssd_fwd": [(("part", 2, 2, 8), "w_up")],
        "ssd_gate_fwd": [(("part", 4, 1, 8), "w_up")],
        "out_proj": [(("part", 5, 2, 8), "w_up")],
        "ln1_fwd": [(("part", 7, 1, 8), "w_up")],
        "up_proj": [("gather", "w_down")],
        "g_w_up": [("rs1", "down")],
        "d_h2": [("rs2a", "down"), ("rs1", "up")],
        "g_w_out": [("rs2b", "down")],
        "ssd_gate_bwd": [("rs1", "out")],
        "ssd_bwd": [("rs2a", "up")],
        "g_w_in": [("rs2b", "up"), ("rs2", "out")],
        "d_h1": [("rs2", "in")],
    }

    def sibling_exchange_in(plan):
        plan.put("r1_in", _run_jobs("rs_sibling_in", [_SiblingJob(plan.get("g_in"))])[0])

    store = {"pos": pos}
    shard_in = _cast_bf16("cast_w_in", w_in[0]).reshape(2, D_MODEL // 2, D_IN_SHARD)
    casts, (part_in,) = _cast_many("cast_w_rest", [w_out[0], w_up[0], w_down[0]],
                                   jobs=[_GatherJob(shard_in, lead=0)])
    store.update(shard_w_in=shard_in, part_w_in=part_in, shard_w_out=casts[0], shard_w_up=casts[1],
                 shard_w_down=casts[2])
    plan = _Plan(hosted, store, hooks={"after_g_w_in": sibling_exchange_in}, two_leg=two_leg)
    loss, grad_x, small = _local_step(
        plan, x[0], loss_target[0], mod, conv_w, ssm_conv_b, ssm_dt_bias_f, ssm_dt_bias_b,
        ssm_a_log_f, ssm_a_log_b, ssm_d, ssm_norm_w, sc_w, sc_norm_w, ln1_g, ln1_b, ln2_g, ln2_b)
    for tag in ("down", "up", "out", "in"):
        adamw(plan, tag)

    parts = [small["dmod"]]
    parts += [_pad_lanes(small[n], _round_up(w, 128)) for n, w in _SUMMED]
    parts += [small["ssm_conv_w"].reshape(1, SSM_CONV * D_XBC), small["sc_conv_w"].reshape(1, SC_CONV * D_SC)]
    parts += [loss]
    gvec = jnp.concatenate(parts, axis=1)
    n_vec = _round_up(gvec.shape[1], 8192)
    gall = _gather_vec("gather_small_grads", _pad_lanes(gvec, n_vec))

    def shard_cols(full, k, per):
        return lax.dynamic_slice_in_dim(full.reshape(k, N_DEV, per), me, 1, axis=1).reshape(1, k * per)

    def placed(vals, n_rows=1):
        return jnp.concatenate(vals, axis=1)

    n_mod = N_MOD * D_MODEL
    ws, ms, vs = [b_ada], [m_b_ada], [v_b_ada]
    for n, w in _SUMMED:
        pw = _round_up(w, 128)
        ws.append(_pad_lanes(args[n], pw))
        ms.append(_pad_lanes(args["m_" + n], pw))
        vs.append(_pad_lanes(args["v_" + n], pw))

    def full_rows(shard, k, per):
        z = jnp.zeros((k, N_DEV, per), F32)
        z = lax.dynamic_update_slice_in_dim(z, shard.reshape(k, 1, per), me, axis=1)
        return z.reshape(1, k * N_DEV * per)

    for nm, k, per in (("ssm_conv_w", SSM_CONV, D_XBC // N_DEV), ("sc_conv_w", SC_CONV, D_SC // N_DEV)):
        ws.append(full_rows(args[nm][0], k, per))
        ms.append(full_rows(args["m_" + nm][0], k, per))
        vs.append(full_rows(args["v_" + nm][0], k, per))
    tail = n_vec - sum(a.shape[1] for a in ws)
    ws.append(jnp.zeros((1, tail), F32))
    ms.append(jnp.zeros((1, tail), F32))
    vs.append(jnp.ones((1, tail), F32))
    g_s, d_s, m_s, v_s = _sum8_adamw(gall, placed(ws), placed(ms), placed(vs))

    off = 0

    def take(w):
        nonlocal off
        sl = tuple(a[:, off:off + w] for a in (g_s, d_s, m_s, v_s))
        off += _round_up(w, 128)
        return sl

    out["b_ada"] = take(n_mod)
    for n, w in _SUMMED:
        out[n] = take(w)
    for nm, k, per in (("ssm_conv_w", SSM_CONV, D_XBC // N_DEV), ("sc_conv_w", SC_CONV, D_SC // N_DEV)):
        full = take(k * N_DEV * per)
        out[nm] = tuple(shard_cols(a, k, per).reshape(1, k, per) for a in full)
    loss_total = g_s[0, off]

    dmod_all = gall[:, :n_mod]
    dmod_cols = lax.dynamic_slice_in_dim(dmod_all.reshape(N_DEV, N_DEV, n_ada), me, 1, axis=1)
    dmod16 = jnp.pad(dmod_cols.reshape(N_DEV, n_ada), ((0, 8), (0, 0)))
    out["w_ada"] = tuple(a[None] for a in _ada_bwd_adamw(c16, dmod16, w_ada[0], m_w_ada[0], v_w_ada[0]))

    names = ['w_ada', 'b_ada', 'w_in', 'ssm_conv_w', 'ssm_conv_b', 'ssm_dt_bias_f', 'ssm_dt_bias_b',
             'ssm_a_log_f', 'ssm_a_log_b', 'ssm_d', 'ssm_norm_w', 'sc_conv_w', 'sc_norm_w', 'w_out',
             'ln1_g', 'ln1_b', 'w_up', 'w_down', 'ln2_g', 'ln2_b']
    res = [loss_total, grad_x[None]]
    for k in range(4):
        res += [out[n][k] for n in names]
    return tuple(res)
```
